```python
import jax, jax.numpy as jnp
from jax import lax
import numpy as np

D_MODEL = 1024
BATCH = 2
SEQ = 16384
DEPTH = 2

GRID_W = 64
CTX_LEN = 256
N_MOD = 6
D_MIX = D_MODEL
D_REC = D_MIX // 2
REC_HEADS = 8
REC_HEAD_DIM = D_REC // REC_HEADS
CONV_W = 4
LRU_C = 8.0
D_POOL = D_MIX // 4
POOL_WINDOWS = (2, 4, 8, 16)
POOL_GROUP = D_POOL // len(POOL_WINDOWS)
D_FOUR = D_MIX - D_REC - D_POOL
FOUR_HEADS = 4
FOUR_HEAD_DIM = D_FOUR // FOUR_HEADS
D_IN = 2 * D_REC + D_POOL + D_FOUR
N_GROUPS = 4
EXPERTS_PER_GROUP = 4
N_EXPERTS = N_GROUPS * EXPERTS_PER_GROUP
TOP_K_EXPERT = 2
D_EXPERT = 512
EPS = 1e-6

kernel_name = 'hybrid_rglru_pool_fourier_hmoe_dit'


def rms_norm(x, g):
    x32 = x.astype(jnp.float32)
    y = x32 * lax.rsqrt(jnp.mean(x32 * x32, axis=-1, keepdims=True) + EPS)
    return (y * g.astype(jnp.float32)).astype(x.dtype)


def modulate(h, shift, scale):
    return h * (1.0 + scale) + shift


def _linear_combine(e1, e2):
    a1, b1 = e1
    a2, b2 = e2
    return a1 * a2, a2 * b1 + b2


def conv_centred(u, w, b):
    n = u.shape[1]
    left = CONV_W // 2
    right = CONV_W - 1 - left
    up = jnp.pad(u, ((0, 0), (left, right), (0, 0)))
    w32 = w.astype(jnp.float32)
    y = b.astype(jnp.float32)
    for k in range(CONV_W):
        y = y + w32[k] * up[:, k:k + n, :]
    return y


def rglru_direction(u, wa, ba, wx, bx, lam, h0, reverse):
    B, N, C = u.shape
    uh = u.reshape(B, N, REC_HEADS, REC_HEAD_DIM)
    r = jax.nn.sigmoid(jnp.einsum('bnhi,hij->bnhj', uh, wa.astype(jnp.float32)).reshape(B, N, C) + ba.astype(jnp.float32))
    ig = jax.nn.sigmoid(jnp.einsum('bnhi,hij->bnhj', uh, wx.astype(jnp.float32)).reshape(B, N, C) + bx.astype(jnp.float32))
    log_a = -LRU_C * r * jax.nn.softplus(-lam.astype(jnp.float32))
    a = jnp.exp(log_a)
    b = jnp.sqrt(-jnp.expm1(2.0 * log_a)) * (ig * u)
    a_cum, b_cum = lax.associative_scan(_linear_combine, (a, b), axis=1, reverse=reverse)
    return b_cum + a_cum * h0[:, None, :]


def recurrent_branch(z_x, conv_w, conv_b, wa, ba, wx, bx, lam, h0f, h0b):
    u = conv_centred(z_x.astype(jnp.float32), conv_w, conv_b)
    hf = rglru_direction(u, wa[0], ba[0], wx[0], bx[0], lam[0], h0f, False)
    hb = rglru_direction(u, wa[1], ba[1], wx[1], bx[1], lam[1], h0b, True)
    return hf, hb


def box_mean(x, w, axis):
    n = x.shape[axis]
    pads = [(0, 0)] * x.ndim
    pads[axis] = (1, 0)
    cs = jnp.cumsum(jnp.pad(x, pads), axis=axis)
    t = jnp.arange(n)
    lo = jnp.clip(t - w // 2, 0, n)
    hi = jnp.clip(t + (w - w // 2), 0, n)
    total = jnp.take(cs, hi, axis=axis) - jnp.take(cs, lo, axis=axis)
    cnt = (hi - lo).astype(x.dtype).reshape((n,) + (1,) * (x.ndim - axis - 1))
    return total / cnt


def pool_branch(z_p, pool_w, pool_scale, rows):
    B, N, _ = z_p.shape
    u = z_p.astype(jnp.float32).reshape(B, N, len(POOL_WINDOWS), POOL_GROUP)
    pooled = []
    for g, w in enumerate(POOL_WINDOWS):
        ug = u[:, :, g, :]
        if rows is None:
            pg = box_mean(ug, w, axis=1)
        else:
            grid = ug.reshape(B, rows, GRID_W, POOL_GROUP)
            pg = box_mean(box_mean(grid, w, axis=1), w, axis=2).reshape(B, N, POOL_GROUP)
        pooled.append(pg)
    d = jnp.stack(pooled, axis=2) - u
    y = jnp.einsum('bngi,gij->bngj', d, pool_w.astype(jnp.float32)).reshape(B, N, D_POOL)
    return (y * pool_scale.astype(jnp.float32)).astype(z_p.dtype)


def fourier_branch(z_f, fourier_w):
    B, N, _ = z_f.shape
    uh = z_f.astype(jnp.float32).reshape(B, N, FOUR_HEADS, FOUR_HEAD_DIM)
    f = jnp.fft.fft2(uh, axes=(1, 3), norm='ortho').real
    y = jnp.einsum('bnhi,hij->bnhj', f, fourier_w.astype(jnp.float32)).reshape(B, N, D_FOUR)
    return y.astype(z_f.dtype)


def token_mixer(h, w_in, w_out, conv_w, conv_b, wa, ba, wx, bx, lam, pool_w, pool_scale, fourier_w, h0f, h0b, rows):
    z = h @ w_in
    z_x = z[..., :D_REC]
    z_g = z[..., D_REC:2 * D_REC]
    z_p = z[..., 2 * D_REC:2 * D_REC + D_POOL]
    z_f = z[..., 2 * D_REC + D_POOL:]
    hf, hb = recurrent_branch(z_x, conv_w, conv_b, wa, ba, wx, bx, lam, h0f, h0b)
    y_a = ((hf + hb) * jax.nn.gelu(z_g.astype(jnp.float32))).astype(h.dtype)
    y_p = pool_branch(z_p, pool_w, pool_scale, rows)
    y_f = fourier_branch(z_f, fourier_w)
    y = jnp.concatenate([y_a, y_p, y_f], axis=-1) @ w_out
    return y, hf[:, -1, :], hb[:, 0, :]


def hier_moe(h, rg_w, rg_b, re_w, re_b, wg, wu, wd):
    lg = (h @ rg_w + rg_b).astype(jnp.float32)
    pg = jax.nn.softmax(lg, axis=-1)
    _, g_idx = lax.top_k(lg, 1)
    p_group = jnp.take_along_axis(pg, g_idx, axis=-1)
    le = (h @ re_w + re_b).astype(jnp.float32).reshape(h.shape[:-1] + (N_GROUPS, EXPERTS_PER_GROUP))
    g_mask = jax.nn.one_hot(g_idx[..., 0], N_GROUPS, dtype=jnp.float32)
    le_sel = jnp.sum(le * g_mask[..., None], axis=-2)
    e_top, e_idx = lax.top_k(le_sel, TOP_K_EXPERT)
    w_e = jax.nn.softmax(e_top, axis=-1) * p_group
    ids = g_idx * EXPERTS_PER_GROUP + e_idx
    gates = jnp.sum(jax.nn.one_hot(ids, N_EXPERTS, dtype=jnp.float32) * w_e[..., None], axis=-2).astype(h.dtype)
    y = jnp.zeros_like(h)
    for e in range(N_EXPERTS):
        hid = jax.nn.silu(h @ wg[e]) * (h @ wu[e])
        y = y + gates[..., e:e + 1] * (hid @ wd[e])
    return y


def setup_inputs(seed: int = 0) -> dict:
    key = jax.random.key(seed)
    ks = jax.random.split(key, 32)
    f32 = jnp.float32

    def nrm(k, shape, scale):
        return jax.random.normal(k, shape, f32) * scale

    a0 = jax.random.uniform(ks[13], (DEPTH, 2, D_REC), f32, 0.9, 0.999)
    sig_l = a0 ** (1.0 / LRU_C)
    rec_lambda = jnp.log(sig_l) - jnp.log1p(-sig_l)
    return {
        'x': nrm(ks[0], (BATCH, SEQ, D_MODEL), 1.0),
        'c': nrm(ks[1], (BATCH, D_MODEL), 1.0),
        'ctx': nrm(ks[2], (BATCH, CTX_LEN, D_MODEL), 1.0),
        'c_ctx': nrm(ks[3], (D_MODEL,), 1.0),
        'mod_w': nrm(ks[4], (DEPTH, D_MODEL, N_MOD * D_MODEL), 0.5 * D_MODEL ** -0.5),
        'mod_b': nrm(ks[5], (DEPTH, N_MOD * D_MODEL), 0.02),
        'norm_mix_g': 1.0 + nrm(ks[6], (DEPTH, D_MODEL), 0.02),
        'norm_ffn_g': 1.0 + nrm(ks[7], (DEPTH, D_MODEL), 0.02),
        'w_in': nrm(ks[8], (DEPTH, D_MODEL, D_IN), D_MODEL ** -0.5),
        'conv_w': nrm(ks[9], (DEPTH, CONV_W, D_REC), CONV_W ** -0.5),
        'conv_b': nrm(ks[10], (DEPTH, D_REC), 0.02),
        'rec_gate_a_w': nrm(ks[11], (DEPTH, 2, REC_HEADS, REC_HEAD_DIM, REC_HEAD_DIM), REC_HEAD_DIM ** -0.5),
        'rec_gate_a_b': nrm(ks[12], (DEPTH, 2, D_REC), 0.02),
        'rec_gate_x_w': nrm(ks[14], (DEPTH, 2, REC_HEADS, REC_HEAD_DIM, REC_HEAD_DIM), REC_HEAD_DIM ** -0.5),
        'rec_gate_x_b': nrm(ks[15], (DEPTH, 2, D_REC), 0.02),
        'rec_lambda': rec_lambda,
        'pool_w': nrm(ks[16], (DEPTH, len(POOL_WINDOWS), POOL_GROUP, POOL_GROUP), POOL_GROUP ** -0.5),
        'pool_scale': 1.0 + nrm(ks[17], (DEPTH, D_POOL), 0.1),
        'fourier_w': nrm(ks[18], (DEPTH, FOUR_HEADS, FOUR_HEAD_DIM, FOUR_HEAD_DIM), FOUR_HEAD_DIM ** -0.5),
        'w_out': nrm(ks[19], (DEPTH, D_MIX, D_MODEL), D_MIX ** -0.5),
        'router_group_w': nrm(ks[20], (DEPTH, D_MODEL, N_GROUPS), D_MODEL ** -0.5),
        'router_group_b': nrm(ks[21], (DEPTH, N_GROUPS), 0.01),
        'router_expert_w': nrm(ks[22], (DEPTH, D_MODEL, N_EXPERTS), D_MODEL ** -0.5),
        'router_expert_b': nrm(ks[23], (DEPTH, N_EXPERTS), 0.01),
        'expert_w_gate': nrm(ks[24], (DEPTH, N_EXPERTS, D_MODEL, D_EXPERT), D_MODEL ** -0.5),
        'expert_w_up': nrm(ks[25], (DEPTH, N_EXPERTS, D_MODEL, D_EXPERT), D_MODEL ** -0.5),
        'expert_w_down': nrm(ks[26], (DEPTH, N_EXPERTS, D_EXPERT, D_MODEL), D_EXPERT ** -0.5),
        'final_norm_g': 1.0 + nrm(ks[27], (D_MODEL,), 0.02),
    }


def reference(x, c, ctx, c_ctx, mod_w, mod_b, norm_mix_g, norm_ffn_g, w_in, conv_w, conv_b,
              rec_gate_a_w, rec_gate_a_b, rec_gate_x_w, rec_gate_x_b, rec_lambda, pool_w, pool_scale,
              fourier_w, w_out, router_group_w, router_group_b, router_expert_w, router_expert_b,
              expert_w_gate, expert_w_up, expert_w_down, final_norm_g):
    B = x.shape[0]
    rows = x.shape[1] // GRID_W
    zeros_state = jnp.zeros((B, D_REC), jnp.float32)
    for l in range(DEPTH):
        last = l == DEPTH - 1
        mod_lat = (jax.nn.silu(c) @ mod_w[l] + mod_b[l]).reshape(B, N_MOD, D_MODEL)[:, :, None, :]
        mod_ctx = (jax.nn.silu(c_ctx) @ mod_w[l] + mod_b[l]).reshape(N_MOD, D_MODEL)
        rec_p = (conv_w[l], conv_b[l], rec_gate_a_w[l], rec_gate_a_b[l], rec_gate_x_w[l], rec_gate_x_b[l], rec_lambda[l])
        moe_p = (router_group_w[l], router_group_b[l], router_expert_w[l], router_expert_b[l],
                 expert_w_gate[l], expert_w_up[l], expert_w_down[l])

        hc = modulate(rms_norm(ctx, norm_mix_g[l]), mod_ctx[0], mod_ctx[1])
        if not last:
            mix_c, sf, sb = token_mixer(hc, w_in[l], w_out[l], *rec_p, pool_w[l], pool_scale[l], fourier_w[l],
                                        zeros_state, zeros_state, None)
            ctx = ctx + mod_ctx[2] * mix_c
        else:
            hf_c, hb_c = recurrent_branch(hc @ w_in[l][:, :D_REC], *rec_p, zeros_state, zeros_state)
            sf, sb = hf_c[:, -1, :], hb_c[:, 0, :]

        hx = modulate(rms_norm(x, norm_mix_g[l]), mod_lat[:, 1], mod_lat[:, 0] * 0.0 + mod_lat[:, 1]) if False else \
            modulate(rms_norm(x, norm_mix_g[l]), mod_lat[:, 0], mod_lat[:, 1])
        mix_x, _, _ = token_mixer(hx, w_in[l], w_out[l], *rec_p, pool_w[l], pool_scale[l], fourier_w[l], sf, sb, rows)
        x = x + mod_lat[:, 2] * mix_x

        x = x + mod_lat[:, 5] * hier_moe(modulate(rms_norm(x, norm_ffn_g[l]), mod_lat[:, 3], mod_lat[:, 4]), *moe_p)
        if not last:
            ctx = ctx + mod_ctx[5] * hier_moe(modulate(rms_norm(ctx, norm_ffn_g[l]), mod_ctx[3], mod_ctx[4]), *moe_p)
    return rms_norm(x, final_norm_g)
```

```python
import functools
import math

import jax
import jax.numpy as jnp
from jax import lax
from jax.experimental import pallas as pl
from jax.experimental.pallas import tpu as pltpu

F32 = jnp.float32
BF16 = jnp.bfloat16

GRID_W = 64
N_MOD = 6
REC_HEADS = 8
CONV_W = 4
LRU_C = 8.0
POOL_WINDOWS = (2, 4, 8, 16)
FOUR_HEADS = 4
N_GROUPS = 4
EXPERTS_PER_GROUP = 4
EPS = 1e-6

LANES = 128
SUBLANES = 8
POOL_PAD = 8
VMEM_LIMIT = 56 * 1024 * 1024


def _cparams(*sem):
    return pltpu.CompilerParams(dimension_semantics=sem, vmem_limit_bytes=VMEM_LIMIT)


def _split_bf16(a):
    hi = a.astype(BF16)
    lo = (a - hi.astype(F32)).astype(BF16)
    return hi, lo


def _dot(a, b):
    return jnp.dot(a, b, preferred_element_type=F32)


def _mod_kernel(s_ref, w_ref, b_ref, o_ref):
    s = s_ref[...]
    s = s * jax.nn.sigmoid(s)
    s_hi, s_lo = _split_bf16(s)
    w_hi, w_lo = _split_bf16(w_ref[...])
    o_ref[...] = _dot(s_hi, w_hi) + _dot(s_hi, w_lo) + _dot(s_lo, w_hi) + b_ref[...]


def _modulation(cvec, mod_w, mod_b):
    depth, d, dm = mod_w.shape
    tn = dm // 4
    return pl.pallas_call(
        _mod_kernel,
        grid=(depth, dm // tn),
        in_specs=[pl.BlockSpec((SUBLANES, d), lambda l, j: (0, 0)),
                  pl.BlockSpec((None, d, tn), lambda l, j: (l, 0, j)),
                  pl.BlockSpec((None, 1, tn), lambda l, j: (l, 0, j))],
        out_specs=pl.BlockSpec((None, SUBLANES, tn), lambda l, j: (l, 0, j)),
        out_shape=jax.ShapeDtypeStruct((depth, SUBLANES, dm), F32),
        compiler_params=_cparams("parallel", "parallel"),
    )(cvec, mod_w, mod_b.reshape(depth, 1, dm))


def _rms_mod(x, g, shift, scale):
    ms = jnp.mean(x * x, axis=-1, keepdims=True)
    y = x * lax.rsqrt(ms + EPS) * g
    return y * (1.0 + scale) + shift


def _inproj_kernel(x_ref, g_ref, mod_ref, w_ref, *o_refs, widths):
    h = _rms_mod(x_ref[...], g_ref[...], mod_ref[0:1, :], mod_ref[1:2, :])
    z = _dot(h.astype(BF16), w_ref[...])
    off = 0
    for o_ref, wd in zip(o_refs, widths):
        o_ref[...] = z[:, off:off + wd]
        off += wd


def _inproj(x, g, mod, w_bf16, widths):
    b, l, d = x.shape
    t = min(512, l)
    return pl.pallas_call(
        functools.partial(_inproj_kernel, widths=widths),
        grid=(b, l // t),
        in_specs=[pl.BlockSpec((None, t, d), lambda bi, i: (bi, i, 0)),
                  pl.BlockSpec((1, d), lambda bi, i: (0, 0)),
                  pl.BlockSpec((None, N_MOD, d), lambda bi, i: (bi, 0, 0)),
                  pl.BlockSpec(w_bf16.shape, lambda bi, i: (0, 0))],
        out_specs=[pl.BlockSpec((None, t, wd), lambda bi, i: (bi, i, 0)) for wd in widths],
        out_shape=[jax.ShapeDtypeStruct((b, l, wd), F32) for wd in widths],
        compiler_params=_cparams("parallel", "parallel"),
    )(x, g.reshape(1, d), mod, w_bf16)


def _rec_kernel(cf_ref, pf_ref, nf_ref, cb_ref, pb_ref, nb_ref, cw_ref, cbias_ref, wg_ref, bg_ref, lam_ref, h0_ref,
                hf_ref, hb_ref, a_scr, carry_scr, *, tc, n_chunks):
    i = pl.program_id(1)
    c = cf_ref.shape[-1]
    half = c // 2
    n_grp = tc // SUBLANES

    @pl.when(i == 0)
    def _():
        carry_scr[0] = jnp.broadcast_to(h0_ref[0:1, :], (SUBLANES, c))
        carry_scr[1] = jnp.broadcast_to(h0_ref[1:2, :], (SUBLANES, c))

    row = lax.broadcasted_iota(jnp.int32, (tc, half), 0) & (SUBLANES - 1)

    def direction(d, cur_ref, prev_ref, next_ref, is_first, is_last, out_ref):
        prev = jnp.where(is_first, 0.0, prev_ref[...])
        nxt = jnp.where(is_last, 0.0, next_ref[...])
        zz = jnp.concatenate([prev, cur_ref[...], nxt], axis=0)
        u = cbias_ref[...]
        for k in range(CONV_W):
            o = SUBLANES - CONV_W // 2 + k
            u = u + cw_ref[k:k + 1, :] * zz[o:o + tc, :]
        log_decay = -LRU_C * jax.nn.softplus(-lam_ref[d:d + 1, :])
        for hh in range(2):
            cols = slice(hh * half, (hh + 1) * half)
            uh = u[:, cols]
            pre = _dot(uh.astype(BF16), wg_ref[d, hh])
            r = jax.nn.sigmoid(pre[:, :half] + bg_ref[d, 0:1, cols])
            ig = jax.nn.sigmoid(pre[:, half:] + bg_ref[d, 1:2, cols])
            log_a = log_decay[:, cols] * r
            a = jnp.exp(log_a)
            th = jnp.tanh(log_a)
            bt = jnp.sqrt(-2.0 * th / (1.0 - th)) * (ig * uh)
            for s in (1, 2, 4):
                if d == 0:
                    a_sh, b_sh, keep = pltpu.roll(a, s, 0), pltpu.roll(bt, s, 0), row >= s
                else:
                    a_sh, b_sh, keep = pltpu.roll(a, tc - s, 0), pltpu.roll(bt, tc - s, 0), row < SUBLANES - s
                bt = a * jnp.where(keep, b_sh, 0.0) + bt
                a = a * jnp.where(keep, a_sh, 1.0)
            a_scr[d, :, cols] = a
            out_ref[:, cols] = bt

    direction(0, cf_ref, pf_ref, nf_ref, i == 0, i == n_chunks - 1, hf_ref)
    direction(1, cb_ref, pb_ref, nb_ref, i == n_chunks - 1, i == 0, hb_ref)

    def body(gi, carry):
        cf, cb = carry
        rf = pl.multiple_of(gi * SUBLANES, SUBLANES)
        rb = pl.multiple_of((n_grp - 1 - gi) * SUBLANES, SUBLANES)
        hf = hf_ref[pl.ds(rf, SUBLANES), :] + a_scr[0, pl.ds(rf, SUBLANES), :] * cf
        hb = hb_ref[pl.ds(rb, SUBLANES), :] + a_scr[1, pl.ds(rb, SUBLANES), :] * cb
        hf_ref[pl.ds(rf, SUBLANES), :] = hf
        hb_ref[pl.ds(rb, SUBLANES), :] = hb
        return (jnp.broadcast_to(hf[SUBLANES - 1:SUBLANES, :], (SUBLANES, c)),
                jnp.broadcast_to(hb[0:1, :], (SUBLANES, c)))

    cf, cb = lax.fori_loop(0, n_grp, body, (carry_scr[0], carry_scr[1]))
    carry_scr[0] = cf
    carry_scr[1] = cb


def _recurrent(z_x, conv_w, conv_b, wgate, bgate, lam, h0):
    b, l, c = z_x.shape
    tc = min(1024, l)
    nc = l // tc
    per = tc // SUBLANES
    nblk8 = l // SUBLANES
    cur_f = pl.BlockSpec((None, tc, c), lambda bi, i: (bi, i, 0))
    prev_f = pl.BlockSpec((None, SUBLANES, c), lambda bi, i: (bi, jnp.maximum(i * per - 1, 0), 0))
    next_f = pl.BlockSpec((None, SUBLANES, c), lambda bi, i: (bi, jnp.minimum((i + 1) * per, nblk8 - 1), 0))
    cur_b = pl.BlockSpec((None, tc, c), lambda bi, i: (bi, nc - 1 - i, 0))
    prev_b = pl.BlockSpec((None, SUBLANES, c), lambda bi, i: (bi, jnp.maximum((nc - 1 - i) * per - 1, 0), 0))
    next_b = pl.BlockSpec((None, SUBLANES, c), lambda bi, i: (bi, jnp.minimum((nc - i) * per, nblk8 - 1), 0))
    full = lambda a: pl.BlockSpec(a.shape, lambda bi, i: (0,) * a.ndim)
    return pl.pallas_call(
        functools.partial(_rec_kernel, tc=tc, n_chunks=nc),
        grid=(b, nc),
        in_specs=[cur_f, prev_f, next_f, cur_b, prev_b, next_b,
                  full(conv_w), full(conv_b), full(wgate), full(bgate), full(lam),
                  pl.BlockSpec((None, 2, c), lambda bi, i: (bi, 0, 0))],
        out_specs=[pl.BlockSpec((None, tc, c), lambda bi, i: (bi, i, 0)),
                   pl.BlockSpec((None, tc, c), lambda bi, i: (bi, nc - 1 - i, 0))],
        out_shape=[jax.ShapeDtypeStruct((b, l, c), F32)] * 2,
        scratch_shapes=[pltpu.VMEM((2, tc, c), F32), pltpu.VMEM((2, SUBLANES, c), F32)],
        compiler_params=_cparams("parallel", "arbitrary"),
    )(z_x, z_x, z_x, z_x, z_x, z_x, conv_w, conv_b, wgate, bgate, lam, h0)


def _dft_tables(n):
    j = lax.broadcasted_iota(jnp.int32, (n, n), 0)
    k = lax.broadcasted_iota(jnp.int32, (n, n), 1)
    ang = ((j * k) % n).astype(F32) * (2.0 * math.pi / n)
    return jnp.cos(ang), jnp.sin(ang)


def _block_diag(w):
    return jax.scipy.linalg.block_diag(*[w[h] for h in range(w.shape[0])])


def _channel_stage(gr, gi, c_ref, s_ref, w_ref, scale):
    f = (_dot(gr.astype(BF16), c_ref[...]) + _dot(gi.astype(BF16), s_ref[...])) * scale
    return _dot(f.astype(BF16), w_ref[...])


def _four1_kernel(x_ref, f1_ref, twc_ref, tws_ref, ar_ref, ai_ref, *, n1, nch):
    a = _dot(f1_ref[...], x_ref[...].astype(BF16))
    ar, ai = a[:n1], a[n1:]
    for jj in range(SUBLANES):
        cols = slice(jj * nch, (jj + 1) * nch)
        cc = twc_ref[:, jj:jj + 1]
        ss = tws_ref[:, jj:jj + 1]
        ar_ref[:, cols] = (ar[:, cols] * cc + ai[:, cols] * ss).astype(BF16)
        ai_ref[:, cols] = (ai[:, cols] * cc - ar[:, cols] * ss).astype(BF16)


def _four2_kernel(ar_ref, ai_ref, f2_ref, c_ref, s_ref, w_ref, o_ref, *, n2, nch, scale):
    grs, gis = [], []
    for kk in range(SUBLANES):
        slab = jnp.concatenate([ar_ref[kk], ai_ref[kk]], axis=0)
        g = _dot(f2_ref[...], slab)
        grs.append(g[:n2])
        gis.append(g[n2:])
    y = _channel_stage(jnp.concatenate(grs, axis=0), jnp.concatenate(gis, axis=0), c_ref, s_ref, w_ref, scale)
    for kk in range(SUBLANES):
        o_ref[:, kk * nch:(kk + 1) * nch] = y[kk * n2:(kk + 1) * n2]


def _four_small_kernel(x_ref, f_ref, c_ref, s_ref, w_ref, o_ref, *, n, scale):
    g = _dot(f_ref[...], x_ref[...].astype(BF16))
    o_ref[...] = _channel_stage(g[:n], g[n:], c_ref, s_ref, w_ref, scale)


def _fourier(z_f, four_w_bd):
    b, l, nch = z_f.shape
    hd = nch // FOUR_HEADS
    scale = 1.0 / math.sqrt(l * hd)
    c_h, s_h = _dft_tables(hd)
    eye = jnp.eye(FOUR_HEADS, dtype=F32)
    c_bd = jnp.kron(eye, c_h).astype(BF16)
    s_bd = jnp.kron(eye, s_h).astype(BF16)
    full = lambda a, nd: pl.BlockSpec(a.shape, lambda *_: (0,) * a.ndim)
    if l <= 256:
        cn, sn = _dft_tables(l)
        f = jnp.concatenate([cn, -sn], axis=0).astype(BF16)
        return pl.pallas_call(
            functools.partial(_four_small_kernel, n=l, scale=scale),
            grid=(b,),
            in_specs=[pl.BlockSpec((None, l, nch), lambda bi: (bi, 0, 0)),
                      full(f, 1), full(c_bd, 1), full(s_bd, 1), full(four_w_bd, 1)],
            out_specs=pl.BlockSpec((None, l, nch), lambda bi: (bi, 0, 0)),
            out_shape=jax.ShapeDtypeStruct((b, l, nch), F32),
            compiler_params=_cparams("parallel"),
        )(z_f, f, c_bd, s_bd, four_w_bd)

    n2 = LANES
    n1 = l // n2
    nj = n2 // SUBLANES
    c1, s1 = _dft_tables(n1)
    f1 = jnp.concatenate([c1, -s1], axis=0).astype(BF16)
    c2, s2 = _dft_tables(n2)
    f2 = jnp.concatenate([jnp.concatenate([c2, s2], axis=1),
                          jnp.concatenate([-s2, c2], axis=1)], axis=0).astype(BF16)
    k1 = lax.broadcasted_iota(jnp.int32, (n1, n2), 0)
    t2 = lax.broadcasted_iota(jnp.int32, (n1, n2), 1)
    ang = (k1 * t2).astype(F32) * (2.0 * math.pi / l)
    twc = jnp.cos(ang).reshape(n1, nj, SUBLANES).transpose(1, 0, 2)
    tws = jnp.sin(ang).reshape(n1, nj, SUBLANES).transpose(1, 0, 2)
    wide = SUBLANES * nch
    ar, ai = pl.pallas_call(
        functools.partial(_four1_kernel, n1=n1, nch=nch),
        grid=(b, nj),
        in_specs=[pl.BlockSpec((None, n1, wide), lambda bi, j: (bi, 0, j)),
                  full(f1, 2),
                  pl.BlockSpec((None, n1, SUBLANES), lambda bi, j: (j, 0, 0)),
                  pl.BlockSpec((None, n1, SUBLANES), lambda bi, j: (j, 0, 0))],
        out_specs=[pl.BlockSpec((None, n1, wide), lambda bi, j: (bi, 0, j))] * 2,
        out_shape=[jax.ShapeDtypeStruct((b, n1, n2 * nch), BF16)] * 2,
        compiler_params=_cparams("parallel", "parallel"),
    )(z_f.reshape(b, n1, n2 * nch), f1, twc, tws)
    y = pl.pallas_call(
        functools.partial(_four2_kernel, n2=n2, nch=nch, scale=scale),
        grid=(b, n1 // SUBLANES),
        in_specs=[pl.BlockSpec((None, SUBLANES, n2, nch), lambda bi, j: (bi, j, 0, 0)),
                  pl.BlockSpec((None, SUBLANES, n2, nch), lambda bi, j: (bi, j, 0, 0)),
                  full(f2, 2), full(c_bd, 2), full(s_bd, 2), full(four_w_bd, 2)],
        out_specs=pl.BlockSpec((None, n2, wide), lambda bi, j: (bi, 0, j)),
        out_shape=jax.ShapeDtypeStruct((b, n2, n1 * nch), F32),
        compiler_params=_cparams("parallel", "parallel"),
    )(ar.reshape(b, n1, n2, nch), ai.reshape(b, n1, n2, nch), f2, c_bd, s_bd, four_w_bd)
    return y.reshape(b, l, nch)


def _window_counts(idx, n, w):
    return (jnp.minimum(idx + (w - w // 2), n) - jnp.maximum(idx - w // 2, 0)).astype(F32)


def _pool_kernel(cur_ref, prev_ref, next_ref, pw_ref, ps_ref, o_ref, *, r, w, rows_total, n_tiles):
    i = pl.program_id(1)
    two_d = rows_total > 1
    halo = SUBLANES if two_d else 0
    nr = r + 2 * halo
    stride = w + POOL_PAD
    flat = nr * stride
    lane = lax.broadcasted_iota(jnp.int32, (1, 1, LANES), 2)
    low = lane < (LANES // 2)
    col = lax.broadcasted_iota(jnp.int32, (1, w, 1), 1)
    grow = lax.broadcasted_iota(jnp.int32, (r, 1, 1), 0) + i * r
    outs = []
    for hh in range(2):
        cols = slice(hh * LANES, (hh + 1) * LANES)
        w_lo, w_hi = POOL_WINDOWS[2 * hh], POOL_WINDOWS[2 * hh + 1]
        cur = cur_ref[:, cols]
        if two_d:
            prev = jnp.where(i == 0, 0.0, prev_ref[:, cols])
            nxt = jnp.where(i == n_tiles - 1, 0.0, next_ref[:, cols])
            ext = jnp.concatenate([prev, cur, nxt], axis=0)
        else:
            ext = cur
        ext = ext.reshape(nr, w, LANES)
        x = jnp.concatenate([jnp.zeros((nr, POOL_PAD, LANES), F32), ext], axis=1).reshape(flat, LANES)
        p = x + pltpu.roll(x, 1, 0)
        sums = {2: p}
        for ww, s in ((4, 1), (8, 2), (16, 4)):
            if ww > w_hi:
                break
            p = pltpu.roll(p, s, 0) + pltpu.roll(p, flat - s, 0)
            sums[ww] = p
        y = jnp.where(low[0], sums[w_lo], sums[w_hi]).reshape(nr, stride, LANES)[:, POOL_PAD:, :]
        if two_d:
            q2 = y[0:nr - 1] + y[1:nr]
            rsum = {2: q2[7:7 + r]}
            q4 = q2[0:nr - 3] + q2[2:nr - 1]
            rsum[4] = q4[6:6 + r]
            if w_hi > 4:
                q8 = q4[0:nr - 7] + q4[4:nr - 3]
                rsum[8] = q8[4:4 + r]
                q16 = q8[0:nr - 15] + q8[8:nr - 7]
                rsum[16] = q16[0:r]
            tot = jnp.where(low, rsum[w_lo], rsum[w_hi])
            cnt_r = jnp.where(low, _window_counts(grow, rows_total, w_lo), _window_counts(grow, rows_total, w_hi))
            tot = tot / cnt_r
        else:
            tot = y
        cnt_c = jnp.where(low, _window_counts(col, w, w_lo), _window_counts(col, w, w_hi))
        pooled = (tot / cnt_c).reshape(r * w, LANES)
        outs.append(pooled - cur)
    d = jnp.concatenate(outs, axis=-1)
    o_ref[...] = _dot(d.astype(BF16), pw_ref[...]) * ps_ref[...]


def _pool(z_p, pool_w_bd, pool_scale, rows):
    b, l, nch = z_p.shape
    if rows is None:
        rows_total, w, r = 1, l, 1
    else:
        rows_total, w, r = rows, l // rows, min(16, rows)
    t = r * w
    n_tiles = l // t
    hb = SUBLANES * w if rows is not None else t
    per = t // hb
    nhb = l // hb
    return pl.pallas_call(
        functools.partial(_pool_kernel, r=r, w=w, rows_total=rows_total, n_tiles=n_tiles),
        grid=(b, n_tiles),
        in_specs=[pl.BlockSpec((None, t, nch), lambda bi, i: (bi, i, 0)),
                  pl.BlockSpec((None, hb, nch), lambda bi, i: (bi, jnp.maximum(i * per - 1, 0), 0)),
                  pl.BlockSpec((None, hb, nch), lambda bi, i: (bi, jnp.minimum((i + 1) * per, nhb - 1), 0)),
                  pl.BlockSpec(pool_w_bd.shape, lambda bi, i: (0, 0)),
                  pl.BlockSpec((1, nch), lambda bi, i: (0, 0))],
        out_specs=pl.BlockSpec((None, t, nch), lambda bi, i: (bi, i, 0)),
        out_shape=jax.ShapeDtypeStruct((b, l, nch), F32),
        compiler_params=_cparams("parallel", "parallel"),
    )(z_p, z_p, z_p, pool_w_bd, pool_scale.reshape(1, nch))


def _first_argmax4(v):
    m = jnp.maximum(jnp.maximum(v[0], v[1]), jnp.maximum(v[2], v[3]))
    idx = jnp.where(v[0] >= m, 0, jnp.where(v[1] >= m, 1, jnp.where(v[2] >= m, 2, 3)))
    return m, idx


def _combine_kernel(hf_ref, hb_ref, zg_ref, yp_ref, yf_ref, x_ref, mod_ref, wo_ref, g_ref, wr_ref, br_ref,
                    xo_ref, h2_ref, gates_ref, gt_scr):
    y_a = (hf_ref[...] + hb_ref[...]) * jax.nn.gelu(zg_ref[...], approximate=True)
    ycat = jnp.concatenate([y_a.astype(BF16), yp_ref[...].astype(BF16), yf_ref[...].astype(BF16)], axis=-1)
    x_new = x_ref[...] + mod_ref[2:3, :] * _dot(ycat, wo_ref[...])
    xo_ref[...] = x_new
    h2 = _rms_mod(x_new, g_ref[...], mod_ref[3:4, :], mod_ref[4:5, :])
    h2_ref[...] = h2
    lt = (_dot(h2.astype(BF16), wr_ref[...]) + br_ref[...]).T
    lg = [lt[g:g + 1, :] for g in range(N_GROUPS)]
    m, gidx = _first_argmax4(lg)
    den = sum(jnp.exp(v - m) for v in lg)
    p_group = 1.0 / den
    sel = []
    for e in range(EXPERTS_PER_GROUP):
        rows = [lt[SUBLANES + EXPERTS_PER_GROUP * g + e:SUBLANES + EXPERTS_PER_GROUP * g + e + 1, :]
                for g in range(N_GROUPS)]
        sel.append(jnp.where(gidx == 0, rows[0], jnp.where(gidx == 1, rows[1], jnp.where(gidx == 2, rows[2], rows[3]))))
    v1, e1 = _first_argmax4(sel)
    rest = [jnp.where(e1 == e, -jnp.inf, sel[e]) for e in range(EXPERTS_PER_GROUP)]
    v2, e2 = _first_argmax4(rest)
    t2 = jnp.exp(v2 - v1)
    w1 = p_group / (1.0 + t2)
    w2 = p_group * t2 / (1.0 + t2)
    gt_scr[...] = jnp.zeros_like(gt_scr)
    for g in range(N_GROUPS):
        for e in range(EXPERTS_PER_GROUP):
            val = jnp.where(gidx == g, jnp.where(e1 == e, w1, jnp.where(e2 == e, w2, 0.0)), 0.0)
            gt_scr[EXPERTS_PER_GROUP * g + e:EXPERTS_PER_GROUP * g + e + 1, :] = val
    gates_ref[...] = gt_scr[...].T


def _combine(hf, hb, z_g, y_p, y_f, x, mod, w_out, g_ffn, w_router, b_router):
    b, l, d = x.shape
    t = min(512, l)
    tok = lambda c: pl.BlockSpec((None, t, c), lambda bi, i: (bi, i, 0))
    full = lambda a: pl.BlockSpec(a.shape, lambda bi, i: (0,) * a.ndim)
    return pl.pallas_call(
        _combine_kernel,
        grid=(b, l // t),
        in_specs=[tok(hf.shape[-1]), tok(hb.shape[-1]), tok(z_g.shape[-1]), tok(y_p.shape[-1]), tok(y_f.shape[-1]),
                  tok(d), pl.BlockSpec((None, N_MOD, d), lambda bi, i: (bi, 0, 0)),
                  full(w_out), full(g_ffn), full(w_router), full(b_router)],
        out_specs=[tok(d), tok(d), tok(LANES)],
        out_shape=[jax.ShapeDtypeStruct((b, l, d), F32), jax.ShapeDtypeStruct((b, l, d), F32),
                   jax.ShapeDtypeStruct((b, l, LANES), F32)],
        scratch_shapes=[pltpu.VMEM((LANES, t), F32)],
        compiler_params=_cparams("parallel", "parallel"),
    )(hf, hb, z_g, y_p, y_f, x, mod, w_out, g_ffn, w_router, b_router)


def _moe_kernel(h_ref, gates_ref, x_ref, mod_ref, wgu_ref, wd_ref, gfin_ref, o_ref, hb_scr, acc_scr, *, n_exp, final):
    e = pl.program_id(2)

    @pl.when(e == 0)
    def _():
        hb_scr[...] = h_ref[...].astype(BF16)
        acc_scr[...] = jnp.zeros_like(acc_scr)

    de = wd_ref.shape[0]
    gu = _dot(hb_scr[...], wgu_ref[...])
    hid = jax.nn.silu(gu[:, :de]) * gu[:, de:]
    lane = lax.broadcasted_iota(jnp.int32, gates_ref.shape, 1)
    gate = jnp.sum(jnp.where(lane == e, gates_ref[...], 0.0), axis=-1, keepdims=True)
    acc_scr[...] += gate * _dot(hid.astype(BF16), wd_ref[...])

    @pl.when(e == n_exp - 1)
    def _():
        out = x_ref[...] + mod_ref[5:6, :] * acc_scr[...]
        if final:
            ms = jnp.mean(out * out, axis=-1, keepdims=True)
            out = out * lax.rsqrt(ms + EPS) * gfin_ref[...]
        o_ref[...] = out


def _moe(h2, gates, x, mod, wgu, wd, g_final, final):
    b, l, d = x.shape
    n_exp = wgu.shape[0]
    t = min(512, l)
    tok = lambda c: pl.BlockSpec((None, t, c), lambda bi, i, e: (bi, i, 0))
    return pl.pallas_call(
        functools.partial(_moe_kernel, n_exp=n_exp, final=final),
        grid=(b, l // t, n_exp),
        in_specs=[tok(d), tok(LANES), tok(d),
                  pl.BlockSpec((None, N_MOD, d), lambda bi, i, e: (bi, 0, 0)),
                  pl.BlockSpec((None,) + wgu.shape[1:], lambda bi, i, e: (e, 0, 0)),
                  pl.BlockSpec((None,) + wd.shape[1:], lambda bi, i, e: (e, 0, 0)),
                  pl.BlockSpec((1, d), lambda bi, i, e: (0, 0))],
        out_specs=tok(d),
        out_shape=jax.ShapeDtypeStruct((b, l, d), F32),
        scratch_shapes=[pltpu.VMEM((t, d), BF16), pltpu.VMEM((t, d), F32)],
        compiler_params=_cparams("parallel", "parallel", "arbitrary"),
    )(h2, gates, x, mod, wgu, wd, g_final)


def kernel(x, c, ctx, c_ctx, mod_w, mod_b, norm_mix_g, norm_ffn_g, w_in, conv_w, conv_b, rec_gate_a_w, rec_gate_a_b,
           rec_gate_x_w, rec_gate_x_b, rec_lambda, pool_w, pool_scale, fourier_w, w_out, router_group_w,
           router_group_b, router_expert_w, router_expert_b, expert_w_gate, expert_w_up, expert_w_down, final_norm_g):
    b, l, d = x.shape
    depth = mod_w.shape[0]
    d_rec = conv_w.shape[-1]
    d_pool = pool_scale.shape[-1]
    d_four = fourier_w.shape[1] * fourier_w.shape[2]
    widths = (d_rec, d_rec, d_pool, d_four)
    rows = l // GRID_W
    heads_half = REC_HEADS // 2
    n_exp = expert_w_gate.shape[1]

    cvec = jnp.concatenate([c, c_ctx[None, :], jnp.zeros((SUBLANES - b - 1, d), F32)], axis=0)
    mod_all = _modulation(cvec, mod_w, mod_b).reshape(depth, SUBLANES, N_MOD, d)
    g_final = final_norm_g.reshape(1, d)
    zeros_state = jnp.zeros((b, 2, d_rec), F32)

    for li in range(depth):
        last = li == depth - 1
        mod_lat = mod_all[li, :b]
        mod_ctx = jnp.broadcast_to(mod_all[li, b][None], (b, N_MOD, d))
        w_in_l = w_in[li].astype(BF16)
        w_out_l = w_out[li].astype(BF16)
        wgate = jnp.stack([
            jnp.stack([jnp.concatenate([_block_diag(rec_gate_a_w[li, dd, hh * heads_half:(hh + 1) * heads_half]),
                                        _block_diag(rec_gate_x_w[li, dd, hh * heads_half:(hh + 1) * heads_half])], axis=1)
                       for hh in range(2)]) for dd in range(2)]).astype(BF16)
        bgate = jnp.stack([rec_gate_a_b[li], rec_gate_x_b[li]], axis=1)
        rec_p = (conv_w[li], conv_b[li].reshape(1, d_rec), wgate, bgate, rec_lambda[li])
        pool_w_bd = _block_diag(pool_w[li]).astype(BF16)
        four_w_bd = _block_diag(fourier_w[li]).astype(BF16)
        w_router = jnp.concatenate([router_group_w[li], jnp.zeros((d, SUBLANES - N_GROUPS), F32), router_expert_w[li],
                                    jnp.zeros((d, LANES - SUBLANES - n_exp), F32)], axis=1).astype(BF16)
        b_router = jnp.concatenate([router_group_b[li], jnp.zeros((SUBLANES - N_GROUPS,), F32), router_expert_b[li],
                                    jnp.zeros((LANES - SUBLANES - n_exp,), F32)]).reshape(1, LANES)
        wgu = jnp.concatenate([expert_w_gate[li], expert_w_up[li]], axis=-1).astype(BF16)
        wd = expert_w_down[li].astype(BF16)
        g_ffn = norm_ffn_g[li].reshape(1, d)

        def mixer_tail(hf, hb, z_g, z_p, z_f, stream, mod, grid_rows):
            y_p = _pool(z_p, pool_w_bd, pool_scale[li], grid_rows)
            y_f = _fourier(z_f, four_w_bd)
            return _combine(hf, hb, z_g, y_p, y_f, stream, mod, w_out_l, g_ffn, w_router, b_router)

        zc_x, zc_g, zc_p, zc_f = _inproj(ctx, norm_mix_g[li], mod_ctx, w_in_l, widths)
        hf_c, hb_c = _recurrent(zc_x, *rec_p, zeros_state)
        state = jnp.stack([hf_c[:, -1, :], hb_c[:, 0, :]], axis=1)
        if not last:
            ctx_mid, h2_c, gates_c = mixer_tail(hf_c, hb_c, zc_g, zc_p, zc_f, ctx, mod_ctx, None)
            ctx = _moe(h2_c, gates_c, ctx_mid, mod_ctx, wgu, wd, g_final, False)

        z_x, z_g, z_p, z_f = _inproj(x, norm_mix_g[li], mod_lat, w_in_l, widths)
        hf, hb = _recurrent(z_x, *rec_p, state)
        x_mid, h2, gates = mixer_tail(hf, hb, z_g, z_p, z_f, x, mod_lat, rows)
        x = _moe(h2, gates, x_mid, mod_lat, wgu, wd, g_final, last)
    return x
```

```python
import functools
import math

import jax
import jax.numpy as jnp
from jax import lax
from jax.experimental import pallas as pl
from jax.experimental.pallas import tpu as pltpu

F32 = jnp.float32
BF16 = jnp.bfloat16

GRID_W = 64
N_MOD = 6
REC_HEADS = 8
CONV_W = 4
LRU_C = 8.0
POOL_WINDOWS = (2, 4, 8, 16)
FOUR_HEADS = 4
N_GROUPS = 4
EXPERTS_PER_GROUP = 4
EPS = 1e-6

LANES = 128
SUBLANES = 8
GID_LANE = 16
POOL_PAD = 8
VMEM_LIMIT = 56 * 1024 * 1024


def _cparams(*sem):
    return pltpu.CompilerParams(dimension_semantics=sem, vmem_limit_bytes=VMEM_LIMIT)


def _split_bf16(a):
    hi = a.astype(BF16)
    lo = (a - hi.astype(F32)).astype(BF16)
    return hi, lo


def _dot(a, b):
    return jnp.dot(a, b, preferred_element_type=F32)


def _mod_kernel(s_ref, w_ref, b_ref, o_ref):
    s = s_ref[...]
    s = s * jax.nn.sigmoid(s)
    s_hi, s_lo = _split_bf16(s)
    w_hi, w_lo = _split_bf16(w_ref[...])
    o_ref[...] = _dot(s_hi, w_hi) + _dot(s_hi, w_lo) + _dot(s_lo, w_hi) + b_ref[...]


def _modulation(cvec, mod_w, mod_b):
    depth, d, dm = mod_w.shape
    tn = dm // 4
    return pl.pallas_call(
        _mod_kernel,
        grid=(depth, dm // tn),
        in_specs=[pl.BlockSpec((SUBLANES, d), lambda l, j: (0, 0)),
                  pl.BlockSpec((None, d, tn), lambda l, j: (l, 0, j)),
                  pl.BlockSpec((None, 1, tn), lambda l, j: (l, 0, j))],
        out_specs=pl.BlockSpec((None, SUBLANES, tn), lambda l, j: (l, 0, j)),
        out_shape=jax.ShapeDtypeStruct((depth, SUBLANES, dm), F32),
        compiler_params=_cparams("parallel", "parallel"),
    )(cvec, mod_w, mod_b.reshape(depth, 1, dm))


def _rms_mod(x, g, shift, scale):
    ms = jnp.mean(x * x, axis=-1, keepdims=True)
    y = x * lax.rsqrt(ms + EPS) * g
    return y * (1.0 + scale) + shift


def _inproj_kernel(x_ref, g_ref, mod_ref, w_ref, *o_refs, widths):
    h = _rms_mod(x_ref[...], g_ref[...], mod_ref[0:1, :], mod_ref[1:2, :])
    z = _dot(h.astype(BF16), w_ref[...])
    off = 0
    for o_ref, wd in zip(o_refs, widths):
        o_ref[...] = z[:, off:off + wd]
        off += wd


def _inproj(x, g, mod, w_bf16, widths):
    b, l, d = x.shape
    t = min(512, l)
    return pl.pallas_call(
        functools.partial(_inproj_kernel, widths=widths),
        grid=(b, l // t),
        in_specs=[pl.BlockSpec((None, t, d), lambda bi, i: (bi, i, 0)),
                  pl.BlockSpec((1, d), lambda bi, i: (0, 0)),
                  pl.BlockSpec((None, N_MOD, d), lambda bi, i: (bi, 0, 0)),
                  pl.BlockSpec(w_bf16.shape, lambda bi, i: (0, 0))],
        out_specs=[pl.BlockSpec((None, t, wd), lambda bi, i: (bi, i, 0)) for wd in widths],
        out_shape=[jax.ShapeDtypeStruct((b, l, wd), F32) for wd in widths],
        compiler_params=_cparams("parallel", "parallel"),
    )(x, g.reshape(1, d), mod, w_bf16)


def _rec_kernel(cf_ref, pf_ref, nf_ref, cb_ref, pb_ref, nb_ref, cw_ref, cbias_ref, wg_ref, bg_ref, lam_ref, h0_ref,
                hf_ref, hb_ref, a_scr, carry_scr, *, tc, n_chunks):
    i = pl.program_id(1)
    c = cf_ref.shape[-1]
    half = c // 2
    n_grp = tc // SUBLANES

    @pl.when(i == 0)
    def _():
        carry_scr[0] = jnp.broadcast_to(h0_ref[0:1, :], (SUBLANES, c))
        carry_scr[1] = jnp.broadcast_to(h0_ref[1:2, :], (SUBLANES, c))

    row = lax.broadcasted_iota(jnp.int32, (tc, half), 0) & (SUBLANES - 1)

    def direction(d, cur_ref, prev_ref, next_ref, is_first, is_last, out_ref):
        prev = jnp.where(is_first, 0.0, prev_ref[...])
        nxt = jnp.where(is_last, 0.0, next_ref[...])
        zz = jnp.concatenate([prev, cur_ref[...], nxt], axis=0)
        u = cbias_ref[...]
        for k in range(CONV_W):
            o = SUBLANES - CONV_W // 2 + k
            u = u + cw_ref[k:k + 1, :] * zz[o:o + tc, :]
        log_decay = -LRU_C * jax.nn.softplus(-lam_ref[d:d + 1, :])
        for hh in range(2):
            cols = slice(hh * half, (hh + 1) * half)
            uh = u[:, cols]
            pre = _dot(uh.astype(BF16), wg_ref[d, hh])
            r = jax.nn.sigmoid(pre[:, :half] + bg_ref[d, 0:1, cols])
            ig = jax.nn.sigmoid(pre[:, half:] + bg_ref[d, 1:2, cols])
            log_a = log_decay[:, cols] * r
            a = jnp.exp(log_a)
            th = jnp.tanh(log_a)
            bt = jnp.sqrt(-2.0 * th / (1.0 - th)) * (ig * uh)
            for s in (1, 2, 4):
                if d == 0:
                    a_sh, b_sh, keep = pltpu.roll(a, s, 0), pltpu.roll(bt, s, 0), row >= s
                else:
                    a_sh, b_sh, keep = pltpu.roll(a, tc - s, 0), pltpu.roll(bt, tc - s, 0), row < SUBLANES - s
                bt = a * jnp.where(keep, b_sh, 0.0) + bt
                a = a * jnp.where(keep, a_sh, 1.0)
            a_scr[d, :, cols] = a
            out_ref[:, cols] = bt

    direction(0, cf_ref, pf_ref, nf_ref, i == 0, i == n_chunks - 1, hf_ref)
    direction(1, cb_ref, pb_ref, nb_ref, i == n_chunks - 1, i == 0, hb_ref)

    def body(gi, carry):
        cf, cb = carry
        rf = pl.multiple_of(gi * SUBLANES, SUBLANES)
        rb = pl.multiple_of((n_grp - 1 - gi) * SUBLANES, SUBLANES)
        hf = hf_ref[pl.ds(rf, SUBLANES), :] + a_scr[0, pl.ds(rf, SUBLANES), :] * cf
        hb = hb_ref[pl.ds(rb, SUBLANES), :] + a_scr[1, pl.ds(rb, SUBLANES), :] * cb
        hf_ref[pl.ds(rf, SUBLANES), :] = hf
        hb_ref[pl.ds(rb, SUBLANES), :] = hb
        return (jnp.broadcast_to(hf[SUBLANES - 1:SUBLANES, :], (SUBLANES, c)),
                jnp.broadcast_to(hb[0:1, :], (SUBLANES, c)))

    cf, cb = lax.fori_loop(0, n_grp, body, (carry_scr[0], carry_scr[1]))
    carry_scr[0] = cf
    carry_scr[1] = cb


def _recurrent(z_x, conv_w, conv_b, wgate, bgate, lam, h0):
    b, l, c = z_x.shape
    tc = min(1024, l)
    nc = l // tc
    per = tc // SUBLANES
    nblk8 = l // SUBLANES
    cur_f = pl.BlockSpec((None, tc, c), lambda bi, i: (bi, i, 0))
    prev_f = pl.BlockSpec((None, SUBLANES, c), lambda bi, i: (bi, jnp.maximum(i * per - 1, 0), 0))
    next_f = pl.BlockSpec((None, SUBLANES, c), lambda bi, i: (bi, jnp.minimum((i + 1) * per, nblk8 - 1), 0))
    cur_b = pl.BlockSpec((None, tc, c), lambda bi, i: (bi, nc - 1 - i, 0))
    prev_b = pl.BlockSpec((None, SUBLANES, c), lambda bi, i: (bi, jnp.maximum((nc - 1 - i) * per - 1, 0), 0))
    next_b = pl.BlockSpec((None, SUBLANES, c), lambda bi, i: (bi, jnp.minimum((nc - i) * per, nblk8 - 1), 0))
    full = lambda a: pl.BlockSpec(a.shape, lambda bi, i: (0,) * a.ndim)
    return pl.pallas_call(
        functools.partial(_rec_kernel, tc=tc, n_chunks=nc),
        grid=(b, nc),
        in_specs=[cur_f, prev_f, next_f, cur_b, prev_b, next_b,
                  full(conv_w), full(conv_b), full(wgate), full(bgate), full(lam),
                  pl.BlockSpec((None, 2, c), lambda bi, i: (bi, 0, 0))],
        out_specs=[pl.BlockSpec((None, tc, c), lambda bi, i: (bi, i, 0)),
                   pl.BlockSpec((None, tc, c), lambda bi, i: (bi, nc - 1 - i, 0))],
        out_shape=[jax.ShapeDtypeStruct((b, l, c), F32)] * 2,
        scratch_shapes=[pltpu.VMEM((2, tc, c), F32), pltpu.VMEM((2, SUBLANES, c), F32)],
        compiler_params=_cparams("parallel", "arbitrary"),
    )(z_x, z_x, z_x, z_x, z_x, z_x, conv_w, conv_b, wgate, bgate, lam, h0)


def _dft_tables(n):
    j = lax.broadcasted_iota(jnp.int32, (n, n), 0)
    k = lax.broadcasted_iota(jnp.int32, (n, n), 1)
    ang = ((j * k) % n).astype(F32) * (2.0 * math.pi / n)
    return jnp.cos(ang), jnp.sin(ang)


def _block_diag(w):
    return jax.scipy.linalg.block_diag(*[w[h] for h in range(w.shape[0])])


def _channel_stage(gr, gi, c_ref, s_ref, w_ref, scale):
    f = (_dot(gr.astype(BF16), c_ref[...]) + _dot(gi.astype(BF16), s_ref[...])) * scale
    return _dot(f.astype(BF16), w_ref[...])


def _four1_kernel(x_ref, f1_ref, twc_ref, tws_ref, ar_ref, ai_ref, *, n1, nch):
    a = _dot(f1_ref[...], x_ref[...].astype(BF16))
    ar, ai = a[:n1], a[n1:]
    for jj in range(SUBLANES):
        cols = slice(jj * nch, (jj + 1) * nch)
        cc = twc_ref[:, jj:jj + 1]
        ss = tws_ref[:, jj:jj + 1]
        ar_ref[:, cols] = (ar[:, cols] * cc + ai[:, cols] * ss).astype(BF16)
        ai_ref[:, cols] = (ai[:, cols] * cc - ar[:, cols] * ss).astype(BF16)


def _four2_kernel(ar_ref, ai_ref, f2_ref, c_ref, s_ref, w_ref, o_ref, *, n2, nch, scale):
    grs, gis = [], []
    for kk in range(SUBLANES):
        slab = jnp.concatenate([ar_ref[kk], ai_ref[kk]], axis=0)
        g = _dot(f2_ref[...], slab)
        grs.append(g[:n2])
        gis.append(g[n2:])
    y = _channel_stage(jnp.concatenate(grs, axis=0), jnp.concatenate(gis, axis=0), c_ref, s_ref, w_ref, scale)
    for kk in range(SUBLANES):
        o_ref[:, kk * nch:(kk + 1) * nch] = y[kk * n2:(kk + 1) * n2]


def _four_small_kernel(x_ref, f_ref, c_ref, s_ref, w_ref, o_ref, *, n, scale):
    g = _dot(f_ref[...], x_ref[...].astype(BF16))
    o_ref[...] = _channel_stage(g[:n], g[n:], c_ref, s_ref, w_ref, scale)


def _fourier(z_f, four_w_bd):
    b, l, nch = z_f.shape
    hd = nch // FOUR_HEADS
    scale = 1.0 / math.sqrt(l * hd)
    c_h, s_h = _dft_tables(hd)
    eye = jnp.eye(FOUR_HEADS, dtype=F32)
    c_bd = jnp.kron(eye, c_h).astype(BF16)
    s_bd = jnp.kron(eye, s_h).astype(BF16)
    full = lambda a, nd: pl.BlockSpec(a.shape, lambda *_: (0,) * a.ndim)
    if l <= 256:
        cn, sn = _dft_tables(l)
        f = jnp.concatenate([cn, -sn], axis=0).astype(BF16)
        return pl.pallas_call(
            functools.partial(_four_small_kernel, n=l, scale=scale),
            grid=(b,),
            in_specs=[pl.BlockSpec((None, l, nch), lambda bi: (bi, 0, 0)),
                      full(f, 1), full(c_bd, 1), full(s_bd, 1), full(four_w_bd, 1)],
            out_specs=pl.BlockSpec((None, l, nch), lambda bi: (bi, 0, 0)),
            out_shape=jax.ShapeDtypeStruct((b, l, nch), F32),
            compiler_params=_cparams("parallel"),
        )(z_f, f, c_bd, s_bd, four_w_bd)

    n2 = LANES
    n1 = l // n2
    nj = n2 // SUBLANES
    c1, s1 = _dft_tables(n1)
    f1 = jnp.concatenate([c1, -s1], axis=0).astype(BF16)
    c2, s2 = _dft_tables(n2)
    f2 = jnp.concatenate([jnp.concatenate([c2, s2], axis=1),
                          jnp.concatenate([-s2, c2], axis=1)], axis=0).astype(BF16)
    k1 = lax.broadcasted_iota(jnp.int32, (n1, n2), 0)
    t2 = lax.broadcasted_iota(jnp.int32, (n1, n2), 1)
    ang = (k1 * t2).astype(F32) * (2.0 * math.pi / l)
    twc = jnp.cos(ang).reshape(n1, nj, SUBLANES).transpose(1, 0, 2)
    tws = jnp.sin(ang).reshape(n1, nj, SUBLANES).transpose(1, 0, 2)
    wide = SUBLANES * nch
    ar, ai = pl.pallas_call(
        functools.partial(_four1_kernel, n1=n1, nch=nch),
        grid=(b, nj),
        in_specs=[pl.BlockSpec((None, n1, wide), lambda bi, j: (bi, 0, j)),
                  full(f1, 2),
                  pl.BlockSpec((None, n1, SUBLANES), lambda bi, j: (j, 0, 0)),
                  pl.BlockSpec((None, n1, SUBLANES), lambda bi, j: (j, 0, 0))],
        out_specs=[pl.BlockSpec((None, n1, wide), lambda bi, j: (bi, 0, j))] * 2,
        out_shape=[jax.ShapeDtypeStruct((b, n1, n2 * nch), BF16)] * 2,
        compiler_params=_cparams("parallel", "parallel"),
    )(z_f.reshape(b, n1, n2 * nch), f1, twc, tws)
    y = pl.pallas_call(
        functools.partial(_four2_kernel, n2=n2, nch=nch, scale=scale),
        grid=(b, n1 // SUBLANES),
        in_specs=[pl.BlockSpec((None, SUBLANES, n2, nch), lambda bi, j: (bi, j, 0, 0)),
                  pl.BlockSpec((None, SUBLANES, n2, nch), lambda bi, j: (bi, j, 0, 0)),
                  full(f2, 2), full(c_bd, 2), full(s_bd, 2), full(four_w_bd, 2)],
        out_specs=pl.BlockSpec((None, n2, wide), lambda bi, j: (bi, 0, j)),
        out_shape=jax.ShapeDtypeStruct((b, n2, n1 * nch), F32),
        compiler_params=_cparams("parallel", "parallel"),
    )(ar.reshape(b, n1, n2, nch), ai.reshape(b, n1, n2, nch), f2, c_bd, s_bd, four_w_bd)
    return y.reshape(b, l, nch)


def _window_counts(idx, n, w):
    return (jnp.minimum(idx + (w - w // 2), n) - jnp.maximum(idx - w // 2, 0)).astype(F32)


def _pool_kernel(cur_ref, prev_ref, next_ref, pw_ref, ps_ref, o_ref, *, r, w, rows_total, n_tiles):
    i = pl.program_id(1)
    two_d = rows_total > 1
    halo = SUBLANES if two_d else 0
    nr = r + 2 * halo
    stride = w + POOL_PAD
    flat = nr * stride
    lane = lax.broadcasted_iota(jnp.int32, (1, 1, LANES), 2)
    low = lane < (LANES // 2)
    col = lax.broadcasted_iota(jnp.int32, (1, w, 1), 1)
    grow = lax.broadcasted_iota(jnp.int32, (r, 1, 1), 0) + i * r
    outs = []
    for hh in range(2):
        cols = slice(hh * LANES, (hh + 1) * LANES)
        w_lo, w_hi = POOL_WINDOWS[2 * hh], POOL_WINDOWS[2 * hh + 1]
        cur = cur_ref[:, cols]
        if two_d:
            prev = jnp.where(i == 0, 0.0, prev_ref[:, cols])
            nxt = jnp.where(i == n_tiles - 1, 0.0, next_ref[:, cols])
            ext = jnp.concatenate([prev, cur, nxt], axis=0)
        else:
            ext = cur
        ext = ext.reshape(nr, w, LANES)
        x = jnp.concatenate([jnp.zeros((nr, POOL_PAD, LANES), F32), ext], axis=1).reshape(flat, LANES)
        p = x + pltpu.roll(x, 1, 0)
        sums = {2: p}
        for ww, s in ((4, 1), (8, 2), (16, 4)):
            if ww > w_hi:
                break
            p = pltpu.roll(p, s, 0) + pltpu.roll(p, flat - s, 0)
            sums[ww] = p
        y = jnp.where(low[0], sums[w_lo], sums[w_hi]).reshape(nr, stride, LANES)[:, POOL_PAD:, :]
        if two_d:
            q2 = y[0:nr - 1] + y[1:nr]
            rsum = {2: q2[7:7 + r]}
            q4 = q2[0:nr - 3] + q2[2:nr - 1]
            rsum[4] = q4[6:6 + r]
            if w_hi > 4:
                q8 = q4[0:nr - 7] + q4[4:nr - 3]
                rsum[8] = q8[4:4 + r]
                q16 = q8[0:nr - 15] + q8[8:nr - 7]
                rsum[16] = q16[0:r]
            tot = jnp.where(low, rsum[w_lo], rsum[w_hi])
            cnt_r = jnp.where(low, _window_counts(grow, rows_total, w_lo), _window_counts(grow, rows_total, w_hi))
            tot = tot / cnt_r
        else:
            tot = y
        cnt_c = jnp.where(low, _window_counts(col, w, w_lo), _window_counts(col, w, w_hi))
        pooled = (tot / cnt_c).reshape(r * w, LANES)
        outs.append(pooled - cur)
    d = jnp.concatenate(outs, axis=-1)
    o_ref[...] = _dot(d.astype(BF16), pw_ref[...]) * ps_ref[...]


def _pool(z_p, pool_w_bd, pool_scale, rows):
    b, l, nch = z_p.shape
    if rows is None:
        rows_total, w, r = 1, l, 1
    else:
        rows_total, w, r = rows, l // rows, min(16, rows)
    t = r * w
    n_tiles = l // t
    hb = SUBLANES * w if rows is not None else t
    per = t // hb
    nhb = l // hb
    return pl.pallas_call(
        functools.partial(_pool_kernel, r=r, w=w, rows_total=rows_total, n_tiles=n_tiles),
        grid=(b, n_tiles),
        in_specs=[pl.BlockSpec((None, t, nch), lambda bi, i: (bi, i, 0)),
                  pl.BlockSpec((None, hb, nch), lambda bi, i: (bi, jnp.maximum(i * per - 1, 0), 0)),
                  pl.BlockSpec((None, hb, nch), lambda bi, i: (bi, jnp.minimum((i + 1) * per, nhb - 1), 0)),
                  pl.BlockSpec(pool_w_bd.shape, lambda bi, i: (0, 0)),
                  pl.BlockSpec((1, nch), lambda bi, i: (0, 0))],
        out_specs=pl.BlockSpec((None, t, nch), lambda bi, i: (bi, i, 0)),
        out_shape=jax.ShapeDtypeStruct((b, l, nch), F32),
        compiler_params=_cparams("parallel", "parallel"),
    )(z_p, z_p, z_p, pool_w_bd, pool_scale.reshape(1, nch))


def _first_argmax4(v):
    m = jnp.maximum(jnp.maximum(v[0], v[1]), jnp.maximum(v[2], v[3]))
    idx = jnp.where(v[0] >= m, 0, jnp.where(v[1] >= m, 1, jnp.where(v[2] >= m, 2, 3)))
    return m, idx


def _combine_kernel(hf_ref, hb_ref, zg_ref, yp_ref, yf_ref, x_ref, mod_ref, wo_ref, g_ref, wr_ref, br_ref,
                    xo_ref, h2g_ref, route_ref, gt_scr):
    y_a = (hf_ref[...] + hb_ref[...]) * jax.nn.gelu(zg_ref[...], approximate=True)
    ycat = jnp.concatenate([y_a.astype(BF16), yp_ref[...].astype(BF16), yf_ref[...].astype(BF16)], axis=-1)
    x_new = x_ref[...] + mod_ref[2:3, :] * _dot(ycat, wo_ref[...])
    xo_ref[...] = x_new
    h2 = _rms_mod(x_new, g_ref[...], mod_ref[3:4, :], mod_ref[4:5, :])
    lt = (_dot(h2.astype(BF16), wr_ref[...]) + br_ref[...]).T
    lg = [lt[g:g + 1, :] for g in range(N_GROUPS)]
    m, gidx = _first_argmax4(lg)
    den = sum(jnp.exp(v - m) for v in lg)
    p_group = 1.0 / den
    sel = []
    for e in range(EXPERTS_PER_GROUP):
        rows = [lt[SUBLANES + EXPERTS_PER_GROUP * g + e:SUBLANES + EXPERTS_PER_GROUP * g + e + 1, :]
                for g in range(N_GROUPS)]
        sel.append(jnp.where(gidx == 0, rows[0], jnp.where(gidx == 1, rows[1], jnp.where(gidx == 2, rows[2], rows[3]))))
    v1, e1 = _first_argmax4(sel)
    rest = [jnp.where(e1 == e, -jnp.inf, sel[e]) for e in range(EXPERTS_PER_GROUP)]
    v2, e2 = _first_argmax4(rest)
    t2 = jnp.exp(v2 - v1)
    w1 = p_group / (1.0 + t2)
    w2 = p_group * t2 / (1.0 + t2)
    gt_scr[...] = jnp.zeros_like(gt_scr)
    for g in range(N_GROUPS):
        for e in range(EXPERTS_PER_GROUP):
            val = jnp.where(gidx == g, jnp.where(e1 == e, w1, jnp.where(e2 == e, w2, 0.0)), 0.0)
            gt_scr[EXPERTS_PER_GROUP * g + e:EXPERTS_PER_GROUP * g + e + 1, :] = val
    gt_scr[GID_LANE:GID_LANE + 1, :] = gidx.astype(F32)
    route = gt_scr[...].T
    route_ref[...] = route
    d = h2.shape[-1]
    h2g_ref[:, :d] = h2
    h2g_ref[:, d:] = route


def _combine(hf, hb, z_g, y_p, y_f, x, mod, w_out, g_ffn, w_router, b_router):
    b, l, d = x.shape
    t = min(512, l)
    tok = lambda c: pl.BlockSpec((None, t, c), lambda bi, i: (bi, i, 0))
    full = lambda a: pl.BlockSpec(a.shape, lambda bi, i: (0,) * a.ndim)
    return pl.pallas_call(
        _combine_kernel,
        grid=(b, l // t),
        in_specs=[tok(hf.shape[-1]), tok(hb.shape[-1]), tok(z_g.shape[-1]), tok(y_p.shape[-1]), tok(y_f.shape[-1]),
                  tok(d), pl.BlockSpec((None, N_MOD, d), lambda bi, i: (bi, 0, 0)),
                  full(w_out), full(g_ffn), full(w_router), full(b_router)],
        out_specs=[tok(d), tok(d + LANES), tok(LANES)],
        out_shape=[jax.ShapeDtypeStruct((b, l, d), F32), jax.ShapeDtypeStruct((b, l, d + LANES), F32),
                   jax.ShapeDtypeStruct((b, l, LANES), F32)],
        scratch_shapes=[pltpu.VMEM((LANES, t), F32)],
        compiler_params=_cparams("parallel", "parallel"),
    )(hf, hb, z_g, y_p, y_f, x, mod, w_out, g_ffn, w_router, b_router)


def _moe_kernel(h_ref, gates_ref, x_ref, mod_ref, wgu_ref, wd_ref, gfin_ref, o_ref, hb_scr, acc_scr, *, n_exp, final):
    e = pl.program_id(2)

    @pl.when(e == 0)
    def _():
        hb_scr[...] = h_ref[:, :hb_scr.shape[-1]].astype(BF16)
        acc_scr[...] = jnp.zeros_like(acc_scr)

    de = wd_ref.shape[0]
    gu = _dot(hb_scr[...], wgu_ref[...])
    hid = jax.nn.silu(gu[:, :de]) * gu[:, de:]
    lane = lax.broadcasted_iota(jnp.int32, gates_ref.shape, 1)
    gate = jnp.sum(jnp.where(lane == e, gates_ref[...], 0.0), axis=-1, keepdims=True)
    acc_scr[...] += gate * _dot(hid.astype(BF16), wd_ref[...])

    @pl.when(e == n_exp - 1)
    def _():
        out = x_ref[...] + mod_ref[5:6, :] * acc_scr[...]
        if final:
            ms = jnp.mean(out * out, axis=-1, keepdims=True)
            out = out * lax.rsqrt(ms + EPS) * gfin_ref[...]
        o_ref[...] = out


def _moe(h2g, gates, x, mod, wgu, wd, g_final, final):
    b, l, d = x.shape
    n_exp = wgu.shape[0]
    t = min(512, l)
    tok = lambda c: pl.BlockSpec((None, t, c), lambda bi, i, e: (bi, i, 0))
    return pl.pallas_call(
        functools.partial(_moe_kernel, n_exp=n_exp, final=final),
        grid=(b, l // t, n_exp),
        in_specs=[tok(d + LANES), tok(LANES), tok(d),
                  pl.BlockSpec((None, N_MOD, d), lambda bi, i, e: (bi, 0, 0)),
                  pl.BlockSpec((None,) + wgu.shape[1:], lambda bi, i, e: (e, 0, 0)),
                  pl.BlockSpec((None,) + wd.shape[1:], lambda bi, i, e: (e, 0, 0)),
                  pl.BlockSpec((1, d), lambda bi, i, e: (0, 0))],
        out_specs=tok(d),
        out_shape=jax.ShapeDtypeStruct((b, l, d), F32),
        scratch_shapes=[pltpu.VMEM((t, d), BF16), pltpu.VMEM((t, d), F32)],
        compiler_params=_cparams("parallel", "parallel", "arbitrary"),
    )(h2g, gates, x, mod, wgu, wd, g_final)


MOE_TILE = 512


def _sort_plan(gid, n_groups, t):
    n = gid.shape[0]
    onehot = (gid[:, None] == jnp.arange(n_groups, dtype=jnp.int32)[None, :]).astype(jnp.int32)
    csum = jnp.cumsum(onehot, axis=0)
    rank = jnp.sum((csum - onehot) * onehot, axis=1)
    tiles = (csum[-1] + t - 1) // t
    tile_end = jnp.cumsum(tiles)
    slot = jnp.sum(onehot * (tile_end - tiles)[None, :], axis=1) * t + rank
    n_tiles = n // t + n_groups
    tile_idx = jnp.arange(n_tiles, dtype=jnp.int32)
    tile_group = jnp.minimum(jnp.sum((tile_idx[:, None] >= tile_end[None, :]).astype(jnp.int32), axis=1), n_groups - 1)
    return slot.astype(jnp.int32), tile_group.astype(jnp.int32), tile_end[-1:].astype(jnp.int32), n_tiles


def _row_copy_all_wait(src, dst, sem):
    pltpu.make_async_copy(src, dst, sem).wait()


def _dispatch_kernel(slot_ref, h_ref, sorted_in_ref, sorted_ref, sem, *, t):
    del sorted_in_ref
    base = pl.program_id(0) * t

    def body(r, carry):
        pltpu.make_async_copy(h_ref.at[pl.ds(r, 1)], sorted_ref.at[pl.ds(slot_ref[base + r], 1)], sem).start()
        return carry

    lax.fori_loop(0, t, body, 0, unroll=8)
    _row_copy_all_wait(h_ref, sorted_ref.at[pl.ds(0, t)], sem)


def _dispatch(slot, h2g, n_rows, t):
    n, width = h2g.shape
    return pl.pallas_call(
        functools.partial(_dispatch_kernel, t=t),
        grid_spec=pltpu.PrefetchScalarGridSpec(
            num_scalar_prefetch=1, grid=(n // t,),
            in_specs=[pl.BlockSpec((t, width), lambda i, s: (i, 0)), pl.BlockSpec(memory_space=pl.ANY)],
            out_specs=pl.BlockSpec(memory_space=pl.ANY),
            scratch_shapes=[pltpu.SemaphoreType.DMA]),
        out_shape=jax.ShapeDtypeStruct((n_rows, width), F32),
        input_output_aliases={2: 0},
        compiler_params=_cparams("arbitrary"),
    )(slot, h2g, jnp.zeros((n_rows, width), F32))


def _moe_sorted_kernel(tg_ref, nu_ref, rows_ref, wgu_ref, wd_ref, o_ref, *, d, epg):
    i = pl.program_id(0)

    @pl.when(i < nu_ref[0])
    def _():
        de = wd_ref.shape[1]
        hb = rows_ref[:, :d].astype(BF16)
        gates = rows_ref[:, d:]
        lane = lax.broadcasted_iota(jnp.int32, gates.shape, 1)
        first = tg_ref[i] * epg
        acc = None
        for e in range(epg):
            gu = _dot(hb, wgu_ref[e])
            hid = jax.nn.silu(gu[:, :de]) * gu[:, de:]
            gate = jnp.sum(jnp.where(lane == first + e, gates, 0.0), axis=-1, keepdims=True)
            y = gate * _dot(hid.astype(BF16), wd_ref[e])
            acc = y if acc is None else acc + y
        o_ref[...] = acc

    @pl.when(i >= nu_ref[0])
    def _():
        o_ref[...] = jnp.zeros_like(o_ref)


def _moe_sorted(tile_group, n_used, rows, wgu, wd, n_tiles, t, d):
    epg = wgu.shape[1]
    return pl.pallas_call(
        functools.partial(_moe_sorted_kernel, d=d, epg=epg),
        grid_spec=pltpu.PrefetchScalarGridSpec(
            num_scalar_prefetch=2, grid=(n_tiles,),
            in_specs=[pl.BlockSpec((t, rows.shape[1]), lambda i, tg, nu: (i, 0)),
                      pl.BlockSpec((None,) + wgu.shape[1:], lambda i, tg, nu: (tg[i], 0, 0, 0)),
                      pl.BlockSpec((None,) + wd.shape[1:], lambda i, tg, nu: (tg[i], 0, 0, 0))],
            out_specs=pl.BlockSpec((t, d), lambda i, tg, nu: (i, 0))),
        out_shape=jax.ShapeDtypeStruct((n_tiles * t, d), F32),
        compiler_params=_cparams("arbitrary"),
    )(tile_group, n_used, rows, wgu, wd)


def _return_kernel(slot_ref, x_ref, mod_ref, gfin_ref, ys_ref, o_ref, buf, sem, *, t, nt, final):
    base = (pl.program_id(0) * nt + pl.program_id(1)) * t

    def body(r, carry):
        pltpu.make_async_copy(ys_ref.at[pl.ds(slot_ref[base + r], 1)], buf.at[pl.ds(r, 1)], sem).start()
        return carry

    lax.fori_loop(0, t, body, 0, unroll=8)
    _row_copy_all_wait(ys_ref.at[pl.ds(0, t)], buf, sem)
    out = x_ref[...] + mod_ref[5:6, :] * buf[...]
    if final:
        ms = jnp.mean(out * out, axis=-1, keepdims=True)
        out = out * lax.rsqrt(ms + EPS) * gfin_ref[...]
    o_ref[...] = out


def _moe_return(slot, x, mod, g_final, y_sorted, t, final):
    b, l, d = x.shape
    nt = l // t
    return pl.pallas_call(
        functools.partial(_return_kernel, t=t, nt=nt, final=final),
        grid_spec=pltpu.PrefetchScalarGridSpec(
            num_scalar_prefetch=1, grid=(b, nt),
            in_specs=[pl.BlockSpec((None, t, d), lambda bi, i, s: (bi, i, 0)),
                      pl.BlockSpec((None, N_MOD, d), lambda bi, i, s: (bi, 0, 0)),
                      pl.BlockSpec((1, d), lambda bi, i, s: (0, 0)),
                      pl.BlockSpec(memory_space=pl.ANY)],
            out_specs=pl.BlockSpec((None, t, d), lambda bi, i, s: (bi, i, 0)),
            scratch_shapes=[pltpu.VMEM((t, d), F32), pltpu.SemaphoreType.DMA]),
        out_shape=jax.ShapeDtypeStruct((b, l, d), F32),
        compiler_params=_cparams("arbitrary", "arbitrary"),
    )(slot, x, mod, g_final, y_sorted)


def kernel(x, c, ctx, c_ctx, mod_w, mod_b, norm_mix_g, norm_ffn_g, w_in, conv_w, conv_b, rec_gate_a_w, rec_gate_a_b,
           rec_gate_x_w, rec_gate_x_b, rec_lambda, pool_w, pool_scale, fourier_w, w_out, router_group_w,
           router_group_b, router_expert_w, router_expert_b, expert_w_gate, expert_w_up, expert_w_down, final_norm_g):
    b, l, d = x.shape
    depth = mod_w.shape[0]
    d_rec = conv_w.shape[-1]
    d_pool = pool_scale.shape[-1]
    d_four = fourier_w.shape[1] * fourier_w.shape[2]
    widths = (d_rec, d_rec, d_pool, d_four)
    rows = l // GRID_W
    heads_half = REC_HEADS // 2
    n_exp = expert_w_gate.shape[1]

    cvec = jnp.concatenate([c, c_ctx[None, :], jnp.zeros((SUBLANES - b - 1, d), F32)], axis=0)
    mod_all = _modulation(cvec, mod_w, mod_b).reshape(depth, SUBLANES, N_MOD, d)
    g_final = final_norm_g.reshape(1, d)
    zeros_state = jnp.zeros((b, 2, d_rec), F32)

    for li in range(depth):
        last = li == depth - 1
        mod_lat = mod_all[li, :b]
        mod_ctx = jnp.broadcast_to(mod_all[li, b][None], (b, N_MOD, d))
        w_in_l = w_in[li].astype(BF16)
        w_out_l = w_out[li].astype(BF16)
        wgate = jnp.stack([
            jnp.stack([jnp.concatenate([_block_diag(rec_gate_a_w[li, dd, hh * heads_half:(hh + 1) * heads_half]),
                                        _block_diag(rec_gate_x_w[li, dd, hh * heads_half:(hh + 1) * heads_half])], axis=1)
                       for hh in range(2)]) for dd in range(2)]).astype(BF16)
        bgate = jnp.stack([rec_gate_a_b[li], rec_gate_x_b[li]], axis=1)
        rec_p = (conv_w[li], conv_b[li].reshape(1, d_rec), wgate, bgate, rec_lambda[li])
        pool_w_bd = _block_diag(pool_w[li]).astype(BF16)
        four_w_bd = _block_diag(fourier_w[li]).astype(BF16)
        w_router = jnp.concatenate([router_group_w[li], jnp.zeros((d, SUBLANES - N_GROUPS), F32), router_expert_w[li],
                                    jnp.zeros((d, LANES - SUBLANES - n_exp), F32)], axis=1).astype(BF16)
        b_router = jnp.concatenate([router_group_b[li], jnp.zeros((SUBLANES - N_GROUPS,), F32), router_expert_b[li],
                                    jnp.zeros((LANES - SUBLANES - n_exp,), F32)]).reshape(1, LANES)
        wgu = jnp.concatenate([expert_w_gate[li], expert_w_up[li]], axis=-1).astype(BF16)
        wd = expert_w_down[li].astype(BF16)
        wgu_grp = wgu.reshape((N_GROUPS, EXPERTS_PER_GROUP) + wgu.shape[1:])
        wd_grp = wd.reshape((N_GROUPS, EXPERTS_PER_GROUP) + wd.shape[1:])
        g_ffn = norm_ffn_g[li].reshape(1, d)

        def mixer_tail(hf, hb, z_g, z_p, z_f, stream, mod, grid_rows):
            y_p = _pool(z_p, pool_w_bd, pool_scale[li], grid_rows)
            y_f = _fourier(z_f, four_w_bd)
            return _combine(hf, hb, z_g, y_p, y_f, stream, mod, w_out_l, g_ffn, w_router, b_router)

        zc_x, zc_g, zc_p, zc_f = _inproj(ctx, norm_mix_g[li], mod_ctx, w_in_l, widths)
        hf_c, hb_c = _recurrent(zc_x, *rec_p, zeros_state)
        state = jnp.stack([hf_c[:, -1, :], hb_c[:, 0, :]], axis=1)
        if not last:
            ctx_mid, h2_c, gates_c = mixer_tail(hf_c, hb_c, zc_g, zc_p, zc_f, ctx, mod_ctx, None)
            ctx = _moe(h2_c, gates_c, ctx_mid, mod_ctx, wgu, wd, g_final, False)

        z_x, z_g, z_p, z_f = _inproj(x, norm_mix_g[li], mod_lat, w_in_l, widths)
        hf, hb = _recurrent(z_x, *rec_p, state)
        x_mid, h2g, route = mixer_tail(hf, hb, z_g, z_p, z_f, x, mod_lat, rows)
        t_moe = min(MOE_TILE, l)
        gid = route[..., GID_LANE].astype(jnp.int32).reshape(b * l)
        slot, tile_group, n_used, n_tiles = _sort_plan(gid, N_GROUPS, t_moe)
        rows_sorted = _dispatch(slot, h2g.reshape(b * l, d + LANES), n_tiles * t_moe, t_moe)
        y_sorted = _moe_sorted(tile_group, n_used, rows_sorted, wgu_grp, wd_grp, n_tiles, t_moe, d)
        x = _moe_return(slot, x_mid, mod_lat, g_final, y_sorted, t_moe, last)
    return x
```

```python
import functools
import math

import jax
import jax.numpy as jnp
from jax import lax
from jax.experimental import pallas as pl
from jax.experimental.pallas import tpu as pltpu

F32 = jnp.float32
BF16 = jnp.bfloat16

GRID_W = 64
N_MOD = 6
REC_HEADS = 8
CONV_W = 4
LRU_C = 8.0
POOL_WINDOWS = (2, 4, 8, 16)
FOUR_HEADS = 4
N_GROUPS = 4
EXPERTS_PER_GROUP = 4
EPS = 1e-6

LANES = 128
SUBLANES = 8
GID_LANE = 16
POOL_PAD = 8
VMEM_LIMIT = 56 * 1024 * 1024


def _cparams(*sem):
    return pltpu.CompilerParams(dimension_semantics=sem, vmem_limit_bytes=VMEM_LIMIT)


def _split_bf16(a):
    hi = a.astype(BF16)
    lo = (a - hi.astype(F32)).astype(BF16)
    return hi, lo


def _dot(a, b):
    return jnp.dot(a, b, preferred_element_type=F32)


def _mod_kernel(s_ref, w_ref, b_ref, o_ref):
    s = s_ref[...]
    s = s * jax.nn.sigmoid(s)
    s_hi, s_lo = _split_bf16(s)
    w_hi, w_lo = _split_bf16(w_ref[...])
    o_ref[...] = _dot(s_hi, w_hi) + _dot(s_hi, w_lo) + _dot(s_lo, w_hi) + b_ref[...]


def _modulation(cvec, mod_w, mod_b):
    depth, d, dm = mod_w.shape
    tn = dm // 4
    return pl.pallas_call(
        _mod_kernel,
        grid=(depth, dm // tn),
        in_specs=[pl.BlockSpec((SUBLANES, d), lambda l, j: (0, 0)),
                  pl.BlockSpec((None, d, tn), lambda l, j: (l, 0, j)),
                  pl.BlockSpec((None, 1, tn), lambda l, j: (l, 0, j))],
        out_specs=pl.BlockSpec((None, SUBLANES, tn), lambda l, j: (l, 0, j)),
        out_shape=jax.ShapeDtypeStruct((depth, SUBLANES, dm), F32),
        compiler_params=_cparams("parallel", "parallel"),
    )(cvec, mod_w, mod_b.reshape(depth, 1, dm))


def _rms_mod(x, g, shift, scale):
    ms = jnp.mean(x * x, axis=-1, keepdims=True)
    y = x * lax.rsqrt(ms + EPS) * g
    return y * (1.0 + scale) + shift


def _inproj_kernel(*refs, widths, with_residual):
    if with_residual:
        x_ref, y_ref, pmod_ref, g_ref, mod_ref, w_ref, xo_ref, *o_refs = refs
        x = x_ref[...] + pmod_ref[5:6, :] * y_ref[...]
        xo_ref[...] = x
    else:
        x_ref, g_ref, mod_ref, w_ref, *o_refs = refs
        x = x_ref[...]
    h = _rms_mod(x, g_ref[...], mod_ref[0:1, :], mod_ref[1:2, :])
    z = _dot(h.astype(BF16), w_ref[...])
    off = 0
    for o_ref, wd in zip(o_refs, widths):
        o_ref[...] = z[:, off:off + wd]
        off += wd


def _inproj(x, g, mod, w_bf16, widths, residual=None):
    b, l, d = x.shape
    t = min(512, l)
    nt = l // t
    tok = lambda c: pl.BlockSpec((None, t, c), lambda bi, i: (bi, i, 0))
    modspec = pl.BlockSpec((None, N_MOD, d), lambda bi, i: (bi, 0, 0))
    tail_specs = [pl.BlockSpec((1, d), lambda bi, i: (0, 0)), modspec, pl.BlockSpec(w_bf16.shape, lambda bi, i: (0, 0))]
    tail_args = (g.reshape(1, d), mod, w_bf16)
    z_specs = [tok(wd) for wd in widths]
    z_shapes = [jax.ShapeDtypeStruct((b, l, wd), F32) for wd in widths]
    if residual is None:
        in_specs, args, out_specs, out_shape = [tok(d)] + tail_specs, (x,) + tail_args, z_specs, z_shapes
    else:
        y_rows, prev_mod = residual
        in_specs = [tok(d), pl.BlockSpec((t, d), lambda bi, i: (bi * nt + i, 0)), modspec] + tail_specs
        args = (x, y_rows, prev_mod) + tail_args
        out_specs, out_shape = [tok(d)] + z_specs, [jax.ShapeDtypeStruct((b, l, d), F32)] + z_shapes
    return pl.pallas_call(
        functools.partial(_inproj_kernel, widths=widths, with_residual=residual is not None),
        grid=(b, nt), in_specs=in_specs, out_specs=out_specs, out_shape=out_shape,
        compiler_params=_cparams("parallel", "parallel"),
    )(*args)


def _rec_kernel(cf_ref, pf_ref, nf_ref, cb_ref, pb_ref, nb_ref, cw_ref, cbias_ref, wg_ref, bg_ref, lam_ref, h0_ref,
                hf_ref, hb_ref, a_scr, carry_scr, *, tc, n_chunks):
    i = pl.program_id(1)
    c = cf_ref.shape[-1]
    half = c // 2
    n_grp = tc // SUBLANES

    @pl.when(i == 0)
    def _():
        carry_scr[0] = jnp.broadcast_to(h0_ref[0:1, :], (SUBLANES, c))
        carry_scr[1] = jnp.broadcast_to(h0_ref[1:2, :], (SUBLANES, c))

    row = lax.broadcasted_iota(jnp.int32, (1, SUBLANES, half), 1)

    def direction(d, cur_ref, prev_ref, next_ref, is_first, is_last, out_ref):
        prev = jnp.where(is_first, 0.0, prev_ref[...])
        nxt = jnp.where(is_last, 0.0, next_ref[...])
        zz = jnp.concatenate([prev, cur_ref[...], nxt], axis=0)
        u = cbias_ref[...]
        for k in range(CONV_W):
            o = SUBLANES - CONV_W // 2 + k
            u = u + cw_ref[k:k + 1, :] * zz[o:o + tc, :]
        log_decay = -LRU_C * jax.nn.softplus(-lam_ref[d:d + 1, :])
        for hh in range(2):
            cols = slice(hh * half, (hh + 1) * half)
            uh = u[:, cols]
            pre = _dot(uh.astype(BF16), wg_ref[d, hh])
            r = jax.nn.sigmoid(pre[:, :half] + bg_ref[d, 0:1, cols])
            ig = jax.nn.sigmoid(pre[:, half:] + bg_ref[d, 1:2, cols])
            log_a = log_decay[:, cols] * r
            a = jnp.exp(log_a)
            th = jnp.tanh(log_a)
            bt = jnp.sqrt(-2.0 * th / (1.0 - th)) * (ig * uh)
            a = a.reshape(n_grp, SUBLANES, half)
            bt = bt.reshape(n_grp, SUBLANES, half)
            for s in (1, 2, 4):
                shift, keep = (s, row >= s) if d == 0 else (SUBLANES - s, row < SUBLANES - s)
                a_sh, b_sh = pltpu.roll(a, shift, 1), pltpu.roll(bt, shift, 1)
                bt = a * jnp.where(keep, b_sh, 0.0) + bt
                a = a * jnp.where(keep, a_sh, 1.0)
            a_scr[d, :, cols] = a.reshape(tc, half)
            out_ref[:, cols] = bt.reshape(tc, half)

    direction(0, cf_ref, pf_ref, nf_ref, i == 0, i == n_chunks - 1, hf_ref)
    direction(1, cb_ref, pb_ref, nb_ref, i == n_chunks - 1, i == 0, hb_ref)

    def body(gi, carry):
        cf, cb = carry
        rf = pl.multiple_of(gi * SUBLANES, SUBLANES)
        rb = pl.multiple_of((n_grp - 1 - gi) * SUBLANES, SUBLANES)
        hf = hf_ref[pl.ds(rf, SUBLANES), :] + a_scr[0, pl.ds(rf, SUBLANES), :] * cf
        hb = hb_ref[pl.ds(rb, SUBLANES), :] + a_scr[1, pl.ds(rb, SUBLANES), :] * cb
        hf_ref[pl.ds(rf, SUBLANES), :] = hf
        hb_ref[pl.ds(rb, SUBLANES), :] = hb
        return (jnp.broadcast_to(hf[SUBLANES - 1:SUBLANES, :], (SUBLANES, c)),
                jnp.broadcast_to(hb[0:1, :], (SUBLANES, c)))

    cf, cb = lax.fori_loop(0, n_grp, body, (carry_scr[0], carry_scr[1]))
    carry_scr[0] = cf
    carry_scr[1] = cb


def _recurrent(z_x, conv_w, conv_b, wgate, bgate, lam, h0):
    b, l, c = z_x.shape
    tc = min(1024, l)
    nc = l // tc
    per = tc // SUBLANES
    nblk8 = l // SUBLANES
    cur_f = pl.BlockSpec((None, tc, c), lambda bi, i: (bi, i, 0))
    prev_f = pl.BlockSpec((None, SUBLANES, c), lambda bi, i: (bi, jnp.maximum(i * per - 1, 0), 0))
    next_f = pl.BlockSpec((None, SUBLANES, c), lambda bi, i: (bi, jnp.minimum((i + 1) * per, nblk8 - 1), 0))
    cur_b = pl.BlockSpec((None, tc, c), lambda bi, i: (bi, nc - 1 - i, 0))
    prev_b = pl.BlockSpec((None, SUBLANES, c), lambda bi, i: (bi, jnp.maximum((nc - 1 - i) * per - 1, 0), 0))
    next_b = pl.BlockSpec((None, SUBLANES, c), lambda bi, i: (bi, jnp.minimum((nc - i) * per, nblk8 - 1), 0))
    full = lambda a: pl.BlockSpec(a.shape, lambda bi, i: (0,) * a.ndim)
    return pl.pallas_call(
        functools.partial(_rec_kernel, tc=tc, n_chunks=nc),
        grid=(b, nc),
        in_specs=[cur_f, prev_f, next_f, cur_b, prev_b, next_b,
                  full(conv_w), full(conv_b), full(wgate), full(bgate), full(lam),
                  pl.BlockSpec((None, 2, c), lambda bi, i: (bi, 0, 0))],
        out_specs=[pl.BlockSpec((None, tc, c), lambda bi, i: (bi, i, 0)),
                   pl.BlockSpec((None, tc, c), lambda bi, i: (bi, nc - 1 - i, 0))],
        out_shape=[jax.ShapeDtypeStruct((b, l, c), F32)] * 2,
        scratch_shapes=[pltpu.VMEM((2, tc, c), F32), pltpu.VMEM((2, SUBLANES, c), F32)],
        compiler_params=_cparams("parallel", "arbitrary"),
    )(z_x, z_x, z_x, z_x, z_x, z_x, conv_w, conv_b, wgate, bgate, lam, h0)


def _dft_tables(n):
    j = lax.broadcasted_iota(jnp.int32, (n, n), 0)
    k = lax.broadcasted_iota(jnp.int32, (n, n), 1)
    ang = ((j * k) % n).astype(F32) * (2.0 * math.pi / n)
    return jnp.cos(ang), jnp.sin(ang)


def _block_diag(w):
    return jax.scipy.linalg.block_diag(*[w[h] for h in range(w.shape[0])])


def _channel_stage(gr, gi, c_ref, s_ref, w_ref, scale):
    f = (_dot(gr.astype(BF16), c_ref[...]) + _dot(gi.astype(BF16), s_ref[...])) * scale
    return _dot(f.astype(BF16), w_ref[...])


def _four1_kernel(x_ref, f1_ref, twc_ref, tws_ref, ar_ref, ai_ref, *, n1):
    for jj in range(SUBLANES):
        a = _dot(f1_ref[...], x_ref[:, jj, :].astype(BF16))
        ar, ai = a[:n1], a[n1:]
        cc = twc_ref[:, jj:jj + 1]
        ss = tws_ref[:, jj:jj + 1]
        ar_ref[:, jj, :] = ar * cc + ai * ss
        ai_ref[:, jj, :] = ai * cc - ar * ss


def _four2_kernel(ar_ref, ai_ref, f2_ref, c_ref, s_ref, w_ref, o_ref, *, n2, scale):
    grs, gis = [], []
    for kk in range(SUBLANES):
        slab = jnp.concatenate([ar_ref[kk], ai_ref[kk]], axis=0).astype(BF16)
        g = _dot(f2_ref[...], slab)
        grs.append(g[:n2])
        gis.append(g[n2:])
    y = _channel_stage(jnp.concatenate(grs, axis=0), jnp.concatenate(gis, axis=0), c_ref, s_ref, w_ref, scale)
    for kk in range(SUBLANES):
        o_ref[:, kk, :] = y[kk * n2:(kk + 1) * n2]


def _four_small_kernel(x_ref, f_ref, c_ref, s_ref, w_ref, o_ref, *, n, scale):
    g = _dot(f_ref[...], x_ref[...].astype(BF16))
    o_ref[...] = _channel_stage(g[:n], g[n:], c_ref, s_ref, w_ref, scale)


def _fourier(z_f, four_w_bd):
    b, l, nch = z_f.shape
    hd = nch // FOUR_HEADS
    scale = 1.0 / math.sqrt(l * hd)
    c_h, s_h = _dft_tables(hd)
    eye = jnp.eye(FOUR_HEADS, dtype=F32)
    c_bd = jnp.kron(eye, c_h).astype(BF16)
    s_bd = jnp.kron(eye, s_h).astype(BF16)
    full = lambda a, nd: pl.BlockSpec(a.shape, lambda *_: (0,) * a.ndim)
    if l <= 256:
        cn, sn = _dft_tables(l)
        f = jnp.concatenate([cn, -sn], axis=0).astype(BF16)
        return pl.pallas_call(
            functools.partial(_four_small_kernel, n=l, scale=scale),
            grid=(b,),
            in_specs=[pl.BlockSpec((None, l, nch), lambda bi: (bi, 0, 0)),
                      full(f, 1), full(c_bd, 1), full(s_bd, 1), full(four_w_bd, 1)],
            out_specs=pl.BlockSpec((None, l, nch), lambda bi: (bi, 0, 0)),
            out_shape=jax.ShapeDtypeStruct((b, l, nch), F32),
            compiler_params=_cparams("parallel"),
        )(z_f, f, c_bd, s_bd, four_w_bd)

    n2 = LANES
    n1 = l // n2
    nj = n2 // SUBLANES
    c1, s1 = _dft_tables(n1)
    f1 = jnp.concatenate([c1, -s1], axis=0).astype(BF16)
    c2, s2 = _dft_tables(n2)
    f2 = jnp.concatenate([jnp.concatenate([c2, s2], axis=1),
                          jnp.concatenate([-s2, c2], axis=1)], axis=0).astype(BF16)
    k1 = lax.broadcasted_iota(jnp.int32, (n1, n2), 0)
    t2 = lax.broadcasted_iota(jnp.int32, (n1, n2), 1)
    ang = (k1 * t2).astype(F32) * (2.0 * math.pi / l)
    twc = jnp.cos(ang).reshape(n1, nj, SUBLANES).transpose(1, 0, 2)
    tws = jnp.sin(ang).reshape(n1, nj, SUBLANES).transpose(1, 0, 2)
    ar, ai = pl.pallas_call(
        functools.partial(_four1_kernel, n1=n1),
        grid=(b, nj),
        in_specs=[pl.BlockSpec((None, n1, SUBLANES, nch), lambda bi, j: (bi, 0, j, 0)),
                  full(f1, 2),
                  pl.BlockSpec((None, n1, SUBLANES), lambda bi, j: (j, 0, 0)),
                  pl.BlockSpec((None, n1, SUBLANES), lambda bi, j: (j, 0, 0))],
        out_specs=[pl.BlockSpec((None, n1, SUBLANES, nch), lambda bi, j: (bi, 0, j, 0))] * 2,
        out_shape=[jax.ShapeDtypeStruct((b, n1, n2, nch), F32)] * 2,
        compiler_params=_cparams("parallel", "parallel"),
    )(z_f.reshape(b, n1, n2, nch), f1, twc, tws)
    y = pl.pallas_call(
        functools.partial(_four2_kernel, n2=n2, scale=scale),
        grid=(b, n1 // SUBLANES),
        in_specs=[pl.BlockSpec((None, SUBLANES, n2, nch), lambda bi, j: (bi, j, 0, 0)),
                  pl.BlockSpec((None, SUBLANES, n2, nch), lambda bi, j: (bi, j, 0, 0)),
                  full(f2, 2), full(c_bd, 2), full(s_bd, 2), full(four_w_bd, 2)],
        out_specs=pl.BlockSpec((None, n2, SUBLANES, nch), lambda bi, j: (bi, 0, j, 0)),
        out_shape=jax.ShapeDtypeStruct((b, n2, n1, nch), F32),
        compiler_params=_cparams("parallel", "parallel"),
    )(ar, ai, f2, c_bd, s_bd, four_w_bd)
    return y.reshape(b, l, nch)


def _window_counts(idx, n, w):
    return (jnp.minimum(idx + (w - w // 2), n) - jnp.maximum(idx - w // 2, 0)).astype(F32)


def _pool_kernel(cur_ref, prev_ref, next_ref, pw_ref, ps_ref, o_ref, *, r, w, rows_total, n_tiles):
    i = pl.program_id(1)
    two_d = rows_total > 1
    halo = SUBLANES if two_d else 0
    nr = r + 2 * halo
    stride = w + POOL_PAD
    flat = nr * stride
    lane = lax.broadcasted_iota(jnp.int32, (1, 1, LANES), 2)
    low = lane < (LANES // 2)
    col = lax.broadcasted_iota(jnp.int32, (1, w, 1), 1)
    grow = lax.broadcasted_iota(jnp.int32, (r, 1, 1), 0) + i * r
    outs = []
    for hh in range(2):
        cols = slice(hh * LANES, (hh + 1) * LANES)
        w_lo, w_hi = POOL_WINDOWS[2 * hh], POOL_WINDOWS[2 * hh + 1]
        cur = cur_ref[:, cols]
        if two_d:
            prev = jnp.where(i == 0, 0.0, prev_ref[:, cols])
            nxt = jnp.where(i == n_tiles - 1, 0.0, next_ref[:, cols])
            ext = jnp.concatenate([prev, cur, nxt], axis=0)
        else:
            ext = cur
        ext = ext.reshape(nr, w, LANES)
        x = jnp.concatenate([jnp.zeros((nr, POOL_PAD, LANES), F32), ext], axis=1).reshape(flat, LANES)
        p = x + pltpu.roll(x, 1, 0)
        sums = {2: p}
        for ww, s in ((4, 1), (8, 2), (16, 4)):
            if ww > w_hi:
                break
            p = pltpu.roll(p, s, 0) + pltpu.roll(p, flat - s, 0)
            sums[ww] = p
        y = jnp.where(low[0], sums[w_lo], sums[w_hi]).reshape(nr, stride, LANES)[:, POOL_PAD:, :]
        if two_d:
            q2 = y[0:nr - 1] + y[1:nr]
            rsum = {2: q2[7:7 + r]}
            q4 = q2[0:nr - 3] + q2[2:nr - 1]
            rsum[4] = q4[6:6 + r]
            if w_hi > 4:
                q8 = q4[0:nr - 7] + q4[4:nr - 3]
                rsum[8] = q8[4:4 + r]
                q16 = q8[0:nr - 15] + q8[8:nr - 7]
                rsum[16] = q16[0:r]
            tot = jnp.where(low, rsum[w_lo], rsum[w_hi])
            cnt_r = jnp.where(low, _window_counts(grow, rows_total, w_lo), _window_counts(grow, rows_total, w_hi))
            tot = tot / cnt_r
        else:
            tot = y
        cnt_c = jnp.where(low, _window_counts(col, w, w_lo), _window_counts(col, w, w_hi))
        pooled = (tot / cnt_c).reshape(r * w, LANES)
        outs.append(pooled - cur)
    d = jnp.concatenate(outs, axis=-1)
    o_ref[...] = _dot(d.astype(BF16), pw_ref[...]) * ps_ref[...]


def _pool(z_p, pool_w_bd, pool_scale, rows):
    b, l, nch = z_p.shape
    if rows is None:
        rows_total, w, r = 1, l, 1
    else:
        rows_total, w, r = rows, l // rows, min(16, rows)
    t = r * w
    n_tiles = l // t
    hb = SUBLANES * w if rows is not None else t
    per = t // hb
    nhb = l // hb
    return pl.pallas_call(
        functools.partial(_pool_kernel, r=r, w=w, rows_total=rows_total, n_tiles=n_tiles),
        grid=(b, n_tiles),
        in_specs=[pl.BlockSpec((None, t, nch), lambda bi, i: (bi, i, 0)),
                  pl.BlockSpec((None, hb, nch), lambda bi, i: (bi, jnp.maximum(i * per - 1, 0), 0)),
                  pl.BlockSpec((None, hb, nch), lambda bi, i: (bi, jnp.minimum((i + 1) * per, nhb - 1), 0)),
                  pl.BlockSpec(pool_w_bd.shape, lambda bi, i: (0, 0)),
                  pl.BlockSpec((1, nch), lambda bi, i: (0, 0))],
        out_specs=pl.BlockSpec((None, t, nch), lambda bi, i: (bi, i, 0)),
        out_shape=jax.ShapeDtypeStruct((b, l, nch), F32),
        compiler_params=_cparams("parallel", "parallel"),
    )(z_p, z_p, z_p, pool_w_bd, pool_scale.reshape(1, nch))


def _first_argmax4(v):
    m = jnp.maximum(jnp.maximum(v[0], v[1]), jnp.maximum(v[2], v[3]))
    idx = jnp.where(v[0] >= m, 0, jnp.where(v[1] >= m, 1, jnp.where(v[2] >= m, 2, 3)))
    return m, idx


def _combine_kernel(hf_ref, hb_ref, zg_ref, yp_ref, yf_ref, x_ref, mod_ref, wo_ref, g_ref, wr_ref, br_ref,
                    xo_ref, h2g_ref, route_ref, gt_scr):
    y_a = (hf_ref[...] + hb_ref[...]) * jax.nn.gelu(zg_ref[...], approximate=True)
    ycat = jnp.concatenate([y_a.astype(BF16), yp_ref[...].astype(BF16), yf_ref[...].astype(BF16)], axis=-1)
    x_new = x_ref[...] + mod_ref[2:3, :] * _dot(ycat, wo_ref[...])
    xo_ref[...] = x_new
    h2 = _rms_mod(x_new, g_ref[...], mod_ref[3:4, :], mod_ref[4:5, :])
    lt = (_dot(h2.astype(BF16), wr_ref[...]) + br_ref[...]).T
    lg = [lt[g:g + 1, :] for g in range(N_GROUPS)]
    m, gidx = _first_argmax4(lg)
    den = sum(jnp.exp(v - m) for v in lg)
    p_group = 1.0 / den
    sel = []
    for e in range(EXPERTS_PER_GROUP):
        rows = [lt[SUBLANES + EXPERTS_PER_GROUP * g + e:SUBLANES + EXPERTS_PER_GROUP * g + e + 1, :]
                for g in range(N_GROUPS)]
        sel.append(jnp.where(gidx == 0, rows[0], jnp.where(gidx == 1, rows[1], jnp.where(gidx == 2, rows[2], rows[3]))))
    v1, e1 = _first_argmax4(sel)
    rest = [jnp.where(e1 == e, -jnp.inf, sel[e]) for e in range(EXPERTS_PER_GROUP)]
    v2, e2 = _first_argmax4(rest)
    t2 = jnp.exp(v2 - v1)
    w1 = p_group / (1.0 + t2)
    w2 = p_group * t2 / (1.0 + t2)
    gt_scr[...] = jnp.zeros_like(gt_scr)
    for g in range(N_GROUPS):
        for e in range(EXPERTS_PER_GROUP):
            val = jnp.where(gidx == g, jnp.where(e1 == e, w1, jnp.where(e2 == e, w2, 0.0)), 0.0)
            gt_scr[EXPERTS_PER_GROUP * g + e:EXPERTS_PER_GROUP * g + e + 1, :] = val
    gt_scr[GID_LANE:GID_LANE + 1, :] = gidx.astype(F32)
    route = gt_scr[...].T
    route_ref[...] = route
    d = h2.shape[-1]
    h2g_ref[:, :d] = h2
    h2g_ref[:, d:] = route


def _combine(hf, hb, z_g, y_p, y_f, x, mod, w_out, g_ffn, w_router, b_router):
    b, l, d = x.shape
    t = min(512, l)
    tok = lambda c: pl.BlockSpec((None, t, c), lambda bi, i: (bi, i, 0))
    full = lambda a: pl.BlockSpec(a.shape, lambda bi, i: (0,) * a.ndim)
    return pl.pallas_call(
        _combine_kernel,
        grid=(b, l // t),
        in_specs=[tok(hf.shape[-1]), tok(hb.shape[-1]), tok(z_g.shape[-1]), tok(y_p.shape[-1]), tok(y_f.shape[-1]),
                  tok(d), pl.BlockSpec((None, N_MOD, d), lambda bi, i: (bi, 0, 0)),
                  full(w_out), full(g_ffn), full(w_router), full(b_router)],
        out_specs=[tok(d), tok(d + LANES), tok(LANES)],
        out_shape=[jax.ShapeDtypeStruct((b, l, d), F32), jax.ShapeDtypeStruct((b, l, d + LANES), F32),
                   jax.ShapeDtypeStruct((b, l, LANES), F32)],
        scratch_shapes=[pltpu.VMEM((LANES, t), F32)],
        compiler_params=_cparams("parallel", "parallel"),
    )(hf, hb, z_g, y_p, y_f, x, mod, w_out, g_ffn, w_router, b_router)


def _moe_kernel(h_ref, gates_ref, x_ref, mod_ref, wgu_ref, wd_ref, gfin_ref, o_ref, hb_scr, acc_scr, *, n_exp, final):
    e = pl.program_id(2)

    @pl.when(e == 0)
    def _():
        hb_scr[...] = h_ref[:, :hb_scr.shape[-1]].astype(BF16)
        acc_scr[...] = jnp.zeros_like(acc_scr)

    de = wd_ref.shape[0]
    gu = _dot(hb_scr[...], wgu_ref[...])
    hid = jax.nn.silu(gu[:, :de]) * gu[:, de:]
    lane = lax.broadcasted_iota(jnp.int32, gates_ref.shape, 1)
    gate = jnp.sum(jnp.where(lane == e, gates_ref[...], 0.0), axis=-1, keepdims=True)
    acc_scr[...] += gate * _dot(hid.astype(BF16), wd_ref[...])

    @pl.when(e == n_exp - 1)
    def _():
        out = x_ref[...] + mod_ref[5:6, :] * acc_scr[...]
        if final:
            ms = jnp.mean(out * out, axis=-1, keepdims=True)
            out = out * lax.rsqrt(ms + EPS) * gfin_ref[...]
        o_ref[...] = out


def _moe(h2g, gates, x, mod, wgu, wd, g_final, final):
    b, l, d = x.shape
    n_exp = wgu.shape[0]
    t = min(512, l)
    tok = lambda c: pl.BlockSpec((None, t, c), lambda bi, i, e: (bi, i, 0))
    return pl.pallas_call(
        functools.partial(_moe_kernel, n_exp=n_exp, final=final),
        grid=(b, l // t, n_exp),
        in_specs=[tok(d + LANES), tok(LANES), tok(d),
                  pl.BlockSpec((None, N_MOD, d), lambda bi, i, e: (bi, 0, 0)),
                  pl.BlockSpec((None,) + wgu.shape[1:], lambda bi, i, e: (e, 0, 0)),
                  pl.BlockSpec((None,) + wd.shape[1:], lambda bi, i, e: (e, 0, 0)),
                  pl.BlockSpec((1, d), lambda bi, i, e: (0, 0))],
        out_specs=tok(d),
        out_shape=jax.ShapeDtypeStruct((b, l, d), F32),
        scratch_shapes=[pltpu.VMEM((t, d), BF16), pltpu.VMEM((t, d), F32)],
        compiler_params=_cparams("parallel", "parallel", "arbitrary"),
    )(h2g, gates, x, mod, wgu, wd, g_final)


MOE_TILE = 512


def _sort_plan(gid, n_groups, t):
    n = gid.shape[0]
    onehot = (gid[:, None] == jnp.arange(n_groups, dtype=jnp.int32)[None, :]).astype(jnp.int32)
    csum = jnp.cumsum(onehot, axis=0)
    rank = jnp.sum((csum - onehot) * onehot, axis=1)
    tiles = (csum[-1] + t - 1) // t
    tile_end = jnp.cumsum(tiles)
    slot = jnp.sum(onehot * (tile_end - tiles)[None, :], axis=1) * t + rank
    n_tiles = n // t + n_groups
    tile_idx = jnp.arange(n_tiles, dtype=jnp.int32)
    tile_group = jnp.minimum(jnp.sum((tile_idx[:, None] >= tile_end[None, :]).astype(jnp.int32), axis=1), n_groups - 1)
    return slot.astype(jnp.int32), tile_group.astype(jnp.int32), n_tiles


def _invert_kernel(slot_ref, src_ref):
    n_rows = src_ref.shape[0]
    n = slot_ref.shape[0]

    def clear(s, carry):
        src_ref[s] = -1
        return carry

    lax.fori_loop(0, n_rows, clear, 0, unroll=8)

    def put(tok, carry):
        src_ref[slot_ref[tok]] = tok
        return carry

    lax.fori_loop(0, n, put, 0, unroll=8)


def _invert(slot, n_rows):
    return pl.pallas_call(
        _invert_kernel,
        in_specs=[pl.BlockSpec(memory_space=pltpu.SMEM)],
        out_specs=pl.BlockSpec(memory_space=pltpu.SMEM),
        out_shape=jax.ShapeDtypeStruct((n_rows,), jnp.int32),
    )(slot)


def _moe_fused_kernel(tg_ref, nv_ref, src_ref, dst_ref, h_ref, wgu_ref, wd_ref, y_ref, rows, ybuf, hb_scr, gsem, ssem,
                      *, t, d, epg):
    i = pl.program_id(0)
    nt = pl.num_programs(0) - 1
    s = lax.rem(i, 2)
    de = wd_ref.shape[1]
    part = t // epg

    def gather_rows(tile, slot, lo, hi):
        def body(r, carry):
            pltpu.make_async_copy(h_ref.at[pl.ds(src_ref[tile * t + r], 1)], rows.at[slot, pl.ds(r, 1)],
                                  gsem.at[slot]).start()
            return carry
        lax.fori_loop(lo, hi, body, 0, unroll=8)

    def scatter_rows(tile, slot, lo, hi):
        def body(r, carry):
            pltpu.make_async_copy(ybuf.at[slot, pl.ds(r, 1)], y_ref.at[pl.ds(dst_ref[tile * t + r], 1)],
                                  ssem.at[slot]).start()
            return carry
        lax.fori_loop(lo, hi, body, 0, unroll=8)

    def wait_gather(slot):
        pltpu.make_async_copy(h_ref.at[pl.ds(0, t)], rows.at[slot], gsem.at[slot]).wait()

    def wait_scatter(slot):
        pltpu.make_async_copy(ybuf.at[slot], y_ref.at[pl.ds(0, t)], ssem.at[slot]).wait()

    @pl.when(i == 0)
    def _():
        gather_rows(0, 0, 0, t)

    @pl.when(i >= 2)
    def _():
        wait_scatter(s)

    @pl.when(i < nt)
    def _():
        wait_gather(s)
        hb_scr[...] = rows[s, :, :d].astype(BF16)
        row_id = lax.broadcasted_iota(jnp.int32, (t, LANES), 0)
        lane = lax.broadcasted_iota(jnp.int32, (t, LANES), 1)
        gates = jnp.where(row_id < nv_ref[i], rows[s, :, d:], 0.0)
        first = tg_ref[i] * epg
        for e in range(epg):
            @pl.when(i + 1 < nt)
            def _():
                gather_rows(i + 1, 1 - s, e * part, (e + 1) * part)

            @pl.when(i >= 1)
            def _():
                scatter_rows(i - 1, 1 - s, e * part, (e + 1) * part)

            gu = _dot(hb_scr[...], wgu_ref[e])
            hid = jax.nn.silu(gu[:, :de]) * gu[:, de:]
            gate = jnp.sum(jnp.where(lane == first + e, gates, 0.0), axis=-1, keepdims=True)
            y = gate * _dot(hid.astype(BF16), wd_ref[e])
            if e == 0:
                ybuf[s] = y
            else:
                ybuf[s] += y

    @pl.when(i == nt)
    def _():
        scatter_rows(nt - 1, 1 - s, 0, t)
        wait_scatter(1 - s)


def _moe_fused(tile_group, n_valid, src, dst, h2g, wgu, wd, n_tiles, t, d):
    epg = wgu.shape[1]
    last = n_tiles - 1
    return pl.pallas_call(
        functools.partial(_moe_fused_kernel, t=t, d=d, epg=epg),
        grid_spec=pltpu.PrefetchScalarGridSpec(
            num_scalar_prefetch=4, grid=(n_tiles + 1,),
            in_specs=[pl.BlockSpec(memory_space=pl.ANY),
                      pl.BlockSpec((None,) + wgu.shape[1:], lambda i, tg, nv, sr, ds: (tg[jnp.minimum(i, last)], 0, 0, 0)),
                      pl.BlockSpec((None,) + wd.shape[1:], lambda i, tg, nv, sr, ds: (tg[jnp.minimum(i, last)], 0, 0, 0))],
            out_specs=pl.BlockSpec(memory_space=pl.ANY),
            scratch_shapes=[pltpu.VMEM((2, t, h2g.shape[1]), F32), pltpu.VMEM((2, t, d), F32), pltpu.VMEM((t, d), BF16),
                            pltpu.SemaphoreType.DMA((2,)), pltpu.SemaphoreType.DMA((2,))]),
        out_shape=jax.ShapeDtypeStruct((n_tiles * t, d), F32),
        compiler_params=_cparams("arbitrary"),
    )(tile_group, n_valid, src, dst, h2g, wgu, wd)


def _final_kernel(x_ref, y_ref, mod_ref, g_ref, o_ref):
    out = x_ref[...] + mod_ref[5:6, :] * y_ref[...]
    ms = jnp.mean(out * out, axis=-1, keepdims=True)
    o_ref[...] = out * lax.rsqrt(ms + EPS) * g_ref[...]


def _final(x, y_rows, mod, g_final):
    b, l, d = x.shape
    t = min(512, l)
    nt = l // t
    return pl.pallas_call(
        _final_kernel,
        grid=(b, nt),
        in_specs=[pl.BlockSpec((None, t, d), lambda bi, i: (bi, i, 0)),
                  pl.BlockSpec((t, d), lambda bi, i: (bi * nt + i, 0)),
                  pl.BlockSpec((None, N_MOD, d), lambda bi, i: (bi, 0, 0)),
                  pl.BlockSpec((1, d), lambda bi, i: (0, 0))],
        out_specs=pl.BlockSpec((None, t, d), lambda bi, i: (bi, i, 0)),
        out_shape=jax.ShapeDtypeStruct((b, l, d), F32),
        compiler_params=_cparams("parallel", "parallel"),
    )(x, y_rows, mod, g_final)


def kernel(x, c, ctx, c_ctx, mod_w, mod_b, norm_mix_g, norm_ffn_g, w_in, conv_w, conv_b, rec_gate_a_w, rec_gate_a_b,
           rec_gate_x_w, rec_gate_x_b, rec_lambda, pool_w, pool_scale, fourier_w, w_out, router_group_w,
           router_group_b, router_expert_w, router_expert_b, expert_w_gate, expert_w_up, expert_w_down, final_norm_g):
    b, l, d = x.shape
    depth = mod_w.shape[0]
    d_rec = conv_w.shape[-1]
    d_pool = pool_scale.shape[-1]
    d_four = fourier_w.shape[1] * fourier_w.shape[2]
    widths = (d_rec, d_rec, d_pool, d_four)
    rows = l // GRID_W
    heads_half = REC_HEADS // 2
    n_exp = expert_w_gate.shape[1]

    cvec = jnp.concatenate([c, c_ctx[None, :], jnp.zeros((SUBLANES - b - 1, d), F32)], axis=0)
    mod_all = _modulation(cvec, mod_w, mod_b).reshape(depth, SUBLANES, N_MOD, d)
    g_final = final_norm_g.reshape(1, d)
    zeros_state = jnp.zeros((b, 2, d_rec), F32)
    pending = None

    for li in range(depth):
        last = li == depth - 1
        mod_lat = mod_all[li, :b]
        mod_ctx = jnp.broadcast_to(mod_all[li, b][None], (b, N_MOD, d))
        w_in_l = w_in[li].astype(BF16)
        w_out_l = w_out[li].astype(BF16)
        wgate = jnp.stack([
            jnp.stack([jnp.concatenate([_block_diag(rec_gate_a_w[li, dd, hh * heads_half:(hh + 1) * heads_half]),
                                        _block_diag(rec_gate_x_w[li, dd, hh * heads_half:(hh + 1) * heads_half])], axis=1)
                       for hh in range(2)]) for dd in range(2)]).astype(BF16)
        bgate = jnp.stack([rec_gate_a_b[li], rec_gate_x_b[li]], axis=1)
        rec_p = (conv_w[li], conv_b[li].reshape(1, d_rec), wgate, bgate, rec_lambda[li])
        pool_w_bd = _block_diag(pool_w[li]).astype(BF16)
        four_w_bd = _block_diag(fourier_w[li]).astype(BF16)
        w_router = jnp.concatenate([router_group_w[li], jnp.zeros((d, SUBLANES - N_GROUPS), F32), router_expert_w[li],
                                    jnp.zeros((d, LANES - SUBLANES - n_exp), F32)], axis=1).astype(BF16)
        b_router = jnp.concatenate([router_group_b[li], jnp.zeros((SUBLANES - N_GROUPS,), F32), router_expert_b[li],
                                    jnp.zeros((LANES - SUBLANES - n_exp,), F32)]).reshape(1, LANES)
        wgu = jnp.concatenate([expert_w_gate[li], expert_w_up[li]], axis=-1).astype(BF16)
        wd = expert_w_down[li].astype(BF16)
        wgu_grp = wgu.reshape((N_GROUPS, EXPERTS_PER_GROUP) + wgu.shape[1:])
        wd_grp = wd.reshape((N_GROUPS, EXPERTS_PER_GROUP) + wd.shape[1:])
        g_ffn = norm_ffn_g[li].reshape(1, d)

        def mixer_tail(hf, hb, z_g, z_p, z_f, stream, mod, grid_rows):
            y_p = _pool(z_p, pool_w_bd, pool_scale[li], grid_rows)
            y_f = _fourier(z_f, four_w_bd)
            return _combine(hf, hb, z_g, y_p, y_f, stream, mod, w_out_l, g_ffn, w_router, b_router)

        zc_x, zc_g, zc_p, zc_f = _inproj(ctx, norm_mix_g[li], mod_ctx, w_in_l, widths)
        hf_c, hb_c = _recurrent(zc_x, *rec_p, zeros_state)
        state = jnp.stack([hf_c[:, -1, :], hb_c[:, 0, :]], axis=1)
        if not last:
            ctx_mid, h2_c, gates_c = mixer_tail(hf_c, hb_c, zc_g, zc_p, zc_f, ctx, mod_ctx, None)
            ctx = _moe(h2_c, gates_c, ctx_mid, mod_ctx, wgu, wd, g_final, False)

        if pending is None:
            z_x, z_g, z_p, z_f = _inproj(x, norm_mix_g[li], mod_lat, w_in_l, widths)
        else:
            x, z_x, z_g, z_p, z_f = _inproj(x, norm_mix_g[li], mod_lat, w_in_l, widths, residual=pending)
        hf, hb = _recurrent(z_x, *rec_p, state)
        x, h2g, route = mixer_tail(hf, hb, z_g, z_p, z_f, x, mod_lat, rows)
        t_moe = min(MOE_TILE, l)
        n_tok = b * l
        gid = route[..., GID_LANE].astype(jnp.int32).reshape(n_tok)
        slot, tile_group, n_tiles = _sort_plan(gid, N_GROUPS, t_moe)
        src = _invert(slot, n_tiles * t_moe)
        valid = src >= 0
        pad_rank = jnp.cumsum(jnp.logical_not(valid).astype(jnp.int32)) - 1
        dst = jnp.where(valid, src, n_tok + pad_rank)
        n_valid = jnp.sum(valid.reshape(n_tiles, t_moe).astype(jnp.int32), axis=1)
        y_rows = _moe_fused(tile_group, n_valid, jnp.maximum(src, 0), dst, h2g.reshape(n_tok, d + LANES),
                            wgu_grp, wd_grp, n_tiles, t_moe, d)
        pending = (y_rows, mod_lat)
    return _final(x, pending[0], pending[1], g_final)
```

```python
import functools
import math

import jax
import jax.numpy as jnp
from jax import lax
from jax.experimental import pallas as pl
from jax.experimental.pallas import tpu as pltpu

F32 = jnp.float32
BF16 = jnp.bfloat16

GRID_W = 64
N_MOD = 6
REC_HEADS = 8
CONV_W = 4
LRU_C = 8.0
POOL_WINDOWS = (2, 4, 8, 16)
FOUR_HEADS = 4
N_GROUPS = 4
EXPERTS_PER_GROUP = 4
EPS = 1e-6

LANES = 128
SUBLANES = 8
GID_LANE = 16
POOL_PAD = 8
VMEM_LIMIT = 56 * 1024 * 1024


def _cparams(*sem):
    return pltpu.CompilerParams(dimension_semantics=sem, vmem_limit_bytes=VMEM_LIMIT)


def _split_bf16(a):
    hi = a.astype(BF16)
    lo = (a - hi.astype(F32)).astype(BF16)
    return hi, lo


def _dot(a, b):
    return jnp.dot(a, b, preferred_element_type=F32)


def _mod_kernel(s_ref, w_ref, b_ref, o_ref):
    s = s_ref[...]
    s = s * jax.nn.sigmoid(s)
    s_hi, s_lo = _split_bf16(s)
    w_hi, w_lo = _split_bf16(w_ref[...])
    o_ref[...] = _dot(s_hi, w_hi) + _dot(s_hi, w_lo) + _dot(s_lo, w_hi) + b_ref[...]


def _modulation(cvec, mod_w, mod_b):
    depth, d, dm = mod_w.shape
    tn = dm // 4
    return pl.pallas_call(
        _mod_kernel,
        grid=(depth, dm // tn),
        in_specs=[pl.BlockSpec((SUBLANES, d), lambda l, j: (0, 0)),
                  pl.BlockSpec((None, d, tn), lambda l, j: (l, 0, j)),
                  pl.BlockSpec((None, 1, tn), lambda l, j: (l, 0, j))],
        out_specs=pl.BlockSpec((None, SUBLANES, tn), lambda l, j: (l, 0, j)),
        out_shape=jax.ShapeDtypeStruct((depth, SUBLANES, dm), F32),
        compiler_params=_cparams("parallel", "parallel"),
    )(cvec, mod_w, mod_b.reshape(depth, 1, dm))


def _rms_mod(x, g, shift, scale):
    ms = jnp.mean(x * x, axis=-1, keepdims=True)
    y = x * lax.rsqrt(ms + EPS) * g
    return y * (1.0 + scale) + shift


def _from_token_tiles(ref):
    return jnp.concatenate([ref[:, j, :] for j in range(ref.shape[1])], axis=-1)


def _inproj_kernel(*refs, widths, with_residual, n_tiles):
    i = pl.program_id(1)
    if with_residual:
        (x_ref, xp_ref, xn_ref, y_ref, yp_ref, yn_ref, pmod_ref,
         g_ref, mod_ref, w_ref, cw_ref, cb_ref, xo_ref, *o_refs) = refs
        gate = pmod_ref[5:6, :]
        x = x_ref[...] + gate * _from_token_tiles(y_ref)
        xo_ref[...] = x
        x_halo = jnp.concatenate([xp_ref[...] + gate * _from_token_tiles(yp_ref),
                                  xn_ref[...] + gate * _from_token_tiles(yn_ref)], axis=0)
    else:
        x_ref, xp_ref, xn_ref, g_ref, mod_ref, w_ref, cw_ref, cb_ref, *o_refs = refs
        x = x_ref[...]
        x_halo = jnp.concatenate([xp_ref[...], xn_ref[...]], axis=0)
    t = x.shape[0]
    d_rec = widths[0]
    z = _dot(_rms_mod(x, g_ref[...], mod_ref[0:1, :], mod_ref[1:2, :]).astype(BF16), w_ref[...])
    zh = _dot(_rms_mod(x_halo, g_ref[...], mod_ref[0:1, :], mod_ref[1:2, :]).astype(BF16), w_ref[:, :d_rec])
    zz = jnp.concatenate([jnp.where(i == 0, 0.0, zh[:SUBLANES]), z[:, :d_rec],
                          jnp.where(i == n_tiles - 1, 0.0, zh[SUBLANES:])], axis=0)
    u = cb_ref[...]
    for k in range(CONV_W):
        o = SUBLANES - CONV_W // 2 + k
        u = u + cw_ref[k:k + 1, :] * zz[o:o + t, :]
    o_refs[0][...] = u
    off = d_rec
    for o_ref, wd in zip(o_refs[1:], widths[1:]):
        o_ref[...] = z[:, off:off + wd]
        off += wd


def _inproj(x, g, mod, w_bf16, conv_w, conv_b, widths, residual=None):
    b, l, d = x.shape
    t = min(512, l)
    nt = l // t
    per = t // SUBLANES
    n8 = l // SUBLANES
    tok = lambda c: pl.BlockSpec((None, t, c), lambda bi, i: (bi, i, 0))
    prev8 = lambda bi, i: jnp.maximum(i * per - 1, 0)
    next8 = lambda bi, i: jnp.minimum((i + 1) * per, n8 - 1)
    x_specs = [tok(d), pl.BlockSpec((None, SUBLANES, d), lambda bi, i: (bi, prev8(bi, i), 0)),
               pl.BlockSpec((None, SUBLANES, d), lambda bi, i: (bi, next8(bi, i), 0))]
    modspec = pl.BlockSpec((None, N_MOD, d), lambda bi, i: (bi, 0, 0))
    full = lambda a: pl.BlockSpec(a.shape, lambda bi, i: (0,) * a.ndim)
    tail_specs = [pl.BlockSpec((1, d), lambda bi, i: (0, 0)), modspec, full(w_bf16), full(conv_w), full(conv_b)]
    tail_args = (g.reshape(1, d), mod, w_bf16, conv_w, conv_b)
    z_specs = [tok(wd) for wd in widths]
    z_shapes = [jax.ShapeDtypeStruct((b, l, wd), F32) for wd in widths]
    if residual is None:
        in_specs, args, out_specs, out_shape = x_specs + tail_specs, (x, x, x) + tail_args, z_specs, z_shapes
    else:
        y_rows, row0, prev_mod = residual
        sub = y_rows.shape[1]
        blk0, blk8 = row0 // t, row0 // SUBLANES
        y_specs = [pl.BlockSpec((t, sub, LANES), lambda bi, i: (blk0 + bi * nt + i, 0, 0)),
                   pl.BlockSpec((SUBLANES, sub, LANES), lambda bi, i: (blk8 + bi * n8 + prev8(bi, i), 0, 0)),
                   pl.BlockSpec((SUBLANES, sub, LANES), lambda bi, i: (blk8 + bi * n8 + next8(bi, i), 0, 0))]
        in_specs = x_specs + y_specs + [modspec] + tail_specs
        args = (x, x, x, y_rows, y_rows, y_rows, prev_mod) + tail_args
        out_specs, out_shape = [tok(d)] + z_specs, [jax.ShapeDtypeStruct((b, l, d), F32)] + z_shapes
    return pl.pallas_call(
        functools.partial(_inproj_kernel, widths=widths, with_residual=residual is not None, n_tiles=nt),
        grid=(b, nt), in_specs=in_specs, out_specs=out_specs, out_shape=out_shape,
        compiler_params=_cparams("parallel", "parallel"),
    )(*args)


def _rec_kernel(uf_ref, ub_ref, wg_ref, bg_ref, lam_ref, h0_ref, hf_ref, hb_ref, a_scr, carry_scr, *, tc):
    i = pl.program_id(1)
    c = uf_ref.shape[-1]
    half = c // 2
    n_grp = tc // SUBLANES

    @pl.when(i == 0)
    def _():
        carry_scr[0] = jnp.broadcast_to(h0_ref[0:1, :], (SUBLANES, c))
        carry_scr[1] = jnp.broadcast_to(h0_ref[1:2, :], (SUBLANES, c))

    row = lax.broadcasted_iota(jnp.int32, (1, SUBLANES, half), 1)

    def direction(d, u_ref, out_ref):
        u = u_ref[...]
        log_decay = -LRU_C * jax.nn.softplus(-lam_ref[d:d + 1, :])
        for hh in range(2):
            cols = slice(hh * half, (hh + 1) * half)
            uh = u[:, cols]
            pre = _dot(uh.astype(BF16), wg_ref[d, hh])
            r = jax.nn.sigmoid(pre[:, :half] + bg_ref[d, 0:1, cols])
            ig = jax.nn.sigmoid(pre[:, half:] + bg_ref[d, 1:2, cols])
            log_a = log_decay[:, cols] * r
            a = jnp.exp(log_a)
            th = jnp.tanh(log_a)
            bt = jnp.sqrt(-2.0 * th / (1.0 - th)) * (ig * uh)
            a = a.reshape(n_grp, SUBLANES, half)
            bt = bt.reshape(n_grp, SUBLANES, half)
            for s in (1, 2, 4):
                shift, keep = (s, row >= s) if d == 0 else (SUBLANES - s, row < SUBLANES - s)
                a_sh, b_sh = pltpu.roll(a, shift, 1), pltpu.roll(bt, shift, 1)
                bt = a * jnp.where(keep, b_sh, 0.0) + bt
                a = a * jnp.where(keep, a_sh, 1.0)
            a_scr[d, :, cols] = a.reshape(tc, half)
            out_ref[:, cols] = bt.reshape(tc, half)

    direction(0, uf_ref, hf_ref)
    direction(1, ub_ref, hb_ref)

    def body(gi, carry):
        cf, cb = carry
        rf = pl.multiple_of(gi * SUBLANES, SUBLANES)
        rb = pl.multiple_of((n_grp - 1 - gi) * SUBLANES, SUBLANES)
        hf = hf_ref[pl.ds(rf, SUBLANES), :] + a_scr[0, pl.ds(rf, SUBLANES), :] * cf
        hb = hb_ref[pl.ds(rb, SUBLANES), :] + a_scr[1, pl.ds(rb, SUBLANES), :] * cb
        hf_ref[pl.ds(rf, SUBLANES), :] = hf
        hb_ref[pl.ds(rb, SUBLANES), :] = hb
        return (jnp.broadcast_to(hf[SUBLANES - 1:SUBLANES, :], (SUBLANES, c)),
                jnp.broadcast_to(hb[0:1, :], (SUBLANES, c)))

    cf, cb = lax.fori_loop(0, n_grp, body, (carry_scr[0], carry_scr[1]))
    carry_scr[0] = cf
    carry_scr[1] = cb


def _recurrent(u, wgate, bgate, lam, h0):
    b, l, c = u.shape
    tc = min(1024, l)
    nc = l // tc
    fwd = pl.BlockSpec((None, tc, c), lambda bi, i: (bi, i, 0))
    bwd = pl.BlockSpec((None, tc, c), lambda bi, i: (bi, nc - 1 - i, 0))
    full = lambda a: pl.BlockSpec(a.shape, lambda bi, i: (0,) * a.ndim)
    return pl.pallas_call(
        functools.partial(_rec_kernel, tc=tc),
        grid=(b, nc),
        in_specs=[fwd, bwd, full(wgate), full(bgate), full(lam), pl.BlockSpec((None, 2, c), lambda bi, i: (bi, 0, 0))],
        out_specs=[fwd, bwd],
        out_shape=[jax.ShapeDtypeStruct((b, l, c), F32)] * 2,
        scratch_shapes=[pltpu.VMEM((2, tc, c), F32), pltpu.VMEM((2, SUBLANES, c), F32)],
        compiler_params=_cparams("parallel", "arbitrary"),
    )(u, u, wgate, bgate, lam, h0)


def _dft_tables(n):
    j = lax.broadcasted_iota(jnp.int32, (n, n), 0)
    k = lax.broadcasted_iota(jnp.int32, (n, n), 1)
    ang = ((j * k) % n).astype(F32) * (2.0 * math.pi / n)
    return jnp.cos(ang), jnp.sin(ang)


def _block_diag(w):
    return jax.scipy.linalg.block_diag(*[w[h] for h in range(w.shape[0])])


def _channel_stage(gr, gi, c_ref, s_ref, w_ref, scale):
    f = (_dot(gr.astype(BF16), c_ref[...]) + _dot(gi.astype(BF16), s_ref[...])) * scale
    return _dot(f.astype(BF16), w_ref[...])


def _four1_kernel(x_ref, f1_ref, twc_ref, tws_ref, ar_ref, ai_ref, *, n1):
    for jj in range(SUBLANES):
        a = _dot(f1_ref[...], x_ref[:, jj, :].astype(BF16))
        ar, ai = a[:n1], a[n1:]
        cc = twc_ref[:, jj:jj + 1]
        ss = tws_ref[:, jj:jj + 1]
        ar_ref[:, jj, :] = ar * cc + ai * ss
        ai_ref[:, jj, :] = ai * cc - ar * ss


def _four2_kernel(ar_ref, ai_ref, f2_ref, c_ref, s_ref, w_ref, o_ref, *, n2, scale):
    grs, gis = [], []
    for kk in range(SUBLANES):
        slab = jnp.concatenate([ar_ref[kk], ai_ref[kk]], axis=0).astype(BF16)
        g = _dot(f2_ref[...], slab)
        grs.append(g[:n2])
        gis.append(g[n2:])
    y = _channel_stage(jnp.concatenate(grs, axis=0), jnp.concatenate(gis, axis=0), c_ref, s_ref, w_ref, scale)
    for kk in range(SUBLANES):
        o_ref[:, kk, :] = y[kk * n2:(kk + 1) * n2]


def _four_small_kernel(x_ref, f_ref, c_ref, s_ref, w_ref, o_ref, *, n, scale):
    g = _dot(f_ref[...], x_ref[...].astype(BF16))
    o_ref[...] = _channel_stage(g[:n], g[n:], c_ref, s_ref, w_ref, scale)


def _fourier(z_f, four_w_bd):
    b, l, nch = z_f.shape
    hd = nch // FOUR_HEADS
    scale = 1.0 / math.sqrt(l * hd)
    c_h, s_h = _dft_tables(hd)
    eye = jnp.eye(FOUR_HEADS, dtype=F32)
    c_bd = jnp.kron(eye, c_h).astype(BF16)
    s_bd = jnp.kron(eye, s_h).astype(BF16)
    full = lambda a, nd: pl.BlockSpec(a.shape, lambda *_: (0,) * a.ndim)
    if l <= 256:
        cn, sn = _dft_tables(l)
        f = jnp.concatenate([cn, -sn], axis=0).astype(BF16)
        return pl.pallas_call(
            functools.partial(_four_small_kernel, n=l, scale=scale),
            grid=(b,),
            in_specs=[pl.BlockSpec((None, l, nch), lambda bi: (bi, 0, 0)),
                      full(f, 1), full(c_bd, 1), full(s_bd, 1), full(four_w_bd, 1)],
            out_specs=pl.BlockSpec((None, l, nch), lambda bi: (bi, 0, 0)),
            out_shape=jax.ShapeDtypeStruct((b, l, nch), F32),
            compiler_params=_cparams("parallel"),
        )(z_f, f, c_bd, s_bd, four_w_bd)

    n2 = LANES
    n1 = l // n2
    nj = n2 // SUBLANES
    c1, s1 = _dft_tables(n1)
    f1 = jnp.concatenate([c1, -s1], axis=0).astype(BF16)
    c2, s2 = _dft_tables(n2)
    f2 = jnp.concatenate([jnp.concatenate([c2, s2], axis=1),
                          jnp.concatenate([-s2, c2], axis=1)], axis=0).astype(BF16)
    k1 = lax.broadcasted_iota(jnp.int32, (n1, n2), 0)
    t2 = lax.broadcasted_iota(jnp.int32, (n1, n2), 1)
    ang = (k1 * t2).astype(F32) * (2.0 * math.pi / l)
    twc = jnp.cos(ang).reshape(n1, nj, SUBLANES).transpose(1, 0, 2)
    tws = jnp.sin(ang).reshape(n1, nj, SUBLANES).transpose(1, 0, 2)
    ar, ai = pl.pallas_call(
        functools.partial(_four1_kernel, n1=n1),
        grid=(b, nj),
        in_specs=[pl.BlockSpec((None, n1, SUBLANES, nch), lambda bi, j: (bi, 0, j, 0)),
                  full(f1, 2),
                  pl.BlockSpec((None, n1, SUBLANES), lambda bi, j: (j, 0, 0)),
                  pl.BlockSpec((None, n1, SUBLANES), lambda bi, j: (j, 0, 0))],
        out_specs=[pl.BlockSpec((None, n1, SUBLANES, nch), lambda bi, j: (bi, 0, j, 0))] * 2,
        out_shape=[jax.ShapeDtypeStruct((b, n1, n2, nch), F32)] * 2,
        compiler_params=_cparams("parallel", "parallel"),
    )(z_f.reshape(b, n1, n2, nch), f1, twc, tws)
    y = pl.pallas_call(
        functools.partial(_four2_kernel, n2=n2, scale=scale),
        grid=(b, n1 // SUBLANES),
        in_specs=[pl.BlockSpec((None, SUBLANES, n2, nch), lambda bi, j: (bi, j, 0, 0)),
                  pl.BlockSpec((None, SUBLANES, n2, nch), lambda bi, j: (bi, j, 0, 0)),
                  full(f2, 2), full(c_bd, 2), full(s_bd, 2), full(four_w_bd, 2)],
        out_specs=pl.BlockSpec((None, n2, SUBLANES, nch), lambda bi, j: (bi, 0, j, 0)),
        out_shape=jax.ShapeDtypeStruct((b, n2, n1, nch), F32),
        compiler_params=_cparams("parallel", "parallel"),
    )(ar, ai, f2, c_bd, s_bd, four_w_bd)
    return y.reshape(b, l, nch)


def _window_counts(idx, n, w):
    return (jnp.minimum(idx + (w - w // 2), n) - jnp.maximum(idx - w // 2, 0)).astype(F32)


def _pool_kernel(cur_ref, prev_ref, next_ref, pw_ref, ps_ref, o_ref, *, r, w, rows_total, n_tiles):
    i = pl.program_id(1)
    two_d = rows_total > 1
    halo = SUBLANES if two_d else 0
    nr = r + 2 * halo
    stride = w + POOL_PAD
    flat = nr * stride
    lane = lax.broadcasted_iota(jnp.int32, (1, 1, LANES), 2)
    low = lane < (LANES // 2)
    col = lax.broadcasted_iota(jnp.int32, (1, w, 1), 1)
    grow = lax.broadcasted_iota(jnp.int32, (r, 1, 1), 0) + i * r
    outs = []
    for hh in range(2):
        cols = slice(hh * LANES, (hh + 1) * LANES)
        w_lo, w_hi = POOL_WINDOWS[2 * hh], POOL_WINDOWS[2 * hh + 1]
        cur = cur_ref[:, cols]
        if two_d:
            prev = jnp.where(i == 0, 0.0, prev_ref[:, cols])
            nxt = jnp.where(i == n_tiles - 1, 0.0, next_ref[:, cols])
            ext = jnp.concatenate([prev, cur, nxt], axis=0)
        else:
            ext = cur
        ext = ext.reshape(nr, w, LANES)
        x = jnp.concatenate([jnp.zeros((nr, POOL_PAD, LANES), F32), ext], axis=1).reshape(flat, LANES)
        p = x + pltpu.roll(x, 1, 0)
        sums = {2: p}
        for ww, s in ((4, 1), (8, 2), (16, 4)):
            if ww > w_hi:
                break
            p = pltpu.roll(p, s, 0) + pltpu.roll(p, flat - s, 0)
            sums[ww] = p
        y = jnp.where(low[0], sums[w_lo], sums[w_hi]).reshape(nr, stride, LANES)[:, POOL_PAD:, :]
        if two_d:
            q2 = y[0:nr - 1] + y[1:nr]
            rsum = {2: q2[7:7 + r]}
            q4 = q2[0:nr - 3] + q2[2:nr - 1]
            rsum[4] = q4[6:6 + r]
            if w_hi > 4:
                q8 = q4[0:nr - 7] + q4[4:nr - 3]
                rsum[8] = q8[4:4 + r]
                q16 = q8[0:nr - 15] + q8[8:nr - 7]
                rsum[16] = q16[0:r]
            tot = jnp.where(low, rsum[w_lo], rsum[w_hi])
            cnt_r = jnp.where(low, _window_counts(grow, rows_total, w_lo), _window_counts(grow, rows_total, w_hi))
            tot = tot / cnt_r
        else:
            tot = y
        cnt_c = jnp.where(low, _window_counts(col, w, w_lo), _window_counts(col, w, w_hi))
        pooled = (tot / cnt_c).reshape(r * w, LANES)
        outs.append(pooled - cur)
    d = jnp.concatenate(outs, axis=-1)
    o_ref[...] = _dot(d.astype(BF16), pw_ref[...]) * ps_ref[...]


def _pool(z_p, pool_w_bd, pool_scale, rows):
    b, l, nch = z_p.shape
    if rows is None:
        rows_total, w, r = 1, l, 1
    else:
        rows_total, w, r = rows, l // rows, min(16, rows)
    t = r * w
    n_tiles = l // t
    hb = SUBLANES * w if rows is not None else t
    per = t // hb
    nhb = l // hb
    return pl.pallas_call(
        functools.partial(_pool_kernel, r=r, w=w, rows_total=rows_total, n_tiles=n_tiles),
        grid=(b, n_tiles),
        in_specs=[pl.BlockSpec((None, t, nch), lambda bi, i: (bi, i, 0)),
                  pl.BlockSpec((None, hb, nch), lambda bi, i: (bi, jnp.maximum(i * per - 1, 0), 0)),
                  pl.BlockSpec((None, hb, nch), lambda bi, i: (bi, jnp.minimum((i + 1) * per, nhb - 1), 0)),
                  pl.BlockSpec(pool_w_bd.shape, lambda bi, i: (0, 0)),
                  pl.BlockSpec((1, nch), lambda bi, i: (0, 0))],
        out_specs=pl.BlockSpec((None, t, nch), lambda bi, i: (bi, i, 0)),
        out_shape=jax.ShapeDtypeStruct((b, l, nch), F32),
        compiler_params=_cparams("parallel", "parallel"),
    )(z_p, z_p, z_p, pool_w_bd, pool_scale.reshape(1, nch))


def _first_argmax4(v):
    m = jnp.maximum(jnp.maximum(v[0], v[1]), jnp.maximum(v[2], v[3]))
    idx = jnp.where(v[0] >= m, 0, jnp.where(v[1] >= m, 1, jnp.where(v[2] >= m, 2, 3)))
    return m, idx


def _select4(idx, v):
    return jnp.where(idx == 0, v[0], jnp.where(idx == 1, v[1], jnp.where(idx == 2, v[2], v[3])))


def _router_logits(h_bf16, wr_ref, br_ref):
    return (_dot(h_bf16, wr_ref[...]) + br_ref[...]).T


def _group_logits(lt):
    return [lt[g:g + 1, :] for g in range(N_GROUPS)]


def _expert_gates(lt, group):
    lg = _group_logits(lt)
    m, _ = _first_argmax4(lg)
    den = sum(jnp.exp(v - m) for v in lg)
    p_group = jnp.exp(_select4(group, lg) - m) / den
    sel = [_select4(group, [lt[SUBLANES + EXPERTS_PER_GROUP * g + e:SUBLANES + EXPERTS_PER_GROUP * g + e + 1, :]
                            for g in range(N_GROUPS)]) for e in range(EXPERTS_PER_GROUP)]
    v1, e1 = _first_argmax4(sel)
    rest = [jnp.where(e1 == e, -jnp.inf, sel[e]) for e in range(EXPERTS_PER_GROUP)]
    v2, e2 = _first_argmax4(rest)
    t2 = jnp.exp(v2 - v1)
    w1 = p_group / (1.0 + t2)
    w2 = p_group * t2 / (1.0 + t2)
    return [jnp.where(e1 == e, w1, jnp.where(e2 == e, w2, 0.0)) for e in range(EXPERTS_PER_GROUP)]


def _combine_kernel(hf_ref, hb_ref, zg_ref, yp_ref, yf_ref, x_ref, mod_ref, wo_ref, g_ref, wr_ref, br_ref,
                    xo_ref, h2t_ref, gid_ref):
    y_a = (hf_ref[...] + hb_ref[...]) * jax.nn.gelu(zg_ref[...], approximate=True)
    ycat = jnp.concatenate([y_a.astype(BF16), yp_ref[...].astype(BF16), yf_ref[...].astype(BF16)], axis=-1)
    x_new = x_ref[...] + mod_ref[2:3, :] * _dot(ycat, wo_ref[...])
    xo_ref[...] = x_new
    h2 = _rms_mod(x_new, g_ref[...], mod_ref[3:4, :], mod_ref[4:5, :])
    for j in range(h2t_ref.shape[1]):
        h2t_ref[:, j, :] = h2[:, j * LANES:(j + 1) * LANES]
    _, gidx = _first_argmax4(_group_logits(_router_logits(h2.astype(BF16), wr_ref, br_ref)))
    gid_ref[...] = gidx


def _combine(hf, hb, z_g, y_p, y_f, x, mod, w_out, g_ffn, w_router, b_router):
    b, l, d = x.shape
    t = min(512, l)
    sub = d // LANES
    tok = lambda c: pl.BlockSpec((None, t, c), lambda bi, i: (bi, i, 0))
    full = lambda a: pl.BlockSpec(a.shape, lambda bi, i: (0,) * a.ndim)
    return pl.pallas_call(
        _combine_kernel,
        grid=(b, l // t),
        in_specs=[tok(hf.shape[-1]), tok(hb.shape[-1]), tok(z_g.shape[-1]), tok(y_p.shape[-1]), tok(y_f.shape[-1]),
                  tok(d), pl.BlockSpec((None, N_MOD, d), lambda bi, i: (bi, 0, 0)),
                  full(w_out), full(g_ffn), full(w_router), full(b_router)],
        out_specs=[tok(d), pl.BlockSpec((None, t, sub, LANES), lambda bi, i: (bi, i, 0, 0)),
                   pl.BlockSpec((None, 1, t), lambda bi, i: (bi, 0, i))],
        out_shape=[jax.ShapeDtypeStruct((b, l, d), F32), jax.ShapeDtypeStruct((b, l, sub, LANES), F32),
                   jax.ShapeDtypeStruct((b, 1, l), jnp.int32)],
        compiler_params=_cparams("parallel", "parallel"),
    )(hf, hb, z_g, y_p, y_f, x, mod, w_out, g_ffn, w_router, b_router)


MOE_TILE = 512


def _sort_plan(gid, n_groups, t):
    n = gid.shape[0]
    onehot = (gid[:, None] == jnp.arange(n_groups, dtype=jnp.int32)[None, :]).astype(jnp.int32)
    csum = jnp.cumsum(onehot, axis=0)
    rank = jnp.sum((csum - onehot) * onehot, axis=1)
    tiles = (csum[-1] + t - 1) // t
    tile_end = jnp.cumsum(tiles)
    slot = jnp.sum(onehot * (tile_end - tiles)[None, :], axis=1) * t + rank
    n_tiles = n // t + n_groups
    tile_idx = jnp.arange(n_tiles, dtype=jnp.int32)
    tile_group = jnp.minimum(jnp.sum((tile_idx[:, None] >= tile_end[None, :]).astype(jnp.int32), axis=1), n_groups - 1)
    return slot.astype(jnp.int32), tile_group.astype(jnp.int32), n_tiles


def _invert_kernel(slot_ref, src_ref):
    n_rows = src_ref.shape[0]
    n = slot_ref.shape[0]

    def clear(s, carry):
        src_ref[s] = -1
        return carry

    lax.fori_loop(0, n_rows, clear, 0, unroll=8)

    def put(tok, carry):
        src_ref[slot_ref[tok]] = tok
        return carry

    lax.fori_loop(0, n, put, 0, unroll=8)


def _invert(slot, n_rows):
    return pl.pallas_call(
        _invert_kernel,
        in_specs=[pl.BlockSpec(memory_space=pltpu.SMEM)],
        out_specs=pl.BlockSpec(memory_space=pltpu.SMEM),
        out_shape=jax.ShapeDtypeStruct((n_rows,), jnp.int32),
    )(slot)


def _moe_fused_kernel(tg_ref, nv_ref, src_ref, dst_ref, h_ref, wr_ref, br_ref, wgu_ref, wd_ref, y_ref,
                      rows, ybuf, hb_scr, acc_scr, gt_scr, gsem, ssem, *, t, epg):
    i = pl.program_id(0)
    nt = pl.num_programs(0) - 1
    s = lax.rem(i, 2)
    de = wd_ref.shape[1]
    sub = rows.shape[2]
    part = t // epg

    def gather_rows(tile, slot, lo, hi):
        def body(r, carry):
            pltpu.make_async_copy(h_ref.at[src_ref[tile * t + r]], rows.at[slot, r], gsem.at[slot]).start()
            return carry
        lax.fori_loop(lo, hi, body, 0, unroll=8)

    def scatter_rows(tile, slot, lo, hi):
        def body(r, carry):
            pltpu.make_async_copy(ybuf.at[slot, r], y_ref.at[dst_ref[tile * t + r]], ssem.at[slot]).start()
            return carry
        lax.fori_loop(lo, hi, body, 0, unroll=8)

    def wait_gather(slot):
        pltpu.make_async_copy(h_ref.at[pl.ds(0, t)], rows.at[slot], gsem.at[slot]).wait()

    def wait_scatter(slot):
        pltpu.make_async_copy(ybuf.at[slot], y_ref.at[pl.ds(0, t)], ssem.at[slot]).wait()

    @pl.when(i == 0)
    def _():
        gt_scr[...] = jnp.zeros_like(gt_scr)
        gather_rows(0, 0, 0, t)

    @pl.when(i >= 2)
    def _():
        wait_scatter(s)

    @pl.when(i < nt)
    def _():
        wait_gather(s)
        hb_scr[...] = jnp.concatenate([rows[s, :, j, :] for j in range(sub)], axis=-1).astype(BF16)
        tok_id = lax.broadcasted_iota(jnp.int32, (1, t), 1)
        gate_rows = _expert_gates(_router_logits(hb_scr[...], wr_ref, br_ref), tg_ref[i])
        for e in range(epg):
            gt_scr[e:e + 1, :] = jnp.where(tok_id < nv_ref[i], gate_rows[e], 0.0)
        gates = gt_scr[...].T
        for e in range(epg):
            @pl.when(i + 1 < nt)
            def _():
                gather_rows(i + 1, 1 - s, e * part, (e + 1) * part)

            @pl.when(i >= 1)
            def _():
                scatter_rows(i - 1, 1 - s, e * part, (e + 1) * part)

            gu = _dot(hb_scr[...], wgu_ref[e])
            hid = jax.nn.silu(gu[:, :de]) * gu[:, de:]
            y = gates[:, e:e + 1] * _dot(hid.astype(BF16), wd_ref[e])
            if e == 0:
                acc_scr[...] = y
            else:
                acc_scr[...] += y
        for j in range(sub):
            ybuf[s, :, j, :] = acc_scr[:, j * LANES:(j + 1) * LANES]

    @pl.when(i == nt)
    def _():
        scatter_rows(nt - 1, 1 - s, 0, t)
        wait_scatter(1 - s)


def _moe_fused(tile_group, n_valid, src, dst, h2t, w_router, b_router, wgu, wd, n_tiles, t):
    epg = wgu.shape[1]
    last = n_tiles - 1
    sub = h2t.shape[1]
    d = sub * LANES
    full = lambda a: pl.BlockSpec(a.shape, lambda i, tg, nv, sr, ds: (0,) * a.ndim)
    by_group = lambda a: pl.BlockSpec((None,) + a.shape[1:], lambda i, tg, nv, sr, ds: (tg[jnp.minimum(i, last)], 0, 0, 0))
    return pl.pallas_call(
        functools.partial(_moe_fused_kernel, t=t, epg=epg),
        grid_spec=pltpu.PrefetchScalarGridSpec(
            num_scalar_prefetch=4, grid=(n_tiles + 1,),
            in_specs=[pl.BlockSpec(memory_space=pl.ANY), full(w_router), full(b_router), by_group(wgu), by_group(wd)],
            out_specs=pl.BlockSpec(memory_space=pl.ANY),
            scratch_shapes=[pltpu.VMEM((2, t, sub, LANES), F32), pltpu.VMEM((2, t, sub, LANES), F32),
                            pltpu.VMEM((t, d), BF16), pltpu.VMEM((t, d), F32), pltpu.VMEM((LANES, t), F32),
                            pltpu.SemaphoreType.DMA((2,)), pltpu.SemaphoreType.DMA((2,))]),
        out_shape=jax.ShapeDtypeStruct((n_tiles * t, sub, LANES), F32),
        compiler_params=_cparams("arbitrary"),
    )(tile_group, n_valid, src, dst, h2t, w_router, b_router, wgu, wd)


def _final_kernel(x_ref, y_ref, mod_ref, g_ref, o_ref):
    out = x_ref[...] + mod_ref[5:6, :] * _from_token_tiles(y_ref)
    ms = jnp.mean(out * out, axis=-1, keepdims=True)
    o_ref[...] = out * lax.rsqrt(ms + EPS) * g_ref[...]


def _final(x, y_rows, mod, g_final):
    b, l, d = x.shape
    t = min(512, l)
    nt = l // t
    return pl.pallas_call(
        _final_kernel,
        grid=(b, nt),
        in_specs=[pl.BlockSpec((None, t, d), lambda bi, i: (bi, i, 0)),
                  pl.BlockSpec((t,) + y_rows.shape[1:], lambda bi, i: (bi * nt + i, 0, 0)),
                  pl.BlockSpec((None, N_MOD, d), lambda bi, i: (bi, 0, 0)),
                  pl.BlockSpec((1, d), lambda bi, i: (0, 0))],
        out_specs=pl.BlockSpec((None, t, d), lambda bi, i: (bi, i, 0)),
        out_shape=jax.ShapeDtypeStruct((b, l, d), F32),
        compiler_params=_cparams("parallel", "parallel"),
    )(x, y_rows, mod, g_final)


def kernel(x, c, ctx, c_ctx, mod_w, mod_b, norm_mix_g, norm_ffn_g, w_in, conv_w, conv_b, rec_gate_a_w, rec_gate_a_b,
           rec_gate_x_w, rec_gate_x_b, rec_lambda, pool_w, pool_scale, fourier_w, w_out, router_group_w,
           router_group_b, router_expert_w, router_expert_b, expert_w_gate, expert_w_up, expert_w_down, final_norm_g):
    b, l, d = x.shape
    depth = mod_w.shape[0]
    d_rec = conv_w.shape[-1]
    d_pool = pool_scale.shape[-1]
    d_four = fourier_w.shape[1] * fourier_w.shape[2]
    widths = (d_rec, d_rec, d_pool, d_four)
    rows = l // GRID_W
    heads_half = REC_HEADS // 2
    n_exp = expert_w_gate.shape[1]

    cvec = jnp.concatenate([c, c_ctx[None, :], jnp.zeros((SUBLANES - b - 1, d), F32)], axis=0)
    mod_all = _modulation(cvec, mod_w, mod_b).reshape(depth, SUBLANES, N_MOD, d)
    g_final = final_norm_g.reshape(1, d)
    zeros_state = jnp.zeros((b, 2, d_rec), F32)
    pending = pending_ctx = None

    for li in range(depth):
        last = li == depth - 1
        mod_lat = mod_all[li, :b]
        mod_ctx = jnp.broadcast_to(mod_all[li, b][None], (b, N_MOD, d))
        w_in_l = w_in[li].astype(BF16)
        w_out_l = w_out[li].astype(BF16)
        wgate = jnp.stack([
            jnp.stack([jnp.concatenate([_block_diag(rec_gate_a_w[li, dd, hh * heads_half:(hh + 1) * heads_half]),
                                        _block_diag(rec_gate_x_w[li, dd, hh * heads_half:(hh + 1) * heads_half])], axis=1)
                       for hh in range(2)]) for dd in range(2)]).astype(BF16)
        bgate = jnp.stack([rec_gate_a_b[li], rec_gate_x_b[li]], axis=1)
        rec_p = (wgate, bgate, rec_lambda[li])
        proj_p = (w_in_l, conv_w[li], conv_b[li].reshape(1, d_rec), widths)
        pool_w_bd = _block_diag(pool_w[li]).astype(BF16)
        four_w_bd = _block_diag(fourier_w[li]).astype(BF16)
        w_router = jnp.concatenate([router_group_w[li], jnp.zeros((d, SUBLANES - N_GROUPS), F32), router_expert_w[li],
                                    jnp.zeros((d, LANES - SUBLANES - n_exp), F32)], axis=1).astype(BF16)
        b_router = jnp.concatenate([router_group_b[li], jnp.zeros((SUBLANES - N_GROUPS,), F32), router_expert_b[li],
                                    jnp.zeros((LANES - SUBLANES - n_exp,), F32)]).reshape(1, LANES)
        wgu = jnp.concatenate([expert_w_gate[li], expert_w_up[li]], axis=-1).astype(BF16)
        wd = expert_w_down[li].astype(BF16)
        wgu_grp = wgu.reshape((N_GROUPS, EXPERTS_PER_GROUP) + wgu.shape[1:])
        wd_grp = wd.reshape((N_GROUPS, EXPERTS_PER_GROUP) + wd.shape[1:])
        g_ffn = norm_ffn_g[li].reshape(1, d)

        def mixer_tail(hf, hb, z_g, z_p, z_f, stream, mod, grid_rows):
            y_p = _pool(z_p, pool_w_bd, pool_scale[li], grid_rows)
            y_f = _fourier(z_f, four_w_bd)
            return _combine(hf, hb, z_g, y_p, y_f, stream, mod, w_out_l, g_ffn, w_router, b_router)

        def expert_stage(h2t, gid):
            n_tok = gid.size
            t_moe = min(MOE_TILE, h2t.shape[1])
            slot, tile_group, n_tiles = _sort_plan(gid.reshape(n_tok), N_GROUPS, t_moe)
            src = _invert(slot, n_tiles * t_moe)
            valid = src >= 0
            pad_rank = jnp.cumsum(jnp.logical_not(valid).astype(jnp.int32)) - 1
            dst = jnp.where(valid, src, n_tok + pad_rank)
            n_valid = jnp.sum(valid.reshape(n_tiles, t_moe).astype(jnp.int32), axis=1)
            return _moe_fused(tile_group, n_valid, jnp.maximum(src, 0), dst, h2t.reshape((n_tok,) + h2t.shape[2:]),
                              w_router, b_router, wgu_grp, wd_grp, n_tiles, t_moe)

        if pending_ctx is None:
            uc, zc_g, zc_p, zc_f = _inproj(ctx, norm_mix_g[li], mod_ctx, *proj_p)
        else:
            ctx, uc, zc_g, zc_p, zc_f = _inproj(ctx, norm_mix_g[li], mod_ctx, *proj_p, residual=pending_ctx)
        hf_c, hb_c = _recurrent(uc, *rec_p, zeros_state)
        state = jnp.stack([hf_c[:, -1, :], hb_c[:, 0, :]], axis=1)
        if not last:
            ctx, h2t_c, gid_c = mixer_tail(hf_c, hb_c, zc_g, zc_p, zc_f, ctx, mod_ctx, None)
            pending_ctx = (expert_stage(h2t_c, gid_c), 0, mod_ctx)

        if pending is None:
            u, z_g, z_p, z_f = _inproj(x, norm_mix_g[li], mod_lat, *proj_p)
        else:
            x, u, z_g, z_p, z_f = _inproj(x, norm_mix_g[li], mod_lat, *proj_p, residual=pending)
        hf, hb = _recurrent(u, *rec_p, state)
        x, h2t, gid = mixer_tail(hf, hb, z_g, z_p, z_f, x, mod_lat, rows)
        pending = (expert_stage(h2t, gid), 0, mod_lat)
    return _final(x, pending[0], pending[2], g_final)
```

```python
import functools
import math

import jax
import jax.numpy as jnp
from jax import lax
from jax.experimental import pallas as pl
from jax.experimental.pallas import tpu as pltpu

F32 = jnp.float32
BF16 = jnp.bfloat16

GRID_W = 64
N_MOD = 6
REC_HEADS = 8
CONV_W = 4
LRU_C = 8.0
POOL_WINDOWS = (2, 4, 8, 16)
FOUR_HEADS = 4
N_GROUPS = 4
EXPERTS_PER_GROUP = 4
EPS = 1e-6

LANES = 128
SUBLANES = 8
POOL_PAD = 8
VMEM_LIMIT = 56 * 1024 * 1024


def _cparams(*sem):
    return pltpu.CompilerParams(dimension_semantics=sem, vmem_limit_bytes=VMEM_LIMIT)


def _split_bf16(a):
    hi = a.astype(BF16)
    lo = (a - hi.astype(F32)).astype(BF16)
    return hi, lo


def _dot(a, b):
    return jnp.dot(a, b, preferred_element_type=F32)


def _sigmoid(x):
    return 0.5 * jnp.tanh(0.5 * x) + 0.5


def _mod_kernel(s_ref, w_ref, b_ref, o_ref):
    s = s_ref[...]
    s = s * jax.nn.sigmoid(s)
    s_hi, s_lo = _split_bf16(s)
    w_hi, w_lo = _split_bf16(w_ref[...])
    o_ref[...] = _dot(s_hi, w_hi) + _dot(s_hi, w_lo) + _dot(s_lo, w_hi) + b_ref[...]


def _modulation(cvec, mod_w, mod_b):
    depth, d, dm = mod_w.shape
    tn = dm // 4
    return pl.pallas_call(
        _mod_kernel,
        grid=(depth, dm // tn),
        in_specs=[pl.BlockSpec((SUBLANES, d), lambda l, j: (0, 0)),
                  pl.BlockSpec((None, d, tn), lambda l, j: (l, 0, j)),
                  pl.BlockSpec((None, 1, tn), lambda l, j: (l, 0, j))],
        out_specs=pl.BlockSpec((None, SUBLANES, tn), lambda l, j: (l, 0, j)),
        out_shape=jax.ShapeDtypeStruct((depth, SUBLANES, dm), F32),
        compiler_params=_cparams("parallel", "parallel"),
    )(cvec, mod_w, mod_b.reshape(depth, 1, dm))


def _rms_mod(x, g, shift, scale):
    ms = jnp.mean(x * x, axis=-1, keepdims=True)
    y = x * lax.rsqrt(ms + EPS) * g
    return y * (1.0 + scale) + shift


def _from_token_tiles(ref):
    return jnp.concatenate([ref[:, j, :] for j in range(ref.shape[1])], axis=-1)


def _inproj_kernel(*refs, widths, with_residual, n_tiles):
    i = pl.program_id(1)
    if with_residual:
        (x_ref, xp_ref, xn_ref, y_ref, yp_ref, yn_ref, pmod_ref,
         g_ref, mod_ref, w_ref, cw_ref, cb_ref, xo_ref, *o_refs) = refs
        gate = pmod_ref[5:6, :]
        x = x_ref[...] + gate * _from_token_tiles(y_ref)
        xo_ref[...] = x
        x_halo = jnp.concatenate([xp_ref[...] + gate * _from_token_tiles(yp_ref),
                                  xn_ref[...] + gate * _from_token_tiles(yn_ref)], axis=0)
    else:
        x_ref, xp_ref, xn_ref, g_ref, mod_ref, w_ref, cw_ref, cb_ref, *o_refs = refs
        x = x_ref[...]
        x_halo = jnp.concatenate([xp_ref[...], xn_ref[...]], axis=0)
    t = x.shape[0]
    d_rec = widths[0]
    z = _dot(_rms_mod(x, g_ref[...], mod_ref[0:1, :], mod_ref[1:2, :]).astype(BF16), w_ref[...])
    zh = _dot(_rms_mod(x_halo, g_ref[...], mod_ref[0:1, :], mod_ref[1:2, :]).astype(BF16), w_ref[:, :d_rec])
    zz = jnp.concatenate([jnp.where(i == 0, 0.0, zh[:SUBLANES]), z[:, :d_rec],
                          jnp.where(i == n_tiles - 1, 0.0, zh[SUBLANES:])], axis=0)
    u = cb_ref[...]
    for k in range(CONV_W):
        o = SUBLANES - CONV_W // 2 + k
        u = u + cw_ref[k:k + 1, :] * zz[o:o + t, :]
    o_refs[0][...] = u
    off = d_rec
    for o_ref, wd in zip(o_refs[1:], widths[1:]):
        o_ref[...] = z[:, off:off + wd]
        off += wd


def _inproj(x, g, mod, w_bf16, conv_w, conv_b, widths, residual=None):
    b, l, d = x.shape
    t = min(512, l)
    nt = l // t
    per = t // SUBLANES
    n8 = l // SUBLANES
    tok = lambda c: pl.BlockSpec((None, t, c), lambda bi, i: (bi, i, 0))
    prev8 = lambda bi, i: jnp.maximum(i * per - 1, 0)
    next8 = lambda bi, i: jnp.minimum((i + 1) * per, n8 - 1)
    x_specs = [tok(d), pl.BlockSpec((None, SUBLANES, d), lambda bi, i: (bi, prev8(bi, i), 0)),
               pl.BlockSpec((None, SUBLANES, d), lambda bi, i: (bi, next8(bi, i), 0))]
    modspec = pl.BlockSpec((None, N_MOD, d), lambda bi, i: (bi, 0, 0))
    full = lambda a: pl.BlockSpec(a.shape, lambda bi, i: (0,) * a.ndim)
    tail_specs = [pl.BlockSpec((1, d), lambda bi, i: (0, 0)), modspec, full(w_bf16), full(conv_w), full(conv_b)]
    tail_args = (g.reshape(1, d), mod, w_bf16, conv_w, conv_b)
    z_specs = [tok(wd) for wd in widths]
    z_shapes = [jax.ShapeDtypeStruct((b, l, wd), F32) for wd in widths]
    if residual is None:
        in_specs, args, out_specs, out_shape = x_specs + tail_specs, (x, x, x) + tail_args, z_specs, z_shapes
    else:
        y_rows, row0, prev_mod = residual
        sub = y_rows.shape[1]
        blk0, blk8 = row0 // t, row0 // SUBLANES
        y_specs = [pl.BlockSpec((t, sub, LANES), lambda bi, i: (blk0 + bi * nt + i, 0, 0)),
                   pl.BlockSpec((SUBLANES, sub, LANES), lambda bi, i: (blk8 + bi * n8 + prev8(bi, i), 0, 0)),
                   pl.BlockSpec((SUBLANES, sub, LANES), lambda bi, i: (blk8 + bi * n8 + next8(bi, i), 0, 0))]
        in_specs = x_specs + y_specs + [modspec] + tail_specs
        args = (x, x, x, y_rows, y_rows, y_rows, prev_mod) + tail_args
        out_specs, out_shape = [tok(d)] + z_specs, [jax.ShapeDtypeStruct((b, l, d), F32)] + z_shapes
    return pl.pallas_call(
        functools.partial(_inproj_kernel, widths=widths, with_residual=residual is not None, n_tiles=nt),
        grid=(b, nt), in_specs=in_specs, out_specs=out_specs, out_shape=out_shape,
        compiler_params=_cparams("parallel", "parallel"),
    )(*args)


def _rec_kernel(uf_ref, ub_ref, wg_ref, bg_ref, lam_ref, h0_ref, hf_ref, hb_ref, a_scr, carry_scr, *, tc):
    i = pl.program_id(1)
    c = uf_ref.shape[-1]
    half = c // 2
    n_grp = tc // SUBLANES

    @pl.when(i == 0)
    def _():
        carry_scr[0] = jnp.broadcast_to(h0_ref[0:1, :], (SUBLANES, c))
        carry_scr[1] = jnp.broadcast_to(h0_ref[1:2, :], (SUBLANES, c))

    row = lax.broadcasted_iota(jnp.int32, (1, SUBLANES, half), 1)

    def direction(d, u_ref, out_ref):
        u = u_ref[...]
        log_decay = -LRU_C * jax.nn.softplus(-lam_ref[d:d + 1, :])
        for hh in range(2):
            cols = slice(hh * half, (hh + 1) * half)
            uh = u[:, cols]
            pre = _dot(uh.astype(BF16), wg_ref[d, hh])
            r = _sigmoid(pre[:, :half] + bg_ref[d, 0:1, cols])
            ig = _sigmoid(pre[:, half:] + bg_ref[d, 1:2, cols])
            log_a = log_decay[:, cols] * r
            a = jnp.exp(log_a)
            th = jnp.tanh(log_a)
            bt = jnp.sqrt(-2.0 * th / (1.0 - th)) * (ig * uh)
            a = a.reshape(n_grp, SUBLANES, half)
            bt = bt.reshape(n_grp, SUBLANES, half)
            for s in (1, 2, 4):
                shift, keep = (s, row >= s) if d == 0 else (SUBLANES - s, row < SUBLANES - s)
                a_sh, b_sh = pltpu.roll(a, shift, 1), pltpu.roll(bt, shift, 1)
                bt = a * jnp.where(keep, b_sh, 0.0) + bt
                a = a * jnp.where(keep, a_sh, 1.0)
            a_scr[d, :, cols] = a.reshape(tc, half)
            out_ref[:, cols] = bt.reshape(tc, half)

    direction(0, uf_ref, hf_ref)
    direction(1, ub_ref, hb_ref)

    def body(gi, carry):
        cf, cb = carry
        rf = pl.multiple_of(gi * SUBLANES, SUBLANES)
        rb = pl.multiple_of((n_grp - 1 - gi) * SUBLANES, SUBLANES)
        hf = hf_ref[pl.ds(rf, SUBLANES), :] + a_scr[0, pl.ds(rf, SUBLANES), :] * cf
        hb = hb_ref[pl.ds(rb, SUBLANES), :] + a_scr[1, pl.ds(rb, SUBLANES), :] * cb
        hf_ref[pl.ds(rf, SUBLANES), :] = hf
        hb_ref[pl.ds(rb, SUBLANES), :] = hb
        return (jnp.broadcast_to(hf[SUBLANES - 1:SUBLANES, :], (SUBLANES, c)),
                jnp.broadcast_to(hb[0:1, :], (SUBLANES, c)))

    cf, cb = lax.fori_loop(0, n_grp, body, (carry_scr[0], carry_scr[1]))
    carry_scr[0] = cf
    carry_scr[1] = cb


def _recurrent(u, wgate, bgate, lam, h0):
    b, l, c = u.shape
    tc = min(1024, l)
    nc = l // tc
    fwd = pl.BlockSpec((None, tc, c), lambda bi, i: (bi, i, 0))
    bwd = pl.BlockSpec((None, tc, c), lambda bi, i: (bi, nc - 1 - i, 0))
    full = lambda a: pl.BlockSpec(a.shape, lambda bi, i: (0,) * a.ndim)
    return pl.pallas_call(
        functools.partial(_rec_kernel, tc=tc),
        grid=(b, nc),
        in_specs=[fwd, bwd, full(wgate), full(bgate), full(lam), pl.BlockSpec((None, 2, c), lambda bi, i: (bi, 0, 0))],
        out_specs=[fwd, bwd],
        out_shape=[jax.ShapeDtypeStruct((b, l, c), F32)] * 2,
        scratch_shapes=[pltpu.VMEM((2, tc, c), F32), pltpu.VMEM((2, SUBLANES, c), F32)],
        compiler_params=_cparams("parallel", "arbitrary"),
    )(u, u, wgate, bgate, lam, h0)


def _dft_tables(n):
    j = lax.broadcasted_iota(jnp.int32, (n, n), 0)
    k = lax.broadcasted_iota(jnp.int32, (n, n), 1)
    ang = ((j * k) % n).astype(F32) * (2.0 * math.pi / n)
    return jnp.cos(ang), jnp.sin(ang)


def _block_diag(w):
    return jax.scipy.linalg.block_diag(*[w[h] for h in range(w.shape[0])])


def _channel_stage(gr, gi, c_ref, s_ref, w_ref, scale):
    f = (_dot(gr.astype(BF16), c_ref[...]) + _dot(gi.astype(BF16), s_ref[...])) * scale
    return _dot(f.astype(BF16), w_ref[...])


def _four1_kernel(x_ref, f1_ref, twc_ref, tws_ref, ar_ref, ai_ref, *, n1):
    for jj in range(SUBLANES):
        a = _dot(f1_ref[...], x_ref[:, jj, :].astype(BF16))
        ar, ai = a[:n1], a[n1:]
        cc = twc_ref[:, jj:jj + 1]
        ss = tws_ref[:, jj:jj + 1]
        ar_ref[:, jj, :] = ar * cc + ai * ss
        ai_ref[:, jj, :] = ai * cc - ar * ss


def _four2_kernel(ar_ref, ai_ref, f2_ref, c_ref, s_ref, w_ref, o_ref, *, n2, scale):
    grs, gis = [], []
    for kk in range(SUBLANES):
        slab = jnp.concatenate([ar_ref[kk], ai_ref[kk]], axis=0).astype(BF16)
        g = _dot(f2_ref[...], slab)
        grs.append(g[:n2])
        gis.append(g[n2:])
    y = _channel_stage(jnp.concatenate(grs, axis=0), jnp.concatenate(gis, axis=0), c_ref, s_ref, w_ref, scale)
    for kk in range(SUBLANES):
        o_ref[:, kk, :] = y[kk * n2:(kk + 1) * n2]


def _four_small_kernel(x_ref, f_ref, c_ref, s_ref, w_ref, o_ref, *, n, scale):
    g = _dot(f_ref[...], x_ref[...].astype(BF16))
    o_ref[...] = _channel_stage(g[:n], g[n:], c_ref, s_ref, w_ref, scale)


def _fourier(z_f, four_w_bd):
    b, l, nch = z_f.shape
    hd = nch // FOUR_HEADS
    scale = 1.0 / math.sqrt(l * hd)
    c_h, s_h = _dft_tables(hd)
    eye = jnp.eye(FOUR_HEADS, dtype=F32)
    c_bd = jnp.kron(eye, c_h).astype(BF16)
    s_bd = jnp.kron(eye, s_h).astype(BF16)
    full = lambda a, nd: pl.BlockSpec(a.shape, lambda *_: (0,) * a.ndim)
    if l <= 256:
        cn, sn = _dft_tables(l)
        f = jnp.concatenate([cn, -sn], axis=0).astype(BF16)
        return pl.pallas_call(
            functools.partial(_four_small_kernel, n=l, scale=scale),
            grid=(b,),
            in_specs=[pl.BlockSpec((None, l, nch), lambda bi: (bi, 0, 0)),
                      full(f, 1), full(c_bd, 1), full(s_bd, 1), full(four_w_bd, 1)],
            out_specs=pl.BlockSpec((None, l, nch), lambda bi: (bi, 0, 0)),
            out_shape=jax.ShapeDtypeStruct((b, l, nch), F32),
            compiler_params=_cparams("parallel"),
        )(z_f, f, c_bd, s_bd, four_w_bd)

    n2 = LANES
    n1 = l // n2
    nj = n2 // SUBLANES
    c1, s1 = _dft_tables(n1)
    f1 = jnp.concatenate([c1, -s1], axis=0).astype(BF16)
    c2, s2 = _dft_tables(n2)
    f2 = jnp.concatenate([jnp.concatenate([c2, s2], axis=1),
                          jnp.concatenate([-s2, c2], axis=1)], axis=0).astype(BF16)
    k1 = lax.broadcasted_iota(jnp.int32, (n1, n2), 0)
    t2 = lax.broadcasted_iota(jnp.int32, (n1, n2), 1)
    ang = (k1 * t2).astype(F32) * (2.0 * math.pi / l)
    twc = jnp.cos(ang).reshape(n1, nj, SUBLANES).transpose(1, 0, 2)
    tws = jnp.sin(ang).reshape(n1, nj, SUBLANES).transpose(1, 0, 2)
    ar, ai = pl.pallas_call(
        functools.partial(_four1_kernel, n1=n1),
        grid=(b, nj),
        in_specs=[pl.BlockSpec((None, n1, SUBLANES, nch), lambda bi, j: (bi, 0, j, 0)),
                  full(f1, 2),
                  pl.BlockSpec((None, n1, SUBLANES), lambda bi, j: (j, 0, 0)),
                  pl.BlockSpec((None, n1, SUBLANES), lambda bi, j: (j, 0, 0))],
        out_specs=[pl.BlockSpec((None, n1, SUBLANES, nch), lambda bi, j: (bi, 0, j, 0))] * 2,
        out_shape=[jax.ShapeDtypeStruct((b, n1, n2, nch), F32)] * 2,
        compiler_params=_cparams("parallel", "parallel"),
    )(z_f.reshape(b, n1, n2, nch), f1, twc, tws)
    y = pl.pallas_call(
        functools.partial(_four2_kernel, n2=n2, scale=scale),
        grid=(b, n1 // SUBLANES),
        in_specs=[pl.BlockSpec((None, SUBLANES, n2, nch), lambda bi, j: (bi, j, 0, 0)),
                  pl.BlockSpec((None, SUBLANES, n2, nch), lambda bi, j: (bi, j, 0, 0)),
                  full(f2, 2), full(c_bd, 2), full(s_bd, 2), full(four_w_bd, 2)],
        out_specs=pl.BlockSpec((None, n2, SUBLANES, nch), lambda bi, j: (bi, 0, j, 0)),
        out_shape=jax.ShapeDtypeStruct((b, n2, n1, nch), F32),
        compiler_params=_cparams("parallel", "parallel"),
    )(ar, ai, f2, c_bd, s_bd, four_w_bd)
    return y.reshape(b, l, nch)


def _window_counts(idx, n, w):
    return (jnp.minimum(idx + (w - w // 2), n) - jnp.maximum(idx - w // 2, 0)).astype(F32)


def _pool_kernel(cur_ref, prev_ref, next_ref, pw_ref, ps_ref, o_ref, *, r, w, rows_total, n_tiles):
    i = pl.program_id(1)
    two_d = rows_total > 1
    halo = SUBLANES if two_d else 0
    nr = r + 2 * halo
    stride = w + POOL_PAD
    flat = nr * stride
    lane = lax.broadcasted_iota(jnp.int32, (1, 1, LANES), 2)
    low = lane < (LANES // 2)
    col = lax.broadcasted_iota(jnp.int32, (1, w, 1), 1)
    grow = lax.broadcasted_iota(jnp.int32, (r, 1, 1), 0) + i * r
    outs = []
    for hh in range(2):
        cols = slice(hh * LANES, (hh + 1) * LANES)
        w_lo, w_hi = POOL_WINDOWS[2 * hh], POOL_WINDOWS[2 * hh + 1]
        cur = cur_ref[:, cols]
        if two_d:
            prev = jnp.where(i == 0, 0.0, prev_ref[:, cols])
            nxt = jnp.where(i == n_tiles - 1, 0.0, next_ref[:, cols])
            ext = jnp.concatenate([prev, cur, nxt], axis=0)
        else:
            ext = cur
        ext = ext.reshape(nr, w, LANES)
        x = jnp.concatenate([jnp.zeros((nr, POOL_PAD, LANES), F32), ext], axis=1).reshape(flat, LANES)
        p = x + pltpu.roll(x, 1, 0)
        sums = {2: p}
        for ww, s in ((4, 1), (8, 2), (16, 4)):
            if ww > w_hi:
                break
            p = pltpu.roll(p, s, 0) + pltpu.roll(p, flat - s, 0)
            sums[ww] = p
        y = jnp.where(low[0], sums[w_lo], sums[w_hi]).reshape(nr, stride, LANES)[:, POOL_PAD:, :]
        if two_d:
            q2 = y[0:nr - 1] + y[1:nr]
            rsum = {2: q2[7:7 + r]}
            q4 = q2[0:nr - 3] + q2[2:nr - 1]
            rsum[4] = q4[6:6 + r]
            if w_hi > 4:
                q8 = q4[0:nr - 7] + q4[4:nr - 3]
                rsum[8] = q8[4:4 + r]
                q16 = q8[0:nr - 15] + q8[8:nr - 7]
                rsum[16] = q16[0:r]
            tot = jnp.where(low, rsum[w_lo], rsum[w_hi])
            cnt_r = jnp.where(low, _window_counts(grow, rows_total, w_lo), _window_counts(grow, rows_total, w_hi))
            tot = tot / cnt_r
        else:
            tot = y
        cnt_c = jnp.where(low, _window_counts(col, w, w_lo), _window_counts(col, w, w_hi))
        pooled = (tot / cnt_c).reshape(r * w, LANES)
        outs.append(pooled - cur)
    d = jnp.concatenate(outs, axis=-1)
    o_ref[...] = (_dot(d.astype(BF16), pw_ref[...]) * ps_ref[...]).astype(o_ref.dtype)


def _pool(z_p, pool_w_bd, pool_scale, rows):
    b, l, nch = z_p.shape
    if rows is None:
        rows_total, w, r = 1, l, 1
    else:
        rows_total, w, r = rows, l // rows, min(16, rows)
    t = r * w
    n_tiles = l // t
    hb = SUBLANES * w if rows is not None else t
    per = t // hb
    nhb = l // hb
    return pl.pallas_call(
        functools.partial(_pool_kernel, r=r, w=w, rows_total=rows_total, n_tiles=n_tiles),
        grid=(b, n_tiles),
        in_specs=[pl.BlockSpec((None, t, nch), lambda bi, i: (bi, i, 0)),
                  pl.BlockSpec((None, hb, nch), lambda bi, i: (bi, jnp.maximum(i * per - 1, 0), 0)),
                  pl.BlockSpec((None, hb, nch), lambda bi, i: (bi, jnp.minimum((i + 1) * per, nhb - 1), 0)),
                  pl.BlockSpec(pool_w_bd.shape, lambda bi, i: (0, 0)),
                  pl.BlockSpec((1, nch), lambda bi, i: (0, 0))],
        out_specs=pl.BlockSpec((None, t, nch), lambda bi, i: (bi, i, 0)),
        out_shape=jax.ShapeDtypeStruct((b, l, nch), BF16),
        compiler_params=_cparams("parallel", "parallel"),
    )(z_p, z_p, z_p, pool_w_bd, pool_scale.reshape(1, nch))


def _first_argmax4(v):
    m = jnp.maximum(jnp.maximum(v[0], v[1]), jnp.maximum(v[2], v[3]))
    idx = jnp.where(v[0] >= m, 0, jnp.where(v[1] >= m, 1, jnp.where(v[2] >= m, 2, 3)))
    return m, idx


def _select4(idx, v):
    return jnp.where(idx == 0, v[0], jnp.where(idx == 1, v[1], jnp.where(idx == 2, v[2], v[3])))


def _router_logits(h_bf16, wr_ref, br_ref):
    return (_dot(h_bf16, wr_ref[...]) + br_ref[...]).T


def _group_logits(lt):
    return [lt[g:g + 1, :] for g in range(N_GROUPS)]


def _expert_gates(lt, group):
    lg = _group_logits(lt)
    m, _ = _first_argmax4(lg)
    den = sum(jnp.exp(v - m) for v in lg)
    p_group = jnp.exp(_select4(group, lg) - m) / den
    sel = [_select4(group, [lt[SUBLANES + EXPERTS_PER_GROUP * g + e:SUBLANES + EXPERTS_PER_GROUP * g + e + 1, :]
                            for g in range(N_GROUPS)]) for e in range(EXPERTS_PER_GROUP)]
    v1, e1 = _first_argmax4(sel)
    rest = [jnp.where(e1 == e, -jnp.inf, sel[e]) for e in range(EXPERTS_PER_GROUP)]
    v2, e2 = _first_argmax4(rest)
    t2 = jnp.exp(v2 - v1)
    w1 = p_group / (1.0 + t2)
    w2 = p_group * t2 / (1.0 + t2)
    return [jnp.where(e1 == e, w1, jnp.where(e2 == e, w2, 0.0)) for e in range(EXPERTS_PER_GROUP)]


def _combine_kernel(hf_ref, hb_ref, zg_ref, yp_ref, yf_ref, x_ref, mod_ref, wo_ref, g_ref, wr_ref, br_ref,
                    xo_ref, h2t_ref, gid_ref):
    y_a = (hf_ref[...] + hb_ref[...]) * jax.nn.gelu(zg_ref[...], approximate=True)
    ycat = jnp.concatenate([y_a.astype(BF16), yp_ref[...].astype(BF16), yf_ref[...].astype(BF16)], axis=-1)
    x_new = x_ref[...] + mod_ref[2:3, :] * _dot(ycat, wo_ref[...])
    xo_ref[...] = x_new
    h2 = _rms_mod(x_new, g_ref[...], mod_ref[3:4, :], mod_ref[4:5, :])
    for j in range(h2t_ref.shape[1]):
        h2t_ref[:, j, :] = h2[:, j * LANES:(j + 1) * LANES]
    _, gidx = _first_argmax4(_group_logits(_router_logits(h2.astype(BF16), wr_ref, br_ref)))
    gid_ref[...] = gidx


def _combine(hf, hb, z_g, y_p, y_f, x, mod, w_out, g_ffn, w_router, b_router):
    b, l, d = x.shape
    t = min(512, l)
    sub = d // LANES
    tok = lambda c: pl.BlockSpec((None, t, c), lambda bi, i: (bi, i, 0))
    full = lambda a: pl.BlockSpec(a.shape, lambda bi, i: (0,) * a.ndim)
    return pl.pallas_call(
        _combine_kernel,
        grid=(b, l // t),
        in_specs=[tok(hf.shape[-1]), tok(hb.shape[-1]), tok(z_g.shape[-1]), tok(y_p.shape[-1]), tok(y_f.shape[-1]),
                  tok(d), pl.BlockSpec((None, N_MOD, d), lambda bi, i: (bi, 0, 0)),
                  full(w_out), full(g_ffn), full(w_router), full(b_router)],
        out_specs=[tok(d), pl.BlockSpec((None, t, sub, LANES), lambda bi, i: (bi, i, 0, 0)),
                   pl.BlockSpec((None, 1, t), lambda bi, i: (bi, 0, i))],
        out_shape=[jax.ShapeDtypeStruct((b, l, d), F32), jax.ShapeDtypeStruct((b, l, sub, LANES), F32),
                   jax.ShapeDtypeStruct((b, 1, l), jnp.int32)],
        compiler_params=_cparams("parallel", "parallel"),
    )(hf, hb, z_g, y_p, y_f, x, mod, w_out, g_ffn, w_router, b_router)


MOE_TILE = 512


def _sort_plan(gid, n_groups, t):
    n = gid.shape[0]
    onehot = (gid[:, None] == jnp.arange(n_groups, dtype=jnp.int32)[None, :]).astype(jnp.int32)
    csum = jnp.cumsum(onehot, axis=0)
    rank = jnp.sum((csum - onehot) * onehot, axis=1)
    tiles = (csum[-1] + t - 1) // t
    tile_end = jnp.cumsum(tiles)
    slot = jnp.sum(onehot * (tile_end - tiles)[None, :], axis=1) * t + rank
    n_tiles = n // t + n_groups
    tile_idx = jnp.arange(n_tiles, dtype=jnp.int32)
    tile_group = jnp.minimum(jnp.sum((tile_idx[:, None] >= tile_end[None, :]).astype(jnp.int32), axis=1), n_groups - 1)
    return slot.astype(jnp.int32), tile_group.astype(jnp.int32), n_tiles


def _invert_kernel(slot_ref, src_ref):
    n_rows = src_ref.shape[0]
    n = slot_ref.shape[0]

    def clear(s, carry):
        src_ref[s] = -1
        return carry

    lax.fori_loop(0, n_rows, clear, 0, unroll=8)

    def put(tok, carry):
        src_ref[slot_ref[tok]] = tok
        return carry

    lax.fori_loop(0, n, put, 0, unroll=8)


def _invert(slot, n_rows):
    return pl.pallas_call(
        _invert_kernel,
        in_specs=[pl.BlockSpec(memory_space=pltpu.SMEM)],
        out_specs=pl.BlockSpec(memory_space=pltpu.SMEM),
        out_shape=jax.ShapeDtypeStruct((n_rows,), jnp.int32),
    )(slot)


def _moe_fused_kernel(tg_ref, nv_ref, src_ref, dst_ref, h_ref, wr_ref, br_ref, wg_ref, wu_ref, wd_ref, y_ref,
                      rows, ybuf, hb_scr, acc_scr, gt_scr, gsem, ssem, *, t, epg):
    i = pl.program_id(0)
    nt = pl.num_programs(0) - 1
    s = lax.rem(i, 2)
    sub = rows.shape[2]
    part = t // epg

    def gather_rows(tile, slot, lo, hi):
        def body(g, carry):
            for k in range(SUBLANES):
                r = g * SUBLANES + k
                pltpu.make_async_copy(h_ref.at[src_ref[tile * t + r]], rows.at[slot, r],
                                      gsem.at[slot]).start(priority=k % 2)
            return carry
        lax.fori_loop(lo // SUBLANES, hi // SUBLANES, body, 0)

    def scatter_rows(tile, slot, lo, hi):
        def body(g, carry):
            for k in range(SUBLANES):
                r = g * SUBLANES + k
                pltpu.make_async_copy(ybuf.at[slot, r], y_ref.at[dst_ref[tile * t + r]],
                                      ssem.at[slot]).start(priority=k % 2)
            return carry
        lax.fori_loop(lo // SUBLANES, hi // SUBLANES, body, 0)

    def wait_gather(slot):
        pltpu.make_async_copy(h_ref.at[pl.ds(0, t)], rows.at[slot], gsem.at[slot]).wait()

    def wait_scatter(slot):
        pltpu.make_async_copy(ybuf.at[slot], y_ref.at[pl.ds(0, t)], ssem.at[slot]).wait()

    @pl.when(i == 0)
    def _():
        gt_scr[...] = jnp.zeros_like(gt_scr)
        gather_rows(0, 0, 0, t)

    @pl.when(i >= 2)
    def _():
        wait_scatter(s)

    @pl.when(i < nt)
    def _():
        wait_gather(s)
        hb_scr[...] = jnp.concatenate([rows[s, :, j, :] for j in range(sub)], axis=-1).astype(BF16)
        tok_id = lax.broadcasted_iota(jnp.int32, (1, t), 1)
        gate_rows = _expert_gates(_router_logits(hb_scr[...], wr_ref, br_ref), tg_ref[i])
        for e in range(epg):
            gt_scr[e:e + 1, :] = jnp.where(tok_id < nv_ref[i], gate_rows[e], 0.0)
        gates = gt_scr[...].T
        for e in range(epg):
            @pl.when(i + 1 < nt)
            def _():
                gather_rows(i + 1, 1 - s, e * part, (e + 1) * part)

            @pl.when(i >= 1)
            def _():
                scatter_rows(i - 1, 1 - s, e * part, (e + 1) * part)

            hid = jax.nn.silu(_dot(hb_scr[...], wg_ref[e])) * _dot(hb_scr[...], wu_ref[e])
            y = gates[:, e:e + 1] * _dot(hid.astype(BF16), wd_ref[e])
            if e == 0:
                acc_scr[...] = y
            else:
                acc_scr[...] += y
        for j in range(sub):
            ybuf[s, :, j, :] = acc_scr[:, j * LANES:(j + 1) * LANES]

    @pl.when(i == nt)
    def _():
        scatter_rows(nt - 1, 1 - s, 0, t)
        wait_scatter(1 - s)


def _moe_fused(tile_group, n_valid, src, dst, h2t, w_router, b_router, wg, wu, wd, n_tiles, t):
    epg = wg.shape[1]
    last = n_tiles - 1
    sub = h2t.shape[1]
    d = sub * LANES
    full = lambda a: pl.BlockSpec(a.shape, lambda i, tg, nv, sr, ds: (0,) * a.ndim)
    by_group = lambda a: pl.BlockSpec((None,) + a.shape[1:], lambda i, tg, nv, sr, ds: (tg[jnp.minimum(i, last)], 0, 0, 0))
    return pl.pallas_call(
        functools.partial(_moe_fused_kernel, t=t, epg=epg),
        grid_spec=pltpu.PrefetchScalarGridSpec(
            num_scalar_prefetch=4, grid=(n_tiles + 1,),
            in_specs=[pl.BlockSpec(memory_space=pl.ANY), full(w_router), full(b_router),
                      by_group(wg), by_group(wu), by_group(wd)],
            out_specs=pl.BlockSpec(memory_space=pl.ANY),
            scratch_shapes=[pltpu.VMEM((2, t, sub, LANES), F32), pltpu.VMEM((2, t, sub, LANES), F32),
                            pltpu.VMEM((t, d), BF16), pltpu.VMEM((t, d), F32), pltpu.VMEM((LANES, t), F32),
                            pltpu.SemaphoreType.DMA((2,)), pltpu.SemaphoreType.DMA((2,))]),
        out_shape=jax.ShapeDtypeStruct((n_tiles * t, sub, LANES), F32),
        compiler_params=_cparams("arbitrary"),
    )(tile_group, n_valid, src, dst, h2t, w_router, b_router, wg, wu, wd)


def _final_kernel(x_ref, y_ref, mod_ref, g_ref, o_ref):
    out = x_ref[...] + mod_ref[5:6, :] * _from_token_tiles(y_ref)
    ms = jnp.mean(out * out, axis=-1, keepdims=True)
    o_ref[...] = out * lax.rsqrt(ms + EPS) * g_ref[...]


def _final(x, y_rows, mod, g_final):
    b, l, d = x.shape
    t = min(512, l)
    nt = l // t
    return pl.pallas_call(
        _final_kernel,
        grid=(b, nt),
        in_specs=[pl.BlockSpec((None, t, d), lambda bi, i: (bi, i, 0)),
                  pl.BlockSpec((t,) + y_rows.shape[1:], lambda bi, i: (bi * nt + i, 0, 0)),
                  pl.BlockSpec((None, N_MOD, d), lambda bi, i: (bi, 0, 0)),
                  pl.BlockSpec((1, d), lambda bi, i: (0, 0))],
        out_specs=pl.BlockSpec((None, t, d), lambda bi, i: (bi, i, 0)),
        out_shape=jax.ShapeDtypeStruct((b, l, d), F32),
        compiler_params=_cparams("parallel", "parallel"),
    )(x, y_rows, mod, g_final)


def kernel(x, c, ctx, c_ctx, mod_w, mod_b, norm_mix_g, norm_ffn_g, w_in, conv_w, conv_b, rec_gate_a_w, rec_gate_a_b,
           rec_gate_x_w, rec_gate_x_b, rec_lambda, pool_w, pool_scale, fourier_w, w_out, router_group_w,
           router_group_b, router_expert_w, router_expert_b, expert_w_gate, expert_w_up, expert_w_down, final_norm_g):
    b, l, d = x.shape
    depth = mod_w.shape[0]
    d_rec = conv_w.shape[-1]
    d_pool = pool_scale.shape[-1]
    d_four = fourier_w.shape[1] * fourier_w.shape[2]
    widths = (d_rec, d_rec, d_pool, d_four)
    rows = l // GRID_W
    heads_half = REC_HEADS // 2
    n_exp = expert_w_gate.shape[1]

    cvec = jnp.concatenate([c, c_ctx[None, :], jnp.zeros((SUBLANES - b - 1, d), F32)], axis=0)
    mod_all = _modulation(cvec, mod_w, mod_b).reshape(depth, SUBLANES, N_MOD, d)
    g_final = final_norm_g.reshape(1, d)
    zeros_state = jnp.zeros((b, 2, d_rec), F32)
    pending = pending_ctx = None

    for li in range(depth):
        last = li == depth - 1
        mod_lat = mod_all[li, :b]
        mod_ctx = jnp.broadcast_to(mod_all[li, b][None], (b, N_MOD, d))
        w_in_l = w_in[li].astype(BF16)
        w_out_l = w_out[li].astype(BF16)
        wgate = jnp.stack([
            jnp.stack([jnp.concatenate([_block_diag(rec_gate_a_w[li, dd, hh * heads_half:(hh + 1) * heads_half]),
                                        _block_diag(rec_gate_x_w[li, dd, hh * heads_half:(hh + 1) * heads_half])], axis=1)
                       for hh in range(2)]) for dd in range(2)]).astype(BF16)
        bgate = jnp.stack([rec_gate_a_b[li], rec_gate_x_b[li]], axis=1)
        rec_p = (wgate, bgate, rec_lambda[li])
        proj_p = (w_in_l, conv_w[li], conv_b[li].reshape(1, d_rec), widths)
        pool_w_bd = _block_diag(pool_w[li]).astype(BF16)
        four_w_bd = _block_diag(fourier_w[li]).astype(BF16)
        w_router = jnp.concatenate([router_group_w[li], jnp.zeros((d, SUBLANES - N_GROUPS), F32), router_expert_w[li],
                                    jnp.zeros((d, LANES - SUBLANES - n_exp), F32)], axis=1).astype(BF16)
        b_router = jnp.concatenate([router_group_b[li], jnp.zeros((SUBLANES - N_GROUPS,), F32), router_expert_b[li],
                                    jnp.zeros((LANES - SUBLANES - n_exp,), F32)]).reshape(1, LANES)
        by_group = lambda w: w.astype(BF16).reshape((N_GROUPS, EXPERTS_PER_GROUP) + w.shape[1:])
        expert_w = (by_group(expert_w_gate[li]), by_group(expert_w_up[li]), by_group(expert_w_down[li]))
        g_ffn = norm_ffn_g[li].reshape(1, d)

        def mixer_tail(hf, hb, z_g, z_p, z_f, stream, mod, grid_rows):
            y_p = _pool(z_p, pool_w_bd, pool_scale[li], grid_rows)
            y_f = _fourier(z_f, four_w_bd)
            return _combine(hf, hb, z_g, y_p, y_f, stream, mod, w_out_l, g_ffn, w_router, b_router)

        def expert_stage(h2t, gid):
            n_tok = gid.size
            t_moe = min(MOE_TILE, h2t.shape[1])
            slot, tile_group, n_tiles = _sort_plan(gid.reshape(n_tok), N_GROUPS, t_moe)
            src = _invert(slot, n_tiles * t_moe)
            valid = src >= 0
            pad_rank = jnp.cumsum(jnp.logical_not(valid).astype(jnp.int32)) - 1
            dst = jnp.where(valid, src, n_tok + pad_rank)
            n_valid = jnp.sum(valid.reshape(n_tiles, t_moe).astype(jnp.int32), axis=1)
            return _moe_fused(tile_group, n_valid, jnp.maximum(src, 0), dst, h2t.reshape((n_tok,) + h2t.shape[2:]),
                              w_router, b_router, *expert_w, n_tiles, t_moe)

        if pending_ctx is None:
            uc, zc_g, zc_p, zc_f = _inproj(ctx, norm_mix_g[li], mod_ctx, *proj_p)
        else:
            ctx, uc, zc_g, zc_p, zc_f = _inproj(ctx, norm_mix_g[li], mod_ctx, *proj_p, residual=pending_ctx)
        hf_c, hb_c = _recurrent(uc, *rec_p, zeros_state)
        state = jnp.stack([hf_c[:, -1, :], hb_c[:, 0, :]], axis=1)
        if not last:
            ctx, h2t_c, gid_c = mixer_tail(hf_c, hb_c, zc_g, zc_p, zc_f, ctx, mod_ctx, None)
            pending_ctx = (expert_stage(h2t_c, gid_c), 0, mod_ctx)

        if pending is None:
            u, z_g, z_p, z_f = _inproj(x, norm_mix_g[li], mod_lat, *proj_p)
        else:
            x, u, z_g, z_p, z_f = _inproj(x, norm_mix_g[li], mod_lat, *proj_p, residual=pending)
        hf, hb = _recurrent(u, *rec_p, state)
        x, h2t, gid = mixer_tail(hf, hb, z_g, z_p, z_f, x, mod_lat, rows)
        pending = (expert_stage(h2t, gid), 0, mod_lat)
    return _final(x, pending[0], pending[2], g_final)
```

```python
import functools
import math

import jax
import jax.numpy as jnp
from jax import lax
from jax.experimental import pallas as pl
from jax.experimental.pallas import tpu as pltpu

F32 = jnp.float32
BF16 = jnp.bfloat16

GRID_W = 64
N_MOD = 6
REC_HEADS = 8
CONV_W = 4
LRU_C = 8.0
POOL_WINDOWS = (2, 4, 8, 16)
FOUR_HEADS = 4
N_GROUPS = 4
EXPERTS_PER_GROUP = 4
EPS = 1e-6

LANES = 128
SUBLANES = 8
POOL_PAD = 8
VMEM_LIMIT = 56 * 1024 * 1024


def _cparams(*sem):
    return pltpu.CompilerParams(dimension_semantics=sem, vmem_limit_bytes=VMEM_LIMIT)


def _split_bf16(a):
    hi = a.astype(BF16)
    lo = (a - hi.astype(F32)).astype(BF16)
    return hi, lo


def _dot(a, b):
    return jnp.dot(a, b, preferred_element_type=F32)


def _sigmoid(x):
    return 0.5 * jnp.tanh(0.5 * x) + 0.5


def _mod_kernel(s_ref, w_ref, b_ref, o_ref):
    s = s_ref[...]
    s = s * jax.nn.sigmoid(s)
    s_hi, s_lo = _split_bf16(s)
    w_hi, w_lo = _split_bf16(w_ref[...])
    o_ref[...] = _dot(s_hi, w_hi) + _dot(s_hi, w_lo) + _dot(s_lo, w_hi) + b_ref[...]


def _modulation(cvec, mod_w, mod_b):
    depth, d, dm = mod_w.shape
    tn = dm // 4
    return pl.pallas_call(
        _mod_kernel,
        grid=(depth, dm // tn),
        in_specs=[pl.BlockSpec((SUBLANES, d), lambda l, j: (0, 0)),
                  pl.BlockSpec((None, d, tn), lambda l, j: (l, 0, j)),
                  pl.BlockSpec((None, 1, tn), lambda l, j: (l, 0, j))],
        out_specs=pl.BlockSpec((None, SUBLANES, tn), lambda l, j: (l, 0, j)),
        out_shape=jax.ShapeDtypeStruct((depth, SUBLANES, dm), F32),
        compiler_params=_cparams("parallel", "parallel"),
    )(cvec, mod_w, mod_b.reshape(depth, 1, dm))


def _rms_mod(x, g, shift, scale):
    ms = jnp.mean(x * x, axis=-1, keepdims=True)
    y = x * lax.rsqrt(ms + EPS) * g
    return y * (1.0 + scale) + shift


def _from_token_tiles(ref):
    return jnp.concatenate([ref[:, j, :] for j in range(ref.shape[1])], axis=-1)


def _inproj_kernel(*refs, widths, with_residual, n_tiles):
    i = pl.program_id(1)
    if with_residual:
        (x_ref, xp_ref, xn_ref, y_ref, yp_ref, yn_ref, pmod_ref,
         g_ref, mod_ref, w_ref, cw_ref, cb_ref, xo_ref, *o_refs) = refs
        gate = pmod_ref[5:6, :]
        x = x_ref[...] + gate * _from_token_tiles(y_ref)
        xo_ref[...] = x
        x_halo = jnp.concatenate([xp_ref[...] + gate * _from_token_tiles(yp_ref),
                                  xn_ref[...] + gate * _from_token_tiles(yn_ref)], axis=0)
    else:
        x_ref, xp_ref, xn_ref, g_ref, mod_ref, w_ref, cw_ref, cb_ref, *o_refs = refs
        x = x_ref[...]
        x_halo = jnp.concatenate([xp_ref[...], xn_ref[...]], axis=0)
    t = x.shape[0]
    d_rec = widths[0]
    z = _dot(_rms_mod(x, g_ref[...], mod_ref[0:1, :], mod_ref[1:2, :]).astype(BF16), w_ref[...])
    zh = _dot(_rms_mod(x_halo, g_ref[...], mod_ref[0:1, :], mod_ref[1:2, :]).astype(BF16), w_ref[:, :d_rec])
    zz = jnp.concatenate([jnp.where(i == 0, 0.0, zh[:SUBLANES]), z[:, :d_rec],
                          jnp.where(i == n_tiles - 1, 0.0, zh[SUBLANES:])], axis=0)
    u = cb_ref[...]
    for k in range(CONV_W):
        o = SUBLANES - CONV_W // 2 + k
        u = u + cw_ref[k:k + 1, :] * zz[o:o + t, :]
    seg = t // SUBLANES
    for s in range(SUBLANES):
        o_refs[0][:, s, :] = u[s * seg:(s + 1) * seg, :]
    off = d_rec
    for o_ref, wd in zip(o_refs[1:], widths[1:]):
        o_ref[...] = z[:, off:off + wd].astype(o_ref.dtype)
        off += wd


def _inproj(x, g, mod, w_bf16, conv_w, conv_b, widths, residual=None):
    b, l, d = x.shape
    t = min(512, l)
    nt = l // t
    per = t // SUBLANES
    n8 = l // SUBLANES
    tok = lambda c: pl.BlockSpec((None, t, c), lambda bi, i: (bi, i, 0))
    prev8 = lambda bi, i: jnp.maximum(i * per - 1, 0)
    next8 = lambda bi, i: jnp.minimum((i + 1) * per, n8 - 1)
    x_specs = [tok(d), pl.BlockSpec((None, SUBLANES, d), lambda bi, i: (bi, prev8(bi, i), 0)),
               pl.BlockSpec((None, SUBLANES, d), lambda bi, i: (bi, next8(bi, i), 0))]
    modspec = pl.BlockSpec((None, N_MOD, d), lambda bi, i: (bi, 0, 0))
    full = lambda a: pl.BlockSpec(a.shape, lambda bi, i: (0,) * a.ndim)
    tail_specs = [pl.BlockSpec((1, d), lambda bi, i: (0, 0)), modspec, full(w_bf16), full(conv_w), full(conv_b)]
    tail_args = (g.reshape(1, d), mod, w_bf16, conv_w, conv_b)
    seg = t // SUBLANES
    z_specs = ([pl.BlockSpec((None, None, seg, SUBLANES, widths[0]), lambda bi, i: (bi, i, 0, 0, 0))]
               + [tok(wd) for wd in widths[1:]])
    z_shapes = ([jax.ShapeDtypeStruct((b, nt, seg, SUBLANES, widths[0]), F32)]
                + [jax.ShapeDtypeStruct((b, l, wd), BF16 if k == 0 else F32) for k, wd in enumerate(widths[1:])])
    if residual is None:
        in_specs, args, out_specs, out_shape = x_specs + tail_specs, (x, x, x) + tail_args, z_specs, z_shapes
    else:
        y_rows, row0, prev_mod = residual
        sub = y_rows.shape[1]
        blk0, blk8 = row0 // t, row0 // SUBLANES
        y_specs = [pl.BlockSpec((t, sub, LANES), lambda bi, i: (blk0 + bi * nt + i, 0, 0)),
                   pl.BlockSpec((SUBLANES, sub, LANES), lambda bi, i: (blk8 + bi * n8 + prev8(bi, i), 0, 0)),
                   pl.BlockSpec((SUBLANES, sub, LANES), lambda bi, i: (blk8 + bi * n8 + next8(bi, i), 0, 0))]
        in_specs = x_specs + y_specs + [modspec] + tail_specs
        args = (x, x, x, y_rows, y_rows, y_rows, prev_mod) + tail_args
        out_specs, out_shape = [tok(d)] + z_specs, [jax.ShapeDtypeStruct((b, l, d), F32)] + z_shapes
    return pl.pallas_call(
        functools.partial(_inproj_kernel, widths=widths, with_residual=residual is not None, n_tiles=nt),
        grid=(b, nt), in_specs=in_specs, out_specs=out_specs, out_shape=out_shape,
        compiler_params=_cparams("parallel", "parallel"),
    )(*args)


def _rec_kernel(uf_ref, ub_ref, wg_ref, bg_ref, lam_ref, h0_ref, hf_ref, hb_ref, a_scr, b_scr, carry_scr, *, tc):
    i = pl.program_id(1)
    c = uf_ref.shape[-1]
    half = c // 2
    seg = tc // SUBLANES

    @pl.when(i == 0)
    def _():
        carry_scr[...] = h0_ref[...]

    def coefficients(d, u_ref):
        u = u_ref[...].reshape(tc, c)
        log_decay = -LRU_C * jax.nn.softplus(-lam_ref[d:d + 1, :])
        for hh in range(2):
            cols = slice(hh * half, (hh + 1) * half)
            uh = u[:, cols]
            pre = _dot(uh.astype(BF16), wg_ref[d, hh])
            r = _sigmoid(pre[:, :half] + bg_ref[d, 0:1, cols])
            ig = _sigmoid(pre[:, half:] + bg_ref[d, 1:2, cols])
            log_a = log_decay[:, cols] * r
            a = jnp.exp(log_a)
            th = jnp.tanh(log_a)
            bt = jnp.sqrt(-2.0 * th / (1.0 - th)) * (ig * uh)
            a_scr[d, :, :, cols] = a.reshape(seg, SUBLANES, half)
            b_scr[d, :, :, cols] = bt.reshape(seg, SUBLANES, half)

    coefficients(0, uf_ref)
    coefficients(1, ub_ref)

    def body(j, carry):
        hf, af, hb, ab = carry
        jb = seg - 1 - j
        a8 = a_scr[0, j]
        hf = a8 * hf + b_scr[0, j]
        af = a8 * af
        b_scr[0, j] = hf
        a_scr[0, j] = af
        a8 = a_scr[1, jb]
        hb = a8 * hb + b_scr[1, jb]
        ab = a8 * ab
        b_scr[1, jb] = hb
        a_scr[1, jb] = ab
        return hf, af, hb, ab

    zero = jnp.zeros((SUBLANES, c), F32)
    one = jnp.ones((SUBLANES, c), F32)
    hf_end, af_end, hb_end, ab_end = lax.fori_loop(0, seg, body, (zero, one, zero, one), unroll=2)

    def chain(h_end, a_end, state, order):
        entering = [None] * SUBLANES
        for s in order:
            entering[s] = state
            state = a_end[s:s + 1, :] * state + h_end[s:s + 1, :]
        return jnp.concatenate(entering, axis=0), state

    cin_f, out_f = chain(hf_end, af_end, carry_scr[0:1, :], range(SUBLANES))
    cin_b, out_b = chain(hb_end, ab_end, carry_scr[1:2, :], reversed(range(SUBLANES)))
    carry_scr[0:1, :] = out_f
    carry_scr[1:2, :] = out_b

    for d, cin, out_ref in ((0, cin_f, hf_ref), (1, cin_b, hb_ref)):
        b_scr[d] = b_scr[d] + a_scr[d] * cin[None]
        for s in range(SUBLANES):
            out_ref[s * seg:(s + 1) * seg, :] = b_scr[d, :, s, :].astype(out_ref.dtype)


def _recurrent(u, wgate, bgate, lam, h0, out_dtype):
    b, nc, seg, _, c = u.shape
    tc = seg * SUBLANES
    u_fwd = pl.BlockSpec((None, None, seg, SUBLANES, c), lambda bi, i: (bi, i, 0, 0, 0))
    u_bwd = pl.BlockSpec((None, None, seg, SUBLANES, c), lambda bi, i: (bi, nc - 1 - i, 0, 0, 0))
    h_fwd = pl.BlockSpec((None, tc, c), lambda bi, i: (bi, i, 0))
    h_bwd = pl.BlockSpec((None, tc, c), lambda bi, i: (bi, nc - 1 - i, 0))
    full = lambda a: pl.BlockSpec(a.shape, lambda bi, i: (0,) * a.ndim)
    return pl.pallas_call(
        functools.partial(_rec_kernel, tc=tc),
        grid=(b, nc),
        in_specs=[u_fwd, u_bwd, full(wgate), full(bgate), full(lam),
                  pl.BlockSpec((None, 2, c), lambda bi, i: (bi, 0, 0))],
        out_specs=[h_fwd, h_bwd],
        out_shape=[jax.ShapeDtypeStruct((b, nc * tc, c), out_dtype)] * 2,
        scratch_shapes=[pltpu.VMEM((2, seg, SUBLANES, c), F32), pltpu.VMEM((2, seg, SUBLANES, c), F32),
                        pltpu.VMEM((2, c), F32)],
        compiler_params=_cparams("parallel", "arbitrary"),
    )(u, u, wgate, bgate, lam, h0)


def _dft_tables(n):
    j = lax.broadcasted_iota(jnp.int32, (n, n), 0)
    k = lax.broadcasted_iota(jnp.int32, (n, n), 1)
    ang = ((j * k) % n).astype(F32) * (2.0 * math.pi / n)
    return jnp.cos(ang), jnp.sin(ang)


def _block_diag(w):
    return jax.scipy.linalg.block_diag(*[w[h] for h in range(w.shape[0])])


def _channel_stage(gr, gi, c_ref, s_ref, w_ref, scale):
    f = (_dot(gr.astype(BF16), c_ref[...]) + _dot(gi.astype(BF16), s_ref[...])) * scale
    return _dot(f.astype(BF16), w_ref[...])


def _four1_kernel(x_ref, f1_ref, twc_ref, tws_ref, ar_ref, ai_ref, *, n1):
    for jj in range(SUBLANES):
        a = _dot(f1_ref[...], x_ref[:, jj, :].astype(BF16))
        ar, ai = a[:n1], a[n1:]
        cc = twc_ref[:, jj:jj + 1]
        ss = tws_ref[:, jj:jj + 1]
        ar_ref[:, jj, :] = ar * cc + ai * ss
        ai_ref[:, jj, :] = ai * cc - ar * ss


def _four2_kernel(ar_ref, ai_ref, f2_ref, c_ref, s_ref, w_ref, o_ref, *, n2, scale):
    grs, gis = [], []
    for kk in range(SUBLANES):
        slab = jnp.concatenate([ar_ref[kk], ai_ref[kk]], axis=0).astype(BF16)
        g = _dot(f2_ref[...], slab)
        grs.append(g[:n2])
        gis.append(g[n2:])
    y = _channel_stage(jnp.concatenate(grs, axis=0), jnp.concatenate(gis, axis=0), c_ref, s_ref, w_ref, scale)
    for kk in range(SUBLANES):
        o_ref[:, kk, :] = y[kk * n2:(kk + 1) * n2]


def _four_small_kernel(x_ref, f_ref, c_ref, s_ref, w_ref, o_ref, *, n, scale):
    g = _dot(f_ref[...], x_ref[...].astype(BF16))
    o_ref[...] = _channel_stage(g[:n], g[n:], c_ref, s_ref, w_ref, scale)


def _fourier(z_f, four_w_bd):
    b, l, nch = z_f.shape
    hd = nch // FOUR_HEADS
    scale = 1.0 / math.sqrt(l * hd)
    c_h, s_h = _dft_tables(hd)
    eye = jnp.eye(FOUR_HEADS, dtype=F32)
    c_bd = jnp.kron(eye, c_h).astype(BF16)
    s_bd = jnp.kron(eye, s_h).astype(BF16)
    full = lambda a, nd: pl.BlockSpec(a.shape, lambda *_: (0,) * a.ndim)
    if l <= 256:
        cn, sn = _dft_tables(l)
        f = jnp.concatenate([cn, -sn], axis=0).astype(BF16)
        return pl.pallas_call(
            functools.partial(_four_small_kernel, n=l, scale=scale),
            grid=(b,),
            in_specs=[pl.BlockSpec((None, l, nch), lambda bi: (bi, 0, 0)),
                      full(f, 1), full(c_bd, 1), full(s_bd, 1), full(four_w_bd, 1)],
            out_specs=pl.BlockSpec((None, l, nch), lambda bi: (bi, 0, 0)),
            out_shape=jax.ShapeDtypeStruct((b, l, nch), F32),
            compiler_params=_cparams("parallel"),
        )(z_f, f, c_bd, s_bd, four_w_bd)

    n2 = LANES
    n1 = l // n2
    nj = n2 // SUBLANES
    c1, s1 = _dft_tables(n1)
    f1 = jnp.concatenate([c1, -s1], axis=0).astype(BF16)
    c2, s2 = _dft_tables(n2)
    f2 = jnp.concatenate([jnp.concatenate([c2, s2], axis=1),
                          jnp.concatenate([-s2, c2], axis=1)], axis=0).astype(BF16)
    k1 = lax.broadcasted_iota(jnp.int32, (n1, n2), 0)
    t2 = lax.broadcasted_iota(jnp.int32, (n1, n2), 1)
    ang = (k1 * t2).astype(F32) * (2.0 * math.pi / l)
    twc = jnp.cos(ang).reshape(n1, nj, SUBLANES).transpose(1, 0, 2)
    tws = jnp.sin(ang).reshape(n1, nj, SUBLANES).transpose(1, 0, 2)
    ar, ai = pl.pallas_call(
        functools.partial(_four1_kernel, n1=n1),
        grid=(b, nj),
        in_specs=[pl.BlockSpec((None, n1, SUBLANES, nch), lambda bi, j: (bi, 0, j, 0)),
                  full(f1, 2),
                  pl.BlockSpec((None, n1, SUBLANES), lambda bi, j: (j, 0, 0)),
                  pl.BlockSpec((None, n1, SUBLANES), lambda bi, j: (j, 0, 0))],
        out_specs=[pl.BlockSpec((None, n1, SUBLANES, nch), lambda bi, j: (bi, 0, j, 0))] * 2,
        out_shape=[jax.ShapeDtypeStruct((b, n1, n2, nch), F32)] * 2,
        compiler_params=_cparams("parallel", "parallel"),
    )(z_f.reshape(b, n1, n2, nch), f1, twc, tws)
    y = pl.pallas_call(
        functools.partial(_four2_kernel, n2=n2, scale=scale),
        grid=(b, n1 // SUBLANES),
        in_specs=[pl.BlockSpec((None, SUBLANES, n2, nch), lambda bi, j: (bi, j, 0, 0)),
                  pl.BlockSpec((None, SUBLANES, n2, nch), lambda bi, j: (bi, j, 0, 0)),
                  full(f2, 2), full(c_bd, 2), full(s_bd, 2), full(four_w_bd, 2)],
        out_specs=pl.BlockSpec((None, n2, SUBLANES, nch), lambda bi, j: (bi, 0, j, 0)),
        out_shape=jax.ShapeDtypeStruct((b, n2, n1, nch), F32),
        compiler_params=_cparams("parallel", "parallel"),
    )(ar, ai, f2, c_bd, s_bd, four_w_bd)
    return y.reshape(b, l, nch)


def _window_counts(idx, n, w):
    return (jnp.minimum(idx + (w - w // 2), n) - jnp.maximum(idx - w // 2, 0)).astype(F32)


def _pool_kernel(cur_ref, prev_ref, next_ref, pw_ref, ps_ref, o_ref, *, r, w, rows_total, n_tiles):
    i = pl.program_id(1)
    two_d = rows_total > 1
    halo = SUBLANES if two_d else 0
    nr = r + 2 * halo
    stride = w + POOL_PAD
    flat = nr * stride
    lane = lax.broadcasted_iota(jnp.int32, (1, 1, LANES), 2)
    low = lane < (LANES // 2)
    col = lax.broadcasted_iota(jnp.int32, (1, w, 1), 1)
    grow = lax.broadcasted_iota(jnp.int32, (r, 1, 1), 0) + i * r
    outs = []
    for hh in range(2):
        cols = slice(hh * LANES, (hh + 1) * LANES)
        w_lo, w_hi = POOL_WINDOWS[2 * hh], POOL_WINDOWS[2 * hh + 1]
        cur = cur_ref[:, cols]
        if two_d:
            prev = jnp.where(i == 0, 0.0, prev_ref[:, cols])
            nxt = jnp.where(i == n_tiles - 1, 0.0, next_ref[:, cols])
            ext = jnp.concatenate([prev, cur, nxt], axis=0)
        else:
            ext = cur
        ext = ext.reshape(nr, w, LANES)
        x = jnp.concatenate([jnp.zeros((nr, POOL_PAD, LANES), F32), ext], axis=1).reshape(flat, LANES)
        p = x + pltpu.roll(x, 1, 0)
        sums = {2: p}
        for ww, s in ((4, 1), (8, 2), (16, 4)):
            if ww > w_hi:
                break
            p = pltpu.roll(p, s, 0) + pltpu.roll(p, flat - s, 0)
            sums[ww] = p
        y = jnp.where(low[0], sums[w_lo], sums[w_hi]).reshape(nr, stride, LANES)[:, POOL_PAD:, :]
        if two_d:
            q2 = y[0:nr - 1] + y[1:nr]
            rsum = {2: q2[7:7 + r]}
            q4 = q2[0:nr - 3] + q2[2:nr - 1]
            rsum[4] = q4[6:6 + r]
            if w_hi > 4:
                q8 = q4[0:nr - 7] + q4[4:nr - 3]
                rsum[8] = q8[4:4 + r]
                q16 = q8[0:nr - 15] + q8[8:nr - 7]
                rsum[16] = q16[0:r]
            tot = jnp.where(low, rsum[w_lo], rsum[w_hi])
            cnt_r = jnp.where(low, _window_counts(grow, rows_total, w_lo), _window_counts(grow, rows_total, w_hi))
            tot = tot / cnt_r
        else:
            tot = y
        cnt_c = jnp.where(low, _window_counts(col, w, w_lo), _window_counts(col, w, w_hi))
        pooled = (tot / cnt_c).reshape(r * w, LANES)
        outs.append(pooled - cur)
    d = jnp.concatenate(outs, axis=-1)
    o_ref[...] = (_dot(d.astype(BF16), pw_ref[...]) * ps_ref[...]).astype(o_ref.dtype)


def _pool(z_p, pool_w_bd, pool_scale, rows):
    b, l, nch = z_p.shape
    if rows is None:
        rows_total, w, r = 1, l, 1
    else:
        rows_total, w, r = rows, l // rows, min(16, rows)
    t = r * w
    n_tiles = l // t
    hb = SUBLANES * w if rows is not None else t
    per = t // hb
    nhb = l // hb
    return pl.pallas_call(
        functools.partial(_pool_kernel, r=r, w=w, rows_total=rows_total, n_tiles=n_tiles),
        grid=(b, n_tiles),
        in_specs=[pl.BlockSpec((None, t, nch), lambda bi, i: (bi, i, 0)),
                  pl.BlockSpec((None, hb, nch), lambda bi, i: (bi, jnp.maximum(i * per - 1, 0), 0)),
                  pl.BlockSpec((None, hb, nch), lambda bi, i: (bi, jnp.minimum((i + 1) * per, nhb - 1), 0)),
                  pl.BlockSpec(pool_w_bd.shape, lambda bi, i: (0, 0)),
                  pl.BlockSpec((1, nch), lambda bi, i: (0, 0))],
        out_specs=pl.BlockSpec((None, t, nch), lambda bi, i: (bi, i, 0)),
        out_shape=jax.ShapeDtypeStruct((b, l, nch), BF16),
        compiler_params=_cparams("parallel", "parallel"),
    )(z_p, z_p, z_p, pool_w_bd, pool_scale.reshape(1, nch))


def _first_argmax4(v):
    m = jnp.maximum(jnp.maximum(v[0], v[1]), jnp.maximum(v[2], v[3]))
    idx = jnp.where(v[0] >= m, 0, jnp.where(v[1] >= m, 1, jnp.where(v[2] >= m, 2, 3)))
    return m, idx


def _select4(idx, v):
    return jnp.where(idx == 0, v[0], jnp.where(idx == 1, v[1], jnp.where(idx == 2, v[2], v[3])))


def _router_logits(h_bf16, wr_ref, br_ref):
    return (_dot(h_bf16, wr_ref[...]) + br_ref[...]).T


def _group_logits(lt):
    return [lt[g:g + 1, :] for g in range(N_GROUPS)]


def _expert_gates(lt, group):
    lg = _group_logits(lt)
    m, _ = _first_argmax4(lg)
    den = sum(jnp.exp(v - m) for v in lg)
    p_group = jnp.exp(_select4(group, lg) - m) / den
    sel = [_select4(group, [lt[SUBLANES + EXPERTS_PER_GROUP * g + e:SUBLANES + EXPERTS_PER_GROUP * g + e + 1, :]
                            for g in range(N_GROUPS)]) for e in range(EXPERTS_PER_GROUP)]
    v1, e1 = _first_argmax4(sel)
    rest = [jnp.where(e1 == e, -jnp.inf, sel[e]) for e in range(EXPERTS_PER_GROUP)]
    v2, e2 = _first_argmax4(rest)
    t2 = jnp.exp(v2 - v1)
    w1 = p_group / (1.0 + t2)
    w2 = p_group * t2 / (1.0 + t2)
    return [jnp.where(e1 == e, w1, jnp.where(e2 == e, w2, 0.0)) for e in range(EXPERTS_PER_GROUP)]


def _combine_kernel(hf_ref, hb_ref, zg_ref, yp_ref, yf_ref, x_ref, mod_ref, wo_ref, g_ref, wr_ref, br_ref,
                    xo_ref, h2t_ref, gid_ref):
    y_a = ((hf_ref[...].astype(F32) + hb_ref[...].astype(F32))
           * jax.nn.gelu(zg_ref[...].astype(F32), approximate=True))
    ycat = jnp.concatenate([y_a.astype(BF16), yp_ref[...].astype(BF16), yf_ref[...].astype(BF16)], axis=-1)
    x_new = x_ref[...] + mod_ref[2:3, :] * _dot(ycat, wo_ref[...])
    xo_ref[...] = x_new
    h2 = _rms_mod(x_new, g_ref[...], mod_ref[3:4, :], mod_ref[4:5, :])
    for j in range(h2t_ref.shape[1]):
        h2t_ref[:, j, :] = h2[:, j * LANES:(j + 1) * LANES]
    _, gidx = _first_argmax4(_group_logits(_router_logits(h2.astype(BF16), wr_ref, br_ref)))
    gid_ref[...] = gidx


def _combine(hf, hb, z_g, y_p, y_f, x, mod, w_out, g_ffn, w_router, b_router):
    b, l, d = x.shape
    t = min(512, l)
    sub = d // LANES
    tok = lambda c: pl.BlockSpec((None, t, c), lambda bi, i: (bi, i, 0))
    full = lambda a: pl.BlockSpec(a.shape, lambda bi, i: (0,) * a.ndim)
    return pl.pallas_call(
        _combine_kernel,
        grid=(b, l // t),
        in_specs=[tok(hf.shape[-1]), tok(hb.shape[-1]), tok(z_g.shape[-1]), tok(y_p.shape[-1]), tok(y_f.shape[-1]),
                  tok(d), pl.BlockSpec((None, N_MOD, d), lambda bi, i: (bi, 0, 0)),
                  full(w_out), full(g_ffn), full(w_router), full(b_router)],
        out_specs=[tok(d), pl.BlockSpec((None, t, sub, LANES), lambda bi, i: (bi, i, 0, 0)),
                   pl.BlockSpec((None, 1, t), lambda bi, i: (bi, 0, i))],
        out_shape=[jax.ShapeDtypeStruct((b, l, d), F32), jax.ShapeDtypeStruct((b, l, sub, LANES), F32),
                   jax.ShapeDtypeStruct((b, 1, l), jnp.int32)],
        compiler_params=_cparams("parallel", "parallel"),
    )(hf, hb, z_g, y_p, y_f, x, mod, w_out, g_ffn, w_router, b_router)


MOE_TILE = 512


def _sort_plan(gid, n_groups, t):
    n = gid.shape[0]
    onehot = (gid[:, None] == jnp.arange(n_groups, dtype=jnp.int32)[None, :]).astype(jnp.int32)
    csum = jnp.cumsum(onehot, axis=0)
    rank = jnp.sum((csum - onehot) * onehot, axis=1)
    tiles = (csum[-1] + t - 1) // t
    tile_end = jnp.cumsum(tiles)
    slot = jnp.sum(onehot * (tile_end - tiles)[None, :], axis=1) * t + rank
    n_tiles = n // t + n_groups
    tile_idx = jnp.arange(n_tiles, dtype=jnp.int32)
    tile_group = jnp.minimum(jnp.sum((tile_idx[:, None] >= tile_end[None, :]).astype(jnp.int32), axis=1), n_groups - 1)
    return slot.astype(jnp.int32), tile_group.astype(jnp.int32), tile_end[-1:].astype(jnp.int32), n_tiles


def _invert_kernel(slot_ref, src_ref):
    n_rows = src_ref.shape[0]
    n = slot_ref.shape[0]

    def clear(s, carry):
        src_ref[s] = -1
        return carry

    lax.fori_loop(0, n_rows, clear, 0, unroll=8)

    def put(tok, carry):
        src_ref[slot_ref[tok]] = tok
        return carry

    lax.fori_loop(0, n, put, 0, unroll=8)


def _invert(slot, n_rows):
    return pl.pallas_call(
        _invert_kernel,
        in_specs=[pl.BlockSpec(memory_space=pltpu.SMEM)],
        out_specs=pl.BlockSpec(memory_space=pltpu.SMEM),
        out_shape=jax.ShapeDtypeStruct((n_rows,), jnp.int32),
    )(slot)


def _moe_fused_kernel(tg_ref, nu_ref, nv_ref, src_ref, dst_ref, h_ref, wr_ref, br_ref, wg_ref, wu_ref, wd_ref, y_ref,
                      rows, ybuf, hb_scr, acc_scr, gt_scr, gsem, ssem, *, t, epg):
    i = pl.program_id(0)
    nu = nu_ref[0]
    s = lax.rem(i, 2)
    sub = rows.shape[2]
    part = t // epg

    def gather_rows(tile, slot, lo, hi):
        def body(g, carry):
            for k in range(SUBLANES):
                r = g * SUBLANES + k
                pltpu.make_async_copy(h_ref.at[src_ref[tile * t + r]], rows.at[slot, r],
                                      gsem.at[slot]).start(priority=k % 2)
            return carry
        lax.fori_loop(lo // SUBLANES, hi // SUBLANES, body, 0)

    def scatter_rows(tile, slot, lo, hi):
        def body(g, carry):
            for k in range(SUBLANES):
                r = g * SUBLANES + k
                pltpu.make_async_copy(ybuf.at[slot, r], y_ref.at[dst_ref[tile * t + r]],
                                      ssem.at[slot]).start(priority=k % 2)
            return carry
        lax.fori_loop(lo // SUBLANES, hi // SUBLANES, body, 0)

    def wait_gather(slot):
        pltpu.make_async_copy(h_ref.at[pl.ds(0, t)], rows.at[slot], gsem.at[slot]).wait()

    def wait_scatter(slot):
        pltpu.make_async_copy(ybuf.at[slot], y_ref.at[pl.ds(0, t)], ssem.at[slot]).wait()

    @pl.when(i == 0)
    def _():
        gt_scr[...] = jnp.zeros_like(gt_scr)
        gather_rows(0, 0, 0, t)

    @pl.when(jnp.logical_and(i >= 2, i - 2 < nu))
    def _():
        wait_scatter(s)

    @pl.when(i < nu)
    def _():
        wait_gather(s)
        hb_scr[...] = jnp.concatenate([rows[s, :, j, :] for j in range(sub)], axis=-1).astype(BF16)
        tok_id = lax.broadcasted_iota(jnp.int32, (1, t), 1)
        gate_rows = _expert_gates(_router_logits(hb_scr[...], wr_ref, br_ref), tg_ref[i])
        for e in range(epg):
            gt_scr[e:e + 1, :] = jnp.where(tok_id < nv_ref[i], gate_rows[e], 0.0)
        gates = gt_scr[...].T
        for e in range(epg):
            @pl.when(i + 1 < nu)
            def _():
                gather_rows(i + 1, 1 - s, e * part, (e + 1) * part)

            @pl.when(i >= 1)
            def _():
                scatter_rows(i - 1, 1 - s, e * part, (e + 1) * part)

            hid = jax.nn.silu(_dot(hb_scr[...], wg_ref[e])) * _dot(hb_scr[...], wu_ref[e])
            y = gates[:, e:e + 1] * _dot(hid.astype(BF16), wd_ref[e])
            if e == 0:
                acc_scr[...] = y
            else:
                acc_scr[...] += y
        for j in range(sub):
            ybuf[s, :, j, :] = acc_scr[:, j * LANES:(j + 1) * LANES]

    @pl.when(i == nu)
    def _():
        scatter_rows(i - 1, 1 - s, 0, t)

    @pl.when(jnp.logical_and(i > nu, i < pl.num_programs(0) - 1))
    def _():
        rows[0] = jnp.zeros(rows.shape[1:], F32)
        fill = pltpu.make_async_copy(rows.at[0], y_ref.at[pl.ds(dst_ref[(i - 1) * t], t)], gsem.at[0])
        fill.start()
        fill.wait()


def _moe_fused(tile_group, n_used, n_valid, src, dst, h2t, w_router, b_router, wg, wu, wd, n_tiles, t):
    epg = wg.shape[1]
    last = n_tiles - 1
    sub = h2t.shape[1]
    d = sub * LANES
    full = lambda a: pl.BlockSpec(a.shape, lambda i, *prefetch: (0,) * a.ndim)
    by_group = lambda a: pl.BlockSpec((None,) + a.shape[1:], lambda i, tg, *rest: (tg[jnp.minimum(i, last)], 0, 0, 0))
    return pl.pallas_call(
        functools.partial(_moe_fused_kernel, t=t, epg=epg),
        grid_spec=pltpu.PrefetchScalarGridSpec(
            num_scalar_prefetch=5, grid=(n_tiles + 2,),
            in_specs=[pl.BlockSpec(memory_space=pl.ANY), full(w_router), full(b_router),
                      by_group(wg), by_group(wu), by_group(wd)],
            out_specs=pl.BlockSpec(memory_space=pl.ANY),
            scratch_shapes=[pltpu.VMEM((2, t, sub, LANES), F32), pltpu.VMEM((2, t, sub, LANES), F32),
                            pltpu.VMEM((t, d), BF16), pltpu.VMEM((t, d), F32), pltpu.VMEM((LANES, t), F32),
                            pltpu.SemaphoreType.DMA((2,)), pltpu.SemaphoreType.DMA((2,))]),
        out_shape=jax.ShapeDtypeStruct((n_tiles * t, sub, LANES), F32),
        compiler_params=_cparams("arbitrary"),
    )(tile_group, n_used, n_valid, src, dst, h2t, w_router, b_router, wg, wu, wd)


def _final_kernel(x_ref, y_ref, mod_ref, g_ref, o_ref):
    out = x_ref[...] + mod_ref[5:6, :] * _from_token_tiles(y_ref)
    ms = jnp.mean(out * out, axis=-1, keepdims=True)
    o_ref[...] = out * lax.rsqrt(ms + EPS) * g_ref[...]


def _final(x, y_rows, mod, g_final):
    b, l, d = x.shape
    t = min(512, l)
    nt = l // t
    return pl.pallas_call(
        _final_kernel,
        grid=(b, nt),
        in_specs=[pl.BlockSpec((None, t, d), lambda bi, i: (bi, i, 0)),
                  pl.BlockSpec((t,) + y_rows.shape[1:], lambda bi, i: (bi * nt + i, 0, 0)),
                  pl.BlockSpec((None, N_MOD, d), lambda bi, i: (bi, 0, 0)),
                  pl.BlockSpec((1, d), lambda bi, i: (0, 0))],
        out_specs=pl.BlockSpec((None, t, d), lambda bi, i: (bi, i, 0)),
        out_shape=jax.ShapeDtypeStruct((b, l, d), F32),
        compiler_params=_cparams("parallel", "parallel"),
    )(x, y_rows, mod, g_final)


def kernel(x, c, ctx, c_ctx, mod_w, mod_b, norm_mix_g, norm_ffn_g, w_in, conv_w, conv_b, rec_gate_a_w, rec_gate_a_b,
           rec_gate_x_w, rec_gate_x_b, rec_lambda, pool_w, pool_scale, fourier_w, w_out, router_group_w,
           router_group_b, router_expert_w, router_expert_b, expert_w_gate, expert_w_up, expert_w_down, final_norm_g):
    b, l, d = x.shape
    depth = mod_w.shape[0]
    d_rec = conv_w.shape[-1]
    d_pool = pool_scale.shape[-1]
    d_four = fourier_w.shape[1] * fourier_w.shape[2]
    widths = (d_rec, d_rec, d_pool, d_four)
    rows = l // GRID_W
    heads_half = REC_HEADS // 2
    n_exp = expert_w_gate.shape[1]

    cvec = jnp.concatenate([c, c_ctx[None, :], jnp.zeros((SUBLANES - b - 1, d), F32)], axis=0)
    mod_all = _modulation(cvec, mod_w, mod_b).reshape(depth, SUBLANES, N_MOD, d)
    g_final = final_norm_g.reshape(1, d)
    zeros_state = jnp.zeros((b, 2, d_rec), F32)
    pending = pending_ctx = None

    for li in range(depth):
        last = li == depth - 1
        mod_lat = mod_all[li, :b]
        mod_ctx = jnp.broadcast_to(mod_all[li, b][None], (b, N_MOD, d))
        w_in_l = w_in[li].astype(BF16)
        w_out_l = w_out[li].astype(BF16)
        wgate = jnp.stack([
            jnp.stack([jnp.concatenate([_block_diag(rec_gate_a_w[li, dd, hh * heads_half:(hh + 1) * heads_half]),
                                        _block_diag(rec_gate_x_w[li, dd, hh * heads_half:(hh + 1) * heads_half])], axis=1)
                       for hh in range(2)]) for dd in range(2)]).astype(BF16)
        bgate = jnp.stack([rec_gate_a_b[li], rec_gate_x_b[li]], axis=1)
        rec_p = (wgate, bgate, rec_lambda[li])
        proj_p = (w_in_l, conv_w[li], conv_b[li].reshape(1, d_rec), widths)
        pool_w_bd = _block_diag(pool_w[li]).astype(BF16)
        four_w_bd = _block_diag(fourier_w[li]).astype(BF16)
        w_router = jnp.concatenate([router_group_w[li], jnp.zeros((d, SUBLANES - N_GROUPS), F32), router_expert_w[li],
                                    jnp.zeros((d, LANES - SUBLANES - n_exp), F32)], axis=1).astype(BF16)
        b_router = jnp.concatenate([router_group_b[li], jnp.zeros((SUBLANES - N_GROUPS,), F32), router_expert_b[li],
                                    jnp.zeros((LANES - SUBLANES - n_exp,), F32)]).reshape(1, LANES)
        by_group = lambda w: w.astype(BF16).reshape((N_GROUPS, EXPERTS_PER_GROUP) + w.shape[1:])
        expert_w = (by_group(expert_w_gate[li]), by_group(expert_w_up[li]), by_group(expert_w_down[li]))
        g_ffn = norm_ffn_g[li].reshape(1, d)

        def mixer_tail(hf, hb, z_g, z_p, z_f, stream, mod, grid_rows):
            y_p = _pool(z_p, pool_w_bd, pool_scale[li], grid_rows)
            y_f = _fourier(z_f, four_w_bd)
            return _combine(hf, hb, z_g, y_p, y_f, stream, mod, w_out_l, g_ffn, w_router, b_router)

        def expert_stage(h2t, gid):
            n_tok = gid.size
            t_moe = min(MOE_TILE, h2t.shape[1])
            slot, tile_group, n_used, n_tiles = _sort_plan(gid.reshape(n_tok), N_GROUPS, t_moe)
            src = _invert(slot, n_tiles * t_moe)
            valid = src >= 0
            pad_rank = jnp.cumsum(jnp.logical_not(valid).astype(jnp.int32)) - 1
            dst = jnp.where(valid, src, n_tok + pad_rank)
            n_valid = jnp.sum(valid.reshape(n_tiles, t_moe).astype(jnp.int32), axis=1)
            return _moe_fused(tile_group, n_used, n_valid, jnp.maximum(src, 0), dst, h2t.reshape((n_tok,) + h2t.shape[2:]),
                              w_router, b_router, *expert_w, n_tiles, t_moe)

        if pending_ctx is None:
            uc, zc_g, zc_p, zc_f = _inproj(ctx, norm_mix_g[li], mod_ctx, *proj_p)
        else:
            ctx, uc, zc_g, zc_p, zc_f = _inproj(ctx, norm_mix_g[li], mod_ctx, *proj_p, residual=pending_ctx)
        hf_c, hb_c = _recurrent(uc, *rec_p, zeros_state, F32)
        state = jnp.stack([hf_c[:, -1, :], hb_c[:, 0, :]], axis=1)
        if not last:
            ctx, h2t_c, gid_c = mixer_tail(hf_c, hb_c, zc_g, zc_p, zc_f, ctx, mod_ctx, None)
            pending_ctx = (expert_stage(h2t_c, gid_c), 0, mod_ctx)

        if pending is None:
            u, z_g, z_p, z_f = _inproj(x, norm_mix_g[li], mod_lat, *proj_p)
        else:
            x, u, z_g, z_p, z_f = _inproj(x, norm_mix_g[li], mod_lat, *proj_p, residual=pending)
        hf, hb = _recurrent(u, *rec_p, state, BF16)
        x, h2t, gid = mixer_tail(hf, hb, z_g, z_p, z_f, x, mod_lat, rows)
        pending = (expert_stage(h2t, gid), 0, mod_lat)
    return _final(x, pending[0], pending[2], g_final)
```

```python
import functools
import math

import jax
import jax.numpy as jnp
from jax import lax
from jax.experimental import pallas as pl
from jax.experimental.pallas import tpu as pltpu

F32 = jnp.float32
BF16 = jnp.bfloat16

GRID_W = 64
N_MOD = 6
REC_HEADS = 8
CONV_W = 4
LRU_C = 8.0
POOL_WINDOWS = (2, 4, 8, 16)
FOUR_HEADS = 4
N_GROUPS = 4
EXPERTS_PER_GROUP = 4
EPS = 1e-6

LANES = 128
SUBLANES = 8
POOL_PAD = 8
VMEM_LIMIT = 56 * 1024 * 1024


def _cparams(*sem):
    return pltpu.CompilerParams(dimension_semantics=sem, vmem_limit_bytes=VMEM_LIMIT)


def _split_bf16(a):
    hi = a.astype(BF16)
    lo = (a - hi.astype(F32)).astype(BF16)
    return hi, lo


def _dot(a, b):
    return jnp.dot(a, b, preferred_element_type=F32)


def _sigmoid(x):
    return 0.5 * jnp.tanh(0.5 * x) + 0.5


def _mod_kernel(s_ref, w_ref, b_ref, o_ref):
    s = s_ref[...]
    s = s * jax.nn.sigmoid(s)
    s_hi, s_lo = _split_bf16(s)
    w_hi, w_lo = _split_bf16(w_ref[...])
    o_ref[...] = _dot(s_hi, w_hi) + _dot(s_hi, w_lo) + _dot(s_lo, w_hi) + b_ref[...]


def _modulation(cvec, mod_w, mod_b):
    depth, d, dm = mod_w.shape
    tn = dm // 4
    return pl.pallas_call(
        _mod_kernel,
        grid=(depth, dm // tn),
        in_specs=[pl.BlockSpec((SUBLANES, d), lambda l, j: (0, 0)),
                  pl.BlockSpec((None, d, tn), lambda l, j: (l, 0, j)),
                  pl.BlockSpec((None, 1, tn), lambda l, j: (l, 0, j))],
        out_specs=pl.BlockSpec((None, SUBLANES, tn), lambda l, j: (l, 0, j)),
        out_shape=jax.ShapeDtypeStruct((depth, SUBLANES, dm), F32),
        compiler_params=_cparams("parallel", "parallel"),
    )(cvec, mod_w, mod_b.reshape(depth, 1, dm))


def _rms_mod(x, g, shift, scale):
    ms = jnp.mean(x * x, axis=-1, keepdims=True)
    y = x * lax.rsqrt(ms + EPS) * g
    return y * (1.0 + scale) + shift


def _from_token_tiles(ref, lead=()):
    n = ref.shape[-2] // SUBLANES
    return jnp.concatenate([ref[lead + (pl.ds(j, n, stride=SUBLANES), slice(None))] for j in range(SUBLANES)], axis=-1)


def _to_token_tiles(ref, val, lead=()):
    n = val.shape[0]
    for j in range(SUBLANES):
        ref[lead + (pl.ds(j, n, stride=SUBLANES), slice(None))] = val[:, j * LANES:(j + 1) * LANES]


def _inproj_kernel(*refs, widths, with_residual, n_tiles):
    i = pl.program_id(1)
    if with_residual:
        (x_ref, xp_ref, xn_ref, y_ref, yp_ref, yn_ref, pmod_ref,
         g_ref, mod_ref, w_ref, cw_ref, cb_ref, xo_ref, *o_refs) = refs
        gate = pmod_ref[5:6, :]
        x = x_ref[...] + gate * _from_token_tiles(y_ref)
        xo_ref[...] = x
        x_halo = jnp.concatenate([xp_ref[...] + gate * _from_token_tiles(yp_ref),
                                  xn_ref[...] + gate * _from_token_tiles(yn_ref)], axis=0)
    else:
        x_ref, xp_ref, xn_ref, g_ref, mod_ref, w_ref, cw_ref, cb_ref, *o_refs = refs
        x = x_ref[...]
        x_halo = jnp.concatenate([xp_ref[...], xn_ref[...]], axis=0)
    t = x.shape[0]
    d_rec = widths[0]
    z = _dot(_rms_mod(x, g_ref[...], mod_ref[0:1, :], mod_ref[1:2, :]).astype(BF16), w_ref[...])
    zh = _dot(_rms_mod(x_halo, g_ref[...], mod_ref[0:1, :], mod_ref[1:2, :]).astype(BF16), w_ref[:, :d_rec])
    zz = jnp.concatenate([jnp.where(i == 0, 0.0, zh[:SUBLANES]), z[:, :d_rec],
                          jnp.where(i == n_tiles - 1, 0.0, zh[SUBLANES:])], axis=0)
    u = cb_ref[...]
    for k in range(CONV_W):
        o = SUBLANES - CONV_W // 2 + k
        u = u + cw_ref[k:k + 1, :] * zz[o:o + t, :]
    seg = t // SUBLANES
    for q in range(d_rec // LANES):
        for s in range(SUBLANES):
            o_refs[0][q, pl.ds(s, seg, stride=SUBLANES), :] = u[s * seg:(s + 1) * seg, q * LANES:(q + 1) * LANES]
    off = d_rec
    for o_ref, wd in zip(o_refs[1:], widths[1:]):
        o_ref[...] = z[:, off:off + wd].astype(o_ref.dtype)
        off += wd


def _inproj(x, g, mod, w_bf16, conv_w, conv_b, widths, residual=None):
    b, l, d = x.shape
    t = min(512, l)
    nt = l // t
    per = t // SUBLANES
    n8 = l // SUBLANES
    tok = lambda c: pl.BlockSpec((None, t, c), lambda bi, i: (bi, i, 0))
    prev8 = lambda bi, i: jnp.maximum(i * per - 1, 0)
    next8 = lambda bi, i: jnp.minimum((i + 1) * per, n8 - 1)
    x_specs = [tok(d), pl.BlockSpec((None, SUBLANES, d), lambda bi, i: (bi, prev8(bi, i), 0)),
               pl.BlockSpec((None, SUBLANES, d), lambda bi, i: (bi, next8(bi, i), 0))]
    modspec = pl.BlockSpec((None, N_MOD, d), lambda bi, i: (bi, 0, 0))
    full = lambda a: pl.BlockSpec(a.shape, lambda bi, i: (0,) * a.ndim)
    tail_specs = [pl.BlockSpec((1, d), lambda bi, i: (0, 0)), modspec, full(w_bf16), full(conv_w), full(conv_b)]
    tail_args = (g.reshape(1, d), mod, w_bf16, conv_w, conv_b)
    slabs = widths[0] // LANES
    z_specs = ([pl.BlockSpec((None, None, slabs, t, LANES), lambda bi, i: (bi, i, 0, 0, 0))]
               + [tok(wd) for wd in widths[1:]])
    z_shapes = ([jax.ShapeDtypeStruct((b, nt, slabs, t, LANES), F32)]
                + [jax.ShapeDtypeStruct((b, l, wd), BF16 if k == 0 else F32) for k, wd in enumerate(widths[1:])])
    if residual is None:
        in_specs, args, out_specs, out_shape = x_specs + tail_specs, (x, x, x) + tail_args, z_specs, z_shapes
    else:
        y_rows, row0, prev_mod = residual
        blk0, blk8 = row0 // t, row0 // SUBLANES
        halo = SUBLANES * SUBLANES
        y_specs = [pl.BlockSpec((t * SUBLANES, LANES), lambda bi, i: (blk0 + bi * nt + i, 0)),
                   pl.BlockSpec((halo, LANES), lambda bi, i: (blk8 + bi * n8 + prev8(bi, i), 0)),
                   pl.BlockSpec((halo, LANES), lambda bi, i: (blk8 + bi * n8 + next8(bi, i), 0))]
        in_specs = x_specs + y_specs + [modspec] + tail_specs
        args = (x, x, x, y_rows, y_rows, y_rows, prev_mod) + tail_args
        out_specs, out_shape = [tok(d)] + z_specs, [jax.ShapeDtypeStruct((b, l, d), F32)] + z_shapes
    return pl.pallas_call(
        functools.partial(_inproj_kernel, widths=widths, with_residual=residual is not None, n_tiles=nt),
        grid=(b, nt), in_specs=in_specs, out_specs=out_specs, out_shape=out_shape,
        compiler_params=_cparams("parallel", "parallel"),
    )(*args)


def _rec_kernel(uf_ref, ub_ref, wg_ref, bg_ref, lam_ref, h0_ref, hf_ref, hb_ref, a_scr, b_scr, carry_scr, *, tc):
    i = pl.program_id(1)
    nq = uf_ref.shape[0]
    c = nq * LANES
    half = c // 2
    seg = tc // SUBLANES
    lanes = lambda q: slice(q * LANES, (q + 1) * LANES)

    @pl.when(i == 0)
    def _():
        carry_scr[...] = h0_ref[...]

    def coefficients(d, u_ref):
        u = jnp.concatenate([u_ref[q] for q in range(nq)], axis=-1)
        log_decay = -LRU_C * jax.nn.softplus(-lam_ref[d:d + 1, :])
        for hh in range(2):
            cols = slice(hh * half, (hh + 1) * half)
            uh = u[:, cols]
            pre = _dot(uh.astype(BF16), wg_ref[d, hh])
            r = _sigmoid(pre[:, :half] + bg_ref[d, 0:1, cols])
            ig = _sigmoid(pre[:, half:] + bg_ref[d, 1:2, cols])
            log_a = log_decay[:, cols] * r
            a = jnp.exp(log_a)
            th = jnp.tanh(log_a)
            bt = jnp.sqrt(-2.0 * th / (1.0 - th)) * (ig * uh)
            for k in range(half // LANES):
                q = hh * (half // LANES) + k
                a_scr[d, q] = a[:, lanes(k)]
                b_scr[d, q] = bt[:, lanes(k)]

    coefficients(0, uf_ref)
    coefficients(1, ub_ref)

    def step(d, j, h, acc):
        rows8 = pl.ds(pl.multiple_of(j * SUBLANES, SUBLANES), SUBLANES)
        a8 = jnp.concatenate([a_scr[d, q, rows8, :] for q in range(nq)], axis=-1)
        b8 = jnp.concatenate([b_scr[d, q, rows8, :] for q in range(nq)], axis=-1)
        h = a8 * h + b8
        acc = a8 * acc
        for q in range(nq):
            b_scr[d, q, rows8, :] = h[:, lanes(q)]
            a_scr[d, q, rows8, :] = acc[:, lanes(q)]
        return h, acc

    def body(j, carry):
        hf, af, hb, ab = carry
        hf, af = step(0, j, hf, af)
        hb, ab = step(1, seg - 1 - j, hb, ab)
        return hf, af, hb, ab

    zero = jnp.zeros((SUBLANES, c), F32)
    one = jnp.ones((SUBLANES, c), F32)
    hf_end, af_end, hb_end, ab_end = lax.fori_loop(0, seg, body, (zero, one, zero, one), unroll=2)

    def chain(h_end, a_end, state, order):
        entering = [None] * SUBLANES
        for s in order:
            entering[s] = state
            state = a_end[s:s + 1, :] * state + h_end[s:s + 1, :]
        return jnp.concatenate(entering, axis=0), state

    cin_f, out_f = chain(hf_end, af_end, carry_scr[0:1, :], range(SUBLANES))
    cin_b, out_b = chain(hb_end, ab_end, carry_scr[1:2, :], reversed(range(SUBLANES)))
    carry_scr[0:1, :] = out_f
    carry_scr[1:2, :] = out_b

    for d, cin, out_ref in ((0, cin_f, hf_ref), (1, cin_b, hb_ref)):
        for q in range(nq):
            full_h = (b_scr[d, q].reshape(seg, SUBLANES, LANES)
                      + a_scr[d, q].reshape(seg, SUBLANES, LANES) * cin[None, :, lanes(q)])
            b_scr[d, q] = full_h.reshape(tc, LANES)
            for s in range(SUBLANES):
                out_ref[s * seg:(s + 1) * seg, lanes(q)] = (
                    b_scr[d, q, pl.ds(s, seg, stride=SUBLANES), :].astype(out_ref.dtype))


def _recurrent(u, wgate, bgate, lam, h0, out_dtype):
    b, nc, nq, tc, _ = u.shape
    c = nq * LANES
    u_fwd = pl.BlockSpec((None, None, nq, tc, LANES), lambda bi, i: (bi, i, 0, 0, 0))
    u_bwd = pl.BlockSpec((None, None, nq, tc, LANES), lambda bi, i: (bi, nc - 1 - i, 0, 0, 0))
    h_fwd = pl.BlockSpec((None, tc, c), lambda bi, i: (bi, i, 0))
    h_bwd = pl.BlockSpec((None, tc, c), lambda bi, i: (bi, nc - 1 - i, 0))
    full = lambda a: pl.BlockSpec(a.shape, lambda bi, i: (0,) * a.ndim)
    return pl.pallas_call(
        functools.partial(_rec_kernel, tc=tc),
        grid=(b, nc),
        in_specs=[u_fwd, u_bwd, full(wgate), full(bgate), full(lam),
                  pl.BlockSpec((None, 2, c), lambda bi, i: (bi, 0, 0))],
        out_specs=[h_fwd, h_bwd],
        out_shape=[jax.ShapeDtypeStruct((b, nc * tc, c), out_dtype)] * 2,
        scratch_shapes=[pltpu.VMEM((2, nq, tc, LANES), F32), pltpu.VMEM((2, nq, tc, LANES), F32),
                        pltpu.VMEM((2, c), F32)],
        compiler_params=_cparams("parallel", "arbitrary"),
    )(u, u, wgate, bgate, lam, h0)


def _dft_tables(n):
    j = lax.broadcasted_iota(jnp.int32, (n, n), 0)
    k = lax.broadcasted_iota(jnp.int32, (n, n), 1)
    ang = ((j * k) % n).astype(F32) * (2.0 * math.pi / n)
    return jnp.cos(ang), jnp.sin(ang)


def _block_diag(w):
    return jax.scipy.linalg.block_diag(*[w[h] for h in range(w.shape[0])])


def _channel_stage(gr, gi, c_ref, s_ref, w_ref, scale):
    f = (_dot(gr.astype(BF16), c_ref[...]) + _dot(gi.astype(BF16), s_ref[...])) * scale
    return _dot(f.astype(BF16), w_ref[...])


def _slabs_from(scr, val):
    for q in range(scr.shape[0]):
        scr[q] = val[:, q * LANES:(q + 1) * LANES]


def _slab_rows(scr, start, n):
    return jnp.concatenate([scr[q, pl.ds(start, n, stride=SUBLANES), :] for q in range(scr.shape[0])], axis=-1)


def _set_slab_rows(scr, start, val):
    for q in range(scr.shape[0]):
        scr[q, pl.ds(start, val.shape[0], stride=SUBLANES), :] = val[:, q * LANES:(q + 1) * LANES]


def _slabs_value(scr):
    return jnp.concatenate([scr[q] for q in range(scr.shape[0])], axis=-1)


def _four1_kernel(x_ref, f1_ref, twc_ref, tws_ref, ar_ref, ai_ref, xs, ars, ais, *, n1):
    shape = x_ref.shape
    _slabs_from(xs, x_ref[...].reshape(n1 * SUBLANES, shape[-1]))
    for jj in range(SUBLANES):
        a = _dot(f1_ref[...], _slab_rows(xs, jj, n1).astype(BF16))
        ar, ai = a[:n1], a[n1:]
        cc = twc_ref[:, jj:jj + 1]
        ss = tws_ref[:, jj:jj + 1]
        _set_slab_rows(ars, jj, ar * cc + ai * ss)
        _set_slab_rows(ais, jj, ai * cc - ar * ss)
    ar_ref[...] = _slabs_value(ars).reshape(shape)
    ai_ref[...] = _slabs_value(ais).reshape(shape)


def _four2_kernel(ar_ref, ai_ref, f2_ref, c_ref, s_ref, w_ref, o_ref, ys, *, n2, scale):
    grs, gis = [], []
    for kk in range(SUBLANES):
        slab = jnp.concatenate([ar_ref[kk], ai_ref[kk]], axis=0).astype(BF16)
        g = _dot(f2_ref[...], slab)
        grs.append(g[:n2])
        gis.append(g[n2:])
    y = _channel_stage(jnp.concatenate(grs, axis=0), jnp.concatenate(gis, axis=0), c_ref, s_ref, w_ref, scale)
    for kk in range(SUBLANES):
        _set_slab_rows(ys, kk, y[kk * n2:(kk + 1) * n2])
    o_ref[...] = _slabs_value(ys).reshape(o_ref.shape)


def _four_small_kernel(x_ref, f_ref, c_ref, s_ref, w_ref, o_ref, *, n, scale):
    g = _dot(f_ref[...], x_ref[...].astype(BF16))
    o_ref[...] = _channel_stage(g[:n], g[n:], c_ref, s_ref, w_ref, scale)


def _fourier(z_f, four_w_bd):
    b, l, nch = z_f.shape
    hd = nch // FOUR_HEADS
    scale = 1.0 / math.sqrt(l * hd)
    c_h, s_h = _dft_tables(hd)
    eye = jnp.eye(FOUR_HEADS, dtype=F32)
    c_bd = jnp.kron(eye, c_h).astype(BF16)
    s_bd = jnp.kron(eye, s_h).astype(BF16)
    full = lambda a, nd: pl.BlockSpec(a.shape, lambda *_: (0,) * a.ndim)
    if l <= 256:
        cn, sn = _dft_tables(l)
        f = jnp.concatenate([cn, -sn], axis=0).astype(BF16)
        return pl.pallas_call(
            functools.partial(_four_small_kernel, n=l, scale=scale),
            grid=(b,),
            in_specs=[pl.BlockSpec((None, l, nch), lambda bi: (bi, 0, 0)),
                      full(f, 1), full(c_bd, 1), full(s_bd, 1), full(four_w_bd, 1)],
            out_specs=pl.BlockSpec((None, l, nch), lambda bi: (bi, 0, 0)),
            out_shape=jax.ShapeDtypeStruct((b, l, nch), F32),
            compiler_params=_cparams("parallel"),
        )(z_f, f, c_bd, s_bd, four_w_bd)

    n2 = LANES
    n1 = l // n2
    nj = n2 // SUBLANES
    c1, s1 = _dft_tables(n1)
    f1 = jnp.concatenate([c1, -s1], axis=0).astype(BF16)
    c2, s2 = _dft_tables(n2)
    f2 = jnp.concatenate([jnp.concatenate([c2, s2], axis=1),
                          jnp.concatenate([-s2, c2], axis=1)], axis=0).astype(BF16)
    k1 = lax.broadcasted_iota(jnp.int32, (n1, n2), 0)
    t2 = lax.broadcasted_iota(jnp.int32, (n1, n2), 1)
    ang = (k1 * t2).astype(F32) * (2.0 * math.pi / l)
    twc = jnp.cos(ang).reshape(n1, nj, SUBLANES).transpose(1, 0, 2)
    tws = jnp.sin(ang).reshape(n1, nj, SUBLANES).transpose(1, 0, 2)
    ar, ai = pl.pallas_call(
        functools.partial(_four1_kernel, n1=n1),
        grid=(b, nj),
        in_specs=[pl.BlockSpec((None, n1, SUBLANES, nch), lambda bi, j: (bi, 0, j, 0)),
                  full(f1, 2),
                  pl.BlockSpec((None, n1, SUBLANES), lambda bi, j: (j, 0, 0)),
                  pl.BlockSpec((None, n1, SUBLANES), lambda bi, j: (j, 0, 0))],
        out_specs=[pl.BlockSpec((None, n1, SUBLANES, nch), lambda bi, j: (bi, 0, j, 0))] * 2,
        out_shape=[jax.ShapeDtypeStruct((b, n1, n2, nch), F32)] * 2,
        scratch_shapes=[pltpu.VMEM((nch // LANES, n1 * SUBLANES, LANES), F32)] * 3,
        compiler_params=_cparams("parallel", "parallel"),
    )(z_f.reshape(b, n1, n2, nch), f1, twc, tws)
    y = pl.pallas_call(
        functools.partial(_four2_kernel, n2=n2, scale=scale),
        grid=(b, n1 // SUBLANES),
        in_specs=[pl.BlockSpec((None, SUBLANES, n2, nch), lambda bi, j: (bi, j, 0, 0)),
                  pl.BlockSpec((None, SUBLANES, n2, nch), lambda bi, j: (bi, j, 0, 0)),
                  full(f2, 2), full(c_bd, 2), full(s_bd, 2), full(four_w_bd, 2)],
        out_specs=pl.BlockSpec((None, n2, SUBLANES, nch), lambda bi, j: (bi, 0, j, 0)),
        out_shape=jax.ShapeDtypeStruct((b, n2, n1, nch), F32),
        scratch_shapes=[pltpu.VMEM((nch // LANES, n2 * SUBLANES, LANES), F32)],
        compiler_params=_cparams("parallel", "parallel"),
    )(ar, ai, f2, c_bd, s_bd, four_w_bd)
    return y.reshape(b, l, nch)


def _window_counts(idx, n, w):
    return (jnp.minimum(idx + (w - w // 2), n) - jnp.maximum(idx - w // 2, 0)).astype(F32)


def _pool_kernel(cur_ref, prev_ref, next_ref, pw_ref, ps_ref, o_ref, *, r, w, rows_total, n_tiles):
    i = pl.program_id(1)
    two_d = rows_total > 1
    halo = SUBLANES if two_d else 0
    nr = r + 2 * halo
    stride = w + POOL_PAD
    flat = nr * stride
    lane = lax.broadcasted_iota(jnp.int32, (1, 1, LANES), 2)
    low = lane < (LANES // 2)
    col = lax.broadcasted_iota(jnp.int32, (1, w, 1), 1)
    grow = lax.broadcasted_iota(jnp.int32, (r, 1, 1), 0) + i * r
    outs = []
    for hh in range(2):
        cols = slice(hh * LANES, (hh + 1) * LANES)
        w_lo, w_hi = POOL_WINDOWS[2 * hh], POOL_WINDOWS[2 * hh + 1]
        cur = cur_ref[:, cols]
        if two_d:
            prev = jnp.where(i == 0, 0.0, prev_ref[:, cols])
            nxt = jnp.where(i == n_tiles - 1, 0.0, next_ref[:, cols])
            ext = jnp.concatenate([prev, cur, nxt], axis=0)
        else:
            ext = cur
        ext = ext.reshape(nr, w, LANES)
        x = jnp.concatenate([jnp.zeros((nr, POOL_PAD, LANES), F32), ext], axis=1).reshape(flat, LANES)
        p = x + pltpu.roll(x, 1, 0)
        sums = {2: p}
        for ww, s in ((4, 1), (8, 2), (16, 4)):
            if ww > w_hi:
                break
            p = pltpu.roll(p, s, 0) + pltpu.roll(p, flat - s, 0)
            sums[ww] = p
        y = jnp.where(low[0], sums[w_lo], sums[w_hi]).reshape(nr, stride, LANES)[:, POOL_PAD:, :]
        if two_d:
            q2 = y[0:nr - 1] + y[1:nr]
            rsum = {2: q2[7:7 + r]}
            q4 = q2[0:nr - 3] + q2[2:nr - 1]
            rsum[4] = q4[6:6 + r]
            if w_hi > 4:
                q8 = q4[0:nr - 7] + q4[4:nr - 3]
                rsum[8] = q8[4:4 + r]
                q16 = q8[0:nr - 15] + q8[8:nr - 7]
                rsum[16] = q16[0:r]
            tot = jnp.where(low, rsum[w_lo], rsum[w_hi])
            cnt_r = jnp.where(low, _window_counts(grow, rows_total, w_lo), _window_counts(grow, rows_total, w_hi))
            tot = tot / cnt_r
        else:
            tot = y
        cnt_c = jnp.where(low, _window_counts(col, w, w_lo), _window_counts(col, w, w_hi))
        pooled = (tot / cnt_c).reshape(r * w, LANES)
        outs.append(pooled - cur)
    d = jnp.concatenate(outs, axis=-1)
    o_ref[...] = (_dot(d.astype(BF16), pw_ref[...]) * ps_ref[...]).astype(o_ref.dtype)


def _pool(z_p, pool_w_bd, pool_scale, rows):
    b, l, nch = z_p.shape
    if rows is None:
        rows_total, w, r = 1, l, 1
    else:
        rows_total, w, r = rows, l // rows, min(16, rows)
    t = r * w
    n_tiles = l // t
    hb = SUBLANES * w if rows is not None else t
    per = t // hb
    nhb = l // hb
    return pl.pallas_call(
        functools.partial(_pool_kernel, r=r, w=w, rows_total=rows_total, n_tiles=n_tiles),
        grid=(b, n_tiles),
        in_specs=[pl.BlockSpec((None, t, nch), lambda bi, i: (bi, i, 0)),
                  pl.BlockSpec((None, hb, nch), lambda bi, i: (bi, jnp.maximum(i * per - 1, 0), 0)),
                  pl.BlockSpec((None, hb, nch), lambda bi, i: (bi, jnp.minimum((i + 1) * per, nhb - 1), 0)),
                  pl.BlockSpec(pool_w_bd.shape, lambda bi, i: (0, 0)),
                  pl.BlockSpec((1, nch), lambda bi, i: (0, 0))],
        out_specs=pl.BlockSpec((None, t, nch), lambda bi, i: (bi, i, 0)),
        out_shape=jax.ShapeDtypeStruct((b, l, nch), BF16),
        compiler_params=_cparams("parallel", "parallel"),
    )(z_p, z_p, z_p, pool_w_bd, pool_scale.reshape(1, nch))


def _first_argmax4(v):
    m = jnp.maximum(jnp.maximum(v[0], v[1]), jnp.maximum(v[2], v[3]))
    idx = jnp.where(v[0] >= m, 0, jnp.where(v[1] >= m, 1, jnp.where(v[2] >= m, 2, 3)))
    return m, idx


def _select4(idx, v):
    return jnp.where(idx == 0, v[0], jnp.where(idx == 1, v[1], jnp.where(idx == 2, v[2], v[3])))


def _router_logits(h_bf16, wr_ref, br_ref):
    return (_dot(h_bf16, wr_ref[...]) + br_ref[...]).T


def _group_logits(lt):
    return [lt[g:g + 1, :] for g in range(N_GROUPS)]


def _expert_gates(lt, group):
    lg = _group_logits(lt)
    m, _ = _first_argmax4(lg)
    den = sum(jnp.exp(v - m) for v in lg)
    p_group = jnp.exp(_select4(group, lg) - m) / den
    sel = [_select4(group, [lt[SUBLANES + EXPERTS_PER_GROUP * g + e:SUBLANES + EXPERTS_PER_GROUP * g + e + 1, :]
                            for g in range(N_GROUPS)]) for e in range(EXPERTS_PER_GROUP)]
    v1, e1 = _first_argmax4(sel)
    rest = [jnp.where(e1 == e, -jnp.inf, sel[e]) for e in range(EXPERTS_PER_GROUP)]
    v2, e2 = _first_argmax4(rest)
    t2 = jnp.exp(v2 - v1)
    w1 = p_group / (1.0 + t2)
    w2 = p_group * t2 / (1.0 + t2)
    return [jnp.where(e1 == e, w1, jnp.where(e2 == e, w2, 0.0)) for e in range(EXPERTS_PER_GROUP)]


def _combine_kernel(hf_ref, hb_ref, zg_ref, yp_ref, yf_ref, x_ref, mod_ref, wo_ref, g_ref, wr_ref, br_ref,
                    xo_ref, h2t_ref, gid_ref):
    y_a = ((hf_ref[...].astype(F32) + hb_ref[...].astype(F32))
           * jax.nn.gelu(zg_ref[...].astype(F32), approximate=True))
    ycat = jnp.concatenate([y_a.astype(BF16), yp_ref[...].astype(BF16), yf_ref[...].astype(BF16)], axis=-1)
    x_new = x_ref[...] + mod_ref[2:3, :] * _dot(ycat, wo_ref[...])
    xo_ref[...] = x_new
    h2 = _rms_mod(x_new, g_ref[...], mod_ref[3:4, :], mod_ref[4:5, :])
    _to_token_tiles(h2t_ref, h2)
    _, gidx = _first_argmax4(_group_logits(_router_logits(h2.astype(BF16), wr_ref, br_ref)))
    gid_ref[...] = gidx


def _combine(hf, hb, z_g, y_p, y_f, x, mod, w_out, g_ffn, w_router, b_router):
    b, l, d = x.shape
    assert d == SUBLANES * LANES, "a token tile is one (8, 128) f32 tile"
    t = min(512, l)
    tok = lambda c: pl.BlockSpec((None, t, c), lambda bi, i: (bi, i, 0))
    full = lambda a: pl.BlockSpec(a.shape, lambda bi, i: (0,) * a.ndim)
    return pl.pallas_call(
        _combine_kernel,
        grid=(b, l // t),
        in_specs=[tok(hf.shape[-1]), tok(hb.shape[-1]), tok(z_g.shape[-1]), tok(y_p.shape[-1]), tok(y_f.shape[-1]),
                  tok(d), pl.BlockSpec((None, N_MOD, d), lambda bi, i: (bi, 0, 0)),
                  full(w_out), full(g_ffn), full(w_router), full(b_router)],
        out_specs=[tok(d), pl.BlockSpec((None, t * SUBLANES, LANES), lambda bi, i: (bi, i, 0)),
                   pl.BlockSpec((None, 1, t), lambda bi, i: (bi, 0, i))],
        out_shape=[jax.ShapeDtypeStruct((b, l, d), F32), jax.ShapeDtypeStruct((b, l * SUBLANES, LANES), F32),
                   jax.ShapeDtypeStruct((b, 1, l), jnp.int32)],
        compiler_params=_cparams("parallel", "parallel"),
    )(hf, hb, z_g, y_p, y_f, x, mod, w_out, g_ffn, w_router, b_router)


MOE_TILE = 512


def _sort_plan(gid, n_groups, t):
    n = gid.shape[0]
    onehot = (gid[:, None] == jnp.arange(n_groups, dtype=jnp.int32)[None, :]).astype(jnp.int32)
    csum = jnp.cumsum(onehot, axis=0)
    rank = jnp.sum((csum - onehot) * onehot, axis=1)
    tiles = (csum[-1] + t - 1) // t
    tile_end = jnp.cumsum(tiles)
    slot = jnp.sum(onehot * (tile_end - tiles)[None, :], axis=1) * t + rank
    n_tiles = n // t + n_groups
    tile_idx = jnp.arange(n_tiles, dtype=jnp.int32)
    tile_group = jnp.minimum(jnp.sum((tile_idx[:, None] >= tile_end[None, :]).astype(jnp.int32), axis=1), n_groups - 1)
    return slot.astype(jnp.int32), tile_group.astype(jnp.int32), tile_end[-1:].astype(jnp.int32), n_tiles


def _invert_kernel(slot_ref, src_ref):
    n_rows = src_ref.shape[0]
    n = slot_ref.shape[0]

    def clear(s, carry):
        src_ref[s] = -1
        return carry

    lax.fori_loop(0, n_rows, clear, 0, unroll=8)

    def put(tok, carry):
        src_ref[slot_ref[tok]] = tok
        return carry

    lax.fori_loop(0, n, put, 0, unroll=8)


def _invert(slot, n_rows):
    return pl.pallas_call(
        _invert_kernel,
        in_specs=[pl.BlockSpec(memory_space=pltpu.SMEM)],
        out_specs=pl.BlockSpec(memory_space=pltpu.SMEM),
        out_shape=jax.ShapeDtypeStruct((n_rows,), jnp.int32),
    )(slot)


def _moe_fused_kernel(tg_ref, nu_ref, nv_ref, src_ref, dst_ref, h_ref, wr_ref, br_ref, wg_ref, wu_ref, wd_ref, y_ref,
                      rows, ybuf, hb_scr, acc_scr, gt_scr, gsem, ssem, *, t, epg):
    i = pl.program_id(0)
    nu = nu_ref[0]
    s = lax.rem(i, 2)
    part = t // epg
    tile_rows = t * SUBLANES

    def token(ref, first_row):
        return ref.at[pl.ds(pl.multiple_of(first_row, SUBLANES), SUBLANES)]

    def gather_rows(tile, slot, lo, hi):
        def body(g, carry):
            for k in range(SUBLANES):
                r = g * SUBLANES + k
                pltpu.make_async_copy(token(h_ref, src_ref[tile * t + r]), token(rows.at[slot], r * SUBLANES),
                                      gsem.at[slot]).start(priority=k % 2)
            return carry
        lax.fori_loop(lo // SUBLANES, hi // SUBLANES, body, 0)

    def scatter_rows(tile, slot, lo, hi):
        def body(g, carry):
            for k in range(SUBLANES):
                r = g * SUBLANES + k
                pltpu.make_async_copy(token(ybuf.at[slot], r * SUBLANES), token(y_ref, dst_ref[tile * t + r]),
                                      ssem.at[slot]).start(priority=k % 2)
            return carry
        lax.fori_loop(lo // SUBLANES, hi // SUBLANES, body, 0)

    def wait_gather(slot):
        pltpu.make_async_copy(h_ref.at[pl.ds(0, tile_rows)], rows.at[slot], gsem.at[slot]).wait()

    def wait_scatter(slot):
        pltpu.make_async_copy(ybuf.at[slot], y_ref.at[pl.ds(0, tile_rows)], ssem.at[slot]).wait()

    @pl.when(i == 0)
    def _():
        gt_scr[...] = jnp.zeros_like(gt_scr)
        gather_rows(0, 0, 0, t)

    @pl.when(jnp.logical_and(i >= 2, i - 2 < nu))
    def _():
        wait_scatter(s)

    @pl.when(i < nu)
    def _():
        wait_gather(s)
        hb_scr[...] = _from_token_tiles(rows, (s,)).astype(BF16)
        tok_id = lax.broadcasted_iota(jnp.int32, (1, t), 1)
        gate_rows = _expert_gates(_router_logits(hb_scr[...], wr_ref, br_ref), tg_ref[i])
        for e in range(epg):
            gt_scr[e:e + 1, :] = jnp.where(tok_id < nv_ref[i], gate_rows[e], 0.0)
        gates = gt_scr[...].T
        for e in range(epg):
            @pl.when(i + 1 < nu)
            def _():
                gather_rows(i + 1, 1 - s, e * part, (e + 1) * part)

            @pl.when(i >= 1)
            def _():
                scatter_rows(i - 1, 1 - s, e * part, (e + 1) * part)

            hid = jax.nn.silu(_dot(hb_scr[...], wg_ref[e])) * _dot(hb_scr[...], wu_ref[e])
            y = gates[:, e:e + 1] * _dot(hid.astype(BF16), wd_ref[e])
            if e == 0:
                acc_scr[...] = y
            else:
                acc_scr[...] += y
        _to_token_tiles(ybuf, acc_scr[...], (s,))

    @pl.when(i == nu)
    def _():
        scatter_rows(i - 1, 1 - s, 0, t)

    @pl.when(jnp.logical_and(i > nu, i < pl.num_programs(0) - 1))
    def _():
        rows[0] = jnp.zeros(rows.shape[1:], F32)
        first = pl.multiple_of(dst_ref[(i - 1) * t], SUBLANES)
        fill = pltpu.make_async_copy(rows.at[0], y_ref.at[pl.ds(first, tile_rows)], gsem.at[0])
        fill.start()
        fill.wait()


def _moe_fused(tile_group, n_used, n_valid, src, dst, h2t, w_router, b_router, wg, wu, wd, n_tiles, t):
    epg = wg.shape[1]
    last = n_tiles - 1
    d = SUBLANES * LANES
    tile_rows = t * SUBLANES
    full = lambda a: pl.BlockSpec(a.shape, lambda i, *prefetch: (0,) * a.ndim)
    by_group = lambda a: pl.BlockSpec((None,) + a.shape[1:], lambda i, tg, *rest: (tg[jnp.minimum(i, last)], 0, 0, 0))
    return pl.pallas_call(
        functools.partial(_moe_fused_kernel, t=t, epg=epg),
        grid_spec=pltpu.PrefetchScalarGridSpec(
            num_scalar_prefetch=5, grid=(n_tiles + 2,),
            in_specs=[pl.BlockSpec(memory_space=pl.ANY), full(w_router), full(b_router),
                      by_group(wg), by_group(wu), by_group(wd)],
            out_specs=pl.BlockSpec(memory_space=pl.ANY),
            scratch_shapes=[pltpu.VMEM((2, tile_rows, LANES), F32), pltpu.VMEM((2, tile_rows, LANES), F32),
                            pltpu.VMEM((t, d), BF16), pltpu.VMEM((t, d), F32), pltpu.VMEM((LANES, t), F32),
                            pltpu.SemaphoreType.DMA((2,)), pltpu.SemaphoreType.DMA((2,))]),
        out_shape=jax.ShapeDtypeStruct((n_tiles * tile_rows, LANES), F32),
        compiler_params=_cparams("arbitrary"),
    )(tile_group, n_used, n_valid, src, dst, h2t, w_router, b_router, wg, wu, wd)


def _final_kernel(x_ref, y_ref, mod_ref, g_ref, o_ref):
    out = x_ref[...] + mod_ref[5:6, :] * _from_token_tiles(y_ref)
    ms = jnp.mean(out * out, axis=-1, keepdims=True)
    o_ref[...] = out * lax.rsqrt(ms + EPS) * g_ref[...]


def _final(x, y_rows, mod, g_final):
    b, l, d = x.shape
    t = min(512, l)
    nt = l // t
    return pl.pallas_call(
        _final_kernel,
        grid=(b, nt),
        in_specs=[pl.BlockSpec((None, t, d), lambda bi, i: (bi, i, 0)),
                  pl.BlockSpec((t * SUBLANES, LANES), lambda bi, i: (bi * nt + i, 0)),
                  pl.BlockSpec((None, N_MOD, d), lambda bi, i: (bi, 0, 0)),
                  pl.BlockSpec((1, d), lambda bi, i: (0, 0))],
        out_specs=pl.BlockSpec((None, t, d), lambda bi, i: (bi, i, 0)),
        out_shape=jax.ShapeDtypeStruct((b, l, d), F32),
        compiler_params=_cparams("parallel", "parallel"),
    )(x, y_rows, mod, g_final)


def kernel(x, c, ctx, c_ctx, mod_w, mod_b, norm_mix_g, norm_ffn_g, w_in, conv_w, conv_b, rec_gate_a_w, rec_gate_a_b,
           rec_gate_x_w, rec_gate_x_b, rec_lambda, pool_w, pool_scale, fourier_w, w_out, router_group_w,
           router_group_b, router_expert_w, router_expert_b, expert_w_gate, expert_w_up, expert_w_down, final_norm_g):
    b, l, d = x.shape
    depth = mod_w.shape[0]
    d_rec = conv_w.shape[-1]
    d_pool = pool_scale.shape[-1]
    d_four = fourier_w.shape[1] * fourier_w.shape[2]
    widths = (d_rec, d_rec, d_pool, d_four)
    rows = l // GRID_W
    heads_half = REC_HEADS // 2
    n_exp = expert_w_gate.shape[1]

    cvec = jnp.concatenate([c, c_ctx[None, :], jnp.zeros((SUBLANES - b - 1, d), F32)], axis=0)
    mod_all = _modulation(cvec, mod_w, mod_b).reshape(depth, SUBLANES, N_MOD, d)
    g_final = final_norm_g.reshape(1, d)
    zeros_state = jnp.zeros((b, 2, d_rec), F32)
    pending = pending_ctx = None

    for li in range(depth):
        last = li == depth - 1
        mod_lat = mod_all[li, :b]
        mod_ctx = jnp.broadcast_to(mod_all[li, b][None], (b, N_MOD, d))
        w_in_l = w_in[li].astype(BF16)
        w_out_l = w_out[li].astype(BF16)
        wgate = jnp.stack([
            jnp.stack([jnp.concatenate([_block_diag(rec_gate_a_w[li, dd, hh * heads_half:(hh + 1) * heads_half]),
                                        _block_diag(rec_gate_x_w[li, dd, hh * heads_half:(hh + 1) * heads_half])], axis=1)
                       for hh in range(2)]) for dd in range(2)]).astype(BF16)
        bgate = jnp.stack([rec_gate_a_b[li], rec_gate_x_b[li]], axis=1)
        rec_p = (wgate, bgate, rec_lambda[li])
        proj_p = (w_in_l, conv_w[li], conv_b[li].reshape(1, d_rec), widths)
        pool_w_bd = _block_diag(pool_w[li]).astype(BF16)
        four_w_bd = _block_diag(fourier_w[li]).astype(BF16)
        w_router = jnp.concatenate([router_group_w[li], jnp.zeros((d, SUBLANES - N_GROUPS), F32), router_expert_w[li],
                                    jnp.zeros((d, LANES - SUBLANES - n_exp), F32)], axis=1).astype(BF16)
        b_router = jnp.concatenate([router_group_b[li], jnp.zeros((SUBLANES - N_GROUPS,), F32), router_expert_b[li],
                                    jnp.zeros((LANES - SUBLANES - n_exp,), F32)]).reshape(1, LANES)
        by_group = lambda w: w.astype(BF16).reshape((N_GROUPS, EXPERTS_PER_GROUP) + w.shape[1:])
        expert_w = (by_group(expert_w_gate[li]), by_group(expert_w_up[li]), by_group(expert_w_down[li]))
        g_ffn = norm_ffn_g[li].reshape(1, d)

        def mixer_tail(hf, hb, z_g, z_p, z_f, stream, mod, grid_rows):
            y_p = _pool(z_p, pool_w_bd, pool_scale[li], grid_rows)
            y_f = _fourier(z_f, four_w_bd)
            return _combine(hf, hb, z_g, y_p, y_f, stream, mod, w_out_l, g_ffn, w_router, b_router)

        def expert_stage(h2t, gid):
            n_tok = gid.size
            t_moe = min(MOE_TILE, h2t.shape[1] // SUBLANES)
            slot, tile_group, n_used, n_tiles = _sort_plan(gid.reshape(n_tok), N_GROUPS, t_moe)
            src = _invert(slot, n_tiles * t_moe)
            valid = src >= 0
            pad_rank = jnp.cumsum(jnp.logical_not(valid).astype(jnp.int32)) - 1
            dst = jnp.where(valid, src, n_tok + pad_rank)
            n_valid = jnp.sum(valid.reshape(n_tiles, t_moe).astype(jnp.int32), axis=1)
            return _moe_fused(tile_group, n_used, n_valid, jnp.maximum(src, 0) * SUBLANES, dst * SUBLANES,
                              h2t.reshape(n_tok * SUBLANES, LANES),
                              w_router, b_router, *expert_w, n_tiles, t_moe)

        if pending_ctx is None:
            uc, zc_g, zc_p, zc_f = _inproj(ctx, norm_mix_g[li], mod_ctx, *proj_p)
        else:
            ctx, uc, zc_g, zc_p, zc_f = _inproj(ctx, norm_mix_g[li], mod_ctx, *proj_p, residual=pending_ctx)
        hf_c, hb_c = _recurrent(uc, *rec_p, zeros_state, F32)
        state = jnp.stack([hf_c[:, -1, :], hb_c[:, 0, :]], axis=1)
        if not last:
            ctx, h2t_c, gid_c = mixer_tail(hf_c, hb_c, zc_g, zc_p, zc_f, ctx, mod_ctx, None)
            pending_ctx = (expert_stage(h2t_c, gid_c), 0, mod_ctx)

        if pending is None:
            u, z_g, z_p, z_f = _inproj(x, norm_mix_g[li], mod_lat, *proj_p)
        else:
            x, u, z_g, z_p, z_f = _inproj(x, norm_mix_g[li], mod_lat, *proj_p, residual=pending)
        hf, hb = _recurrent(u, *rec_p, state, BF16)
        x, h2t, gid = mixer_tail(hf, hb, z_g, z_p, z_f, x, mod_lat, rows)
        pending = (expert_stage(h2t, gid), 0, mod_lat)
    return _final(x, pending[0], pending[2], g_final)
```

```python
import functools
import math

import jax
import jax.numpy as jnp
from jax import lax
from jax.experimental import pallas as pl
from jax.experimental.pallas import tpu as pltpu

F32 = jnp.float32
BF16 = jnp.bfloat16

GRID_W = 64
N_MOD = 6
REC_HEADS = 8
CONV_W = 4
LRU_C = 8.0
POOL_WINDOWS = (2, 4, 8, 16)
FOUR_HEADS = 4
N_GROUPS = 4
EXPERTS_PER_GROUP = 4
EPS = 1e-6

LANES = 128
SUBLANES = 8
POOL_PAD = 8
VMEM_LIMIT = 56 * 1024 * 1024


def _cparams(*sem):
    return pltpu.CompilerParams(dimension_semantics=sem, vmem_limit_bytes=VMEM_LIMIT)


def _split_bf16(a):
    hi = a.astype(BF16)
    lo = (a - hi.astype(F32)).astype(BF16)
    return hi, lo


def _dot(a, b):
    return jnp.dot(a, b, preferred_element_type=F32)


def _sigmoid(x):
    return 0.5 * jnp.tanh(0.5 * x) + 0.5


def _mod_kernel(s_ref, w_ref, b_ref, o_ref):
    s = s_ref[...]
    s = s * jax.nn.sigmoid(s)
    s_hi, s_lo = _split_bf16(s)
    w_hi, w_lo = _split_bf16(w_ref[...])
    o_ref[...] = _dot(s_hi, w_hi) + _dot(s_hi, w_lo) + _dot(s_lo, w_hi) + b_ref[...]


def _modulation(cvec, mod_w, mod_b):
    depth, d, dm = mod_w.shape
    tn = dm // 4
    return pl.pallas_call(
        _mod_kernel,
        grid=(depth, dm // tn),
        in_specs=[pl.BlockSpec((SUBLANES, d), lambda l, j: (0, 0)),
                  pl.BlockSpec((None, d, tn), lambda l, j: (l, 0, j)),
                  pl.BlockSpec((None, 1, tn), lambda l, j: (l, 0, j))],
        out_specs=pl.BlockSpec((None, SUBLANES, tn), lambda l, j: (l, 0, j)),
        out_shape=jax.ShapeDtypeStruct((depth, SUBLANES, dm), F32),
        compiler_params=_cparams("parallel", "parallel"),
    )(cvec, mod_w, mod_b.reshape(depth, 1, dm))


def _rms_mod(x, g, shift, scale):
    ms = jnp.mean(x * x, axis=-1, keepdims=True)
    y = x * lax.rsqrt(ms + EPS) * g
    return y * (1.0 + scale) + shift


def _from_token_tiles(ref, lead=()):
    n = ref.shape[-2] // SUBLANES
    return jnp.concatenate([ref[lead + (pl.ds(j, n, stride=SUBLANES), slice(None))] for j in range(SUBLANES)], axis=-1)


def _to_token_tiles(ref, val, lead=()):
    n = val.shape[0]
    for j in range(SUBLANES):
        ref[lead + (pl.ds(j, n, stride=SUBLANES), slice(None))] = val[:, j * LANES:(j + 1) * LANES]


def _inproj_kernel(*refs, widths, with_residual, n_tiles):
    i = pl.program_id(1)
    if with_residual:
        (x_ref, xp_ref, xn_ref, y_ref, yp_ref, yn_ref, pmod_ref,
         g_ref, mod_ref, w_ref, cw_ref, cb_ref, xo_ref, *o_refs) = refs
        gate = pmod_ref[5:6, :]
        x = x_ref[...] + gate * _from_token_tiles(y_ref)
        xo_ref[...] = x
        x_halo = jnp.concatenate([xp_ref[...] + gate * _from_token_tiles(yp_ref),
                                  xn_ref[...] + gate * _from_token_tiles(yn_ref)], axis=0)
    else:
        x_ref, xp_ref, xn_ref, g_ref, mod_ref, w_ref, cw_ref, cb_ref, *o_refs = refs
        x = x_ref[...]
        x_halo = jnp.concatenate([xp_ref[...], xn_ref[...]], axis=0)
    t = x.shape[0]
    d_rec = widths[0]
    z = _dot(_rms_mod(x, g_ref[...], mod_ref[0:1, :], mod_ref[1:2, :]).astype(BF16), w_ref[...])
    zh = _dot(_rms_mod(x_halo, g_ref[...], mod_ref[0:1, :], mod_ref[1:2, :]).astype(BF16), w_ref[:, :d_rec])
    zz = jnp.concatenate([jnp.where(i == 0, 0.0, zh[:SUBLANES]), z[:, :d_rec],
                          jnp.where(i == n_tiles - 1, 0.0, zh[SUBLANES:])], axis=0)
    u = cb_ref[...]
    for k in range(CONV_W):
        o = SUBLANES - CONV_W // 2 + k
        u = u + cw_ref[k:k + 1, :] * zz[o:o + t, :]
    seg = t // SUBLANES
    for q in range(d_rec // LANES):
        for s in range(SUBLANES):
            o_refs[0][q, pl.ds(s, seg, stride=SUBLANES), :] = u[s * seg:(s + 1) * seg, q * LANES:(q + 1) * LANES]
    off = d_rec
    for o_ref, wd in zip(o_refs[1:], widths[1:]):
        o_ref[...] = z[:, off:off + wd].astype(o_ref.dtype)
        off += wd


def _inproj(x, g, mod, w_bf16, conv_w, conv_b, widths, residual=None):
    b, l, d = x.shape
    t = min(512, l)
    nt = l // t
    per = t // SUBLANES
    n8 = l // SUBLANES
    tok = lambda c: pl.BlockSpec((None, t, c), lambda bi, i: (bi, i, 0))
    prev8 = lambda bi, i: jnp.maximum(i * per - 1, 0)
    next8 = lambda bi, i: jnp.minimum((i + 1) * per, n8 - 1)
    x_specs = [tok(d), pl.BlockSpec((None, SUBLANES, d), lambda bi, i: (bi, prev8(bi, i), 0)),
               pl.BlockSpec((None, SUBLANES, d), lambda bi, i: (bi, next8(bi, i), 0))]
    modspec = pl.BlockSpec((None, N_MOD, d), lambda bi, i: (bi, 0, 0))
    full = lambda a: pl.BlockSpec(a.shape, lambda bi, i: (0,) * a.ndim)
    tail_specs = [pl.BlockSpec((1, d), lambda bi, i: (0, 0)), modspec, full(w_bf16), full(conv_w), full(conv_b)]
    tail_args = (g.reshape(1, d), mod, w_bf16, conv_w, conv_b)
    slabs = widths[0] // LANES
    z_specs = ([pl.BlockSpec((None, None, slabs, t, LANES), lambda bi, i: (bi, i, 0, 0, 0))]
               + [tok(wd) for wd in widths[1:]])
    z_shapes = ([jax.ShapeDtypeStruct((b, nt, slabs, t, LANES), F32)]
                + [jax.ShapeDtypeStruct((b, l, wd), BF16 if k == 0 else F32) for k, wd in enumerate(widths[1:])])
    if residual is None:
        in_specs, args, out_specs, out_shape = x_specs + tail_specs, (x, x, x) + tail_args, z_specs, z_shapes
    else:
        y_rows, row0, prev_mod = residual
        blk0, blk8 = row0 // t, row0 // SUBLANES
        halo = SUBLANES * SUBLANES
        y_specs = [pl.BlockSpec((t * SUBLANES, LANES), lambda bi, i: (blk0 + bi * nt + i, 0)),
                   pl.BlockSpec((halo, LANES), lambda bi, i: (blk8 + bi * n8 + prev8(bi, i), 0)),
                   pl.BlockSpec((halo, LANES), lambda bi, i: (blk8 + bi * n8 + next8(bi, i), 0))]
        in_specs = x_specs + y_specs + [modspec] + tail_specs
        args = (x, x, x, y_rows, y_rows, y_rows, prev_mod) + tail_args
        out_specs, out_shape = [tok(d)] + z_specs, [jax.ShapeDtypeStruct((b, l, d), F32)] + z_shapes
    return pl.pallas_call(
        functools.partial(_inproj_kernel, widths=widths, with_residual=residual is not None, n_tiles=nt),
        grid=(b, nt), in_specs=in_specs, out_specs=out_specs, out_shape=out_shape,
        compiler_params=_cparams("parallel", "parallel"),
    )(*args)


def _rec_kernel(uf_ref, ub_ref, wg_ref, bg_ref, lam_ref, h0_ref, hf_ref, hb_ref, a_scr, b_scr, carry_scr, *, tc):
    i = pl.program_id(1)
    nq = uf_ref.shape[0]
    c = nq * LANES
    half = c // 2
    seg = tc // SUBLANES
    lanes = lambda q: slice(q * LANES, (q + 1) * LANES)

    @pl.when(i == 0)
    def _():
        carry_scr[...] = h0_ref[...]

    def coefficients(d, u_ref):
        u = jnp.concatenate([u_ref[q] for q in range(nq)], axis=-1)
        log_decay = -LRU_C * jax.nn.softplus(-lam_ref[d:d + 1, :])
        for hh in range(2):
            cols = slice(hh * half, (hh + 1) * half)
            uh = u[:, cols]
            pre = _dot(uh.astype(BF16), wg_ref[d, hh])
            r = _sigmoid(pre[:, :half] + bg_ref[d, 0:1, cols])
            ig = _sigmoid(pre[:, half:] + bg_ref[d, 1:2, cols])
            log_a = log_decay[:, cols] * r
            a = jnp.exp(log_a)
            th = jnp.tanh(log_a)
            bt = jnp.sqrt(-2.0 * th / (1.0 - th)) * (ig * uh)
            for k in range(half // LANES):
                q = hh * (half // LANES) + k
                a_scr[d, q] = a[:, lanes(k)]
                b_scr[d, q] = bt[:, lanes(k)]

    coefficients(0, uf_ref)
    coefficients(1, ub_ref)

    def step(d, j, h, acc):
        rows8 = pl.ds(pl.multiple_of(j * SUBLANES, SUBLANES), SUBLANES)
        a8 = jnp.concatenate([a_scr[d, q, rows8, :] for q in range(nq)], axis=-1)
        b8 = jnp.concatenate([b_scr[d, q, rows8, :] for q in range(nq)], axis=-1)
        h = a8 * h + b8
        acc = a8 * acc
        for q in range(nq):
            b_scr[d, q, rows8, :] = h[:, lanes(q)]
            a_scr[d, q, rows8, :] = acc[:, lanes(q)]
        return h, acc

    def body(j, carry):
        hf, af, hb, ab = carry
        hf, af = step(0, j, hf, af)
        hb, ab = step(1, seg - 1 - j, hb, ab)
        return hf, af, hb, ab

    zero = jnp.zeros((SUBLANES, c), F32)
    one = jnp.ones((SUBLANES, c), F32)
    hf_end, af_end, hb_end, ab_end = lax.fori_loop(0, seg, body, (zero, one, zero, one), unroll=2)

    def chain(h_end, a_end, state, order):
        entering = [None] * SUBLANES
        for s in order:
            entering[s] = state
            state = a_end[s:s + 1, :] * state + h_end[s:s + 1, :]
        return jnp.concatenate(entering, axis=0), state

    cin_f, out_f = chain(hf_end, af_end, carry_scr[0:1, :], range(SUBLANES))
    cin_b, out_b = chain(hb_end, ab_end, carry_scr[1:2, :], reversed(range(SUBLANES)))
    carry_scr[0:1, :] = out_f
    carry_scr[1:2, :] = out_b

    for d, cin, out_ref in ((0, cin_f, hf_ref), (1, cin_b, hb_ref)):
        for q in range(nq):
            full_h = (b_scr[d, q].reshape(seg, SUBLANES, LANES)
                      + a_scr[d, q].reshape(seg, SUBLANES, LANES) * cin[None, :, lanes(q)])
            b_scr[d, q] = full_h.reshape(tc, LANES)
            for s in range(SUBLANES):
                out_ref[s * seg:(s + 1) * seg, lanes(q)] = (
                    b_scr[d, q, pl.ds(s, seg, stride=SUBLANES), :].astype(out_ref.dtype))


def _recurrent(u, wgate, bgate, lam, h0, out_dtype):
    b, nc, nq, tc, _ = u.shape
    c = nq * LANES
    u_fwd = pl.BlockSpec((None, None, nq, tc, LANES), lambda bi, i: (bi, i, 0, 0, 0))
    u_bwd = pl.BlockSpec((None, None, nq, tc, LANES), lambda bi, i: (bi, nc - 1 - i, 0, 0, 0))
    h_fwd = pl.BlockSpec((None, tc, c), lambda bi, i: (bi, i, 0))
    h_bwd = pl.BlockSpec((None, tc, c), lambda bi, i: (bi, nc - 1 - i, 0))
    full = lambda a: pl.BlockSpec(a.shape, lambda bi, i: (0,) * a.ndim)
    return pl.pallas_call(
        functools.partial(_rec_kernel, tc=tc),
        grid=(b, nc),
        in_specs=[u_fwd, u_bwd, full(wgate), full(bgate), full(lam),
                  pl.BlockSpec((None, 2, c), lambda bi, i: (bi, 0, 0))],
        out_specs=[h_fwd, h_bwd],
        out_shape=[jax.ShapeDtypeStruct((b, nc * tc, c), out_dtype)] * 2,
        scratch_shapes=[pltpu.VMEM((2, nq, tc, LANES), F32), pltpu.VMEM((2, nq, tc, LANES), F32),
                        pltpu.VMEM((2, c), F32)],
        compiler_params=_cparams("parallel", "arbitrary"),
    )(u, u, wgate, bgate, lam, h0)


def _dft_tables(n):
    j = lax.broadcasted_iota(jnp.int32, (n, n), 0)
    k = lax.broadcasted_iota(jnp.int32, (n, n), 1)
    ang = ((j * k) % n).astype(F32) * (2.0 * math.pi / n)
    return jnp.cos(ang), jnp.sin(ang)


def _block_diag(w):
    return jax.scipy.linalg.block_diag(*[w[h] for h in range(w.shape[0])])


def _channel_stage(gr, gi, c_ref, s_ref, w_ref, scale):
    f = (_dot(gr.astype(BF16), c_ref[...]) + _dot(gi.astype(BF16), s_ref[...])) * scale
    return _dot(f.astype(BF16), w_ref[...])


def _slabs_from(scr, val):
    for q in range(scr.shape[0]):
        scr[q] = val[:, q * LANES:(q + 1) * LANES]


def _slab_rows(scr, start, n):
    return jnp.concatenate([scr[q, pl.ds(start, n, stride=SUBLANES), :] for q in range(scr.shape[0])], axis=-1)


def _set_slab_rows(scr, start, val):
    for q in range(scr.shape[0]):
        scr[q, pl.ds(start, val.shape[0], stride=SUBLANES), :] = val[:, q * LANES:(q + 1) * LANES]


def _slabs_value(scr):
    return jnp.concatenate([scr[q] for q in range(scr.shape[0])], axis=-1)


def _four1_kernel(x_ref, f1_ref, twc_ref, tws_ref, ar_ref, ai_ref, xs, ars, ais, *, n1):
    shape = x_ref.shape
    _slabs_from(xs, x_ref[...].reshape(n1 * SUBLANES, shape[-1]))
    for jj in range(SUBLANES):
        a = _dot(f1_ref[...], _slab_rows(xs, jj, n1).astype(BF16))
        ar, ai = a[:n1], a[n1:]
        cc = twc_ref[:, jj:jj + 1]
        ss = tws_ref[:, jj:jj + 1]
        _set_slab_rows(ars, jj, ar * cc + ai * ss)
        _set_slab_rows(ais, jj, ai * cc - ar * ss)
    ar_ref[...] = _slabs_value(ars).reshape(shape)
    ai_ref[...] = _slabs_value(ais).reshape(shape)


def _four2_kernel(ar_ref, ai_ref, f2_ref, c_ref, s_ref, w_ref, o_ref, ys, *, n2, scale):
    grs, gis = [], []
    for kk in range(SUBLANES):
        slab = jnp.concatenate([ar_ref[kk], ai_ref[kk]], axis=0).astype(BF16)
        g = _dot(f2_ref[...], slab)
        grs.append(g[:n2])
        gis.append(g[n2:])
    y = _channel_stage(jnp.concatenate(grs, axis=0), jnp.concatenate(gis, axis=0), c_ref, s_ref, w_ref, scale)
    for kk in range(SUBLANES):
        _set_slab_rows(ys, kk, y[kk * n2:(kk + 1) * n2])
    o_ref[...] = _slabs_value(ys).reshape(o_ref.shape)


def _four_small_kernel(x_ref, f_ref, c_ref, s_ref, w_ref, o_ref, *, n, scale):
    g = _dot(f_ref[...], x_ref[...].astype(BF16))
    o_ref[...] = _channel_stage(g[:n], g[n:], c_ref, s_ref, w_ref, scale)


def _fourier(z_f, four_w_bd):
    b, l, nch = z_f.shape
    hd = nch // FOUR_HEADS
    scale = 1.0 / math.sqrt(l * hd)
    c_h, s_h = _dft_tables(hd)
    eye = jnp.eye(FOUR_HEADS, dtype=F32)
    c_bd = jnp.kron(eye, c_h).astype(BF16)
    s_bd = jnp.kron(eye, s_h).astype(BF16)
    full = lambda a, nd: pl.BlockSpec(a.shape, lambda *_: (0,) * a.ndim)
    if l <= 256:
        cn, sn = _dft_tables(l)
        f = jnp.concatenate([cn, -sn], axis=0).astype(BF16)
        return pl.pallas_call(
            functools.partial(_four_small_kernel, n=l, scale=scale),
            grid=(b,),
            in_specs=[pl.BlockSpec((None, l, nch), lambda bi: (bi, 0, 0)),
                      full(f, 1), full(c_bd, 1), full(s_bd, 1), full(four_w_bd, 1)],
            out_specs=pl.BlockSpec((None, l, nch), lambda bi: (bi, 0, 0)),
            out_shape=jax.ShapeDtypeStruct((b, l, nch), F32),
            compiler_params=_cparams("parallel"),
        )(z_f, f, c_bd, s_bd, four_w_bd)

    n2 = LANES
    n1 = l // n2
    nj = n2 // SUBLANES
    c1, s1 = _dft_tables(n1)
    f1 = jnp.concatenate([c1, -s1], axis=0).astype(BF16)
    c2, s2 = _dft_tables(n2)
    f2 = jnp.concatenate([jnp.concatenate([c2, s2], axis=1),
                          jnp.concatenate([-s2, c2], axis=1)], axis=0).astype(BF16)
    k1 = lax.broadcasted_iota(jnp.int32, (n1, n2), 0)
    t2 = lax.broadcasted_iota(jnp.int32, (n1, n2), 1)
    ang = (k1 * t2).astype(F32) * (2.0 * math.pi / l)
    twc = jnp.cos(ang).reshape(n1, nj, SUBLANES).transpose(1, 0, 2)
    tws = jnp.sin(ang).reshape(n1, nj, SUBLANES).transpose(1, 0, 2)
    ar, ai = pl.pallas_call(
        functools.partial(_four1_kernel, n1=n1),
        grid=(b, nj),
        in_specs=[pl.BlockSpec((None, n1, SUBLANES, nch), lambda bi, j: (bi, 0, j, 0)),
                  full(f1, 2),
                  pl.BlockSpec((None, n1, SUBLANES), lambda bi, j: (j, 0, 0)),
                  pl.BlockSpec((None, n1, SUBLANES), lambda bi, j: (j, 0, 0))],
        out_specs=[pl.BlockSpec((None, n1, SUBLANES, nch), lambda bi, j: (bi, 0, j, 0))] * 2,
        out_shape=[jax.ShapeDtypeStruct((b, n1, n2, nch), F32)] * 2,
        scratch_shapes=[pltpu.VMEM((nch // LANES, n1 * SUBLANES, LANES), F32)] * 3,
        compiler_params=_cparams("parallel", "parallel"),
    )(z_f.reshape(b, n1, n2, nch), f1, twc, tws)
    y = pl.pallas_call(
        functools.partial(_four2_kernel, n2=n2, scale=scale),
        grid=(b, n1 // SUBLANES),
        in_specs=[pl.BlockSpec((None, SUBLANES, n2, nch), lambda bi, j: (bi, j, 0, 0)),
                  pl.BlockSpec((None, SUBLANES, n2, nch), lambda bi, j: (bi, j, 0, 0)),
                  full(f2, 2), full(c_bd, 2), full(s_bd, 2), full(four_w_bd, 2)],
        out_specs=pl.BlockSpec((None, n2, SUBLANES, nch), lambda bi, j: (bi, 0, j, 0)),
        out_shape=jax.ShapeDtypeStruct((b, n2, n1, nch), F32),
        scratch_shapes=[pltpu.VMEM((nch // LANES, n2 * SUBLANES, LANES), F32)],
        compiler_params=_cparams("parallel", "parallel"),
    )(ar, ai, f2, c_bd, s_bd, four_w_bd)
    return y.reshape(b, l, nch)


def _window_counts(idx, n, w):
    return (jnp.minimum(idx + (w - w // 2), n) - jnp.maximum(idx - w // 2, 0)).astype(F32)


def _pool_kernel(cur_ref, prev_ref, next_ref, pw_ref, ps_ref, o_ref, *, r, w, rows_total, n_tiles):
    i = pl.program_id(1)
    two_d = rows_total > 1
    halo = SUBLANES if two_d else 0
    nr = r + 2 * halo
    stride = w + POOL_PAD
    flat = nr * stride
    lane = lax.broadcasted_iota(jnp.int32, (1, 1, LANES), 2)
    low = lane < (LANES // 2)
    col = lax.broadcasted_iota(jnp.int32, (1, w, 1), 1)
    grow = lax.broadcasted_iota(jnp.int32, (r, 1, 1), 0) + i * r
    outs = []
    for hh in range(2):
        cols = slice(hh * LANES, (hh + 1) * LANES)
        w_lo, w_hi = POOL_WINDOWS[2 * hh], POOL_WINDOWS[2 * hh + 1]
        cur = cur_ref[:, cols]
        if two_d:
            prev = jnp.where(i == 0, 0.0, prev_ref[:, cols])
            nxt = jnp.where(i == n_tiles - 1, 0.0, next_ref[:, cols])
            ext = jnp.concatenate([prev, cur, nxt], axis=0)
        else:
            ext = cur
        ext = ext.reshape(nr, w, LANES)
        x = jnp.concatenate([jnp.zeros((nr, POOL_PAD, LANES), F32), ext], axis=1).reshape(flat, LANES)
        p = x + pltpu.roll(x, 1, 0)
        sums = {2: p}
        for ww, s in ((4, 1), (8, 2), (16, 4)):
            if ww > w_hi:
                break
            p = pltpu.roll(p, s, 0) + pltpu.roll(p, flat - s, 0)
            sums[ww] = p
        y = jnp.where(low[0], sums[w_lo], sums[w_hi]).reshape(nr, stride, LANES)[:, POOL_PAD:, :]
        if two_d:
            q2 = y[0:nr - 1] + y[1:nr]
            rsum = {2: q2[7:7 + r]}
            q4 = q2[0:nr - 3] + q2[2:nr - 1]
            rsum[4] = q4[6:6 + r]
            if w_hi > 4:
                q8 = q4[0:nr - 7] + q4[4:nr - 3]
                rsum[8] = q8[4:4 + r]
                q16 = q8[0:nr - 15] + q8[8:nr - 7]
                rsum[16] = q16[0:r]
            tot = jnp.where(low, rsum[w_lo], rsum[w_hi])
            cnt_r = jnp.where(low, _window_counts(grow, rows_total, w_lo), _window_counts(grow, rows_total, w_hi))
            tot = tot / cnt_r
        else:
            tot = y
        cnt_c = jnp.where(low, _window_counts(col, w, w_lo), _window_counts(col, w, w_hi))
        pooled = (tot / cnt_c).reshape(r * w, LANES)
        outs.append(pooled - cur)
    d = jnp.concatenate(outs, axis=-1)
    o_ref[...] = (_dot(d.astype(BF16), pw_ref[...]) * ps_ref[...]).astype(o_ref.dtype)


def _pool(z_p, pool_w_bd, pool_scale, rows):
    b, l, nch = z_p.shape
    if rows is None:
        rows_total, w, r = 1, l, 1
    else:
        rows_total, w, r = rows, l // rows, min(16, rows)
    t = r * w
    n_tiles = l // t
    hb = SUBLANES * w if rows is not None else t
    per = t // hb
    nhb = l // hb
    return pl.pallas_call(
        functools.partial(_pool_kernel, r=r, w=w, rows_total=rows_total, n_tiles=n_tiles),
        grid=(b, n_tiles),
        in_specs=[pl.BlockSpec((None, t, nch), lambda bi, i: (bi, i, 0)),
                  pl.BlockSpec((None, hb, nch), lambda bi, i: (bi, jnp.maximum(i * per - 1, 0), 0)),
                  pl.BlockSpec((None, hb, nch), lambda bi, i: (bi, jnp.minimum((i + 1) * per, nhb - 1), 0)),
                  pl.BlockSpec(pool_w_bd.shape, lambda bi, i: (0, 0)),
                  pl.BlockSpec((1, nch), lambda bi, i: (0, 0))],
        out_specs=pl.BlockSpec((None, t, nch), lambda bi, i: (bi, i, 0)),
        out_shape=jax.ShapeDtypeStruct((b, l, nch), BF16),
        compiler_params=_cparams("parallel", "parallel"),
    )(z_p, z_p, z_p, pool_w_bd, pool_scale.reshape(1, nch))


def _first_argmax4(v):
    m = jnp.maximum(jnp.maximum(v[0], v[1]), jnp.maximum(v[2], v[3]))
    idx = jnp.where(v[0] >= m, 0, jnp.where(v[1] >= m, 1, jnp.where(v[2] >= m, 2, 3)))
    return m, idx


def _select4(idx, v):
    return jnp.where(idx == 0, v[0], jnp.where(idx == 1, v[1], jnp.where(idx == 2, v[2], v[3])))


def _router_logits(h_bf16, wr_ref, br_ref):
    return (_dot(h_bf16, wr_ref[...]) + br_ref[...]).T


def _group_logits(lt):
    return [lt[g:g + 1, :] for g in range(N_GROUPS)]


def _expert_gates(lt, group):
    lg = _group_logits(lt)
    m, _ = _first_argmax4(lg)
    den = sum(jnp.exp(v - m) for v in lg)
    p_group = jnp.exp(_select4(group, lg) - m) / den
    sel = [_select4(group, [lt[SUBLANES + EXPERTS_PER_GROUP * g + e:SUBLANES + EXPERTS_PER_GROUP * g + e + 1, :]
                            for g in range(N_GROUPS)]) for e in range(EXPERTS_PER_GROUP)]
    v1, e1 = _first_argmax4(sel)
    rest = [jnp.where(e1 == e, -jnp.inf, sel[e]) for e in range(EXPERTS_PER_GROUP)]
    v2, e2 = _first_argmax4(rest)
    t2 = jnp.exp(v2 - v1)
    w1 = p_group / (1.0 + t2)
    w2 = p_group * t2 / (1.0 + t2)
    return [jnp.where(e1 == e, w1, jnp.where(e2 == e, w2, 0.0)) for e in range(EXPERTS_PER_GROUP)]


def _combine_kernel(hf_ref, hb_ref, zg_ref, yp_ref, yf_ref, x_ref, mod_ref, wo_ref, g_ref, wr_ref, br_ref,
                    xo_ref, h2t_ref, gid_ref):
    y_a = ((hf_ref[...].astype(F32) + hb_ref[...].astype(F32))
           * jax.nn.gelu(zg_ref[...].astype(F32), approximate=True))
    ycat = jnp.concatenate([y_a.astype(BF16), yp_ref[...].astype(BF16), yf_ref[...].astype(BF16)], axis=-1)
    x_new = x_ref[...] + mod_ref[2:3, :] * _dot(ycat, wo_ref[...])
    xo_ref[...] = x_new
    h2 = _rms_mod(x_new, g_ref[...], mod_ref[3:4, :], mod_ref[4:5, :])
    _to_token_tiles(h2t_ref, h2)
    _, gidx = _first_argmax4(_group_logits(_router_logits(h2.astype(BF16), wr_ref, br_ref)))
    gid_ref[...] = gidx


def _combine(hf, hb, z_g, y_p, y_f, x, mod, w_out, g_ffn, w_router, b_router):
    b, l, d = x.shape
    assert d == SUBLANES * LANES, "a token tile is one (8, 128) f32 tile"
    t = min(512, l)
    tok = lambda c: pl.BlockSpec((None, t, c), lambda bi, i: (bi, i, 0))
    full = lambda a: pl.BlockSpec(a.shape, lambda bi, i: (0,) * a.ndim)
    return pl.pallas_call(
        _combine_kernel,
        grid=(b, l // t),
        in_specs=[tok(hf.shape[-1]), tok(hb.shape[-1]), tok(z_g.shape[-1]), tok(y_p.shape[-1]), tok(y_f.shape[-1]),
                  tok(d), pl.BlockSpec((None, N_MOD, d), lambda bi, i: (bi, 0, 0)),
                  full(w_out), full(g_ffn), full(w_router), full(b_router)],
        out_specs=[tok(d), pl.BlockSpec((None, t * SUBLANES, LANES), lambda bi, i: (bi, i, 0)),
                   pl.BlockSpec((None, 1, t), lambda bi, i: (bi, 0, i))],
        out_shape=[jax.ShapeDtypeStruct((b, l, d), F32), jax.ShapeDtypeStruct((b, l * SUBLANES, LANES), F32),
                   jax.ShapeDtypeStruct((b, 1, l), jnp.int32)],
        compiler_params=_cparams("parallel", "parallel"),
    )(hf, hb, z_g, y_p, y_f, x, mod, w_out, g_ffn, w_router, b_router)


MOE_TILE = 512


def _sort_plan(gid, n_groups, t):
    n = gid.shape[0]
    onehot = (gid[:, None] == jnp.arange(n_groups, dtype=jnp.int32)[None, :]).astype(jnp.int32)
    csum = jnp.cumsum(onehot, axis=0)
    rank = jnp.sum((csum - onehot) * onehot, axis=1)
    tiles = (csum[-1] + t - 1) // t
    tile_end = jnp.cumsum(tiles)
    slot = jnp.sum(onehot * (tile_end - tiles)[None, :], axis=1) * t + rank
    n_tiles = n // t + n_groups
    tile_idx = jnp.arange(n_tiles, dtype=jnp.int32)
    tile_group = jnp.minimum(jnp.sum((tile_idx[:, None] >= tile_end[None, :]).astype(jnp.int32), axis=1), n_groups - 1)
    pad_lo = jnp.concatenate([(tile_end - tiles) * t + csum[-1], tile_end[-1:] * t]).astype(jnp.int32)
    pad_hi = jnp.concatenate([tile_end * t, jnp.full((1,), n_tiles * t, jnp.int32)]).astype(jnp.int32)
    return (slot.astype(jnp.int32), tile_group.astype(jnp.int32), tile_end[-1:].astype(jnp.int32), pad_lo, pad_hi,
            n_tiles)


def _invert_kernel(slot_ref, pad_lo_ref, pad_hi_ref, src_ref):
    def clear(s, carry):
        src_ref[s] = -1
        return carry

    for k in range(pad_lo_ref.shape[0]):
        lax.fori_loop(pad_lo_ref[k], pad_hi_ref[k], clear, 0)

    def put(tok, carry):
        src_ref[slot_ref[tok]] = tok
        return carry

    lax.fori_loop(0, slot_ref.shape[0], put, 0, unroll=16)


def _invert(slot, pad_lo, pad_hi, n_rows):
    smem = pl.BlockSpec(memory_space=pltpu.SMEM)
    return pl.pallas_call(
        _invert_kernel,
        in_specs=[smem, smem, smem],
        out_specs=smem,
        out_shape=jax.ShapeDtypeStruct((n_rows,), jnp.int32),
    )(slot, pad_lo, pad_hi)


def _moe_fused_kernel(tg_ref, nu_ref, nv_ref, src_ref, dst_ref, h_ref, wr_ref, br_ref, wg_ref, wu_ref, wd_ref, y_ref,
                      rows, ybuf, hb_scr, gt_scr, gsem, ssem, *, t, epg):
    i = pl.program_id(0)
    nu = nu_ref[0]
    s = lax.rem(i, 2)
    tile_rows = t * SUBLANES

    def token(ref, first_row):
        return ref.at[pl.ds(pl.multiple_of(first_row, SUBLANES), SUBLANES)]

    def gather_rows(tile, slot, lo, hi):
        def body(g, carry):
            for k in range(SUBLANES):
                r = g * SUBLANES + k
                pltpu.make_async_copy(token(h_ref, src_ref[tile * t + r]), token(rows.at[slot], r * SUBLANES),
                                      gsem.at[slot]).start(priority=k % 2)
            return carry
        lax.fori_loop(lo // SUBLANES, hi // SUBLANES, body, 0)

    def scatter_rows(tile, slot, lo, hi):
        def body(g, carry):
            for k in range(SUBLANES):
                r = g * SUBLANES + k
                pltpu.make_async_copy(token(ybuf.at[slot], r * SUBLANES), token(y_ref, dst_ref[tile * t + r]),
                                      ssem.at[slot]).start(priority=k % 2)
            return carry
        lax.fori_loop(lo // SUBLANES, hi // SUBLANES, body, 0)

    def wait_gather(slot):
        pltpu.make_async_copy(h_ref.at[pl.ds(0, tile_rows)], rows.at[slot], gsem.at[slot]).wait()

    def wait_scatter(slot):
        pltpu.make_async_copy(ybuf.at[slot], y_ref.at[pl.ds(0, tile_rows)], ssem.at[slot]).wait()

    @pl.when(i == 0)
    def _():
        gt_scr[...] = jnp.zeros_like(gt_scr)
        gather_rows(0, 0, 0, t)

    @pl.when(jnp.logical_and(i >= 2, i - 2 < nu))
    def _():
        wait_scatter(s)

    @pl.when(i < nu)
    def _():
        wait_gather(s)
        hb_scr[...] = _from_token_tiles(rows, (s,)).astype(BF16)
        tok_id = lax.broadcasted_iota(jnp.int32, (1, t), 1)
        gate_rows = _expert_gates(_router_logits(hb_scr[...], wr_ref, br_ref), tg_ref[i])
        for e in range(epg):
            gt_scr[e:e + 1, :] = jnp.where(tok_id < nv_ref[i], gate_rows[e], 0.0)
        gates = gt_scr[...].T
        @pl.when(i + 1 < nu)
        def _():
            gather_rows(i + 1, 1 - s, 0, t)

        @pl.when(i >= 1)
        def _():
            scatter_rows(i - 1, 1 - s, 0, t)

        acc = None
        for e in range(epg):
            hid = jax.nn.silu(_dot(hb_scr[...], wg_ref[e])) * _dot(hb_scr[...], wu_ref[e])
            y = gates[:, e:e + 1] * _dot(hid.astype(BF16), wd_ref[e])
            acc = y if acc is None else acc + y
        _to_token_tiles(ybuf, acc, (s,))

    @pl.when(i == nu)
    def _():
        scatter_rows(i - 1, 1 - s, 0, t)

    @pl.when(jnp.logical_and(i > nu, i < pl.num_programs(0) - 1))
    def _():
        rows[0] = jnp.zeros(rows.shape[1:], F32)
        first = pl.multiple_of(dst_ref[(i - 1) * t], SUBLANES)
        fill = pltpu.make_async_copy(rows.at[0], y_ref.at[pl.ds(first, tile_rows)], gsem.at[0])
        fill.start()
        fill.wait()


def _moe_fused(tile_group, n_used, n_valid, src, dst, h2t, w_router, b_router, wg, wu, wd, n_tiles, t):
    epg = wg.shape[1]
    last = n_tiles - 1
    d = SUBLANES * LANES
    tile_rows = t * SUBLANES
    full = lambda a: pl.BlockSpec(a.shape, lambda i, *prefetch: (0,) * a.ndim)
    by_group = lambda a: pl.BlockSpec((None,) + a.shape[1:], lambda i, tg, *rest: (tg[jnp.minimum(i, last)], 0, 0, 0))
    return pl.pallas_call(
        functools.partial(_moe_fused_kernel, t=t, epg=epg),
        grid_spec=pltpu.PrefetchScalarGridSpec(
            num_scalar_prefetch=5, grid=(n_tiles + 2,),
            in_specs=[pl.BlockSpec(memory_space=pl.ANY), full(w_router), full(b_router),
                      by_group(wg), by_group(wu), by_group(wd)],
            out_specs=pl.BlockSpec(memory_space=pl.ANY),
            scratch_shapes=[pltpu.VMEM((2, tile_rows, LANES), F32), pltpu.VMEM((2, tile_rows, LANES), F32),
                            pltpu.VMEM((t, d), BF16), pltpu.VMEM((LANES, t), F32),
                            pltpu.SemaphoreType.DMA((2,)), pltpu.SemaphoreType.DMA((2,))]),
        out_shape=jax.ShapeDtypeStruct((n_tiles * tile_rows, LANES), F32),
        compiler_params=_cparams("arbitrary"),
    )(tile_group, n_used, n_valid, src, dst, h2t, w_router, b_router, wg, wu, wd)


def _final_kernel(x_ref, y_ref, mod_ref, g_ref, o_ref):
    out = x_ref[...] + mod_ref[5:6, :] * _from_token_tiles(y_ref)
    ms = jnp.mean(out * out, axis=-1, keepdims=True)
    o_ref[...] = out * lax.rsqrt(ms + EPS) * g_ref[...]


def _final(x, y_rows, mod, g_final):
    b, l, d = x.shape
    t = min(512, l)
    nt = l // t
    return pl.pallas_call(
        _final_kernel,
        grid=(b, nt),
        in_specs=[pl.BlockSpec((None, t, d), lambda bi, i: (bi, i, 0)),
                  pl.BlockSpec((t * SUBLANES, LANES), lambda bi, i: (bi * nt + i, 0)),
                  pl.BlockSpec((None, N_MOD, d), lambda bi, i: (bi, 0, 0)),
                  pl.BlockSpec((1, d), lambda bi, i: (0, 0))],
        out_specs=pl.BlockSpec((None, t, d), lambda bi, i: (bi, i, 0)),
        out_shape=jax.ShapeDtypeStruct((b, l, d), F32),
        compiler_params=_cparams("parallel", "parallel"),
    )(x, y_rows, mod, g_final)


def kernel(x, c, ctx, c_ctx, mod_w, mod_b, norm_mix_g, norm_ffn_g, w_in, conv_w, conv_b, rec_gate_a_w, rec_gate_a_b,
           rec_gate_x_w, rec_gate_x_b, rec_lambda, pool_w, pool_scale, fourier_w, w_out, router_group_w,
           router_group_b, router_expert_w, router_expert_b, expert_w_gate, expert_w_up, expert_w_down, final_norm_g):
    b, l, d = x.shape
    depth = mod_w.shape[0]
    d_rec = conv_w.shape[-1]
    d_pool = pool_scale.shape[-1]
    d_four = fourier_w.shape[1] * fourier_w.shape[2]
    widths = (d_rec, d_rec, d_pool, d_four)
    rows = l // GRID_W
    heads_half = REC_HEADS // 2
    n_exp = expert_w_gate.shape[1]

    cvec = jnp.concatenate([c, c_ctx[None, :], jnp.zeros((SUBLANES - b - 1, d), F32)], axis=0)
    mod_all = _modulation(cvec, mod_w, mod_b).reshape(depth, SUBLANES, N_MOD, d)
    g_final = final_norm_g.reshape(1, d)
    zeros_state = jnp.zeros((b, 2, d_rec), F32)
    pending = pending_ctx = None

    for li in range(depth):
        last = li == depth - 1
        mod_lat = mod_all[li, :b]
        mod_ctx = jnp.broadcast_to(mod_all[li, b][None], (b, N_MOD, d))
        w_in_l = w_in[li].astype(BF16)
        w_out_l = w_out[li].astype(BF16)
        wgate = jnp.stack([
            jnp.stack([jnp.concatenate([_block_diag(rec_gate_a_w[li, dd, hh * heads_half:(hh + 1) * heads_half]),
                                        _block_diag(rec_gate_x_w[li, dd, hh * heads_half:(hh + 1) * heads_half])], axis=1)
                       for hh in range(2)]) for dd in range(2)]).astype(BF16)
        bgate = jnp.stack([rec_gate_a_b[li], rec_gate_x_b[li]], axis=1)
        rec_p = (wgate, bgate, rec_lambda[li])
        proj_p = (w_in_l, conv_w[li], conv_b[li].reshape(1, d_rec), widths)
        pool_w_bd = _block_diag(pool_w[li]).astype(BF16)
        four_w_bd = _block_diag(fourier_w[li]).astype(BF16)
        w_router = jnp.concatenate([router_group_w[li], jnp.zeros((d, SUBLANES - N_GROUPS), F32), router_expert_w[li],
                                    jnp.zeros((d, LANES - SUBLANES - n_exp), F32)], axis=1).astype(BF16)
        b_router = jnp.concatenate([router_group_b[li], jnp.zeros((SUBLANES - N_GROUPS,), F32), router_expert_b[li],
                                    jnp.zeros((LANES - SUBLANES - n_exp,), F32)]).reshape(1, LANES)
        by_group = lambda w: w.astype(BF16).reshape((N_GROUPS, EXPERTS_PER_GROUP) + w.shape[1:])
        expert_w = (by_group(expert_w_gate[li]), by_group(expert_w_up[li]), by_group(expert_w_down[li]))
        g_ffn = norm_ffn_g[li].reshape(1, d)

        def mixer_tail(hf, hb, z_g, z_p, z_f, stream, mod, grid_rows):
            y_p = _pool(z_p, pool_w_bd, pool_scale[li], grid_rows)
            y_f = _fourier(z_f, four_w_bd)
            return _combine(hf, hb, z_g, y_p, y_f, stream, mod, w_out_l, g_ffn, w_router, b_router)

        def expert_stage(h2t, gid):
            n_tok = gid.size
            t_moe = min(MOE_TILE, h2t.shape[1] // SUBLANES)
            slot, tile_group, n_used, pad_lo, pad_hi, n_tiles = _sort_plan(gid.reshape(n_tok), N_GROUPS, t_moe)
            src = _invert(slot, pad_lo, pad_hi, n_tiles * t_moe)
            valid = src >= 0
            pad_rank = jnp.cumsum(jnp.logical_not(valid).astype(jnp.int32)) - 1
            dst = jnp.where(valid, src, n_tok + pad_rank)
            n_valid = jnp.sum(valid.reshape(n_tiles, t_moe).astype(jnp.int32), axis=1)
            return _moe_fused(tile_group, n_used, n_valid, jnp.maximum(src, 0) * SUBLANES, dst * SUBLANES,
                              h2t.reshape(n_tok * SUBLANES, LANES),
                              w_router, b_router, *expert_w, n_tiles, t_moe)

        if pending_ctx is None:
            uc, zc_g, zc_p, zc_f = _inproj(ctx, norm_mix_g[li], mod_ctx, *proj_p)
        else:
            ctx, uc, zc_g, zc_p, zc_f = _inproj(ctx, norm_mix_g[li], mod_ctx, *proj_p, residual=pending_ctx)
        hf_c, hb_c = _recurrent(uc, *rec_p, zeros_state, F32)
        state = jnp.stack([hf_c[:, -1, :], hb_c[:, 0, :]], axis=1)
        if not last:
            ctx, h2t_c, gid_c = mixer_tail(hf_c, hb_c, zc_g, zc_p, zc_f, ctx, mod_ctx, None)
            pending_ctx = (expert_stage(h2t_c, gid_c), 0, mod_ctx)

        if pending is None:
            u, z_g, z_p, z_f = _inproj(x, norm_mix_g[li], mod_lat, *proj_p)
        else:
            x, u, z_g, z_p, z_f = _inproj(x, norm_mix_g[li], mod_lat, *proj_p, residual=pending)
        hf, hb = _recurrent(u, *rec_p, state, BF16)
        x, h2t, gid = mixer_tail(hf, hb, z_g, z_p, z_f, x, mod_lat, rows)
        pending = (expert_stage(h2t, gid), 0, mod_lat)
    return _final(x, pending[0], pending[2], g_final)
```

```python
import functools
import math

import jax
import jax.numpy as jnp
from jax import lax
from jax.experimental import pallas as pl
from jax.experimental.pallas import tpu as pltpu

F32 = jnp.float32
BF16 = jnp.bfloat16

GRID_W = 64
N_MOD = 6
REC_HEADS = 8
CONV_W = 4
LRU_C = 8.0
POOL_WINDOWS = (2, 4, 8, 16)
FOUR_HEADS = 4
N_GROUPS = 4
EXPERTS_PER_GROUP = 4
EPS = 1e-6

LANES = 128
SUBLANES = 8
POOL_PAD = 8
VMEM_LIMIT = 56 * 1024 * 1024


def _cparams(*sem):
    return pltpu.CompilerParams(dimension_semantics=sem, vmem_limit_bytes=VMEM_LIMIT)


def _split_bf16(a):
    hi = a.astype(BF16)
    lo = (a - hi.astype(F32)).astype(BF16)
    return hi, lo


def _dot(a, b):
    return jnp.dot(a, b, preferred_element_type=F32)


def _sigmoid(x):
    return 0.5 * jnp.tanh(0.5 * x) + 0.5


def _mod_kernel(s_ref, w_ref, b_ref, o_ref):
    s = s_ref[...]
    s = s * jax.nn.sigmoid(s)
    s_hi, s_lo = _split_bf16(s)
    w_hi, w_lo = _split_bf16(w_ref[...])
    o_ref[...] = _dot(s_hi, w_hi) + _dot(s_hi, w_lo) + _dot(s_lo, w_hi) + b_ref[...]


def _modulation(cvec, mod_w, mod_b):
    depth, d, dm = mod_w.shape
    tn = dm // 4
    return pl.pallas_call(
        _mod_kernel,
        grid=(depth, dm // tn),
        in_specs=[pl.BlockSpec((SUBLANES, d), lambda l, j: (0, 0)),
                  pl.BlockSpec((None, d, tn), lambda l, j: (l, 0, j)),
                  pl.BlockSpec((None, 1, tn), lambda l, j: (l, 0, j))],
        out_specs=pl.BlockSpec((None, SUBLANES, tn), lambda l, j: (l, 0, j)),
        out_shape=jax.ShapeDtypeStruct((depth, SUBLANES, dm), F32),
        compiler_params=_cparams("parallel", "parallel"),
    )(cvec, mod_w, mod_b.reshape(depth, 1, dm))


def _rms_mod(x, g, shift, scale):
    ms = jnp.mean(x * x, axis=-1, keepdims=True)
    y = x * lax.rsqrt(ms + EPS) * g
    return y * (1.0 + scale) + shift


def _from_token_tiles(ref, lead=()):
    n = ref.shape[-2] // SUBLANES
    return jnp.concatenate([ref[lead + (pl.ds(j, n, stride=SUBLANES), slice(None))] for j in range(SUBLANES)], axis=-1)


def _to_token_tiles(ref, val, lead=()):
    n = val.shape[0]
    for j in range(SUBLANES):
        ref[lead + (pl.ds(j, n, stride=SUBLANES), slice(None))] = val[:, j * LANES:(j + 1) * LANES]


def _inproj_kernel(*refs, widths, with_residual, n_tiles):
    i = pl.program_id(1)
    if with_residual:
        (x_ref, xp_ref, xn_ref, y_ref, yp_ref, yn_ref, pmod_ref,
         g_ref, mod_ref, w_ref, cw_ref, cb_ref, xo_ref, *o_refs) = refs
        gate = pmod_ref[5:6, :]
        x = x_ref[...] + gate * _from_token_tiles(y_ref)
        xo_ref[...] = x
        x_halo = jnp.concatenate([xp_ref[...] + gate * _from_token_tiles(yp_ref),
                                  xn_ref[...] + gate * _from_token_tiles(yn_ref)], axis=0)
    else:
        x_ref, xp_ref, xn_ref, g_ref, mod_ref, w_ref, cw_ref, cb_ref, *o_refs = refs
        x = x_ref[...]
        x_halo = jnp.concatenate([xp_ref[...], xn_ref[...]], axis=0)
    t = x.shape[0]
    d_rec = widths[0]
    z = _dot(_rms_mod(x, g_ref[...], mod_ref[0:1, :], mod_ref[1:2, :]).astype(BF16), w_ref[...])
    zh = _dot(_rms_mod(x_halo, g_ref[...], mod_ref[0:1, :], mod_ref[1:2, :]).astype(BF16), w_ref[:, :d_rec])
    zz = jnp.concatenate([jnp.where(i == 0, 0.0, zh[:SUBLANES]), z[:, :d_rec],
                          jnp.where(i == n_tiles - 1, 0.0, zh[SUBLANES:])], axis=0)
    u = cb_ref[...]
    for k in range(CONV_W):
        o = SUBLANES - CONV_W // 2 + k
        u = u + cw_ref[k:k + 1, :] * zz[o:o + t, :]
    seg = t // SUBLANES
    for q in range(d_rec // LANES):
        for s in range(SUBLANES):
            o_refs[0][q, pl.ds(s, seg, stride=SUBLANES), :] = u[s * seg:(s + 1) * seg, q * LANES:(q + 1) * LANES]
    off = d_rec
    for o_ref, wd in zip(o_refs[1:], widths[1:]):
        o_ref[...] = z[:, off:off + wd].astype(o_ref.dtype)
        off += wd


def _inproj(x, g, mod, w_bf16, conv_w, conv_b, widths, residual=None):
    b, l, d = x.shape
    t = min(512, l)
    nt = l // t
    per = t // SUBLANES
    n8 = l // SUBLANES
    tok = lambda c: pl.BlockSpec((None, t, c), lambda bi, i: (bi, i, 0))
    prev8 = lambda bi, i: jnp.maximum(i * per - 1, 0)
    next8 = lambda bi, i: jnp.minimum((i + 1) * per, n8 - 1)
    x_specs = [tok(d), pl.BlockSpec((None, SUBLANES, d), lambda bi, i: (bi, prev8(bi, i), 0)),
               pl.BlockSpec((None, SUBLANES, d), lambda bi, i: (bi, next8(bi, i), 0))]
    modspec = pl.BlockSpec((None, N_MOD, d), lambda bi, i: (bi, 0, 0))
    full = lambda a: pl.BlockSpec(a.shape, lambda bi, i: (0,) * a.ndim)
    tail_specs = [pl.BlockSpec((1, d), lambda bi, i: (0, 0)), modspec, full(w_bf16), full(conv_w), full(conv_b)]
    tail_args = (g.reshape(1, d), mod, w_bf16, conv_w, conv_b)
    slabs = widths[0] // LANES
    z_specs = ([pl.BlockSpec((None, None, slabs, t, LANES), lambda bi, i: (bi, i, 0, 0, 0))]
               + [tok(wd) for wd in widths[1:]])
    z_shapes = ([jax.ShapeDtypeStruct((b, nt, slabs, t, LANES), F32)]
                + [jax.ShapeDtypeStruct((b, l, wd), BF16 if k == 0 else F32) for k, wd in enumerate(widths[1:])])
    if residual is None:
        in_specs, args, out_specs, out_shape = x_specs + tail_specs, (x, x, x) + tail_args, z_specs, z_shapes
    else:
        y_rows, row0, prev_mod = residual
        blk0, blk8 = row0 // t, row0 // SUBLANES
        halo = SUBLANES * SUBLANES
        y_specs = [pl.BlockSpec((t * SUBLANES, LANES), lambda bi, i: (blk0 + bi * nt + i, 0)),
                   pl.BlockSpec((halo, LANES), lambda bi, i: (blk8 + bi * n8 + prev8(bi, i), 0)),
                   pl.BlockSpec((halo, LANES), lambda bi, i: (blk8 + bi * n8 + next8(bi, i), 0))]
        in_specs = x_specs + y_specs + [modspec] + tail_specs
        args = (x, x, x, y_rows, y_rows, y_rows, prev_mod) + tail_args
        out_specs, out_shape = [tok(d)] + z_specs, [jax.ShapeDtypeStruct((b, l, d), F32)] + z_shapes
    return pl.pallas_call(
        functools.partial(_inproj_kernel, widths=widths, with_residual=residual is not None, n_tiles=nt),
        grid=(b, nt), in_specs=in_specs, out_specs=out_specs, out_shape=out_shape,
        compiler_params=_cparams("parallel", "parallel"),
    )(*args)


def _rec_kernel(uf_ref, ub_ref, wg_ref, bg_ref, lam_ref, h0_ref, hf_ref, hb_ref, a_scr, b_scr, carry_scr, *, tc):
    i = pl.program_id(1)
    nq = uf_ref.shape[0]
    c = nq * LANES
    half = c // 2
    seg = tc // SUBLANES
    lanes = lambda q: slice(q * LANES, (q + 1) * LANES)

    @pl.when(i == 0)
    def _():
        carry_scr[...] = h0_ref[...]

    def coefficients(d, u_ref):
        u = jnp.concatenate([u_ref[q] for q in range(nq)], axis=-1)
        log_decay = -LRU_C * jax.nn.softplus(-lam_ref[d:d + 1, :])
        for hh in range(2):
            cols = slice(hh * half, (hh + 1) * half)
            uh = u[:, cols]
            pre = _dot(uh.astype(BF16), wg_ref[d, hh])
            r = _sigmoid(pre[:, :half] + bg_ref[d, 0:1, cols])
            ig = _sigmoid(pre[:, half:] + bg_ref[d, 1:2, cols])
            log_a = log_decay[:, cols] * r
            a = jnp.exp(log_a)
            th = jnp.tanh(log_a)
            bt = jnp.sqrt(-2.0 * th / (1.0 - th)) * (ig * uh)
            for k in range(half // LANES):
                q = hh * (half // LANES) + k
                a_scr[d, q] = a[:, lanes(k)]
                b_scr[d, q] = bt[:, lanes(k)]

    coefficients(0, uf_ref)
    coefficients(1, ub_ref)

    def step(d, j, h, acc):
        rows8 = pl.ds(pl.multiple_of(j * SUBLANES, SUBLANES), SUBLANES)
        a8 = jnp.concatenate([a_scr[d, q, rows8, :] for q in range(nq)], axis=-1)
        b8 = jnp.concatenate([b_scr[d, q, rows8, :] for q in range(nq)], axis=-1)
        h = a8 * h + b8
        acc = a8 * acc
        for q in range(nq):
            b_scr[d, q, rows8, :] = h[:, lanes(q)]
            a_scr[d, q, rows8, :] = acc[:, lanes(q)]
        return h, acc

    def body(j, carry):
        hf, af, hb, ab = carry
        hf, af = step(0, j, hf, af)
        hb, ab = step(1, seg - 1 - j, hb, ab)
        return hf, af, hb, ab

    zero = jnp.zeros((SUBLANES, c), F32)
    one = jnp.ones((SUBLANES, c), F32)
    hf_end, af_end, hb_end, ab_end = lax.fori_loop(0, seg, body, (zero, one, zero, one), unroll=2)

    def chain(h_end, a_end, state, order):
        entering = [None] * SUBLANES
        for s in order:
            entering[s] = state
            state = a_end[s:s + 1, :] * state + h_end[s:s + 1, :]
        return jnp.concatenate(entering, axis=0), state

    cin_f, out_f = chain(hf_end, af_end, carry_scr[0:1, :], range(SUBLANES))
    cin_b, out_b = chain(hb_end, ab_end, carry_scr[1:2, :], reversed(range(SUBLANES)))
    carry_scr[0:1, :] = out_f
    carry_scr[1:2, :] = out_b

    for d, cin, out_ref in ((0, cin_f, hf_ref), (1, cin_b, hb_ref)):
        for q in range(nq):
            full_h = (b_scr[d, q].reshape(seg, SUBLANES, LANES)
                      + a_scr[d, q].reshape(seg, SUBLANES, LANES) * cin[None, :, lanes(q)])
            b_scr[d, q] = full_h.reshape(tc, LANES)
            for s in range(SUBLANES):
                out_ref[s * seg:(s + 1) * seg, lanes(q)] = (
                    b_scr[d, q, pl.ds(s, seg, stride=SUBLANES), :].astype(out_ref.dtype))


def _recurrent(u, wgate, bgate, lam, h0, out_dtype):
    b, nc, nq, tc, _ = u.shape
    c = nq * LANES
    u_fwd = pl.BlockSpec((None, None, nq, tc, LANES), lambda bi, i: (bi, i, 0, 0, 0))
    u_bwd = pl.BlockSpec((None, None, nq, tc, LANES), lambda bi, i: (bi, nc - 1 - i, 0, 0, 0))
    h_fwd = pl.BlockSpec((None, tc, c), lambda bi, i: (bi, i, 0))
    h_bwd = pl.BlockSpec((None, tc, c), lambda bi, i: (bi, nc - 1 - i, 0))
    full = lambda a: pl.BlockSpec(a.shape, lambda bi, i: (0,) * a.ndim)
    return pl.pallas_call(
        functools.partial(_rec_kernel, tc=tc),
        grid=(b, nc),
        in_specs=[u_fwd, u_bwd, full(wgate), full(bgate), full(lam),
                  pl.BlockSpec((None, 2, c), lambda bi, i: (bi, 0, 0))],
        out_specs=[h_fwd, h_bwd],
        out_shape=[jax.ShapeDtypeStruct((b, nc * tc, c), out_dtype)] * 2,
        scratch_shapes=[pltpu.VMEM((2, nq, tc, LANES), F32), pltpu.VMEM((2, nq, tc, LANES), F32),
                        pltpu.VMEM((2, c), F32)],
        compiler_params=_cparams("parallel", "arbitrary"),
    )(u, u, wgate, bgate, lam, h0)


def _dft_tables(n):
    j = lax.broadcasted_iota(jnp.int32, (n, n), 0)
    k = lax.broadcasted_iota(jnp.int32, (n, n), 1)
    ang = ((j * k) % n).astype(F32) * (2.0 * math.pi / n)
    return jnp.cos(ang), jnp.sin(ang)


def _block_diag(w):
    return jax.scipy.linalg.block_diag(*[w[h] for h in range(w.shape[0])])


def _channel_stage(gr, gi, c_ref, s_ref, w_ref, scale):
    f = (_dot(gr.astype(BF16), c_ref[...]) + _dot(gi.astype(BF16), s_ref[...])) * scale
    return _dot(f.astype(BF16), w_ref[...])


def _slabs_from(scr, val):
    for q in range(scr.shape[0]):
        scr[q] = val[:, q * LANES:(q + 1) * LANES]


def _slab_rows(scr, start, n):
    return jnp.concatenate([scr[q, pl.ds(start, n, stride=SUBLANES), :] for q in range(scr.shape[0])], axis=-1)


def _set_slab_rows(scr, start, val):
    for q in range(scr.shape[0]):
        scr[q, pl.ds(start, val.shape[0], stride=SUBLANES), :] = val[:, q * LANES:(q + 1) * LANES]


def _slabs_value(scr):
    return jnp.concatenate([scr[q] for q in range(scr.shape[0])], axis=-1)


def _four1_kernel(x_ref, f1_ref, twc_ref, tws_ref, ar_ref, ai_ref, xs, ars, ais, *, n1):
    shape = x_ref.shape
    _slabs_from(xs, x_ref[...].reshape(n1 * SUBLANES, shape[-1]))
    for jj in range(SUBLANES):
        a = _dot(f1_ref[...], _slab_rows(xs, jj, n1).astype(BF16))
        ar, ai = a[:n1], a[n1:]
        cc = twc_ref[:, jj:jj + 1]
        ss = tws_ref[:, jj:jj + 1]
        _set_slab_rows(ars, jj, ar * cc + ai * ss)
        _set_slab_rows(ais, jj, ai * cc - ar * ss)
    ar_ref[...] = _slabs_value(ars).reshape(shape)
    ai_ref[...] = _slabs_value(ais).reshape(shape)


def _four2_kernel(ar_ref, ai_ref, f2_ref, c_ref, s_ref, w_ref, o_ref, ys, *, n2, scale):
    grs, gis = [], []
    for kk in range(SUBLANES):
        slab = jnp.concatenate([ar_ref[kk], ai_ref[kk]], axis=0).astype(BF16)
        g = _dot(f2_ref[...], slab)
        grs.append(g[:n2])
        gis.append(g[n2:])
    y = _channel_stage(jnp.concatenate(grs, axis=0), jnp.concatenate(gis, axis=0), c_ref, s_ref, w_ref, scale)
    for kk in range(SUBLANES):
        _set_slab_rows(ys, kk, y[kk * n2:(kk + 1) * n2])
    o_ref[...] = _slabs_value(ys).reshape(o_ref.shape)


def _four_small_kernel(x_ref, f_ref, c_ref, s_ref, w_ref, o_ref, *, n, scale):
    g = _dot(f_ref[...], x_ref[...].astype(BF16))
    o_ref[...] = _channel_stage(g[:n], g[n:], c_ref, s_ref, w_ref, scale)


def _fourier(z_f, four_w_bd):
    b, l, nch = z_f.shape
    hd = nch // FOUR_HEADS
    scale = 1.0 / math.sqrt(l * hd)
    c_h, s_h = _dft_tables(hd)
    eye = jnp.eye(FOUR_HEADS, dtype=F32)
    c_bd = jnp.kron(eye, c_h).astype(BF16)
    s_bd = jnp.kron(eye, s_h).astype(BF16)
    full = lambda a, nd: pl.BlockSpec(a.shape, lambda *_: (0,) * a.ndim)
    if l <= 256:
        cn, sn = _dft_tables(l)
        f = jnp.concatenate([cn, -sn], axis=0).astype(BF16)
        return pl.pallas_call(
            functools.partial(_four_small_kernel, n=l, scale=scale),
            grid=(b,),
            in_specs=[pl.BlockSpec((None, l, nch), lambda bi: (bi, 0, 0)),
                      full(f, 1), full(c_bd, 1), full(s_bd, 1), full(four_w_bd, 1)],
            out_specs=pl.BlockSpec((None, l, nch), lambda bi: (bi, 0, 0)),
            out_shape=jax.ShapeDtypeStruct((b, l, nch), F32),
            compiler_params=_cparams("parallel"),
        )(z_f, f, c_bd, s_bd, four_w_bd)

    n2 = LANES
    n1 = l // n2
    nj = n2 // SUBLANES
    c1, s1 = _dft_tables(n1)
    f1 = jnp.concatenate([c1, -s1], axis=0).astype(BF16)
    c2, s2 = _dft_tables(n2)
    f2 = jnp.concatenate([jnp.concatenate([c2, s2], axis=1),
                          jnp.concatenate([-s2, c2], axis=1)], axis=0).astype(BF16)
    k1 = lax.broadcasted_iota(jnp.int32, (n1, n2), 0)
    t2 = lax.broadcasted_iota(jnp.int32, (n1, n2), 1)
    ang = (k1 * t2).astype(F32) * (2.0 * math.pi / l)
    twc = jnp.cos(ang).reshape(n1, nj, SUBLANES).transpose(1, 0, 2)
    tws = jnp.sin(ang).reshape(n1, nj, SUBLANES).transpose(1, 0, 2)
    ar, ai = pl.pallas_call(
        functools.partial(_four1_kernel, n1=n1),
        grid=(b, nj),
        in_specs=[pl.BlockSpec((None, n1, SUBLANES, nch), lambda bi, j: (bi, 0, j, 0)),
                  full(f1, 2),
                  pl.BlockSpec((None, n1, SUBLANES), lambda bi, j: (j, 0, 0)),
                  pl.BlockSpec((None, n1, SUBLANES), lambda bi, j: (j, 0, 0))],
        out_specs=[pl.BlockSpec((None, n1, SUBLANES, nch), lambda bi, j: (bi, 0, j, 0))] * 2,
        out_shape=[jax.ShapeDtypeStruct((b, n1, n2, nch), F32)] * 2,
        scratch_shapes=[pltpu.VMEM((nch // LANES, n1 * SUBLANES, LANES), F32)] * 3,
        compiler_params=_cparams("parallel", "parallel"),
    )(z_f.reshape(b, n1, n2, nch), f1, twc, tws)
    y = pl.pallas_call(
        functools.partial(_four2_kernel, n2=n2, scale=scale),
        grid=(b, n1 // SUBLANES),
        in_specs=[pl.BlockSpec((None, SUBLANES, n2, nch), lambda bi, j: (bi, j, 0, 0)),
                  pl.BlockSpec((None, SUBLANES, n2, nch), lambda bi, j: (bi, j, 0, 0)),
                  full(f2, 2), full(c_bd, 2), full(s_bd, 2), full(four_w_bd, 2)],
        out_specs=pl.BlockSpec((None, n2, SUBLANES, nch), lambda bi, j: (bi, 0, j, 0)),
        out_shape=jax.ShapeDtypeStruct((b, n2, n1, nch), F32),
        scratch_shapes=[pltpu.VMEM((nch // LANES, n2 * SUBLANES, LANES), F32)],
        compiler_params=_cparams("parallel", "parallel"),
    )(ar, ai, f2, c_bd, s_bd, four_w_bd)
    return y.reshape(b, l, nch)


def _window_counts(idx, n, w):
    return (jnp.minimum(idx + (w - w // 2), n) - jnp.maximum(idx - w // 2, 0)).astype(F32)


def _pool_kernel(cur_ref, prev_ref, next_ref, pw_ref, ps_ref, o_ref, *, r, w, rows_total, n_tiles):
    i = pl.program_id(1)
    two_d = rows_total > 1
    halo = SUBLANES if two_d else 0
    nr = r + 2 * halo
    stride = w + POOL_PAD
    flat = nr * stride
    lane = lax.broadcasted_iota(jnp.int32, (1, 1, LANES), 2)
    low = lane < (LANES // 2)
    col = lax.broadcasted_iota(jnp.int32, (1, w, 1), 1)
    grow = lax.broadcasted_iota(jnp.int32, (r, 1, 1), 0) + i * r
    outs = []
    for hh in range(2):
        cols = slice(hh * LANES, (hh + 1) * LANES)
        w_lo, w_hi = POOL_WINDOWS[2 * hh], POOL_WINDOWS[2 * hh + 1]
        cur = cur_ref[:, cols]
        if two_d:
            prev = jnp.where(i == 0, 0.0, prev_ref[:, cols])
            nxt = jnp.where(i == n_tiles - 1, 0.0, next_ref[:, cols])
            ext = jnp.concatenate([prev, cur, nxt], axis=0)
        else:
            ext = cur
        ext = ext.reshape(nr, w, LANES)
        x = jnp.concatenate([jnp.zeros((nr, POOL_PAD, LANES), F32), ext], axis=1).reshape(flat, LANES)
        p = x + pltpu.roll(x, 1, 0)
        sums = {2: p}
        for ww, s in ((4, 1), (8, 2), (16, 4)):
            if ww > w_hi:
                break
            p = pltpu.roll(p, s, 0) + pltpu.roll(p, flat - s, 0)
            sums[ww] = p
        y = jnp.where(low[0], sums[w_lo], sums[w_hi]).reshape(nr, stride, LANES)[:, POOL_PAD:, :]
        if two_d:
            q2 = y[0:nr - 1] + y[1:nr]
            rsum = {2: q2[7:7 + r]}
            q4 = q2[0:nr - 3] + q2[2:nr - 1]
            rsum[4] = q4[6:6 + r]
            if w_hi > 4:
                q8 = q4[0:nr - 7] + q4[4:nr - 3]
                rsum[8] = q8[4:4 + r]
                q16 = q8[0:nr - 15] + q8[8:nr - 7]
                rsum[16] = q16[0:r]
            tot = jnp.where(low, rsum[w_lo], rsum[w_hi])
            cnt_r = jnp.where(low, _window_counts(grow, rows_total, w_lo), _window_counts(grow, rows_total, w_hi))
            tot = tot / cnt_r
        else:
            tot = y
        cnt_c = jnp.where(low, _window_counts(col, w, w_lo), _window_counts(col, w, w_hi))
        pooled = (tot / cnt_c).reshape(r * w, LANES)
        outs.append(pooled - cur)
    d = jnp.concatenate(outs, axis=-1)
    o_ref[...] = (_dot(d.astype(BF16), pw_ref[...]) * ps_ref[...]).astype(o_ref.dtype)


def _pool(z_p, pool_w_bd, pool_scale, rows):
    b, l, nch = z_p.shape
    if rows is None:
        rows_total, w, r = 1, l, 1
    else:
        rows_total, w, r = rows, l // rows, min(16, rows)
    t = r * w
    n_tiles = l // t
    hb = SUBLANES * w if rows is not None else t
    per = t // hb
    nhb = l // hb
    return pl.pallas_call(
        functools.partial(_pool_kernel, r=r, w=w, rows_total=rows_total, n_tiles=n_tiles),
        grid=(b, n_tiles),
        in_specs=[pl.BlockSpec((None, t, nch), lambda bi, i: (bi, i, 0)),
                  pl.BlockSpec((None, hb, nch), lambda bi, i: (bi, jnp.maximum(i * per - 1, 0), 0)),
                  pl.BlockSpec((None, hb, nch), lambda bi, i: (bi, jnp.minimum((i + 1) * per, nhb - 1), 0)),
                  pl.BlockSpec(pool_w_bd.shape, lambda bi, i: (0, 0)),
                  pl.BlockSpec((1, nch), lambda bi, i: (0, 0))],
        out_specs=pl.BlockSpec((None, t, nch), lambda bi, i: (bi, i, 0)),
        out_shape=jax.ShapeDtypeStruct((b, l, nch), BF16),
        compiler_params=_cparams("parallel", "parallel"),
    )(z_p, z_p, z_p, pool_w_bd, pool_scale.reshape(1, nch))


def _first_argmax4(v):
    m = jnp.maximum(jnp.maximum(v[0], v[1]), jnp.maximum(v[2], v[3]))
    idx = jnp.where(v[0] >= m, 0, jnp.where(v[1] >= m, 1, jnp.where(v[2] >= m, 2, 3)))
    return m, idx


def _select4(idx, v):
    return jnp.where(idx == 0, v[0], jnp.where(idx == 1, v[1], jnp.where(idx == 2, v[2], v[3])))


def _router_logits(h_bf16, wr_ref, br_ref):
    return (_dot(h_bf16, wr_ref[...]) + br_ref[...]).T


def _group_logits(lt):
    return [lt[g:g + 1, :] for g in range(N_GROUPS)]


def _expert_gates(lt, group):
    lg = _group_logits(lt)
    m, _ = _first_argmax4(lg)
    den = sum(jnp.exp(v - m) for v in lg)
    p_group = jnp.exp(_select4(group, lg) - m) / den
    sel = [_select4(group, [lt[SUBLANES + EXPERTS_PER_GROUP * g + e:SUBLANES + EXPERTS_PER_GROUP * g + e + 1, :]
                            for g in range(N_GROUPS)]) for e in range(EXPERTS_PER_GROUP)]
    v1, e1 = _first_argmax4(sel)
    rest = [jnp.where(e1 == e, -jnp.inf, sel[e]) for e in range(EXPERTS_PER_GROUP)]
    v2, e2 = _first_argmax4(rest)
    t2 = jnp.exp(v2 - v1)
    w1 = p_group / (1.0 + t2)
    w2 = p_group * t2 / (1.0 + t2)
    return [jnp.where(e1 == e, w1, jnp.where(e2 == e, w2, 0.0)) for e in range(EXPERTS_PER_GROUP)]


def _combine_kernel(hf_ref, hb_ref, zg_ref, yp_ref, yf_ref, x_ref, mod_ref, wo_ref, g_ref, wr_ref, br_ref,
                    xo_ref, h2t_ref, gid_ref):
    y_a = ((hf_ref[...].astype(F32) + hb_ref[...].astype(F32))
           * jax.nn.gelu(zg_ref[...].astype(F32), approximate=True))
    ycat = jnp.concatenate([y_a.astype(BF16), yp_ref[...].astype(BF16), yf_ref[...].astype(BF16)], axis=-1)
    x_new = x_ref[...] + mod_ref[2:3, :] * _dot(ycat, wo_ref[...])
    xo_ref[...] = x_new
    h2 = _rms_mod(x_new, g_ref[...], mod_ref[3:4, :], mod_ref[4:5, :])
    _to_token_tiles(h2t_ref, h2)
    _, gidx = _first_argmax4(_group_logits(_router_logits(h2.astype(BF16), wr_ref, br_ref)))
    gid_ref[...] = gidx


def _combine(hf, hb, z_g, y_p, y_f, x, mod, w_out, g_ffn, w_router, b_router):
    b, l, d = x.shape
    assert d == SUBLANES * LANES, "a token tile is one (8, 128) f32 tile"
    t = min(1024, l)
    tok = lambda c: pl.BlockSpec((None, t, c), lambda bi, i: (bi, i, 0))
    full = lambda a: pl.BlockSpec(a.shape, lambda bi, i: (0,) * a.ndim)
    return pl.pallas_call(
        _combine_kernel,
        grid=(b, l // t),
        in_specs=[tok(hf.shape[-1]), tok(hb.shape[-1]), tok(z_g.shape[-1]), tok(y_p.shape[-1]), tok(y_f.shape[-1]),
                  tok(d), pl.BlockSpec((None, N_MOD, d), lambda bi, i: (bi, 0, 0)),
                  full(w_out), full(g_ffn), full(w_router), full(b_router)],
        out_specs=[tok(d), pl.BlockSpec((None, t * SUBLANES, LANES), lambda bi, i: (bi, i, 0)),
                   pl.BlockSpec((None, 1, t), lambda bi, i: (bi, 0, i))],
        out_shape=[jax.ShapeDtypeStruct((b, l, d), F32), jax.ShapeDtypeStruct((b, l * SUBLANES, LANES), F32),
                   jax.ShapeDtypeStruct((b, 1, l), jnp.int32)],
        compiler_params=_cparams("parallel", "parallel"),
    )(hf, hb, z_g, y_p, y_f, x, mod, w_out, g_ffn, w_router, b_router)


MOE_TILE = 512


def _sort_plan(gid, n_groups, t):
    n = gid.shape[0]
    onehot = (gid[:, None] == jnp.arange(n_groups, dtype=jnp.int32)[None, :]).astype(jnp.int32)
    csum = jnp.cumsum(onehot, axis=0)
    rank = jnp.sum((csum - onehot) * onehot, axis=1)
    tiles = (csum[-1] + t - 1) // t
    tile_end = jnp.cumsum(tiles)
    slot = jnp.sum(onehot * (tile_end - tiles)[None, :], axis=1) * t + rank
    n_tiles = n // t + n_groups
    tile_idx = jnp.arange(n_tiles, dtype=jnp.int32)
    tile_group = jnp.minimum(jnp.sum((tile_idx[:, None] >= tile_end[None, :]).astype(jnp.int32), axis=1), n_groups - 1)
    pad_lo = jnp.concatenate([(tile_end - tiles) * t + csum[-1], tile_end[-1:] * t]).astype(jnp.int32)
    pad_hi = jnp.concatenate([tile_end * t, jnp.full((1,), n_tiles * t, jnp.int32)]).astype(jnp.int32)
    return (slot.astype(jnp.int32), tile_group.astype(jnp.int32), tile_end[-1:].astype(jnp.int32), pad_lo, pad_hi,
            n_tiles)


def _invert_kernel(slot_ref, pad_lo_ref, pad_hi_ref, src_ref):
    def clear(s, carry):
        src_ref[s] = -1
        return carry

    for k in range(pad_lo_ref.shape[0]):
        lax.fori_loop(pad_lo_ref[k], pad_hi_ref[k], clear, 0)

    def put(tok, carry):
        src_ref[slot_ref[tok]] = tok
        return carry

    lax.fori_loop(0, slot_ref.shape[0], put, 0, unroll=16)


def _invert(slot, pad_lo, pad_hi, n_rows):
    smem = pl.BlockSpec(memory_space=pltpu.SMEM)
    return pl.pallas_call(
        _invert_kernel,
        in_specs=[smem, smem, smem],
        out_specs=smem,
        out_shape=jax.ShapeDtypeStruct((n_rows,), jnp.int32),
    )(slot, pad_lo, pad_hi)


def _moe_fused_kernel(tg_ref, nu_ref, nv_ref, src_ref, dst_ref, h_ref, wr_ref, br_ref, wg_ref, wu_ref, wd_ref, y_ref,
                      rows, ybuf, hb_scr, gt_scr, gsem, ssem, *, t, epg):
    i = pl.program_id(0)
    nu = nu_ref[0]
    s = lax.rem(i, 2)
    tile_rows = t * SUBLANES

    def token(ref, first_row):
        return ref.at[pl.ds(pl.multiple_of(first_row, SUBLANES), SUBLANES)]

    def gather_rows(tile, slot, lo, hi):
        def body(g, carry):
            for k in range(SUBLANES):
                r = g * SUBLANES + k
                pltpu.make_async_copy(token(h_ref, src_ref[tile * t + r]), token(rows.at[slot], r * SUBLANES),
                                      gsem.at[slot]).start(priority=k % 2)
            return carry
        lax.fori_loop(lo // SUBLANES, hi // SUBLANES, body, 0)

    def scatter_rows(tile, slot, lo, hi):
        def body(g, carry):
            for k in range(SUBLANES):
                r = g * SUBLANES + k
                pltpu.make_async_copy(token(ybuf.at[slot], r * SUBLANES), token(y_ref, dst_ref[tile * t + r]),
                                      ssem.at[slot]).start(priority=k % 2)
            return carry
        lax.fori_loop(lo // SUBLANES, hi // SUBLANES, body, 0)

    def wait_gather(slot):
        pltpu.make_async_copy(h_ref.at[pl.ds(0, tile_rows)], rows.at[slot], gsem.at[slot]).wait()

    def wait_scatter(slot):
        pltpu.make_async_copy(ybuf.at[slot], y_ref.at[pl.ds(0, tile_rows)], ssem.at[slot]).wait()

    @pl.when(i == 0)
    def _():
        gt_scr[...] = jnp.zeros_like(gt_scr)
        gather_rows(0, 0, 0, t)

    @pl.when(jnp.logical_and(i >= 2, i - 2 < nu))
    def _():
        wait_scatter(s)

    @pl.when(i < nu)
    def _():
        wait_gather(s)
        hb_scr[...] = _from_token_tiles(rows, (s,)).astype(BF16)
        tok_id = lax.broadcasted_iota(jnp.int32, (1, t), 1)
        gate_rows = _expert_gates(_router_logits(hb_scr[...], wr_ref, br_ref), tg_ref[i])
        for e in range(epg):
            gt_scr[e:e + 1, :] = jnp.where(tok_id < nv_ref[i], gate_rows[e], 0.0)
        gates = gt_scr[...].T
        @pl.when(i + 1 < nu)
        def _():
            gather_rows(i + 1, 1 - s, 0, t)

        @pl.when(i >= 1)
        def _():
            scatter_rows(i - 1, 1 - s, 0, t)

        acc = None
        for e in range(epg):
            hid = jax.nn.silu(_dot(hb_scr[...], wg_ref[e])) * _dot(hb_scr[...], wu_ref[e])
            y = gates[:, e:e + 1] * _dot(hid.astype(BF16), wd_ref[e])
            acc = y if acc is None else acc + y
        _to_token_tiles(ybuf, acc, (s,))

    @pl.when(i == nu)
    def _():
        scatter_rows(i - 1, 1 - s, 0, t)

    @pl.when(jnp.logical_and(i > nu, i < pl.num_programs(0) - 1))
    def _():
        rows[0] = jnp.zeros(rows.shape[1:], F32)
        first = pl.multiple_of(dst_ref[(i - 1) * t], SUBLANES)
        fill = pltpu.make_async_copy(rows.at[0], y_ref.at[pl.ds(first, tile_rows)], gsem.at[0])
        fill.start()
        fill.wait()


def _moe_fused(tile_group, n_used, n_valid, src, dst, h2t, w_router, b_router, wg, wu, wd, group0, n_tiles, t):
    epg = wg.shape[1]
    last = n_tiles - 1
    d = SUBLANES * LANES
    tile_rows = t * SUBLANES
    full = lambda a: pl.BlockSpec(a.shape, lambda i, *prefetch: (0,) * a.ndim)
    by_group = lambda a: pl.BlockSpec((None,) + a.shape[1:],
                                      lambda i, tg, *rest: (group0 + tg[jnp.minimum(i, last)], 0, 0, 0))
    return pl.pallas_call(
        functools.partial(_moe_fused_kernel, t=t, epg=epg),
        grid_spec=pltpu.PrefetchScalarGridSpec(
            num_scalar_prefetch=5, grid=(n_tiles + 2,),
            in_specs=[pl.BlockSpec(memory_space=pl.ANY), full(w_router), full(b_router),
                      by_group(wg), by_group(wu), by_group(wd)],
            out_specs=pl.BlockSpec(memory_space=pl.ANY),
            scratch_shapes=[pltpu.VMEM((2, tile_rows, LANES), F32), pltpu.VMEM((2, tile_rows, LANES), F32),
                            pltpu.VMEM((t, d), BF16), pltpu.VMEM((LANES, t), F32),
                            pltpu.SemaphoreType.DMA((2,)), pltpu.SemaphoreType.DMA((2,))]),
        out_shape=jax.ShapeDtypeStruct((n_tiles * tile_rows, LANES), F32),
        compiler_params=_cparams("arbitrary"),
    )(tile_group, n_used, n_valid, src, dst, h2t, w_router, b_router, wg, wu, wd)


def _final_kernel(x_ref, y_ref, mod_ref, g_ref, o_ref):
    out = x_ref[...] + mod_ref[5:6, :] * _from_token_tiles(y_ref)
    ms = jnp.mean(out * out, axis=-1, keepdims=True)
    o_ref[...] = out * lax.rsqrt(ms + EPS) * g_ref[...]


def _final(x, y_rows, mod, g_final):
    b, l, d = x.shape
    t = min(1024, l)
    nt = l // t
    return pl.pallas_call(
        _final_kernel,
        grid=(b, nt),
        in_specs=[pl.BlockSpec((None, t, d), lambda bi, i: (bi, i, 0)),
                  pl.BlockSpec((t * SUBLANES, LANES), lambda bi, i: (bi * nt + i, 0)),
                  pl.BlockSpec((None, N_MOD, d), lambda bi, i: (bi, 0, 0)),
                  pl.BlockSpec((1, d), lambda bi, i: (0, 0))],
        out_specs=pl.BlockSpec((None, t, d), lambda bi, i: (bi, i, 0)),
        out_shape=jax.ShapeDtypeStruct((b, l, d), F32),
        compiler_params=_cparams("parallel", "parallel"),
    )(x, y_rows, mod, g_final)


def kernel(x, c, ctx, c_ctx, mod_w, mod_b, norm_mix_g, norm_ffn_g, w_in, conv_w, conv_b, rec_gate_a_w, rec_gate_a_b,
           rec_gate_x_w, rec_gate_x_b, rec_lambda, pool_w, pool_scale, fourier_w, w_out, router_group_w,
           router_group_b, router_expert_w, router_expert_b, expert_w_gate, expert_w_up, expert_w_down, final_norm_g):
    b, l, d = x.shape
    depth = mod_w.shape[0]
    d_rec = conv_w.shape[-1]
    d_pool = pool_scale.shape[-1]
    d_four = fourier_w.shape[1] * fourier_w.shape[2]
    widths = (d_rec, d_rec, d_pool, d_four)
    rows = l // GRID_W
    heads_half = REC_HEADS // 2
    n_exp = expert_w_gate.shape[1]

    cvec = jnp.concatenate([c, c_ctx[None, :], jnp.zeros((SUBLANES - b - 1, d), F32)], axis=0)
    mod_all = _modulation(cvec, mod_w, mod_b).reshape(depth, SUBLANES, N_MOD, d)
    g_final = final_norm_g.reshape(1, d)
    zeros_state = jnp.zeros((b, 2, d_rec), F32)
    pending = pending_ctx = None
    by_group = lambda w: w.astype(BF16).reshape((depth * N_GROUPS, EXPERTS_PER_GROUP) + w.shape[2:])
    expert_w = (by_group(expert_w_gate), by_group(expert_w_up), by_group(expert_w_down))

    for li in range(depth):
        last = li == depth - 1
        mod_lat = mod_all[li, :b]
        mod_ctx = jnp.broadcast_to(mod_all[li, b][None], (b, N_MOD, d))
        w_in_l = w_in[li].astype(BF16)
        w_out_l = w_out[li].astype(BF16)
        wgate = jnp.stack([
            jnp.stack([jnp.concatenate([_block_diag(rec_gate_a_w[li, dd, hh * heads_half:(hh + 1) * heads_half]),
                                        _block_diag(rec_gate_x_w[li, dd, hh * heads_half:(hh + 1) * heads_half])], axis=1)
                       for hh in range(2)]) for dd in range(2)]).astype(BF16)
        bgate = jnp.stack([rec_gate_a_b[li], rec_gate_x_b[li]], axis=1)
        rec_p = (wgate, bgate, rec_lambda[li])
        proj_p = (w_in_l, conv_w[li], conv_b[li].reshape(1, d_rec), widths)
        pool_w_bd = _block_diag(pool_w[li]).astype(BF16)
        four_w_bd = _block_diag(fourier_w[li]).astype(BF16)
        w_router = jnp.concatenate([router_group_w[li], jnp.zeros((d, SUBLANES - N_GROUPS), F32), router_expert_w[li],
                                    jnp.zeros((d, LANES - SUBLANES - n_exp), F32)], axis=1).astype(BF16)
        b_router = jnp.concatenate([router_group_b[li], jnp.zeros((SUBLANES - N_GROUPS,), F32), router_expert_b[li],
                                    jnp.zeros((LANES - SUBLANES - n_exp,), F32)]).reshape(1, LANES)
        g_ffn = norm_ffn_g[li].reshape(1, d)

        def mixer_tail(hf, hb, z_g, z_p, z_f, stream, mod, grid_rows):
            y_p = _pool(z_p, pool_w_bd, pool_scale[li], grid_rows)
            y_f = _fourier(z_f, four_w_bd)
            return _combine(hf, hb, z_g, y_p, y_f, stream, mod, w_out_l, g_ffn, w_router, b_router)

        def expert_stage(h2t, gid):
            n_tok = gid.size
            t_moe = min(MOE_TILE, h2t.shape[1] // SUBLANES)
            slot, tile_group, n_used, pad_lo, pad_hi, n_tiles = _sort_plan(gid.reshape(n_tok), N_GROUPS, t_moe)
            src = _invert(slot, pad_lo, pad_hi, n_tiles * t_moe)
            valid = src >= 0
            pad_rank = jnp.cumsum(jnp.logical_not(valid).astype(jnp.int32)) - 1
            dst = jnp.where(valid, src, n_tok + pad_rank)
            n_valid = jnp.sum(valid.reshape(n_tiles, t_moe).astype(jnp.int32), axis=1)
            return _moe_fused(tile_group, n_used, n_valid, jnp.maximum(src, 0) * SUBLANES, dst * SUBLANES,
                              h2t.reshape(n_tok * SUBLANES, LANES),
                              w_router, b_router, *expert_w, li * N_GROUPS, n_tiles, t_moe)

        if pending_ctx is None:
            uc, zc_g, zc_p, zc_f = _inproj(ctx, norm_mix_g[li], mod_ctx, *proj_p)
        else:
            ctx, uc, zc_g, zc_p, zc_f = _inproj(ctx, norm_mix_g[li], mod_ctx, *proj_p, residual=pending_ctx)
        hf_c, hb_c = _recurrent(uc, *rec_p, zeros_state, F32)
        state = jnp.stack([hf_c[:, -1, :], hb_c[:, 0, :]], axis=1)
        if not last:
            ctx, h2t_c, gid_c = mixer_tail(hf_c, hb_c, zc_g, zc_p, zc_f, ctx, mod_ctx, None)
            pending_ctx = (expert_stage(h2t_c, gid_c), 0, mod_ctx)

        if pending is None:
            u, z_g, z_p, z_f = _inproj(x, norm_mix_g[li], mod_lat, *proj_p)
        else:
            x, u, z_g, z_p, z_f = _inproj(x, norm_mix_g[li], mod_lat, *proj_p, residual=pending)
        hf, hb = _recurrent(u, *rec_p, state, BF16)
        x, h2t, gid = mixer_tail(hf, hb, z_g, z_p, z_f, x, mod_lat, rows)
        pending = (expert_stage(h2t, gid), 0, mod_lat)
    return _final(x, pending[0], pending[2], g_final)
```

```python
import functools
import math

import jax
import jax.numpy as jnp
from jax import lax
from jax.experimental import pallas as pl
from jax.experimental.pallas import tpu as pltpu

F32 = jnp.float32
BF16 = jnp.bfloat16

GRID_W = 64
N_MOD = 6
REC_HEADS = 8
CONV_W = 4
LRU_C = 8.0
POOL_WINDOWS = (2, 4, 8, 16)
FOUR_HEADS = 4
N_GROUPS = 4
EXPERTS_PER_GROUP = 4
EPS = 1e-6

LANES = 128
SUBLANES = 8
POOL_PAD = 8
VMEM_LIMIT = 56 * 1024 * 1024


def _cparams(*sem):
    return pltpu.CompilerParams(dimension_semantics=sem, vmem_limit_bytes=VMEM_LIMIT)


def _split_bf16(a):
    hi = a.astype(BF16)
    lo = (a - hi.astype(F32)).astype(BF16)
    return hi, lo


def _dot(a, b):
    return jnp.dot(a, b, preferred_element_type=F32)


def _sigmoid(x):
    return 0.5 * jnp.tanh(0.5 * x) + 0.5


def _mod_kernel(s_ref, w_ref, b_ref, o_ref):
    s = s_ref[...]
    s = s * jax.nn.sigmoid(s)
    s_hi, s_lo = _split_bf16(s)
    w_hi, w_lo = _split_bf16(w_ref[...])
    o_ref[...] = _dot(s_hi, w_hi) + _dot(s_hi, w_lo) + _dot(s_lo, w_hi) + b_ref[...]


def _modulation(cvec, mod_w, mod_b):
    depth, d, dm = mod_w.shape
    tn = dm // 4
    return pl.pallas_call(
        _mod_kernel,
        grid=(depth, dm // tn),
        in_specs=[pl.BlockSpec((SUBLANES, d), lambda l, j: (0, 0)),
                  pl.BlockSpec((None, d, tn), lambda l, j: (l, 0, j)),
                  pl.BlockSpec((None, 1, tn), lambda l, j: (l, 0, j))],
        out_specs=pl.BlockSpec((None, SUBLANES, tn), lambda l, j: (l, 0, j)),
        out_shape=jax.ShapeDtypeStruct((depth, SUBLANES, dm), F32),
        compiler_params=_cparams("parallel", "parallel"),
    )(cvec, mod_w, mod_b.reshape(depth, 1, dm))


def _rms_mod(x, g, shift, scale):
    ms = jnp.mean(x * x, axis=-1, keepdims=True)
    y = x * lax.rsqrt(ms + EPS) * g
    return y * (1.0 + scale) + shift


def _from_token_tiles(ref, lead=()):
    n = ref.shape[-2] // SUBLANES
    return jnp.concatenate([ref[lead + (pl.ds(j, n, stride=SUBLANES), slice(None))] for j in range(SUBLANES)], axis=-1)


def _to_token_tiles(ref, val, lead=()):
    n = val.shape[0]
    for j in range(SUBLANES):
        ref[lead + (pl.ds(j, n, stride=SUBLANES), slice(None))] = val[:, j * LANES:(j + 1) * LANES]


def _inproj_kernel(*refs, widths, with_residual, n_tiles):
    i = pl.program_id(1)
    if with_residual:
        (x_ref, xp_ref, xn_ref, y_ref, yp_ref, yn_ref, pmod_ref,
         g_ref, mod_ref, w_ref, cw_ref, cb_ref, xo_ref, *o_refs) = refs
        gate = pmod_ref[5:6, :]
        x = x_ref[...] + gate * _from_token_tiles(y_ref)
        xo_ref[...] = x
        x_halo = jnp.concatenate([xp_ref[...] + gate * _from_token_tiles(yp_ref),
                                  xn_ref[...] + gate * _from_token_tiles(yn_ref)], axis=0)
    else:
        x_ref, xp_ref, xn_ref, g_ref, mod_ref, w_ref, cw_ref, cb_ref, *o_refs = refs
        x = x_ref[...]
        x_halo = jnp.concatenate([xp_ref[...], xn_ref[...]], axis=0)
    t = x.shape[0]
    d_rec = widths[0]
    z = _dot(_rms_mod(x, g_ref[...], mod_ref[0:1, :], mod_ref[1:2, :]).astype(BF16), w_ref[...])
    zh = _dot(_rms_mod(x_halo, g_ref[...], mod_ref[0:1, :], mod_ref[1:2, :]).astype(BF16), w_ref[:, :d_rec])
    zz = jnp.concatenate([jnp.where(i == 0, 0.0, zh[:SUBLANES]), z[:, :d_rec],
                          jnp.where(i == n_tiles - 1, 0.0, zh[SUBLANES:])], axis=0)
    u = cb_ref[...]
    for k in range(CONV_W):
        o = SUBLANES - CONV_W // 2 + k
        u = u + cw_ref[k:k + 1, :] * zz[o:o + t, :]
    seg = t // SUBLANES
    for q in range(d_rec // LANES):
        for s in range(SUBLANES):
            o_refs[0][q, pl.ds(s, seg, stride=SUBLANES), :] = u[s * seg:(s + 1) * seg, q * LANES:(q + 1) * LANES]
    off = d_rec
    for o_ref, wd in zip(o_refs[1:], widths[1:]):
        o_ref[...] = z[:, off:off + wd].astype(o_ref.dtype)
        off += wd


def _inproj(x, g, mod, w_bf16, conv_w, conv_b, widths, residual=None):
    b, l, d = x.shape
    t = min(512, l)
    nt = l // t
    per = t // SUBLANES
    n8 = l // SUBLANES
    tok = lambda c: pl.BlockSpec((None, t, c), lambda bi, i: (bi, i, 0))
    prev8 = lambda bi, i: jnp.maximum(i * per - 1, 0)
    next8 = lambda bi, i: jnp.minimum((i + 1) * per, n8 - 1)
    x_specs = [tok(d), pl.BlockSpec((None, SUBLANES, d), lambda bi, i: (bi, prev8(bi, i), 0)),
               pl.BlockSpec((None, SUBLANES, d), lambda bi, i: (bi, next8(bi, i), 0))]
    modspec = pl.BlockSpec((None, N_MOD, d), lambda bi, i: (bi, 0, 0))
    full = lambda a: pl.BlockSpec(a.shape, lambda bi, i: (0,) * a.ndim)
    tail_specs = [pl.BlockSpec((1, d), lambda bi, i: (0, 0)), modspec, full(w_bf16), full(conv_w), full(conv_b)]
    tail_args = (g.reshape(1, d), mod, w_bf16, conv_w, conv_b)
    slabs = widths[0] // LANES
    z_specs = ([pl.BlockSpec((None, None, slabs, t, LANES), lambda bi, i: (bi, i, 0, 0, 0))]
               + [tok(wd) for wd in widths[1:]])
    z_shapes = ([jax.ShapeDtypeStruct((b, nt, slabs, t, LANES), F32)]
                + [jax.ShapeDtypeStruct((b, l, wd), BF16 if k == 0 else F32) for k, wd in enumerate(widths[1:])])
    if residual is None:
        in_specs, args, out_specs, out_shape = x_specs + tail_specs, (x, x, x) + tail_args, z_specs, z_shapes
    else:
        y_rows, row0, prev_mod = residual
        blk0, blk8 = row0 // t, row0 // SUBLANES
        halo = SUBLANES * SUBLANES
        y_specs = [pl.BlockSpec((t * SUBLANES, LANES), lambda bi, i: (blk0 + bi * nt + i, 0)),
                   pl.BlockSpec((halo, LANES), lambda bi, i: (blk8 + bi * n8 + prev8(bi, i), 0)),
                   pl.BlockSpec((halo, LANES), lambda bi, i: (blk8 + bi * n8 + next8(bi, i), 0))]
        in_specs = x_specs + y_specs + [modspec] + tail_specs
        args = (x, x, x, y_rows, y_rows, y_rows, prev_mod) + tail_args
        out_specs, out_shape = [tok(d)] + z_specs, [jax.ShapeDtypeStruct((b, l, d), F32)] + z_shapes
    return pl.pallas_call(
        functools.partial(_inproj_kernel, widths=widths, with_residual=residual is not None, n_tiles=nt),
        grid=(b, nt), in_specs=in_specs, out_specs=out_specs, out_shape=out_shape,
        compiler_params=_cparams("parallel", "parallel"),
    )(*args)


def _rec_kernel(uf_ref, ub_ref, wg_ref, bg_ref, lam_ref, h0_ref, hf_ref, hb_ref, a_scr, b_scr, carry_scr, *, tc):
    i = pl.program_id(1)
    nq = uf_ref.shape[0]
    c = nq * LANES
    half = c // 2
    seg = tc // SUBLANES
    lanes = lambda q: slice(q * LANES, (q + 1) * LANES)

    @pl.when(i == 0)
    def _():
        carry_scr[...] = h0_ref[...]

    def coefficients(d, u_ref):
        u = jnp.concatenate([u_ref[q] for q in range(nq)], axis=-1)
        log_decay = -LRU_C * jax.nn.softplus(-lam_ref[d:d + 1, :])
        for hh in range(2):
            cols = slice(hh * half, (hh + 1) * half)
            uh = u[:, cols]
            pre = _dot(uh.astype(BF16), wg_ref[d, hh])
            r = _sigmoid(pre[:, :half] + bg_ref[d, 0:1, cols])
            ig = _sigmoid(pre[:, half:] + bg_ref[d, 1:2, cols])
            log_a = log_decay[:, cols] * r
            a = jnp.exp(log_a)
            th = jnp.tanh(log_a)
            bt = jnp.sqrt(-2.0 * th / (1.0 - th)) * (ig * uh)
            for k in range(half // LANES):
                q = hh * (half // LANES) + k
                a_scr[d, q] = a[:, lanes(k)]
                b_scr[d, q] = bt[:, lanes(k)]

    coefficients(0, uf_ref)
    coefficients(1, ub_ref)

    def step(d, j, h, acc):
        rows8 = pl.ds(pl.multiple_of(j * SUBLANES, SUBLANES), SUBLANES)
        a8 = jnp.concatenate([a_scr[d, q, rows8, :] for q in range(nq)], axis=-1)
        b8 = jnp.concatenate([b_scr[d, q, rows8, :] for q in range(nq)], axis=-1)
        h = a8 * h + b8
        acc = a8 * acc
        for q in range(nq):
            b_scr[d, q, rows8, :] = h[:, lanes(q)]
            a_scr[d, q, rows8, :] = acc[:, lanes(q)]
        return h, acc

    def body(j, carry):
        hf, af, hb, ab = carry
        hf, af = step(0, j, hf, af)
        hb, ab = step(1, seg - 1 - j, hb, ab)
        return hf, af, hb, ab

    zero = jnp.zeros((SUBLANES, c), F32)
    one = jnp.ones((SUBLANES, c), F32)
    hf_end, af_end, hb_end, ab_end = lax.fori_loop(0, seg, body, (zero, one, zero, one), unroll=2)

    def chain(h_end, a_end, state, order):
        entering = [None] * SUBLANES
        for s in order:
            entering[s] = state
            state = a_end[s:s + 1, :] * state + h_end[s:s + 1, :]
        return jnp.concatenate(entering, axis=0), state

    cin_f, out_f = chain(hf_end, af_end, carry_scr[0:1, :], range(SUBLANES))
    cin_b, out_b = chain(hb_end, ab_end, carry_scr[1:2, :], reversed(range(SUBLANES)))
    carry_scr[0:1, :] = out_f
    carry_scr[1:2, :] = out_b

    for d, cin, out_ref in ((0, cin_f, hf_ref), (1, cin_b, hb_ref)):
        for q in range(nq):
            full_h = (b_scr[d, q].reshape(seg, SUBLANES, LANES)
                      + a_scr[d, q].reshape(seg, SUBLANES, LANES) * cin[None, :, lanes(q)])
            b_scr[d, q] = full_h.reshape(tc, LANES)
            for s in range(SUBLANES):
                out_ref[s * seg:(s + 1) * seg, lanes(q)] = (
                    b_scr[d, q, pl.ds(s, seg, stride=SUBLANES), :].astype(out_ref.dtype))


def _recurrent(u, wgate, bgate, lam, h0, out_dtype):
    b, nc, nq, tc, _ = u.shape
    c = nq * LANES
    u_fwd = pl.BlockSpec((None, None, nq, tc, LANES), lambda bi, i: (bi, i, 0, 0, 0))
    u_bwd = pl.BlockSpec((None, None, nq, tc, LANES), lambda bi, i: (bi, nc - 1 - i, 0, 0, 0))
    h_fwd = pl.BlockSpec((None, tc, c), lambda bi, i: (bi, i, 0))
    h_bwd = pl.BlockSpec((None, tc, c), lambda bi, i: (bi, nc - 1 - i, 0))
    full = lambda a: pl.BlockSpec(a.shape, lambda bi, i: (0,) * a.ndim)
    return pl.pallas_call(
        functools.partial(_rec_kernel, tc=tc),
        grid=(b, nc),
        in_specs=[u_fwd, u_bwd, full(wgate), full(bgate), full(lam),
                  pl.BlockSpec((None, 2, c), lambda bi, i: (bi, 0, 0))],
        out_specs=[h_fwd, h_bwd],
        out_shape=[jax.ShapeDtypeStruct((b, nc * tc, c), out_dtype)] * 2,
        scratch_shapes=[pltpu.VMEM((2, nq, tc, LANES), F32), pltpu.VMEM((2, nq, tc, LANES), F32),
                        pltpu.VMEM((2, c), F32)],
        compiler_params=_cparams("parallel", "arbitrary"),
    )(u, u, wgate, bgate, lam, h0)


def _dft_tables(n):
    j = lax.broadcasted_iota(jnp.int32, (n, n), 0)
    k = lax.broadcasted_iota(jnp.int32, (n, n), 1)
    ang = ((j * k) % n).astype(F32) * (2.0 * math.pi / n)
    return jnp.cos(ang), jnp.sin(ang)


def _block_diag(w):
    return jax.scipy.linalg.block_diag(*[w[h] for h in range(w.shape[0])])


def _channel_stage(gr, gi, c_ref, s_ref, w_ref, scale):
    f = (_dot(gr.astype(BF16), c_ref[...]) + _dot(gi.astype(BF16), s_ref[...])) * scale
    return _dot(f.astype(BF16), w_ref[...])


def _slabs_from(scr, val):
    for q in range(scr.shape[0]):
        scr[q] = val[:, q * LANES:(q + 1) * LANES]


def _slab_rows(scr, start, n):
    return jnp.concatenate([scr[q, pl.ds(start, n, stride=SUBLANES), :] for q in range(scr.shape[0])], axis=-1)


def _set_slab_rows(scr, start, val):
    for q in range(scr.shape[0]):
        scr[q, pl.ds(start, val.shape[0], stride=SUBLANES), :] = val[:, q * LANES:(q + 1) * LANES]


def _slabs_value(scr):
    return jnp.concatenate([scr[q] for q in range(scr.shape[0])], axis=-1)


def _four1_kernel(x_ref, f1_ref, twc_ref, tws_ref, ar_ref, ai_ref, xs, ars, ais, *, n1):
    shape = x_ref.shape
    _slabs_from(xs, x_ref[...].reshape(n1 * SUBLANES, shape[-1]))
    for jj in range(SUBLANES):
        a = _dot(f1_ref[...], _slab_rows(xs, jj, n1).astype(BF16))
        ar, ai = a[:n1], a[n1:]
        cc = twc_ref[:, jj:jj + 1]
        ss = tws_ref[:, jj:jj + 1]
        _set_slab_rows(ars, jj, ar * cc + ai * ss)
        _set_slab_rows(ais, jj, ai * cc - ar * ss)
    ar_ref[...] = _slabs_value(ars).reshape(shape)
    ai_ref[...] = _slabs_value(ais).reshape(shape)


def _four2_kernel(ar_ref, ai_ref, f2_ref, c_ref, s_ref, w_ref, o_ref, ys, *, n2, scale):
    grs, gis = [], []
    for kk in range(SUBLANES):
        slab = jnp.concatenate([ar_ref[kk], ai_ref[kk]], axis=0).astype(BF16)
        g = _dot(f2_ref[...], slab)
        grs.append(g[:n2])
        gis.append(g[n2:])
    y = _channel_stage(jnp.concatenate(grs, axis=0), jnp.concatenate(gis, axis=0), c_ref, s_ref, w_ref, scale)
    for kk in range(SUBLANES):
        _set_slab_rows(ys, kk, y[kk * n2:(kk + 1) * n2])
    o_ref[...] = _slabs_value(ys).reshape(o_ref.shape)


def _four_small_kernel(x_ref, f_ref, c_ref, s_ref, w_ref, o_ref, *, n, scale):
    g = _dot(f_ref[...], x_ref[...].astype(BF16))
    o_ref[...] = _channel_stage(g[:n], g[n:], c_ref, s_ref, w_ref, scale)


def _fourier(z_f, four_w_bd):
    b, l, nch = z_f.shape
    hd = nch // FOUR_HEADS
    scale = 1.0 / math.sqrt(l * hd)
    c_h, s_h = _dft_tables(hd)
    eye = jnp.eye(FOUR_HEADS, dtype=F32)
    c_bd = jnp.kron(eye, c_h).astype(BF16)
    s_bd = jnp.kron(eye, s_h).astype(BF16)
    full = lambda a, nd: pl.BlockSpec(a.shape, lambda *_: (0,) * a.ndim)
    if l <= 256:
        cn, sn = _dft_tables(l)
        f = jnp.concatenate([cn, -sn], axis=0).astype(BF16)
        return pl.pallas_call(
            functools.partial(_four_small_kernel, n=l, scale=scale),
            grid=(b,),
            in_specs=[pl.BlockSpec((None, l, nch), lambda bi: (bi, 0, 0)),
                      full(f, 1), full(c_bd, 1), full(s_bd, 1), full(four_w_bd, 1)],
            out_specs=pl.BlockSpec((None, l, nch), lambda bi: (bi, 0, 0)),
            out_shape=jax.ShapeDtypeStruct((b, l, nch), F32),
            compiler_params=_cparams("parallel"),
        )(z_f, f, c_bd, s_bd, four_w_bd)

    n2 = LANES
    n1 = l // n2
    nj = n2 // SUBLANES
    c1, s1 = _dft_tables(n1)
    f1 = jnp.concatenate([c1, -s1], axis=0).astype(BF16)
    c2, s2 = _dft_tables(n2)
    f2 = jnp.concatenate([jnp.concatenate([c2, s2], axis=1),
                          jnp.concatenate([-s2, c2], axis=1)], axis=0).astype(BF16)
    k1 = lax.broadcasted_iota(jnp.int32, (n1, n2), 0)
    t2 = lax.broadcasted_iota(jnp.int32, (n1, n2), 1)
    ang = (k1 * t2).astype(F32) * (2.0 * math.pi / l)
    twc = jnp.cos(ang).reshape(n1, nj, SUBLANES).transpose(1, 0, 2)
    tws = jnp.sin(ang).reshape(n1, nj, SUBLANES).transpose(1, 0, 2)
    ar, ai = pl.pallas_call(
        functools.partial(_four1_kernel, n1=n1),
        grid=(b, nj),
        in_specs=[pl.BlockSpec((None, n1, SUBLANES, nch), lambda bi, j: (bi, 0, j, 0)),
                  full(f1, 2),
                  pl.BlockSpec((None, n1, SUBLANES), lambda bi, j: (j, 0, 0)),
                  pl.BlockSpec((None, n1, SUBLANES), lambda bi, j: (j, 0, 0))],
        out_specs=[pl.BlockSpec((None, n1, SUBLANES, nch), lambda bi, j: (bi, 0, j, 0))] * 2,
        out_shape=[jax.ShapeDtypeStruct((b, n1, n2, nch), F32)] * 2,
        scratch_shapes=[pltpu.VMEM((nch // LANES, n1 * SUBLANES, LANES), F32)] * 3,
        compiler_params=_cparams("parallel", "parallel"),
    )(z_f.reshape(b, n1, n2, nch), f1, twc, tws)
    y = pl.pallas_call(
        functools.partial(_four2_kernel, n2=n2, scale=scale),
        grid=(b, n1 // SUBLANES),
        in_specs=[pl.BlockSpec((None, SUBLANES, n2, nch), lambda bi, j: (bi, j, 0, 0)),
                  pl.BlockSpec((None, SUBLANES, n2, nch), lambda bi, j: (bi, j, 0, 0)),
                  full(f2, 2), full(c_bd, 2), full(s_bd, 2), full(four_w_bd, 2)],
        out_specs=pl.BlockSpec((None, n2, SUBLANES, nch), lambda bi, j: (bi, 0, j, 0)),
        out_shape=jax.ShapeDtypeStruct((b, n2, n1, nch), F32),
        scratch_shapes=[pltpu.VMEM((nch // LANES, n2 * SUBLANES, LANES), F32)],
        compiler_params=_cparams("parallel", "parallel"),
    )(ar, ai, f2, c_bd, s_bd, four_w_bd)
    return y.reshape(b, l, nch)


def _window_counts(idx, n, w):
    return (jnp.minimum(idx + (w - w // 2), n) - jnp.maximum(idx - w // 2, 0)).astype(F32)


def _pool_kernel(cur_ref, prev_ref, next_ref, pw_ref, ps_ref, o_ref, *, r, w, rows_total, n_tiles):
    i = pl.program_id(1)
    two_d = rows_total > 1
    halo = SUBLANES if two_d else 0
    nr = r + 2 * halo
    stride = w + POOL_PAD
    flat = nr * stride
    lane = lax.broadcasted_iota(jnp.int32, (1, 1, LANES), 2)
    low = lane < (LANES // 2)
    col = lax.broadcasted_iota(jnp.int32, (1, w, 1), 1)
    grow = lax.broadcasted_iota(jnp.int32, (r, 1, 1), 0) + i * r
    outs = []
    for hh in range(2):
        cols = slice(hh * LANES, (hh + 1) * LANES)
        w_lo, w_hi = POOL_WINDOWS[2 * hh], POOL_WINDOWS[2 * hh + 1]
        cur = cur_ref[:, cols]
        if two_d:
            prev = jnp.where(i == 0, 0.0, prev_ref[:, cols])
            nxt = jnp.where(i == n_tiles - 1, 0.0, next_ref[:, cols])
            ext = jnp.concatenate([prev, cur, nxt], axis=0)
        else:
            ext = cur
        ext = ext.reshape(nr, w, LANES)
        x = jnp.concatenate([jnp.zeros((nr, POOL_PAD, LANES), F32), ext], axis=1).reshape(flat, LANES)
        p = x + pltpu.roll(x, 1, 0)
        sums = {2: p}
        for ww, s in ((4, 1), (8, 2), (16, 4)):
            if ww > w_hi:
                break
            p = pltpu.roll(p, s, 0) + pltpu.roll(p, flat - s, 0)
            sums[ww] = p
        y = jnp.where(low[0], sums[w_lo], sums[w_hi]).reshape(nr, stride, LANES)[:, POOL_PAD:, :]
        if two_d:
            q2 = y[0:nr - 1] + y[1:nr]
            rsum = {2: q2[7:7 + r]}
            q4 = q2[0:nr - 3] + q2[2:nr - 1]
            rsum[4] = q4[6:6 + r]
            if w_hi > 4:
                q8 = q4[0:nr - 7] + q4[4:nr - 3]
                rsum[8] = q8[4:4 + r]
                q16 = q8[0:nr - 15] + q8[8:nr - 7]
                rsum[16] = q16[0:r]
            tot = jnp.where(low, rsum[w_lo], rsum[w_hi])
            cnt_r = jnp.where(low, _window_counts(grow, rows_total, w_lo), _window_counts(grow, rows_total, w_hi))
            tot = tot / cnt_r
        else:
            tot = y
        cnt_c = jnp.where(low, _window_counts(col, w, w_lo), _window_counts(col, w, w_hi))
        pooled = (tot / cnt_c).reshape(r * w, LANES)
        outs.append(pooled - cur)
    d = jnp.concatenate(outs, axis=-1)
    o_ref[...] = (_dot(d.astype(BF16), pw_ref[...]) * ps_ref[...]).astype(o_ref.dtype)


def _pool(z_p, pool_w_bd, pool_scale, rows):
    b, l, nch = z_p.shape
    if rows is None:
        rows_total, w, r = 1, l, 1
    else:
        rows_total, w, r = rows, l // rows, min(16, rows)
    t = r * w
    n_tiles = l // t
    hb = SUBLANES * w if rows is not None else t
    per = t // hb
    nhb = l // hb
    return pl.pallas_call(
        functools.partial(_pool_kernel, r=r, w=w, rows_total=rows_total, n_tiles=n_tiles),
        grid=(b, n_tiles),
        in_specs=[pl.BlockSpec((None, t, nch), lambda bi, i: (bi, i, 0)),
                  pl.BlockSpec((None, hb, nch), lambda bi, i: (bi, jnp.maximum(i * per - 1, 0), 0)),
                  pl.BlockSpec((None, hb, nch), lambda bi, i: (bi, jnp.minimum((i + 1) * per, nhb - 1), 0)),
                  pl.BlockSpec(pool_w_bd.shape, lambda bi, i: (0, 0)),
                  pl.BlockSpec((1, nch), lambda bi, i: (0, 0))],
        out_specs=pl.BlockSpec((None, t, nch), lambda bi, i: (bi, i, 0)),
        out_shape=jax.ShapeDtypeStruct((b, l, nch), BF16),
        compiler_params=_cparams("parallel", "parallel"),
    )(z_p, z_p, z_p, pool_w_bd, pool_scale.reshape(1, nch))


def _first_argmax4(v):
    m = jnp.maximum(jnp.maximum(v[0], v[1]), jnp.maximum(v[2], v[3]))
    idx = jnp.where(v[0] >= m, 0, jnp.where(v[1] >= m, 1, jnp.where(v[2] >= m, 2, 3)))
    return m, idx


def _select4(idx, v):
    return jnp.where(idx == 0, v[0], jnp.where(idx == 1, v[1], jnp.where(idx == 2, v[2], v[3])))


def _router_logits(h_bf16, wr_ref, br_ref):
    return (_dot(h_bf16, wr_ref[...]) + br_ref[...]).T


def _group_logits(lt):
    return [lt[g:g + 1, :] for g in range(N_GROUPS)]


def _expert_gates(lt, group):
    lg = _group_logits(lt)
    m, _ = _first_argmax4(lg)
    den = sum(jnp.exp(v - m) for v in lg)
    p_group = jnp.exp(_select4(group, lg) - m) / den
    sel = [_select4(group, [lt[SUBLANES + EXPERTS_PER_GROUP * g + e:SUBLANES + EXPERTS_PER_GROUP * g + e + 1, :]
                            for g in range(N_GROUPS)]) for e in range(EXPERTS_PER_GROUP)]
    v1, e1 = _first_argmax4(sel)
    rest = [jnp.where(e1 == e, -jnp.inf, sel[e]) for e in range(EXPERTS_PER_GROUP)]
    v2, e2 = _first_argmax4(rest)
    t2 = jnp.exp(v2 - v1)
    w1 = p_group / (1.0 + t2)
    w2 = p_group * t2 / (1.0 + t2)
    return [jnp.where(e1 == e, w1, jnp.where(e2 == e, w2, 0.0)) for e in range(EXPERTS_PER_GROUP)]


def _combine_kernel(hf_ref, hb_ref, zg_ref, yp_ref, yf_ref, x_ref, mod_ref, wo_ref, g_ref, wr_ref, br_ref,
                    xo_ref, h2t_ref, gid_ref):
    y_a = ((hf_ref[...].astype(F32) + hb_ref[...].astype(F32))
           * jax.nn.gelu(zg_ref[...].astype(F32), approximate=True))
    ycat = jnp.concatenate([y_a.astype(BF16), yp_ref[...].astype(BF16), yf_ref[...].astype(BF16)], axis=-1)
    x_new = x_ref[...] + mod_ref[2:3, :] * _dot(ycat, wo_ref[...])
    xo_ref[...] = x_new
    h2 = _rms_mod(x_new, g_ref[...], mod_ref[3:4, :], mod_ref[4:5, :])
    _to_token_tiles(h2t_ref, h2)
    _, gidx = _first_argmax4(_group_logits(_router_logits(h2.astype(BF16), wr_ref, br_ref)))
    gid_ref[...] = gidx


def _combine(hf, hb, z_g, y_p, y_f, x, mod, w_out, g_ffn, w_router, b_router):
    b, l, d = x.shape
    assert d == SUBLANES * LANES, "a token tile is one (8, 128) f32 tile"
    t = min(1024, l)
    tok = lambda c: pl.BlockSpec((None, t, c), lambda bi, i: (bi, i, 0))
    full = lambda a: pl.BlockSpec(a.shape, lambda bi, i: (0,) * a.ndim)
    return pl.pallas_call(
        _combine_kernel,
        grid=(b, l // t),
        in_specs=[tok(hf.shape[-1]), tok(hb.shape[-1]), tok(z_g.shape[-1]), tok(y_p.shape[-1]), tok(y_f.shape[-1]),
                  tok(d), pl.BlockSpec((None, N_MOD, d), lambda bi, i: (bi, 0, 0)),
                  full(w_out), full(g_ffn), full(w_router), full(b_router)],
        out_specs=[tok(d), pl.BlockSpec((None, t * SUBLANES, LANES), lambda bi, i: (bi, i, 0)),
                   pl.BlockSpec((None, 1, t), lambda bi, i: (bi, 0, i))],
        out_shape=[jax.ShapeDtypeStruct((b, l, d), F32), jax.ShapeDtypeStruct((b, l * SUBLANES, LANES), F32),
                   jax.ShapeDtypeStruct((b, 1, l), jnp.int32)],
        compiler_params=_cparams("parallel", "parallel"),
    )(hf, hb, z_g, y_p, y_f, x, mod, w_out, g_ffn, w_router, b_router)


MOE_TILE = 1024


def _sort_plan(gid, n_groups, t):
    n = gid.shape[0]
    onehot = (gid[:, None] == jnp.arange(n_groups, dtype=jnp.int32)[None, :]).astype(jnp.int32)
    csum = jnp.cumsum(onehot, axis=0)
    rank = jnp.sum((csum - onehot) * onehot, axis=1)
    tiles = (csum[-1] + t - 1) // t
    tile_end = jnp.cumsum(tiles)
    slot = jnp.sum(onehot * (tile_end - tiles)[None, :], axis=1) * t + rank
    n_tiles = n // t + n_groups
    tile_idx = jnp.arange(n_tiles, dtype=jnp.int32)
    tile_group = jnp.minimum(jnp.sum((tile_idx[:, None] >= tile_end[None, :]).astype(jnp.int32), axis=1), n_groups - 1)
    pad_lo = jnp.concatenate([(tile_end - tiles) * t + csum[-1], tile_end[-1:] * t]).astype(jnp.int32)
    pad_hi = jnp.concatenate([tile_end * t, jnp.full((1,), n_tiles * t, jnp.int32)]).astype(jnp.int32)
    return (slot.astype(jnp.int32), tile_group.astype(jnp.int32), tile_end[-1:].astype(jnp.int32), pad_lo, pad_hi,
            n_tiles)


def _invert_kernel(slot_ref, pad_lo_ref, pad_hi_ref, src_ref):
    def clear(s, carry):
        src_ref[s] = -1
        return carry

    for k in range(pad_lo_ref.shape[0]):
        lax.fori_loop(pad_lo_ref[k], pad_hi_ref[k], clear, 0)

    def put(tok, carry):
        src_ref[slot_ref[tok]] = tok
        return carry

    lax.fori_loop(0, slot_ref.shape[0], put, 0, unroll=16)


def _invert(slot, pad_lo, pad_hi, n_rows):
    smem = pl.BlockSpec(memory_space=pltpu.SMEM)
    return pl.pallas_call(
        _invert_kernel,
        in_specs=[smem, smem, smem],
        out_specs=smem,
        out_shape=jax.ShapeDtypeStruct((n_rows,), jnp.int32),
    )(slot, pad_lo, pad_hi)


def _moe_fused_kernel(tg_ref, nu_ref, nv_ref, src_ref, dst_ref, h_ref, wr_ref, br_ref, wg_ref, wu_ref, wd_ref, y_ref,
                      rows, ybuf, hb_scr, gt_scr, gsem, ssem, *, t, epg):
    i = pl.program_id(0)
    nu = nu_ref[0]
    s = lax.rem(i, 2)
    tile_rows = t * SUBLANES

    def token(ref, first_row):
        return ref.at[pl.ds(pl.multiple_of(first_row, SUBLANES), SUBLANES)]

    def gather_rows(tile, slot, lo, hi):
        def body(g, carry):
            for k in range(SUBLANES):
                r = g * SUBLANES + k
                pltpu.make_async_copy(token(h_ref, src_ref[tile * t + r]), token(rows.at[slot], r * SUBLANES),
                                      gsem.at[slot]).start(priority=k % 2)
            return carry
        lax.fori_loop(lo // SUBLANES, hi // SUBLANES, body, 0)

    def scatter_rows(tile, slot, lo, hi):
        def body(g, carry):
            for k in range(SUBLANES):
                r = g * SUBLANES + k
                pltpu.make_async_copy(token(ybuf.at[slot], r * SUBLANES), token(y_ref, dst_ref[tile * t + r]),
                                      ssem.at[slot]).start(priority=k % 2)
            return carry
        lax.fori_loop(lo // SUBLANES, hi // SUBLANES, body, 0)

    def wait_gather(slot):
        pltpu.make_async_copy(h_ref.at[pl.ds(0, tile_rows)], rows.at[slot], gsem.at[slot]).wait()

    def wait_scatter(slot):
        pltpu.make_async_copy(ybuf.at[slot], y_ref.at[pl.ds(0, tile_rows)], ssem.at[slot]).wait()

    @pl.when(i == 0)
    def _():
        gt_scr[...] = jnp.zeros_like(gt_scr)
        gather_rows(0, 0, 0, t)

    @pl.when(jnp.logical_and(i >= 2, i - 2 < nu))
    def _():
        wait_scatter(s)

    @pl.when(i < nu)
    def _():
        wait_gather(s)
        hb_scr[...] = _from_token_tiles(rows, (s,)).astype(BF16)
        tok_id = lax.broadcasted_iota(jnp.int32, (1, t), 1)
        gate_rows = _expert_gates(_router_logits(hb_scr[...], wr_ref, br_ref), tg_ref[i])
        for e in range(epg):
            gt_scr[e:e + 1, :] = jnp.where(tok_id < nv_ref[i], gate_rows[e], 0.0)
        gates = gt_scr[...].T
        @pl.when(i + 1 < nu)
        def _():
            gather_rows(i + 1, 1 - s, 0, t)

        @pl.when(i >= 1)
        def _():
            scatter_rows(i - 1, 1 - s, 0, t)

        acc = None
        for e in range(epg):
            hid = jax.nn.silu(_dot(hb_scr[...], wg_ref[e])) * _dot(hb_scr[...], wu_ref[e])
            y = gates[:, e:e + 1] * _dot(hid.astype(BF16), wd_ref[e])
            acc = y if acc is None else acc + y
        _to_token_tiles(ybuf, acc, (s,))

    @pl.when(i == nu)
    def _():
        scatter_rows(i - 1, 1 - s, 0, t)

    @pl.when(jnp.logical_and(i > nu, i < pl.num_programs(0) - 1))
    def _():
        rows[0] = jnp.zeros(rows.shape[1:], F32)
        first = pl.multiple_of(dst_ref[(i - 1) * t], SUBLANES)
        fill = pltpu.make_async_copy(rows.at[0], y_ref.at[pl.ds(first, tile_rows)], gsem.at[0])
        fill.start()
        fill.wait()


def _moe_fused(tile_group, n_used, n_valid, src, dst, h2t, w_router, b_router, wg, wu, wd, group0, n_tiles, t):
    epg = wg.shape[1]
    last = n_tiles - 1
    d = SUBLANES * LANES
    tile_rows = t * SUBLANES
    full = lambda a: pl.BlockSpec(a.shape, lambda i, *prefetch: (0,) * a.ndim)
    by_group = lambda a: pl.BlockSpec((None,) + a.shape[1:],
                                      lambda i, tg, *rest: (group0 + tg[jnp.minimum(i, last)], 0, 0, 0),
                                      pipeline_mode=pl.Buffered(1))
    return pl.pallas_call(
        functools.partial(_moe_fused_kernel, t=t, epg=epg),
        grid_spec=pltpu.PrefetchScalarGridSpec(
            num_scalar_prefetch=5, grid=(n_tiles + 2,),
            in_specs=[pl.BlockSpec(memory_space=pl.ANY), full(w_router), full(b_router),
                      by_group(wg), by_group(wu), by_group(wd)],
            out_specs=pl.BlockSpec(memory_space=pl.ANY),
            scratch_shapes=[pltpu.VMEM((2, tile_rows, LANES), F32), pltpu.VMEM((2, tile_rows, LANES), F32),
                            pltpu.VMEM((t, d), BF16), pltpu.VMEM((LANES, t), F32),
                            pltpu.SemaphoreType.DMA((2,)), pltpu.SemaphoreType.DMA((2,))]),
        out_shape=jax.ShapeDtypeStruct((n_tiles * tile_rows, LANES), F32),
        compiler_params=_cparams("arbitrary"),
    )(tile_group, n_used, n_valid, src, dst, h2t, w_router, b_router, wg, wu, wd)


def _final_kernel(x_ref, y_ref, mod_ref, g_ref, o_ref):
    out = x_ref[...] + mod_ref[5:6, :] * _from_token_tiles(y_ref)
    ms = jnp.mean(out * out, axis=-1, keepdims=True)
    o_ref[...] = out * lax.rsqrt(ms + EPS) * g_ref[...]


def _final(x, y_rows, mod, g_final):
    b, l, d = x.shape
    t = min(1024, l)
    nt = l // t
    return pl.pallas_call(
        _final_kernel,
        grid=(b, nt),
        in_specs=[pl.BlockSpec((None, t, d), lambda bi, i: (bi, i, 0)),
                  pl.BlockSpec((t * SUBLANES, LANES), lambda bi, i: (bi * nt + i, 0)),
                  pl.BlockSpec((None, N_MOD, d), lambda bi, i: (bi, 0, 0)),
                  pl.BlockSpec((1, d), lambda bi, i: (0, 0))],
        out_specs=pl.BlockSpec((None, t, d), lambda bi, i: (bi, i, 0)),
        out_shape=jax.ShapeDtypeStruct((b, l, d), F32),
        compiler_params=_cparams("parallel", "parallel"),
    )(x, y_rows, mod, g_final)


def kernel(x, c, ctx, c_ctx, mod_w, mod_b, norm_mix_g, norm_ffn_g, w_in, conv_w, conv_b, rec_gate_a_w, rec_gate_a_b,
           rec_gate_x_w, rec_gate_x_b, rec_lambda, pool_w, pool_scale, fourier_w, w_out, router_group_w,
           router_group_b, router_expert_w, router_expert_b, expert_w_gate, expert_w_up, expert_w_down, final_norm_g):
    b, l, d = x.shape
    depth = mod_w.shape[0]
    d_rec = conv_w.shape[-1]
    d_pool = pool_scale.shape[-1]
    d_four = fourier_w.shape[1] * fourier_w.shape[2]
    widths = (d_rec, d_rec, d_pool, d_four)
    rows = l // GRID_W
    heads_half = REC_HEADS // 2
    n_exp = expert_w_gate.shape[1]

    cvec = jnp.concatenate([c, c_ctx[None, :], jnp.zeros((SUBLANES - b - 1, d), F32)], axis=0)
    mod_all = _modulation(cvec, mod_w, mod_b).reshape(depth, SUBLANES, N_MOD, d)
    g_final = final_norm_g.reshape(1, d)
    zeros_state = jnp.zeros((b, 2, d_rec), F32)
    pending = pending_ctx = None
    by_group = lambda w: w.astype(BF16).reshape((depth * N_GROUPS, EXPERTS_PER_GROUP) + w.shape[2:])
    expert_w = (by_group(expert_w_gate), by_group(expert_w_up), by_group(expert_w_down))

    for li in range(depth):
        last = li == depth - 1
        mod_lat = mod_all[li, :b]
        mod_ctx = jnp.broadcast_to(mod_all[li, b][None], (b, N_MOD, d))
        w_in_l = w_in[li].astype(BF16)
        w_out_l = w_out[li].astype(BF16)
        wgate = jnp.stack([
            jnp.stack([jnp.concatenate([_block_diag(rec_gate_a_w[li, dd, hh * heads_half:(hh + 1) * heads_half]),
                                        _block_diag(rec_gate_x_w[li, dd, hh * heads_half:(hh + 1) * heads_half])], axis=1)
                       for hh in range(2)]) for dd in range(2)]).astype(BF16)
        bgate = jnp.stack([rec_gate_a_b[li], rec_gate_x_b[li]], axis=1)
        rec_p = (wgate, bgate, rec_lambda[li])
        proj_p = (w_in_l, conv_w[li], conv_b[li].reshape(1, d_rec), widths)
        pool_w_bd = _block_diag(pool_w[li]).astype(BF16)
        four_w_bd = _block_diag(fourier_w[li]).astype(BF16)
        w_router = jnp.concatenate([router_group_w[li], jnp.zeros((d, SUBLANES - N_GROUPS), F32), router_expert_w[li],
                                    jnp.zeros((d, LANES - SUBLANES - n_exp), F32)], axis=1).astype(BF16)
        b_router = jnp.concatenate([router_group_b[li], jnp.zeros((SUBLANES - N_GROUPS,), F32), router_expert_b[li],
                                    jnp.zeros((LANES - SUBLANES - n_exp,), F32)]).reshape(1, LANES)
        g_ffn = norm_ffn_g[li].reshape(1, d)

        def mixer_tail(hf, hb, z_g, z_p, z_f, stream, mod, grid_rows):
            y_p = _pool(z_p, pool_w_bd, pool_scale[li], grid_rows)
            y_f = _fourier(z_f, four_w_bd)
            return _combine(hf, hb, z_g, y_p, y_f, stream, mod, w_out_l, g_ffn, w_router, b_router)

        def expert_stage(h2t, gid):
            n_tok = gid.size
            t_moe = min(MOE_TILE, h2t.shape[1] // SUBLANES)
            slot, tile_group, n_used, pad_lo, pad_hi, n_tiles = _sort_plan(gid.reshape(n_tok), N_GROUPS, t_moe)
            src = _invert(slot, pad_lo, pad_hi, n_tiles * t_moe)
            valid = src >= 0
            pad_rank = jnp.cumsum(jnp.logical_not(valid).astype(jnp.int32)) - 1
            dst = jnp.where(valid, src, n_tok + pad_rank)
            n_valid = jnp.sum(valid.reshape(n_tiles, t_moe).astype(jnp.int32), axis=1)
            return _moe_fused(tile_group, n_used, n_valid, jnp.maximum(src, 0) * SUBLANES, dst * SUBLANES,
                              h2t.reshape(n_tok * SUBLANES, LANES),
                              w_router, b_router, *expert_w, li * N_GROUPS, n_tiles, t_moe)

        if pending_ctx is None:
            uc, zc_g, zc_p, zc_f = _inproj(ctx, norm_mix_g[li], mod_ctx, *proj_p)
        else:
            ctx, uc, zc_g, zc_p, zc_f = _inproj(ctx, norm_mix_g[li], mod_ctx, *proj_p, residual=pending_ctx)
        hf_c, hb_c = _recurrent(uc, *rec_p, zeros_state, F32)
        state = jnp.stack([hf_c[:, -1, :], hb_c[:, 0, :]], axis=1)
        if not last:
            ctx, h2t_c, gid_c = mixer_tail(hf_c, hb_c, zc_g, zc_p, zc_f, ctx, mod_ctx, None)
            pending_ctx = (expert_stage(h2t_c, gid_c), 0, mod_ctx)

        if pending is None:
            u, z_g, z_p, z_f = _inproj(x, norm_mix_g[li], mod_lat, *proj_p)
        else:
            x, u, z_g, z_p, z_f = _inproj(x, norm_mix_g[li], mod_lat, *proj_p, residual=pending)
        hf, hb = _recurrent(u, *rec_p, state, BF16)
        x, h2t, gid = mixer_tail(hf, hb, z_g, z_p, z_f, x, mod_lat, rows)
        pending = (expert_stage(h2t, gid), 0, mod_lat)
    return _final(x, pending[0], pending[2], g_final)
```

```python
import functools
import math

import jax
import jax.numpy as jnp
from jax import lax
from jax.experimental import pallas as pl
from jax.experimental.pallas import tpu as pltpu

F32 = jnp.float32
BF16 = jnp.bfloat16

GRID_W = 64
N_MOD = 6
REC_HEADS = 8
CONV_W = 4
LRU_C = 8.0
POOL_WINDOWS = (2, 4, 8, 16)
FOUR_HEADS = 4
N_GROUPS = 4
EXPERTS_PER_GROUP = 4
EPS = 1e-6

LANES = 128
SUBLANES = 8
POOL_PAD = 8
VMEM_LIMIT = 56 * 1024 * 1024


def _cparams(*sem):
    return pltpu.CompilerParams(dimension_semantics=sem, vmem_limit_bytes=VMEM_LIMIT)


def _split_bf16(a):
    hi = a.astype(BF16)
    lo = (a - hi.astype(F32)).astype(BF16)
    return hi, lo


def _dot(a, b):
    return jnp.dot(a, b, preferred_element_type=F32)


def _sigmoid_of_twice(half_x):
    return 0.5 * jnp.tanh(half_x) + 0.5


def _mod_kernel(s_ref, w_ref, b_ref, o_ref):
    s = s_ref[...]
    s = s * jax.nn.sigmoid(s)
    s_hi, s_lo = _split_bf16(s)
    w_hi, w_lo = _split_bf16(w_ref[...])
    o_ref[...] = _dot(s_hi, w_hi) + _dot(s_hi, w_lo) + _dot(s_lo, w_hi) + b_ref[...]


def _modulation(cvec, mod_w, mod_b):
    depth, d, dm = mod_w.shape
    tn = dm // 4
    return pl.pallas_call(
        _mod_kernel,
        grid=(depth, dm // tn),
        in_specs=[pl.BlockSpec((SUBLANES, d), lambda l, j: (0, 0)),
                  pl.BlockSpec((None, d, tn), lambda l, j: (l, 0, j)),
                  pl.BlockSpec((None, 1, tn), lambda l, j: (l, 0, j))],
        out_specs=pl.BlockSpec((None, SUBLANES, tn), lambda l, j: (l, 0, j)),
        out_shape=jax.ShapeDtypeStruct((depth, SUBLANES, dm), F32),
        compiler_params=_cparams("parallel", "parallel"),
    )(cvec, mod_w, mod_b.reshape(depth, 1, dm))


def _rms_mod(x, g, shift, scale):
    ms = jnp.mean(x * x, axis=-1, keepdims=True)
    y = x * lax.rsqrt(ms + EPS) * g
    return y * (1.0 + scale) + shift


def _from_token_tiles(ref, lead=()):
    n = ref.shape[-2] // SUBLANES
    return jnp.concatenate([ref[lead + (pl.ds(j, n, stride=SUBLANES), slice(None))] for j in range(SUBLANES)], axis=-1)


def _to_token_tiles(ref, val, lead=()):
    n = val.shape[0]
    for j in range(SUBLANES):
        ref[lead + (pl.ds(j, n, stride=SUBLANES), slice(None))] = val[:, j * LANES:(j + 1) * LANES]


def _inproj_kernel(*refs, widths, with_residual, n_tiles):
    i = pl.program_id(1)
    if with_residual:
        (x_ref, xp_ref, xn_ref, y_ref, yp_ref, yn_ref, pmod_ref,
         g_ref, mod_ref, w_ref, cw_ref, cb_ref, xo_ref, *o_refs) = refs
        gate = pmod_ref[5:6, :]
        x = x_ref[...] + gate * _from_token_tiles(y_ref)
        xo_ref[...] = x
        x_halo = jnp.concatenate([xp_ref[...] + gate * _from_token_tiles(yp_ref),
                                  xn_ref[...] + gate * _from_token_tiles(yn_ref)], axis=0)
    else:
        x_ref, xp_ref, xn_ref, g_ref, mod_ref, w_ref, cw_ref, cb_ref, *o_refs = refs
        x = x_ref[...]
        x_halo = jnp.concatenate([xp_ref[...], xn_ref[...]], axis=0)
    t = x.shape[0]
    d_rec = widths[0]
    z = _dot(_rms_mod(x, g_ref[...], mod_ref[0:1, :], mod_ref[1:2, :]).astype(BF16), w_ref[...])
    zh = _dot(_rms_mod(x_halo, g_ref[...], mod_ref[0:1, :], mod_ref[1:2, :]).astype(BF16), w_ref[:, :d_rec])
    zz = jnp.concatenate([jnp.where(i == 0, 0.0, zh[:SUBLANES]), z[:, :d_rec],
                          jnp.where(i == n_tiles - 1, 0.0, zh[SUBLANES:])], axis=0)
    u = cb_ref[...]
    for k in range(CONV_W):
        o = SUBLANES - CONV_W // 2 + k
        u = u + cw_ref[k:k + 1, :] * zz[o:o + t, :]
    seg = t // SUBLANES
    for q in range(d_rec // LANES):
        for s in range(SUBLANES):
            o_refs[0][q, pl.ds(s, seg, stride=SUBLANES), :] = u[s * seg:(s + 1) * seg, q * LANES:(q + 1) * LANES]
    off = d_rec
    for o_ref, wd in zip(o_refs[1:], widths[1:]):
        o_ref[...] = z[:, off:off + wd].astype(o_ref.dtype)
        off += wd


def _inproj(x, g, mod, w_bf16, conv_w, conv_b, widths, residual=None):
    b, l, d = x.shape
    t = min(512, l)
    nt = l // t
    per = t // SUBLANES
    n8 = l // SUBLANES
    tok = lambda c: pl.BlockSpec((None, t, c), lambda bi, i: (bi, i, 0))
    prev8 = lambda bi, i: jnp.maximum(i * per - 1, 0)
    next8 = lambda bi, i: jnp.minimum((i + 1) * per, n8 - 1)
    x_specs = [tok(d), pl.BlockSpec((None, SUBLANES, d), lambda bi, i: (bi, prev8(bi, i), 0)),
               pl.BlockSpec((None, SUBLANES, d), lambda bi, i: (bi, next8(bi, i), 0))]
    modspec = pl.BlockSpec((None, N_MOD, d), lambda bi, i: (bi, 0, 0))
    full = lambda a: pl.BlockSpec(a.shape, lambda bi, i: (0,) * a.ndim)
    tail_specs = [pl.BlockSpec((1, d), lambda bi, i: (0, 0)), modspec, full(w_bf16), full(conv_w), full(conv_b)]
    tail_args = (g.reshape(1, d), mod, w_bf16, conv_w, conv_b)
    slabs = widths[0] // LANES
    z_specs = ([pl.BlockSpec((None, None, slabs, t, LANES), lambda bi, i: (bi, i, 0, 0, 0))]
               + [tok(wd) for wd in widths[1:]])
    z_shapes = ([jax.ShapeDtypeStruct((b, nt, slabs, t, LANES), F32)]
                + [jax.ShapeDtypeStruct((b, l, wd), BF16 if k == 0 else F32) for k, wd in enumerate(widths[1:])])
    if residual is None:
        in_specs, args, out_specs, out_shape = x_specs + tail_specs, (x, x, x) + tail_args, z_specs, z_shapes
    else:
        y_rows, row0, prev_mod = residual
        blk0, blk8 = row0 // t, row0 // SUBLANES
        halo = SUBLANES * SUBLANES
        y_specs = [pl.BlockSpec((t * SUBLANES, LANES), lambda bi, i: (blk0 + bi * nt + i, 0)),
                   pl.BlockSpec((halo, LANES), lambda bi, i: (blk8 + bi * n8 + prev8(bi, i), 0)),
                   pl.BlockSpec((halo, LANES), lambda bi, i: (blk8 + bi * n8 + next8(bi, i), 0))]
        in_specs = x_specs + y_specs + [modspec] + tail_specs
        args = (x, x, x, y_rows, y_rows, y_rows, prev_mod) + tail_args
        out_specs, out_shape = [tok(d)] + z_specs, [jax.ShapeDtypeStruct((b, l, d), F32)] + z_shapes
    return pl.pallas_call(
        functools.partial(_inproj_kernel, widths=widths, with_residual=residual is not None, n_tiles=nt),
        grid=(b, nt), in_specs=in_specs, out_specs=out_specs, out_shape=out_shape,
        compiler_params=_cparams("parallel", "parallel"),
    )(*args)


def _rec_kernel(uf_ref, ub_ref, wg_ref, bg_ref, lam_ref, h0_ref, hf_ref, hb_ref, a_scr, b_scr, carry_scr, *, tc):
    i = pl.program_id(1)
    nq = uf_ref.shape[0]
    c = nq * LANES
    half = c // 2
    seg = tc // SUBLANES
    lanes = lambda q: slice(q * LANES, (q + 1) * LANES)

    @pl.when(i == 0)
    def _():
        carry_scr[...] = h0_ref[...]

    def coefficients(d, u_ref):
        u = jnp.concatenate([u_ref[q] for q in range(nq)], axis=-1)
        log_decay = -LRU_C * jax.nn.softplus(-lam_ref[d:d + 1, :])
        for hh in range(2):
            cols = slice(hh * half, (hh + 1) * half)
            uh = u[:, cols]
            pre = _dot(uh.astype(BF16), wg_ref[d, hh])
            r = _sigmoid_of_twice(pre[:, :half] + bg_ref[d, 0:1, cols])
            ig = _sigmoid_of_twice(pre[:, half:] + bg_ref[d, 1:2, cols])
            log_a = log_decay[:, cols] * r
            a = jnp.exp(log_a)
            th = jnp.tanh(log_a)
            one_minus_a2 = -2.0 * th / (1.0 - th)
            root = jnp.where(one_minus_a2 > 0.0, one_minus_a2 * lax.rsqrt(one_minus_a2), 0.0)
            bt = root * (ig * uh)
            for k in range(half // LANES):
                q = hh * (half // LANES) + k
                a_scr[d, q] = a[:, lanes(k)]
                b_scr[d, q] = bt[:, lanes(k)]

    coefficients(0, uf_ref)
    coefficients(1, ub_ref)

    def step(d, j, h, acc):
        rows8 = pl.ds(pl.multiple_of(j * SUBLANES, SUBLANES), SUBLANES)
        a8 = jnp.concatenate([a_scr[d, q, rows8, :] for q in range(nq)], axis=-1)
        b8 = jnp.concatenate([b_scr[d, q, rows8, :] for q in range(nq)], axis=-1)
        h = a8 * h + b8
        acc = a8 * acc
        for q in range(nq):
            b_scr[d, q, rows8, :] = h[:, lanes(q)]
            a_scr[d, q, rows8, :] = acc[:, lanes(q)]
        return h, acc

    def body(j, carry):
        hf, af, hb, ab = carry
        hf, af = step(0, j, hf, af)
        hb, ab = step(1, seg - 1 - j, hb, ab)
        return hf, af, hb, ab

    zero = jnp.zeros((SUBLANES, c), F32)
    one = jnp.ones((SUBLANES, c), F32)
    hf_end, af_end, hb_end, ab_end = lax.fori_loop(0, seg, body, (zero, one, zero, one), unroll=2)

    def chain(h_end, a_end, state, order):
        entering = [None] * SUBLANES
        for s in order:
            entering[s] = state
            state = a_end[s:s + 1, :] * state + h_end[s:s + 1, :]
        return jnp.concatenate(entering, axis=0), state

    cin_f, out_f = chain(hf_end, af_end, carry_scr[0:1, :], range(SUBLANES))
    cin_b, out_b = chain(hb_end, ab_end, carry_scr[1:2, :], reversed(range(SUBLANES)))
    carry_scr[0:1, :] = out_f
    carry_scr[1:2, :] = out_b

    for d, cin, out_ref in ((0, cin_f, hf_ref), (1, cin_b, hb_ref)):
        for q in range(nq):
            full_h = (b_scr[d, q].reshape(seg, SUBLANES, LANES)
                      + a_scr[d, q].reshape(seg, SUBLANES, LANES) * cin[None, :, lanes(q)])
            b_scr[d, q] = full_h.reshape(tc, LANES)
            for s in range(SUBLANES):
                out_ref[s * seg:(s + 1) * seg, lanes(q)] = (
                    b_scr[d, q, pl.ds(s, seg, stride=SUBLANES), :].astype(out_ref.dtype))


def _recurrent(u, wgate, bgate, lam, h0, out_dtype):
    b, nc, nq, tc, _ = u.shape
    c = nq * LANES
    u_fwd = pl.BlockSpec((None, None, nq, tc, LANES), lambda bi, i: (bi, i, 0, 0, 0))
    u_bwd = pl.BlockSpec((None, None, nq, tc, LANES), lambda bi, i: (bi, nc - 1 - i, 0, 0, 0))
    h_fwd = pl.BlockSpec((None, tc, c), lambda bi, i: (bi, i, 0))
    h_bwd = pl.BlockSpec((None, tc, c), lambda bi, i: (bi, nc - 1 - i, 0))
    full = lambda a: pl.BlockSpec(a.shape, lambda bi, i: (0,) * a.ndim)
    return pl.pallas_call(
        functools.partial(_rec_kernel, tc=tc),
        grid=(b, nc),
        in_specs=[u_fwd, u_bwd, full(wgate), full(bgate), full(lam),
                  pl.BlockSpec((None, 2, c), lambda bi, i: (bi, 0, 0))],
        out_specs=[h_fwd, h_bwd],
        out_shape=[jax.ShapeDtypeStruct((b, nc * tc, c), out_dtype)] * 2,
        scratch_shapes=[pltpu.VMEM((2, nq, tc, LANES), F32), pltpu.VMEM((2, nq, tc, LANES), F32),
                        pltpu.VMEM((2, c), F32)],
        compiler_params=_cparams("parallel", "arbitrary"),
    )(u, u, wgate, bgate, lam, h0)


def _dft_tables(n):
    j = lax.broadcasted_iota(jnp.int32, (n, n), 0)
    k = lax.broadcasted_iota(jnp.int32, (n, n), 1)
    ang = ((j * k) % n).astype(F32) * (2.0 * math.pi / n)
    return jnp.cos(ang), jnp.sin(ang)


def _block_diag(w):
    return jax.scipy.linalg.block_diag(*[w[h] for h in range(w.shape[0])])


def _channel_stage(gr, gi, c_ref, s_ref, w_ref, scale):
    f = (_dot(gr.astype(BF16), c_ref[...]) + _dot(gi.astype(BF16), s_ref[...])) * scale
    return _dot(f.astype(BF16), w_ref[...])


def _slabs_from(scr, val):
    for q in range(scr.shape[0]):
        scr[q] = val[:, q * LANES:(q + 1) * LANES]


def _slab_rows(scr, start, n):
    return jnp.concatenate([scr[q, pl.ds(start, n, stride=SUBLANES), :] for q in range(scr.shape[0])], axis=-1)


def _set_slab_rows(scr, start, val):
    for q in range(scr.shape[0]):
        scr[q, pl.ds(start, val.shape[0], stride=SUBLANES), :] = val[:, q * LANES:(q + 1) * LANES]


def _slabs_value(scr):
    return jnp.concatenate([scr[q] for q in range(scr.shape[0])], axis=-1)


def _four1_kernel(x_ref, f1_ref, twc_ref, tws_ref, ar_ref, ai_ref, xs, ars, ais, *, n1):
    shape = x_ref.shape
    _slabs_from(xs, x_ref[...].reshape(n1 * SUBLANES, shape[-1]))
    for jj in range(SUBLANES):
        a = _dot(f1_ref[...], _slab_rows(xs, jj, n1).astype(BF16))
        ar, ai = a[:n1], a[n1:]
        cc = twc_ref[:, jj:jj + 1]
        ss = tws_ref[:, jj:jj + 1]
        _set_slab_rows(ars, jj, ar * cc + ai * ss)
        _set_slab_rows(ais, jj, ai * cc - ar * ss)
    ar_ref[...] = _slabs_value(ars).reshape(shape)
    ai_ref[...] = _slabs_value(ais).reshape(shape)


def _four2_kernel(ar_ref, ai_ref, f2_ref, c_ref, s_ref, w_ref, o_ref, ys, *, n2, scale):
    grs, gis = [], []
    for kk in range(SUBLANES):
        slab = jnp.concatenate([ar_ref[kk], ai_ref[kk]], axis=0).astype(BF16)
        g = _dot(f2_ref[...], slab)
        grs.append(g[:n2])
        gis.append(g[n2:])
    y = _channel_stage(jnp.concatenate(grs, axis=0), jnp.concatenate(gis, axis=0), c_ref, s_ref, w_ref, scale)
    for kk in range(SUBLANES):
        _set_slab_rows(ys, kk, y[kk * n2:(kk + 1) * n2])
    o_ref[...] = _slabs_value(ys).reshape(o_ref.shape)


def _four_small_kernel(x_ref, f_ref, c_ref, s_ref, w_ref, o_ref, *, n, scale):
    g = _dot(f_ref[...], x_ref[...].astype(BF16))
    o_ref[...] = _channel_stage(g[:n], g[n:], c_ref, s_ref, w_ref, scale)


def _fourier(z_f, four_w_bd):
    b, l, nch = z_f.shape
    hd = nch // FOUR_HEADS
    scale = 1.0 / math.sqrt(l * hd)
    c_h, s_h = _dft_tables(hd)
    eye = jnp.eye(FOUR_HEADS, dtype=F32)
    c_bd = jnp.kron(eye, c_h).astype(BF16)
    s_bd = jnp.kron(eye, s_h).astype(BF16)
    full = lambda a, nd: pl.BlockSpec(a.shape, lambda *_: (0,) * a.ndim)
    if l <= 256:
        cn, sn = _dft_tables(l)
        f = jnp.concatenate([cn, -sn], axis=0).astype(BF16)
        return pl.pallas_call(
            functools.partial(_four_small_kernel, n=l, scale=scale),
            grid=(b,),
            in_specs=[pl.BlockSpec((None, l, nch), lambda bi: (bi, 0, 0)),
                      full(f, 1), full(c_bd, 1), full(s_bd, 1), full(four_w_bd, 1)],
            out_specs=pl.BlockSpec((None, l, nch), lambda bi: (bi, 0, 0)),
            out_shape=jax.ShapeDtypeStruct((b, l, nch), F32),
            compiler_params=_cparams("parallel"),
        )(z_f, f, c_bd, s_bd, four_w_bd)

    n2 = LANES
    n1 = l // n2
    nj = n2 // SUBLANES
    c1, s1 = _dft_tables(n1)
    f1 = jnp.concatenate([c1, -s1], axis=0).astype(BF16)
    c2, s2 = _dft_tables(n2)
    f2 = jnp.concatenate([jnp.concatenate([c2, s2], axis=1),
                          jnp.concatenate([-s2, c2], axis=1)], axis=0).astype(BF16)
    k1 = lax.broadcasted_iota(jnp.int32, (n1, n2), 0)
    t2 = lax.broadcasted_iota(jnp.int32, (n1, n2), 1)
    ang = (k1 * t2).astype(F32) * (2.0 * math.pi / l)
    twc = jnp.cos(ang).reshape(n1, nj, SUBLANES).transpose(1, 0, 2)
    tws = jnp.sin(ang).reshape(n1, nj, SUBLANES).transpose(1, 0, 2)
    ar, ai = pl.pallas_call(
        functools.partial(_four1_kernel, n1=n1),
        grid=(b, nj),
        in_specs=[pl.BlockSpec((None, n1, SUBLANES, nch), lambda bi, j: (bi, 0, j, 0)),
                  full(f1, 2),
                  pl.BlockSpec((None, n1, SUBLANES), lambda bi, j: (j, 0, 0)),
                  pl.BlockSpec((None, n1, SUBLANES), lambda bi, j: (j, 0, 0))],
        out_specs=[pl.BlockSpec((None, n1, SUBLANES, nch), lambda bi, j: (bi, 0, j, 0))] * 2,
        out_shape=[jax.ShapeDtypeStruct((b, n1, n2, nch), F32)] * 2,
        scratch_shapes=[pltpu.VMEM((nch // LANES, n1 * SUBLANES, LANES), F32)] * 3,
        compiler_params=_cparams("parallel", "parallel"),
    )(z_f.reshape(b, n1, n2, nch), f1, twc, tws)
    y = pl.pallas_call(
        functools.partial(_four2_kernel, n2=n2, scale=scale),
        grid=(b, n1 // SUBLANES),
        in_specs=[pl.BlockSpec((None, SUBLANES, n2, nch), lambda bi, j: (bi, j, 0, 0)),
                  pl.BlockSpec((None, SUBLANES, n2, nch), lambda bi, j: (bi, j, 0, 0)),
                  full(f2, 2), full(c_bd, 2), full(s_bd, 2), full(four_w_bd, 2)],
        out_specs=pl.BlockSpec((None, n2, SUBLANES, nch), lambda bi, j: (bi, 0, j, 0)),
        out_shape=jax.ShapeDtypeStruct((b, n2, n1, nch), F32),
        scratch_shapes=[pltpu.VMEM((nch // LANES, n2 * SUBLANES, LANES), F32)],
        compiler_params=_cparams("parallel", "parallel"),
    )(ar, ai, f2, c_bd, s_bd, four_w_bd)
    return y.reshape(b, l, nch)


def _window_counts(idx, n, w):
    return (jnp.minimum(idx + (w - w // 2), n) - jnp.maximum(idx - w // 2, 0)).astype(F32)


def _pool_kernel(cur_ref, prev_ref, next_ref, pw_ref, ps_ref, o_ref, *, r, w, rows_total, n_tiles):
    i = pl.program_id(1)
    two_d = rows_total > 1
    halo = SUBLANES if two_d else 0
    nr = r + 2 * halo
    stride = w + POOL_PAD
    flat = nr * stride
    lane = lax.broadcasted_iota(jnp.int32, (1, 1, LANES), 2)
    low = lane < (LANES // 2)
    col = lax.broadcasted_iota(jnp.int32, (1, w, 1), 1)
    grow = lax.broadcasted_iota(jnp.int32, (r, 1, 1), 0) + i * r
    outs = []
    for hh in range(2):
        cols = slice(hh * LANES, (hh + 1) * LANES)
        w_lo, w_hi = POOL_WINDOWS[2 * hh], POOL_WINDOWS[2 * hh + 1]
        cur = cur_ref[:, cols]
        if two_d:
            prev = jnp.where(i == 0, 0.0, prev_ref[:, cols])
            nxt = jnp.where(i == n_tiles - 1, 0.0, next_ref[:, cols])
            ext = jnp.concatenate([prev, cur, nxt], axis=0)
        else:
            ext = cur
        ext = ext.reshape(nr, w, LANES)
        x = jnp.concatenate([jnp.zeros((nr, POOL_PAD, LANES), F32), ext], axis=1).reshape(flat, LANES)
        p = x + pltpu.roll(x, 1, 0)
        sums = {2: p}
        for ww, s in ((4, 1), (8, 2), (16, 4)):
            if ww > w_hi:
                break
            p = pltpu.roll(p, s, 0) + pltpu.roll(p, flat - s, 0)
            sums[ww] = p
        y = jnp.where(low[0], sums[w_lo], sums[w_hi]).reshape(nr, stride, LANES)[:, POOL_PAD:, :]
        if two_d:
            q2 = y[0:nr - 1] + y[1:nr]
            rsum = {2: q2[7:7 + r]}
            q4 = q2[0:nr - 3] + q2[2:nr - 1]
            rsum[4] = q4[6:6 + r]
            if w_hi > 4:
                q8 = q4[0:nr - 7] + q4[4:nr - 3]
                rsum[8] = q8[4:4 + r]
                q16 = q8[0:nr - 15] + q8[8:nr - 7]
                rsum[16] = q16[0:r]
            tot = jnp.where(low, rsum[w_lo], rsum[w_hi])
            cnt_r = jnp.where(low, _window_counts(grow, rows_total, w_lo), _window_counts(grow, rows_total, w_hi))
            tot = tot / cnt_r
        else:
            tot = y
        cnt_c = jnp.where(low, _window_counts(col, w, w_lo), _window_counts(col, w, w_hi))
        pooled = (tot / cnt_c).reshape(r * w, LANES)
        outs.append(pooled - cur)
    d = jnp.concatenate(outs, axis=-1)
    o_ref[...] = (_dot(d.astype(BF16), pw_ref[...]) * ps_ref[...]).astype(o_ref.dtype)


def _pool(z_p, pool_w_bd, pool_scale, rows):
    b, l, nch = z_p.shape
    if rows is None:
        rows_total, w, r = 1, l, 1
    else:
        rows_total, w, r = rows, l // rows, min(32, rows)
    t = r * w
    n_tiles = l // t
    hb = SUBLANES * w if rows is not None else t
    per = t // hb
    nhb = l // hb
    return pl.pallas_call(
        functools.partial(_pool_kernel, r=r, w=w, rows_total=rows_total, n_tiles=n_tiles),
        grid=(b, n_tiles),
        in_specs=[pl.BlockSpec((None, t, nch), lambda bi, i: (bi, i, 0)),
                  pl.BlockSpec((None, hb, nch), lambda bi, i: (bi, jnp.maximum(i * per - 1, 0), 0)),
                  pl.BlockSpec((None, hb, nch), lambda bi, i: (bi, jnp.minimum((i + 1) * per, nhb - 1), 0)),
                  pl.BlockSpec(pool_w_bd.shape, lambda bi, i: (0, 0)),
                  pl.BlockSpec((1, nch), lambda bi, i: (0, 0))],
        out_specs=pl.BlockSpec((None, t, nch), lambda bi, i: (bi, i, 0)),
        out_shape=jax.ShapeDtypeStruct((b, l, nch), BF16),
        compiler_params=_cparams("parallel", "parallel"),
    )(z_p, z_p, z_p, pool_w_bd, pool_scale.reshape(1, nch))


def _first_argmax4(v):
    m = jnp.maximum(jnp.maximum(v[0], v[1]), jnp.maximum(v[2], v[3]))
    idx = jnp.where(v[0] >= m, 0, jnp.where(v[1] >= m, 1, jnp.where(v[2] >= m, 2, 3)))
    return m, idx


def _select4(idx, v):
    return jnp.where(idx == 0, v[0], jnp.where(idx == 1, v[1], jnp.where(idx == 2, v[2], v[3])))


def _router_logits(h_bf16, wr_ref, br_ref):
    return (_dot(h_bf16, wr_ref[...]) + br_ref[...]).T


def _group_logits(lt):
    return [lt[g:g + 1, :] for g in range(N_GROUPS)]


def _expert_gates(lt, group):
    lg = _group_logits(lt)
    m, _ = _first_argmax4(lg)
    den = sum(jnp.exp(v - m) for v in lg)
    p_group = jnp.exp(_select4(group, lg) - m) / den
    sel = [_select4(group, [lt[SUBLANES + EXPERTS_PER_GROUP * g + e:SUBLANES + EXPERTS_PER_GROUP * g + e + 1, :]
                            for g in range(N_GROUPS)]) for e in range(EXPERTS_PER_GROUP)]
    v1, e1 = _first_argmax4(sel)
    rest = [jnp.where(e1 == e, -jnp.inf, sel[e]) for e in range(EXPERTS_PER_GROUP)]
    v2, e2 = _first_argmax4(rest)
    t2 = jnp.exp(v2 - v1)
    w1 = p_group / (1.0 + t2)
    w2 = p_group * t2 / (1.0 + t2)
    return [jnp.where(e1 == e, w1, jnp.where(e2 == e, w2, 0.0)) for e in range(EXPERTS_PER_GROUP)]


def _combine_kernel(hf_ref, hb_ref, zg_ref, yp_ref, yf_ref, x_ref, mod_ref, wo_ref, g_ref, wr_ref, br_ref,
                    xo_ref, h2t_ref, gid_ref):
    y_a = ((hf_ref[...].astype(F32) + hb_ref[...].astype(F32))
           * jax.nn.gelu(zg_ref[...].astype(F32), approximate=True))
    ycat = jnp.concatenate([y_a.astype(BF16), yp_ref[...].astype(BF16), yf_ref[...].astype(BF16)], axis=-1)
    x_new = x_ref[...] + mod_ref[2:3, :] * _dot(ycat, wo_ref[...])
    xo_ref[...] = x_new
    h2 = _rms_mod(x_new, g_ref[...], mod_ref[3:4, :], mod_ref[4:5, :])
    _to_token_tiles(h2t_ref, h2)
    _, gidx = _first_argmax4(_group_logits(_router_logits(h2.astype(BF16), wr_ref, br_ref)))
    gid_ref[...] = gidx


def _combine(hf, hb, z_g, y_p, y_f, x, mod, w_out, g_ffn, w_router, b_router):
    b, l, d = x.shape
    assert d == SUBLANES * LANES, "a token tile is one (8, 128) f32 tile"
    t = min(1024, l)
    tok = lambda c: pl.BlockSpec((None, t, c), lambda bi, i: (bi, i, 0))
    full = lambda a: pl.BlockSpec(a.shape, lambda bi, i: (0,) * a.ndim)
    return pl.pallas_call(
        _combine_kernel,
        grid=(b, l // t),
        in_specs=[tok(hf.shape[-1]), tok(hb.shape[-1]), tok(z_g.shape[-1]), tok(y_p.shape[-1]), tok(y_f.shape[-1]),
                  tok(d), pl.BlockSpec((None, N_MOD, d), lambda bi, i: (bi, 0, 0)),
                  full(w_out), full(g_ffn), full(w_router), full(b_router)],
        out_specs=[tok(d), pl.BlockSpec((None, t * SUBLANES, LANES), lambda bi, i: (bi, i, 0)),
                   pl.BlockSpec((None, 1, t), lambda bi, i: (bi, 0, i))],
        out_shape=[jax.ShapeDtypeStruct((b, l, d), F32), jax.ShapeDtypeStruct((b, l * SUBLANES, LANES), F32),
                   jax.ShapeDtypeStruct((b, 1, l), jnp.int32)],
        compiler_params=_cparams("parallel", "parallel"),
    )(hf, hb, z_g, y_p, y_f, x, mod, w_out, g_ffn, w_router, b_router)


MOE_TILE = 512


def _sort_plan(gid, n_groups, t):
    n = gid.shape[0]
    onehot = (gid[:, None] == jnp.arange(n_groups, dtype=jnp.int32)[None, :]).astype(jnp.int32)
    csum = jnp.cumsum(onehot, axis=0)
    rank = jnp.sum((csum - onehot) * onehot, axis=1)
    tiles = (csum[-1] + t - 1) // t
    tile_end = jnp.cumsum(tiles)
    slot = jnp.sum(onehot * (tile_end - tiles)[None, :], axis=1) * t + rank
    n_tiles = n // t + n_groups
    tile_idx = jnp.arange(n_tiles, dtype=jnp.int32)
    tile_group = jnp.minimum(jnp.sum((tile_idx[:, None] >= tile_end[None, :]).astype(jnp.int32), axis=1), n_groups - 1)
    pad_lo = jnp.concatenate([(tile_end - tiles) * t + csum[-1], tile_end[-1:] * t]).astype(jnp.int32)
    pad_hi = jnp.concatenate([tile_end * t, jnp.full((1,), n_tiles * t, jnp.int32)]).astype(jnp.int32)
    return (slot.astype(jnp.int32), tile_group.astype(jnp.int32), tile_end[-1:].astype(jnp.int32), pad_lo, pad_hi,
            n_tiles)


def _invert_kernel(slot_ref, pad_lo_ref, pad_hi_ref, src_ref):
    def clear(s, carry):
        src_ref[s] = -1
        return carry

    for k in range(pad_lo_ref.shape[0]):
        lax.fori_loop(pad_lo_ref[k], pad_hi_ref[k], clear, 0)

    def put(tok, carry):
        src_ref[slot_ref[tok]] = tok
        return carry

    lax.fori_loop(0, slot_ref.shape[0], put, 0, unroll=16)


def _invert(slot, pad_lo, pad_hi, n_rows):
    smem = pl.BlockSpec(memory_space=pltpu.SMEM)
    return pl.pallas_call(
        _invert_kernel,
        in_specs=[smem, smem, smem],
        out_specs=smem,
        out_shape=jax.ShapeDtypeStruct((n_rows,), jnp.int32),
    )(slot, pad_lo, pad_hi)


def _moe_fused_kernel(tg_ref, nu_ref, nv_ref, src_ref, dst_ref, h_ref, wr_ref, br_ref, wg_ref, wu_ref, wd_ref, y_ref,
                      rows, ybuf, hb_scr, gt_scr, gsem, ssem, *, t, epg):
    i = pl.program_id(0)
    nu = nu_ref[0]
    s = lax.rem(i, 2)
    tile_rows = t * SUBLANES

    def token(ref, first_row):
        return ref.at[pl.ds(pl.multiple_of(first_row, SUBLANES), SUBLANES)]

    def gather_rows(tile, slot, lo, hi):
        def body(g, carry):
            for k in range(SUBLANES):
                r = g * SUBLANES + k
                pltpu.make_async_copy(token(h_ref, src_ref[tile * t + r]), token(rows.at[slot], r * SUBLANES),
                                      gsem.at[slot]).start(priority=k % 2)
            return carry
        lax.fori_loop(lo // SUBLANES, hi // SUBLANES, body, 0)

    def scatter_rows(tile, slot, lo, hi):
        def body(g, carry):
            for k in range(SUBLANES):
                r = g * SUBLANES + k
                pltpu.make_async_copy(token(ybuf.at[slot], r * SUBLANES), token(y_ref, dst_ref[tile * t + r]),
                                      ssem.at[slot]).start(priority=k % 2)
            return carry
        lax.fori_loop(lo // SUBLANES, hi // SUBLANES, body, 0)

    def wait_gather(slot):
        pltpu.make_async_copy(h_ref.at[pl.ds(0, tile_rows)], rows.at[slot], gsem.at[slot]).wait()

    def wait_scatter(slot):
        pltpu.make_async_copy(ybuf.at[slot], y_ref.at[pl.ds(0, tile_rows)], ssem.at[slot]).wait()

    @pl.when(i == 0)
    def _():
        gt_scr[...] = jnp.zeros_like(gt_scr)
        gather_rows(0, 0, 0, t)

    @pl.when(jnp.logical_and(i >= 2, i - 2 < nu))
    def _():
        wait_scatter(s)

    @pl.when(i < nu)
    def _():
        wait_gather(s)
        hb_scr[...] = _from_token_tiles(rows, (s,)).astype(BF16)
        tok_id = lax.broadcasted_iota(jnp.int32, (1, t), 1)
        gate_rows = _expert_gates(_router_logits(hb_scr[...], wr_ref, br_ref), tg_ref[i])
        for e in range(epg):
            gt_scr[e:e + 1, :] = jnp.where(tok_id < nv_ref[i], gate_rows[e], 0.0)
        gates = gt_scr[...].T
        @pl.when(i + 1 < nu)
        def _():
            gather_rows(i + 1, 1 - s, 0, t)

        @pl.when(i >= 1)
        def _():
            scatter_rows(i - 1, 1 - s, 0, t)

        acc = None
        for e in range(epg):
            hid = jax.nn.silu(_dot(hb_scr[...], wg_ref[e])) * _dot(hb_scr[...], wu_ref[e])
            y = gates[:, e:e + 1] * _dot(hid.astype(BF16), wd_ref[e])
            acc = y if acc is None else acc + y
        _to_token_tiles(ybuf, acc, (s,))

    @pl.when(i == nu)
    def _():
        scatter_rows(i - 1, 1 - s, 0, t)

    @pl.when(jnp.logical_and(i > nu, i < pl.num_programs(0) - 1))
    def _():
        rows[0] = jnp.zeros(rows.shape[1:], F32)
        first = pl.multiple_of(dst_ref[(i - 1) * t], SUBLANES)
        fill = pltpu.make_async_copy(rows.at[0], y_ref.at[pl.ds(first, tile_rows)], gsem.at[0])
        fill.start()
        fill.wait()


def _moe_fused(tile_group, n_used, n_valid, src, dst, h2t, w_router, b_router, wg, wu, wd, group0, n_tiles, t):
    epg = wg.shape[1]
    last = n_tiles - 1
    d = SUBLANES * LANES
    tile_rows = t * SUBLANES
    full = lambda a: pl.BlockSpec(a.shape, lambda i, *prefetch: (0,) * a.ndim)
    by_group = lambda a: pl.BlockSpec((None,) + a.shape[1:],
                                      lambda i, tg, *rest: (group0 + tg[jnp.minimum(i, last)], 0, 0, 0))
    return pl.pallas_call(
        functools.partial(_moe_fused_kernel, t=t, epg=epg),
        grid_spec=pltpu.PrefetchScalarGridSpec(
            num_scalar_prefetch=5, grid=(n_tiles + 2,),
            in_specs=[pl.BlockSpec(memory_space=pl.ANY), full(w_router), full(b_router),
                      by_group(wg), by_group(wu), by_group(wd)],
            out_specs=pl.BlockSpec(memory_space=pl.ANY),
            scratch_shapes=[pltpu.VMEM((2, tile_rows, LANES), F32), pltpu.VMEM((2, tile_rows, LANES), F32),
                            pltpu.VMEM((t, d), BF16), pltpu.VMEM((LANES, t), F32),
                            pltpu.SemaphoreType.DMA((2,)), pltpu.SemaphoreType.DMA((2,))]),
        out_shape=jax.ShapeDtypeStruct((n_tiles * tile_rows, LANES), F32),
        compiler_params=_cparams("arbitrary"),
    )(tile_group, n_used, n_valid, src, dst, h2t, w_router, b_router, wg, wu, wd)


def _final_kernel(x_ref, y_ref, mod_ref, g_ref, o_ref):
    out = x_ref[...] + mod_ref[5:6, :] * _from_token_tiles(y_ref)
    ms = jnp.mean(out * out, axis=-1, keepdims=True)
    o_ref[...] = out * lax.rsqrt(ms + EPS) * g_ref[...]


def _final(x, y_rows, mod, g_final):
    b, l, d = x.shape
    t = min(1024, l)
    nt = l // t
    return pl.pallas_call(
        _final_kernel,
        grid=(b, nt),
        in_specs=[pl.BlockSpec((None, t, d), lambda bi, i: (bi, i, 0)),
                  pl.BlockSpec((t * SUBLANES, LANES), lambda bi, i: (bi * nt + i, 0)),
                  pl.BlockSpec((None, N_MOD, d), lambda bi, i: (bi, 0, 0)),
                  pl.BlockSpec((1, d), lambda bi, i: (0, 0))],
        out_specs=pl.BlockSpec((None, t, d), lambda bi, i: (bi, i, 0)),
        out_shape=jax.ShapeDtypeStruct((b, l, d), F32),
        compiler_params=_cparams("parallel", "parallel"),
    )(x, y_rows, mod, g_final)


def kernel(x, c, ctx, c_ctx, mod_w, mod_b, norm_mix_g, norm_ffn_g, w_in, conv_w, conv_b, rec_gate_a_w, rec_gate_a_b,
           rec_gate_x_w, rec_gate_x_b, rec_lambda, pool_w, pool_scale, fourier_w, w_out, router_group_w,
           router_group_b, router_expert_w, router_expert_b, expert_w_gate, expert_w_up, expert_w_down, final_norm_g):
    b, l, d = x.shape
    depth = mod_w.shape[0]
    d_rec = conv_w.shape[-1]
    d_pool = pool_scale.shape[-1]
    d_four = fourier_w.shape[1] * fourier_w.shape[2]
    widths = (d_rec, d_rec, d_pool, d_four)
    rows = l // GRID_W
    heads_half = REC_HEADS // 2
    n_exp = expert_w_gate.shape[1]

    cvec = jnp.concatenate([c, c_ctx[None, :], jnp.zeros((SUBLANES - b - 1, d), F32)], axis=0)
    mod_all = _modulation(cvec, mod_w, mod_b).reshape(depth, SUBLANES, N_MOD, d)
    g_final = final_norm_g.reshape(1, d)
    zeros_state = jnp.zeros((b, 2, d_rec), F32)
    pending = pending_ctx = None
    by_group = lambda w: w.astype(BF16).reshape((depth * N_GROUPS, EXPERTS_PER_GROUP) + w.shape[2:])
    expert_w = (by_group(expert_w_gate), by_group(expert_w_up), by_group(expert_w_down))

    for li in range(depth):
        last = li == depth - 1
        mod_lat = mod_all[li, :b]
        mod_ctx = jnp.broadcast_to(mod_all[li, b][None], (b, N_MOD, d))
        w_in_l = w_in[li].astype(BF16)
        w_out_l = w_out[li].astype(BF16)
        wgate = (0.5 * jnp.stack([
            jnp.stack([jnp.concatenate([_block_diag(rec_gate_a_w[li, dd, hh * heads_half:(hh + 1) * heads_half]),
                                        _block_diag(rec_gate_x_w[li, dd, hh * heads_half:(hh + 1) * heads_half])], axis=1)
                       for hh in range(2)]) for dd in range(2)])).astype(BF16)
        bgate = 0.5 * jnp.stack([rec_gate_a_b[li], rec_gate_x_b[li]], axis=1)
        rec_p = (wgate, bgate, rec_lambda[li])
        proj_p = (w_in_l, conv_w[li], conv_b[li].reshape(1, d_rec), widths)
        pool_w_bd = _block_diag(pool_w[li]).astype(BF16)
        four_w_bd = _block_diag(fourier_w[li]).astype(BF16)
        w_router = jnp.concatenate([router_group_w[li], jnp.zeros((d, SUBLANES - N_GROUPS), F32), router_expert_w[li],
                                    jnp.zeros((d, LANES - SUBLANES - n_exp), F32)], axis=1).astype(BF16)
        b_router = jnp.concatenate([router_group_b[li], jnp.zeros((SUBLANES - N_GROUPS,), F32), router_expert_b[li],
                                    jnp.zeros((LANES - SUBLANES - n_exp,), F32)]).reshape(1, LANES)
        g_ffn = norm_ffn_g[li].reshape(1, d)

        def mixer_tail(hf, hb, z_g, z_p, z_f, stream, mod, grid_rows):
            y_p = _pool(z_p, pool_w_bd, pool_scale[li], grid_rows)
            y_f = _fourier(z_f, four_w_bd)
            return _combine(hf, hb, z_g, y_p, y_f, stream, mod, w_out_l, g_ffn, w_router, b_router)

        def expert_stage(h2t, gid):
            n_tok = gid.size
            t_moe = min(MOE_TILE, h2t.shape[1] // SUBLANES)
            slot, tile_group, n_used, pad_lo, pad_hi, n_tiles = _sort_plan(gid.reshape(n_tok), N_GROUPS, t_moe)
            src = _invert(slot, pad_lo, pad_hi, n_tiles * t_moe)
            valid = src >= 0
            pad_rank = jnp.cumsum(jnp.logical_not(valid).astype(jnp.int32)) - 1
            dst = jnp.where(valid, src, n_tok + pad_rank)
            n_valid = jnp.sum(valid.reshape(n_tiles, t_moe).astype(jnp.int32), axis=1)
            return _moe_fused(tile_group, n_used, n_valid, jnp.maximum(src, 0) * SUBLANES, dst * SUBLANES,
                              h2t.reshape(n_tok * SUBLANES, LANES),
                              w_router, b_router, *expert_w, li * N_GROUPS, n_tiles, t_moe)

        if pending_ctx is None:
            uc, zc_g, zc_p, zc_f = _inproj(ctx, norm_mix_g[li], mod_ctx, *proj_p)
        else:
            ctx, uc, zc_g, zc_p, zc_f = _inproj(ctx, norm_mix_g[li], mod_ctx, *proj_p, residual=pending_ctx)
        hf_c, hb_c = _recurrent(uc, *rec_p, zeros_state, F32)
        state = jnp.stack([hf_c[:, -1, :], hb_c[:, 0, :]], axis=1)
        if not last:
            ctx, h2t_c, gid_c = mixer_tail(hf_c, hb_c, zc_g, zc_p, zc_f, ctx, mod_ctx, None)
            pending_ctx = (expert_stage(h2t_c, gid_c), 0, mod_ctx)

        if pending is None:
            u, z_g, z_p, z_f = _inproj(x, norm_mix_g[li], mod_lat, *proj_p)
        else:
            x, u, z_g, z_p, z_f = _inproj(x, norm_mix_g[li], mod_lat, *proj_p, residual=pending)
        hf, hb = _recurrent(u, *rec_p, state, BF16)
        x, h2t, gid = mixer_tail(hf, hb, z_g, z_p, z_f, x, mod_lat, rows)
        pending = (expert_stage(h2t, gid), 0, mod_lat)
    return _final(x, pending[0], pending[2], g_final)
```

```python
import functools
import math

import jax
import jax.numpy as jnp
from jax import lax
from jax.experimental import pallas as pl
from jax.experimental.pallas import tpu as pltpu

F32 = jnp.float32
BF16 = jnp.bfloat16

GRID_W = 64
N_MOD = 6
REC_HEADS = 8
CONV_W = 4
LRU_C = 8.0
POOL_WINDOWS = (2, 4, 8, 16)
FOUR_HEADS = 4
N_GROUPS = 4
EXPERTS_PER_GROUP = 4
EPS = 1e-6

LANES = 128
SUBLANES = 8
POOL_PAD = 8
VMEM_LIMIT = 56 * 1024 * 1024


def _cparams(*sem):
    return pltpu.CompilerParams(dimension_semantics=sem, vmem_limit_bytes=VMEM_LIMIT)


def _split_bf16(a):
    hi = a.astype(BF16)
    lo = (a - hi.astype(F32)).astype(BF16)
    return hi, lo


def _dot(a, b):
    return jnp.dot(a, b, preferred_element_type=F32)


def _sigmoid_of_twice(half_x):
    return 0.5 * jnp.tanh(half_x) + 0.5


def _mod_kernel(s_ref, w_ref, b_ref, o_ref):
    s = s_ref[...]
    s = s * jax.nn.sigmoid(s)
    s_hi, s_lo = _split_bf16(s)
    w_hi, w_lo = _split_bf16(w_ref[...])
    o_ref[...] = _dot(s_hi, w_hi) + _dot(s_hi, w_lo) + _dot(s_lo, w_hi) + b_ref[...]


def _modulation(cvec, mod_w, mod_b):
    depth, d, dm = mod_w.shape
    tn = dm // 4
    return pl.pallas_call(
        _mod_kernel,
        grid=(depth, dm // tn),
        in_specs=[pl.BlockSpec((SUBLANES, d), lambda l, j: (0, 0)),
                  pl.BlockSpec((None, d, tn), lambda l, j: (l, 0, j)),
                  pl.BlockSpec((None, 1, tn), lambda l, j: (l, 0, j))],
        out_specs=pl.BlockSpec((None, SUBLANES, tn), lambda l, j: (l, 0, j)),
        out_shape=jax.ShapeDtypeStruct((depth, SUBLANES, dm), F32),
        compiler_params=_cparams("parallel", "parallel"),
    )(cvec, mod_w, mod_b.reshape(depth, 1, dm))


def _rms_mod(x, g, shift, scale):
    ms = jnp.mean(x * x, axis=-1, keepdims=True)
    y = x * lax.rsqrt(ms + EPS) * g
    return y * (1.0 + scale) + shift


def _from_token_tiles(ref, lead=()):
    n = ref.shape[-2] // SUBLANES
    return jnp.concatenate([ref[lead + (pl.ds(j, n, stride=SUBLANES), slice(None))] for j in range(SUBLANES)], axis=-1)


def _to_token_tiles(ref, val, lead=()):
    n = val.shape[0]
    for j in range(SUBLANES):
        ref[lead + (pl.ds(j, n, stride=SUBLANES), slice(None))] = val[:, j * LANES:(j + 1) * LANES]


def _inproj_kernel(*refs, widths, with_residual, n_tiles):
    i = pl.program_id(1)
    if with_residual:
        (x_ref, xp_ref, xn_ref, y_ref, yp_ref, yn_ref, pmod_ref,
         g_ref, mod_ref, w_ref, cw_ref, cb_ref, xo_ref, *o_refs) = refs
        gate = pmod_ref[5:6, :]
        x = x_ref[...] + gate * _from_token_tiles(y_ref)
        xo_ref[...] = x
        x_halo = jnp.concatenate([xp_ref[...] + gate * _from_token_tiles(yp_ref),
                                  xn_ref[...] + gate * _from_token_tiles(yn_ref)], axis=0)
    else:
        x_ref, xp_ref, xn_ref, g_ref, mod_ref, w_ref, cw_ref, cb_ref, *o_refs = refs
        x = x_ref[...]
        x_halo = jnp.concatenate([xp_ref[...], xn_ref[...]], axis=0)
    t = x.shape[0]
    d_rec = widths[0]
    z = _dot(_rms_mod(x, g_ref[...], mod_ref[0:1, :], mod_ref[1:2, :]).astype(BF16), w_ref[...])
    zh = _dot(_rms_mod(x_halo, g_ref[...], mod_ref[0:1, :], mod_ref[1:2, :]).astype(BF16), w_ref[:, :d_rec])
    zz = jnp.concatenate([jnp.where(i == 0, 0.0, zh[:SUBLANES]), z[:, :d_rec],
                          jnp.where(i == n_tiles - 1, 0.0, zh[SUBLANES:])], axis=0)
    grp = t // SUBLANES
    zz3 = zz.reshape(grp + 2, SUBLANES, d_rec)
    sub_row = lax.broadcasted_iota(jnp.int32, (1, SUBLANES, d_rec), 1)

    def shifted(off):
        if off == 0:
            return zz3[1:grp + 1]
        rolled = pltpu.roll(zz3, (-off) % SUBLANES, 1)
        if off < 0:
            return jnp.where(sub_row >= -off, rolled[1:grp + 1], rolled[0:grp])
        return jnp.where(sub_row < SUBLANES - off, rolled[1:grp + 1], rolled[2:grp + 2])

    u = cb_ref[...].reshape(1, 1, d_rec)
    for k in range(CONV_W):
        u = u + cw_ref[k:k + 1, :].reshape(1, 1, d_rec) * shifted(k - CONV_W // 2)
    u = u.reshape(t, d_rec)
    seg = t // SUBLANES
    for q in range(d_rec // LANES):
        for s in range(SUBLANES):
            o_refs[0][q, pl.ds(s, seg, stride=SUBLANES), :] = u[s * seg:(s + 1) * seg, q * LANES:(q + 1) * LANES]
    off = d_rec
    for o_ref, wd in zip(o_refs[1:], widths[1:]):
        o_ref[...] = z[:, off:off + wd].astype(o_ref.dtype)
        off += wd


def _inproj(x, g, mod, w_bf16, conv_w, conv_b, widths, residual=None):
    b, l, d = x.shape
    t = min(512, l)
    nt = l // t
    per = t // SUBLANES
    n8 = l // SUBLANES
    tok = lambda c: pl.BlockSpec((None, t, c), lambda bi, i: (bi, i, 0))
    prev8 = lambda bi, i: jnp.maximum(i * per - 1, 0)
    next8 = lambda bi, i: jnp.minimum((i + 1) * per, n8 - 1)
    x_specs = [tok(d), pl.BlockSpec((None, SUBLANES, d), lambda bi, i: (bi, prev8(bi, i), 0)),
               pl.BlockSpec((None, SUBLANES, d), lambda bi, i: (bi, next8(bi, i), 0))]
    modspec = pl.BlockSpec((None, N_MOD, d), lambda bi, i: (bi, 0, 0))
    full = lambda a: pl.BlockSpec(a.shape, lambda bi, i: (0,) * a.ndim)
    tail_specs = [pl.BlockSpec((1, d), lambda bi, i: (0, 0)), modspec, full(w_bf16), full(conv_w), full(conv_b)]
    tail_args = (g.reshape(1, d), mod, w_bf16, conv_w, conv_b)
    slabs = widths[0] // LANES
    z_specs = ([pl.BlockSpec((None, None, slabs, t, LANES), lambda bi, i: (bi, i, 0, 0, 0))]
               + [tok(wd) for wd in widths[1:]])
    z_shapes = ([jax.ShapeDtypeStruct((b, nt, slabs, t, LANES), F32)]
                + [jax.ShapeDtypeStruct((b, l, wd), BF16 if k == 0 else F32) for k, wd in enumerate(widths[1:])])
    if residual is None:
        in_specs, args, out_specs, out_shape = x_specs + tail_specs, (x, x, x) + tail_args, z_specs, z_shapes
    else:
        y_rows, row0, prev_mod = residual
        blk0, blk8 = row0 // t, row0 // SUBLANES
        halo = SUBLANES * SUBLANES
        y_specs = [pl.BlockSpec((t * SUBLANES, LANES), lambda bi, i: (blk0 + bi * nt + i, 0)),
                   pl.BlockSpec((halo, LANES), lambda bi, i: (blk8 + bi * n8 + prev8(bi, i), 0)),
                   pl.BlockSpec((halo, LANES), lambda bi, i: (blk8 + bi * n8 + next8(bi, i), 0))]
        in_specs = x_specs + y_specs + [modspec] + tail_specs
        args = (x, x, x, y_rows, y_rows, y_rows, prev_mod) + tail_args
        out_specs, out_shape = [tok(d)] + z_specs, [jax.ShapeDtypeStruct((b, l, d), F32)] + z_shapes
    return pl.pallas_call(
        functools.partial(_inproj_kernel, widths=widths, with_residual=residual is not None, n_tiles=nt),
        grid=(b, nt), in_specs=in_specs, out_specs=out_specs, out_shape=out_shape,
        compiler_params=_cparams("parallel", "parallel"),
    )(*args)


def _rec_kernel(uf_ref, ub_ref, wg_ref, bg_ref, lam_ref, h0_ref, hf_ref, hb_ref, a_scr, b_scr, carry_scr, *, tc):
    i = pl.program_id(1)
    nq = uf_ref.shape[0]
    c = nq * LANES
    half = c // 2
    seg = tc // SUBLANES
    lanes = lambda q: slice(q * LANES, (q + 1) * LANES)

    @pl.when(i == 0)
    def _():
        carry_scr[...] = h0_ref[...]

    def coefficients(d, u_ref):
        u = jnp.concatenate([u_ref[q] for q in range(nq)], axis=-1)
        log_decay = -LRU_C * jax.nn.softplus(-lam_ref[d:d + 1, :])
        for hh in range(2):
            cols = slice(hh * half, (hh + 1) * half)
            uh = u[:, cols]
            pre = _dot(uh.astype(BF16), wg_ref[d, hh])
            r = _sigmoid_of_twice(pre[:, :half] + bg_ref[d, 0:1, cols])
            ig = _sigmoid_of_twice(pre[:, half:] + bg_ref[d, 1:2, cols])
            log_a = log_decay[:, cols] * r
            a = jnp.exp(log_a)
            th = jnp.tanh(log_a)
            one_minus_a2 = -2.0 * th / (1.0 - th)
            root = jnp.where(one_minus_a2 > 0.0, one_minus_a2 * lax.rsqrt(one_minus_a2), 0.0)
            bt = root * (ig * uh)
            for k in range(half // LANES):
                q = hh * (half // LANES) + k
                a_scr[d, q] = a[:, lanes(k)]
                b_scr[d, q] = bt[:, lanes(k)]

    coefficients(0, uf_ref)
    coefficients(1, ub_ref)

    def step(d, j, h, acc):
        rows8 = pl.ds(pl.multiple_of(j * SUBLANES, SUBLANES), SUBLANES)
        a8 = jnp.concatenate([a_scr[d, q, rows8, :] for q in range(nq)], axis=-1)
        b8 = jnp.concatenate([b_scr[d, q, rows8, :] for q in range(nq)], axis=-1)
        h = a8 * h + b8
        acc = a8 * acc
        for q in range(nq):
            b_scr[d, q, rows8, :] = h[:, lanes(q)]
            a_scr[d, q, rows8, :] = acc[:, lanes(q)]
        return h, acc

    def body(j, carry):
        hf, af, hb, ab = carry
        hf, af = step(0, j, hf, af)
        hb, ab = step(1, seg - 1 - j, hb, ab)
        return hf, af, hb, ab

    zero = jnp.zeros((SUBLANES, c), F32)
    one = jnp.ones((SUBLANES, c), F32)
    hf_end, af_end, hb_end, ab_end = lax.fori_loop(0, seg, body, (zero, one, zero, one), unroll=2)

    def chain(h_end, a_end, state, order):
        entering = [None] * SUBLANES
        for s in order:
            entering[s] = state
            state = a_end[s:s + 1, :] * state + h_end[s:s + 1, :]
        return jnp.concatenate(entering, axis=0), state

    cin_f, out_f = chain(hf_end, af_end, carry_scr[0:1, :], range(SUBLANES))
    cin_b, out_b = chain(hb_end, ab_end, carry_scr[1:2, :], reversed(range(SUBLANES)))
    carry_scr[0:1, :] = out_f
    carry_scr[1:2, :] = out_b

    for d, cin, out_ref in ((0, cin_f, hf_ref), (1, cin_b, hb_ref)):
        for q in range(nq):
            full_h = (b_scr[d, q].reshape(seg, SUBLANES, LANES)
                      + a_scr[d, q].reshape(seg, SUBLANES, LANES) * cin[None, :, lanes(q)])
            b_scr[d, q] = full_h.reshape(tc, LANES)
            for s in range(SUBLANES):
                out_ref[s * seg:(s + 1) * seg, lanes(q)] = (
                    b_scr[d, q, pl.ds(s, seg, stride=SUBLANES), :].astype(out_ref.dtype))


def _recurrent(u, wgate, bgate, lam, h0, out_dtype):
    b, nc, nq, tc, _ = u.shape
    c = nq * LANES
    u_fwd = pl.BlockSpec((None, None, nq, tc, LANES), lambda bi, i: (bi, i, 0, 0, 0))
    u_bwd = pl.BlockSpec((None, None, nq, tc, LANES), lambda bi, i: (bi, nc - 1 - i, 0, 0, 0))
    h_fwd = pl.BlockSpec((None, tc, c), lambda bi, i: (bi, i, 0))
    h_bwd = pl.BlockSpec((None, tc, c), lambda bi, i: (bi, nc - 1 - i, 0))
    full = lambda a: pl.BlockSpec(a.shape, lambda bi, i: (0,) * a.ndim)
    return pl.pallas_call(
        functools.partial(_rec_kernel, tc=tc),
        grid=(b, nc),
        in_specs=[u_fwd, u_bwd, full(wgate), full(bgate), full(lam),
                  pl.BlockSpec((None, 2, c), lambda bi, i: (bi, 0, 0))],
        out_specs=[h_fwd, h_bwd],
        out_shape=[jax.ShapeDtypeStruct((b, nc * tc, c), out_dtype)] * 2,
        scratch_shapes=[pltpu.VMEM((2, nq, tc, LANES), F32), pltpu.VMEM((2, nq, tc, LANES), F32),
                        pltpu.VMEM((2, c), F32)],
        compiler_params=_cparams("parallel", "arbitrary"),
    )(u, u, wgate, bgate, lam, h0)


def _dft_tables(n):
    j = lax.broadcasted_iota(jnp.int32, (n, n), 0)
    k = lax.broadcasted_iota(jnp.int32, (n, n), 1)
    ang = ((j * k) % n).astype(F32) * (2.0 * math.pi / n)
    return jnp.cos(ang), jnp.sin(ang)


def _block_diag(w):
    return jax.scipy.linalg.block_diag(*[w[h] for h in range(w.shape[0])])


def _channel_stage(gr, gi, c_ref, s_ref, w_ref, scale):
    f = (_dot(gr.astype(BF16), c_ref[...]) + _dot(gi.astype(BF16), s_ref[...])) * scale
    return _dot(f.astype(BF16), w_ref[...])


def _slabs_from(scr, val):
    for q in range(scr.shape[0]):
        scr[q] = val[:, q * LANES:(q + 1) * LANES]


def _slab_rows(scr, start, n):
    return jnp.concatenate([scr[q, pl.ds(start, n, stride=SUBLANES), :] for q in range(scr.shape[0])], axis=-1)


def _set_slab_rows(scr, start, val):
    for q in range(scr.shape[0]):
        scr[q, pl.ds(start, val.shape[0], stride=SUBLANES), :] = val[:, q * LANES:(q + 1) * LANES]


def _slabs_value(scr):
    return jnp.concatenate([scr[q] for q in range(scr.shape[0])], axis=-1)


def _four1_kernel(x_ref, f1_ref, twc_ref, tws_ref, ar_ref, ai_ref, xs, ars, ais, *, n1):
    shape = x_ref.shape
    _slabs_from(xs, x_ref[...].reshape(n1 * SUBLANES, shape[-1]))
    for jj in range(SUBLANES):
        a = _dot(f1_ref[...], _slab_rows(xs, jj, n1).astype(BF16))
        ar, ai = a[:n1], a[n1:]
        cc = twc_ref[:, jj:jj + 1]
        ss = tws_ref[:, jj:jj + 1]
        _set_slab_rows(ars, jj, ar * cc + ai * ss)
        _set_slab_rows(ais, jj, ai * cc - ar * ss)
    ar_ref[...] = _slabs_value(ars).reshape(shape)
    ai_ref[...] = _slabs_value(ais).reshape(shape)


def _four2_kernel(ar_ref, ai_ref, f2_ref, c_ref, s_ref, w_ref, o_ref, ys, *, n2, scale):
    grs, gis = [], []
    for kk in range(SUBLANES):
        slab = jnp.concatenate([ar_ref[kk], ai_ref[kk]], axis=0).astype(BF16)
        g = _dot(f2_ref[...], slab)
        grs.append(g[:n2])
        gis.append(g[n2:])
    y = _channel_stage(jnp.concatenate(grs, axis=0), jnp.concatenate(gis, axis=0), c_ref, s_ref, w_ref, scale)
    for kk in range(SUBLANES):
        _set_slab_rows(ys, kk, y[kk * n2:(kk + 1) * n2])
    o_ref[...] = _slabs_value(ys).reshape(o_ref.shape)


def _four_small_kernel(x_ref, f_ref, c_ref, s_ref, w_ref, o_ref, *, n, scale):
    g = _dot(f_ref[...], x_ref[...].astype(BF16))
    o_ref[...] = _channel_stage(g[:n], g[n:], c_ref, s_ref, w_ref, scale)


def _fourier(z_f, four_w_bd):
    b, l, nch = z_f.shape
    hd = nch // FOUR_HEADS
    scale = 1.0 / math.sqrt(l * hd)
    c_h, s_h = _dft_tables(hd)
    eye = jnp.eye(FOUR_HEADS, dtype=F32)
    c_bd = jnp.kron(eye, c_h).astype(BF16)
    s_bd = jnp.kron(eye, s_h).astype(BF16)
    full = lambda a, nd: pl.BlockSpec(a.shape, lambda *_: (0,) * a.ndim)
    if l <= 256:
        cn, sn = _dft_tables(l)
        f = jnp.concatenate([cn, -sn], axis=0).astype(BF16)
        return pl.pallas_call(
            functools.partial(_four_small_kernel, n=l, scale=scale),
            grid=(b,),
            in_specs=[pl.BlockSpec((None, l, nch), lambda bi: (bi, 0, 0)),
                      full(f, 1), full(c_bd, 1), full(s_bd, 1), full(four_w_bd, 1)],
            out_specs=pl.BlockSpec((None, l, nch), lambda bi: (bi, 0, 0)),
            out_shape=jax.ShapeDtypeStruct((b, l, nch), F32),
            compiler_params=_cparams("parallel"),
        )(z_f, f, c_bd, s_bd, four_w_bd)

    n2 = LANES
    n1 = l // n2
    nj = n2 // SUBLANES
    c1, s1 = _dft_tables(n1)
    f1 = jnp.concatenate([c1, -s1], axis=0).astype(BF16)
    c2, s2 = _dft_tables(n2)
    f2 = jnp.concatenate([jnp.concatenate([c2, s2], axis=1),
                          jnp.concatenate([-s2, c2], axis=1)], axis=0).astype(BF16)
    k1 = lax.broadcasted_iota(jnp.int32, (n1, n2), 0)
    t2 = lax.broadcasted_iota(jnp.int32, (n1, n2), 1)
    ang = (k1 * t2).astype(F32) * (2.0 * math.pi / l)
    twc = jnp.cos(ang).reshape(n1, nj, SUBLANES).transpose(1, 0, 2)
    tws = jnp.sin(ang).reshape(n1, nj, SUBLANES).transpose(1, 0, 2)
    ar, ai = pl.pallas_call(
        functools.partial(_four1_kernel, n1=n1),
        grid=(b, nj),
        in_specs=[pl.BlockSpec((None, n1, SUBLANES, nch), lambda bi, j: (bi, 0, j, 0)),
                  full(f1, 2),
                  pl.BlockSpec((None, n1, SUBLANES), lambda bi, j: (j, 0, 0)),
                  pl.BlockSpec((None, n1, SUBLANES), lambda bi, j: (j, 0, 0))],
        out_specs=[pl.BlockSpec((None, n1, SUBLANES, nch), lambda bi, j: (bi, 0, j, 0))] * 2,
        out_shape=[jax.ShapeDtypeStruct((b, n1, n2, nch), F32)] * 2,
        scratch_shapes=[pltpu.VMEM((nch // LANES, n1 * SUBLANES, LANES), F32)] * 3,
        compiler_params=_cparams("parallel", "parallel"),
    )(z_f.reshape(b, n1, n2, nch), f1, twc, tws)
    y = pl.pallas_call(
        functools.partial(_four2_kernel, n2=n2, scale=scale),
        grid=(b, n1 // SUBLANES),
        in_specs=[pl.BlockSpec((None, SUBLANES, n2, nch), lambda bi, j: (bi, j, 0, 0)),
                  pl.BlockSpec((None, SUBLANES, n2, nch), lambda bi, j: (bi, j, 0, 0)),
                  full(f2, 2), full(c_bd, 2), full(s_bd, 2), full(four_w_bd, 2)],
        out_specs=pl.BlockSpec((None, n2, SUBLANES, nch), lambda bi, j: (bi, 0, j, 0)),
        out_shape=jax.ShapeDtypeStruct((b, n2, n1, nch), F32),
        scratch_shapes=[pltpu.VMEM((nch // LANES, n2 * SUBLANES, LANES), F32)],
        compiler_params=_cparams("parallel", "parallel"),
    )(ar, ai, f2, c_bd, s_bd, four_w_bd)
    return y.reshape(b, l, nch)


def _window_counts(idx, n, w):
    return (jnp.minimum(idx + (w - w // 2), n) - jnp.maximum(idx - w // 2, 0)).astype(F32)


def _pool_kernel(cur_ref, prev_ref, next_ref, pw_ref, ps_ref, o_ref, *, r, w, rows_total, n_tiles):
    i = pl.program_id(1)
    two_d = rows_total > 1
    halo = SUBLANES if two_d else 0
    nr = r + 2 * halo
    stride = w + POOL_PAD
    flat = nr * stride
    lane = lax.broadcasted_iota(jnp.int32, (1, 1, LANES), 2)
    low = lane < (LANES // 2)
    col = lax.broadcasted_iota(jnp.int32, (1, w, 1), 1)
    grow = lax.broadcasted_iota(jnp.int32, (r, 1, 1), 0) + i * r
    outs = []
    for hh in range(2):
        cols = slice(hh * LANES, (hh + 1) * LANES)
        w_lo, w_hi = POOL_WINDOWS[2 * hh], POOL_WINDOWS[2 * hh + 1]
        cur = cur_ref[:, cols]
        if two_d:
            prev = jnp.where(i == 0, 0.0, prev_ref[:, cols])
            nxt = jnp.where(i == n_tiles - 1, 0.0, next_ref[:, cols])
            ext = jnp.concatenate([prev, cur, nxt], axis=0)
        else:
            ext = cur
        ext = ext.reshape(nr, w, LANES)
        x = jnp.concatenate([jnp.zeros((nr, POOL_PAD, LANES), F32), ext], axis=1).reshape(flat, LANES)
        p = x + pltpu.roll(x, 1, 0)
        sums = {2: p}
        for ww, s in ((4, 1), (8, 2), (16, 4)):
            if ww > w_hi:
                break
            p = pltpu.roll(p, s, 0) + pltpu.roll(p, flat - s, 0)
            sums[ww] = p
        y = jnp.where(low[0], sums[w_lo], sums[w_hi]).reshape(nr, stride, LANES)[:, POOL_PAD:, :]
        if two_d:
            q2 = y[0:nr - 1] + y[1:nr]
            rsum = {2: q2[7:7 + r]}
            q4 = q2[0:nr - 3] + q2[2:nr - 1]
            rsum[4] = q4[6:6 + r]
            if w_hi > 4:
                q8 = q4[0:nr - 7] + q4[4:nr - 3]
                rsum[8] = q8[4:4 + r]
                q16 = q8[0:nr - 15] + q8[8:nr - 7]
                rsum[16] = q16[0:r]
            tot = jnp.where(low, rsum[w_lo], rsum[w_hi])
            cnt_r = jnp.where(low, _window_counts(grow, rows_total, w_lo), _window_counts(grow, rows_total, w_hi))
            tot = tot / cnt_r
        else:
            tot = y
        cnt_c = jnp.where(low, _window_counts(col, w, w_lo), _window_counts(col, w, w_hi))
        pooled = (tot / cnt_c).reshape(r * w, LANES)
        outs.append(pooled - cur)
    d = jnp.concatenate(outs, axis=-1)
    o_ref[...] = (_dot(d.astype(BF16), pw_ref[...]) * ps_ref[...]).astype(o_ref.dtype)


def _pool(z_p, pool_w_bd, pool_scale, rows):
    b, l, nch = z_p.shape
    if rows is None:
        rows_total, w, r = 1, l, 1
    else:
        rows_total, w, r = rows, l // rows, min(32, rows)
    t = r * w
    n_tiles = l // t
    hb = SUBLANES * w if rows is not None else t
    per = t // hb
    nhb = l // hb
    return pl.pallas_call(
        functools.partial(_pool_kernel, r=r, w=w, rows_total=rows_total, n_tiles=n_tiles),
        grid=(b, n_tiles),
        in_specs=[pl.BlockSpec((None, t, nch), lambda bi, i: (bi, i, 0)),
                  pl.BlockSpec((None, hb, nch), lambda bi, i: (bi, jnp.maximum(i * per - 1, 0), 0)),
                  pl.BlockSpec((None, hb, nch), lambda bi, i: (bi, jnp.minimum((i + 1) * per, nhb - 1), 0)),
                  pl.BlockSpec(pool_w_bd.shape, lambda bi, i: (0, 0)),
                  pl.BlockSpec((1, nch), lambda bi, i: (0, 0))],
        out_specs=pl.BlockSpec((None, t, nch), lambda bi, i: (bi, i, 0)),
        out_shape=jax.ShapeDtypeStruct((b, l, nch), BF16),
        compiler_params=_cparams("parallel", "parallel"),
    )(z_p, z_p, z_p, pool_w_bd, pool_scale.reshape(1, nch))


def _first_argmax4(v):
    m = jnp.maximum(jnp.maximum(v[0], v[1]), jnp.maximum(v[2], v[3]))
    idx = jnp.where(v[0] >= m, 0, jnp.where(v[1] >= m, 1, jnp.where(v[2] >= m, 2, 3)))
    return m, idx


def _select4(idx, v):
    return jnp.where(idx == 0, v[0], jnp.where(idx == 1, v[1], jnp.where(idx == 2, v[2], v[3])))


def _router_logits(h_bf16, wr_ref, br_ref):
    return (_dot(h_bf16, wr_ref[...]) + br_ref[...]).T


def _group_logits(lt):
    return [lt[g:g + 1, :] for g in range(N_GROUPS)]


def _expert_gates(lt, group):
    lg = _group_logits(lt)
    m, _ = _first_argmax4(lg)
    den = sum(jnp.exp(v - m) for v in lg)
    p_group = jnp.exp(_select4(group, lg) - m) / den
    sel = [_select4(group, [lt[SUBLANES + EXPERTS_PER_GROUP * g + e:SUBLANES + EXPERTS_PER_GROUP * g + e + 1, :]
                            for g in range(N_GROUPS)]) for e in range(EXPERTS_PER_GROUP)]
    v1, e1 = _first_argmax4(sel)
    rest = [jnp.where(e1 == e, -jnp.inf, sel[e]) for e in range(EXPERTS_PER_GROUP)]
    v2, e2 = _first_argmax4(rest)
    t2 = jnp.exp(v2 - v1)
    w1 = p_group / (1.0 + t2)
    w2 = p_group * t2 / (1.0 + t2)
    return [jnp.where(e1 == e, w1, jnp.where(e2 == e, w2, 0.0)) for e in range(EXPERTS_PER_GROUP)]


def _combine_kernel(hf_ref, hb_ref, zg_ref, yp_ref, yf_ref, x_ref, mod_ref, wo_ref, g_ref, wr_ref, br_ref,
                    xo_ref, h2t_ref, gid_ref):
    y_a = ((hf_ref[...].astype(F32) + hb_ref[...].astype(F32))
           * jax.nn.gelu(zg_ref[...].astype(F32), approximate=True))
    ycat = jnp.concatenate([y_a.astype(BF16), yp_ref[...].astype(BF16), yf_ref[...].astype(BF16)], axis=-1)
    x_new = x_ref[...] + mod_ref[2:3, :] * _dot(ycat, wo_ref[...])
    xo_ref[...] = x_new
    h2 = _rms_mod(x_new, g_ref[...], mod_ref[3:4, :], mod_ref[4:5, :])
    _to_token_tiles(h2t_ref, h2)
    _, gidx = _first_argmax4(_group_logits(_router_logits(h2.astype(BF16), wr_ref, br_ref)))
    gid_ref[...] = gidx


def _combine(hf, hb, z_g, y_p, y_f, x, mod, w_out, g_ffn, w_router, b_router):
    b, l, d = x.shape
    assert d == SUBLANES * LANES, "a token tile is one (8, 128) f32 tile"
    t = min(1024, l)
    tok = lambda c: pl.BlockSpec((None, t, c), lambda bi, i: (bi, i, 0))
    full = lambda a: pl.BlockSpec(a.shape, lambda bi, i: (0,) * a.ndim)
    return pl.pallas_call(
        _combine_kernel,
        grid=(b, l // t),
        in_specs=[tok(hf.shape[-1]), tok(hb.shape[-1]), tok(z_g.shape[-1]), tok(y_p.shape[-1]), tok(y_f.shape[-1]),
                  tok(d), pl.BlockSpec((None, N_MOD, d), lambda bi, i: (bi, 0, 0)),
                  full(w_out), full(g_ffn), full(w_router), full(b_router)],
        out_specs=[tok(d), pl.BlockSpec((None, t * SUBLANES, LANES), lambda bi, i: (bi, i, 0)),
                   pl.BlockSpec((None, 1, t), lambda bi, i: (bi, 0, i))],
        out_shape=[jax.ShapeDtypeStruct((b, l, d), F32), jax.ShapeDtypeStruct((b, l * SUBLANES, LANES), F32),
                   jax.ShapeDtypeStruct((b, 1, l), jnp.int32)],
        compiler_params=_cparams("parallel", "parallel"),
    )(hf, hb, z_g, y_p, y_f, x, mod, w_out, g_ffn, w_router, b_router)


MOE_TILE = 512


def _sort_plan(gid, n_groups, t):
    n = gid.shape[0]
    onehot = (gid[:, None] == jnp.arange(n_groups, dtype=jnp.int32)[None, :]).astype(jnp.int32)
    csum = jnp.cumsum(onehot, axis=0)
    rank = jnp.sum((csum - onehot) * onehot, axis=1)
    tiles = (csum[-1] + t - 1) // t
    tile_end = jnp.cumsum(tiles)
    slot = jnp.sum(onehot * (tile_end - tiles)[None, :], axis=1) * t + rank
    n_tiles = n // t + n_groups
    tile_idx = jnp.arange(n_tiles, dtype=jnp.int32)
    tile_group = jnp.minimum(jnp.sum((tile_idx[:, None] >= tile_end[None, :]).astype(jnp.int32), axis=1), n_groups - 1)
    pad_lo = jnp.concatenate([(tile_end - tiles) * t + csum[-1], tile_end[-1:] * t]).astype(jnp.int32)
    pad_hi = jnp.concatenate([tile_end * t, jnp.full((1,), n_tiles * t, jnp.int32)]).astype(jnp.int32)
    return (slot.astype(jnp.int32), tile_group.astype(jnp.int32), tile_end[-1:].astype(jnp.int32), pad_lo, pad_hi,
            n_tiles)


def _invert_kernel(slot_ref, pad_lo_ref, pad_hi_ref, src_ref):
    def clear(s, carry):
        src_ref[s] = -1
        return carry

    for k in range(pad_lo_ref.shape[0]):
        lax.fori_loop(pad_lo_ref[k], pad_hi_ref[k], clear, 0)

    def put(tok, carry):
        src_ref[slot_ref[tok]] = tok
        return carry

    lax.fori_loop(0, slot_ref.shape[0], put, 0, unroll=16)


def _invert(slot, pad_lo, pad_hi, n_rows):
    smem = pl.BlockSpec(memory_space=pltpu.SMEM)
    return pl.pallas_call(
        _invert_kernel,
        in_specs=[smem, smem, smem],
        out_specs=smem,
        out_shape=jax.ShapeDtypeStruct((n_rows,), jnp.int32),
    )(slot, pad_lo, pad_hi)


def _moe_fused_kernel(tg_ref, nu_ref, nv_ref, src_ref, dst_ref, h_ref, wr_ref, br_ref, wg_ref, wu_ref, wd_ref, y_ref,
                      rows, ybuf, hb_scr, gt_scr, gsem, ssem, *, t, epg):
    i = pl.program_id(0)
    nu = nu_ref[0]
    s = lax.rem(i, 2)
    tile_rows = t * SUBLANES

    def token(ref, first_row):
        return ref.at[pl.ds(pl.multiple_of(first_row, SUBLANES), SUBLANES)]

    def gather_rows(tile, slot, lo, hi):
        def body(g, carry):
            for k in range(SUBLANES):
                r = g * SUBLANES + k
                pltpu.make_async_copy(token(h_ref, src_ref[tile * t + r]), token(rows.at[slot], r * SUBLANES),
                                      gsem.at[slot]).start(priority=k % 2)
            return carry
        lax.fori_loop(lo // SUBLANES, hi // SUBLANES, body, 0)

    def scatter_rows(tile, slot, lo, hi):
        def body(g, carry):
            for k in range(SUBLANES):
                r = g * SUBLANES + k
                pltpu.make_async_copy(token(ybuf.at[slot], r * SUBLANES), token(y_ref, dst_ref[tile * t + r]),
                                      ssem.at[slot]).start(priority=k % 2)
            return carry
        lax.fori_loop(lo // SUBLANES, hi // SUBLANES, body, 0)

    def wait_gather(slot):
        pltpu.make_async_copy(h_ref.at[pl.ds(0, tile_rows)], rows.at[slot], gsem.at[slot]).wait()

    def wait_scatter(slot):
        pltpu.make_async_copy(ybuf.at[slot], y_ref.at[pl.ds(0, tile_rows)], ssem.at[slot]).wait()

    @pl.when(i == 0)
    def _():
        gt_scr[...] = jnp.zeros_like(gt_scr)
        gather_rows(0, 0, 0, t)

    @pl.when(jnp.logical_and(i >= 2, i - 2 < nu))
    def _():
        wait_scatter(s)

    @pl.when(i < nu)
    def _():
        wait_gather(s)
        hb_scr[...] = _from_token_tiles(rows, (s,)).astype(BF16)
        tok_id = lax.broadcasted_iota(jnp.int32, (1, t), 1)
        gate_rows = _expert_gates(_router_logits(hb_scr[...], wr_ref, br_ref), tg_ref[i])
        for e in range(epg):
            gt_scr[e:e + 1, :] = jnp.where(tok_id < nv_ref[i], gate_rows[e], 0.0)
        gates = gt_scr[...].T
        @pl.when(i + 1 < nu)
        def _():
            gather_rows(i + 1, 1 - s, 0, t)

        @pl.when(i >= 1)
        def _():
            scatter_rows(i - 1, 1 - s, 0, t)

        acc = None
        for e in range(epg):
            hid = jax.nn.silu(_dot(hb_scr[...], wg_ref[e])) * _dot(hb_scr[...], wu_ref[e])
            y = gates[:, e:e + 1] * _dot(hid.astype(BF16), wd_ref[e])
            acc = y if acc is None else acc + y
        _to_token_tiles(ybuf, acc, (s,))

    @pl.when(i == nu)
    def _():
        scatter_rows(i - 1, 1 - s, 0, t)

    @pl.when(jnp.logical_and(i > nu, i < pl.num_programs(0) - 1))
    def _():
        rows[0] = jnp.zeros(rows.shape[1:], F32)
        first = pl.multiple_of(dst_ref[(i - 1) * t], SUBLANES)
        fill = pltpu.make_async_copy(rows.at[0], y_ref.at[pl.ds(first, tile_rows)], gsem.at[0])
        fill.start()
        fill.wait()


def _moe_fused(tile_group, n_used, n_valid, src, dst, h2t, w_router, b_router, wg, wu, wd, group0, n_tiles, t):
    epg = wg.shape[1]
    last = n_tiles - 1
    d = SUBLANES * LANES
    tile_rows = t * SUBLANES
    full = lambda a: pl.BlockSpec(a.shape, lambda i, *prefetch: (0,) * a.ndim)
    by_group = lambda a: pl.BlockSpec((None,) + a.shape[1:],
                                      lambda i, tg, *rest: (group0 + tg[jnp.minimum(i, last)], 0, 0, 0))
    return pl.pallas_call(
        functools.partial(_moe_fused_kernel, t=t, epg=epg),
        grid_spec=pltpu.PrefetchScalarGridSpec(
            num_scalar_prefetch=5, grid=(n_tiles + 2,),
            in_specs=[pl.BlockSpec(memory_space=pl.ANY), full(w_router), full(b_router),
                      by_group(wg), by_group(wu), by_group(wd)],
            out_specs=pl.BlockSpec(memory_space=pl.ANY),
            scratch_shapes=[pltpu.VMEM((2, tile_rows, LANES), F32), pltpu.VMEM((2, tile_rows, LANES), F32),
                            pltpu.VMEM((t, d), BF16), pltpu.VMEM((LANES, t), F32),
                            pltpu.SemaphoreType.DMA((2,)), pltpu.SemaphoreType.DMA((2,))]),
        out_shape=jax.ShapeDtypeStruct((n_tiles * tile_rows, LANES), F32),
        compiler_params=_cparams("arbitrary"),
    )(tile_group, n_used, n_valid, src, dst, h2t, w_router, b_router, wg, wu, wd)


def _final_kernel(x_ref, y_ref, mod_ref, g_ref, o_ref):
    out = x_ref[...] + mod_ref[5:6, :] * _from_token_tiles(y_ref)
    ms = jnp.mean(out * out, axis=-1, keepdims=True)
    o_ref[...] = out * lax.rsqrt(ms + EPS) * g_ref[...]


def _final(x, y_rows, mod, g_final):
    b, l, d = x.shape
    t = min(1024, l)
    nt = l // t
    return pl.pallas_call(
        _final_kernel,
        grid=(b, nt),
        in_specs=[pl.BlockSpec((None, t, d), lambda bi, i: (bi, i, 0)),
                  pl.BlockSpec((t * SUBLANES, LANES), lambda bi, i: (bi * nt + i, 0)),
                  pl.BlockSpec((None, N_MOD, d), lambda bi, i: (bi, 0, 0)),
                  pl.BlockSpec((1, d), lambda bi, i: (0, 0))],
        out_specs=pl.BlockSpec((None, t, d), lambda bi, i: (bi, i, 0)),
        out_shape=jax.ShapeDtypeStruct((b, l, d), F32),
        compiler_params=_cparams("parallel", "parallel"),
    )(x, y_rows, mod, g_final)


def kernel(x, c, ctx, c_ctx, mod_w, mod_b, norm_mix_g, norm_ffn_g, w_in, conv_w, conv_b, rec_gate_a_w, rec_gate_a_b,
           rec_gate_x_w, rec_gate_x_b, rec_lambda, pool_w, pool_scale, fourier_w, w_out, router_group_w,
           router_group_b, router_expert_w, router_expert_b, expert_w_gate, expert_w_up, expert_w_down, final_norm_g):
    b, l, d = x.shape
    depth = mod_w.shape[0]
    d_rec = conv_w.shape[-1]
    d_pool = pool_scale.shape[-1]
    d_four = fourier_w.shape[1] * fourier_w.shape[2]
    widths = (d_rec, d_rec, d_pool, d_four)
    rows = l // GRID_W
    heads_half = REC_HEADS // 2
    n_exp = expert_w_gate.shape[1]

    cvec = jnp.concatenate([c, c_ctx[None, :], jnp.zeros((SUBLANES - b - 1, d), F32)], axis=0)
    mod_all = _modulation(cvec, mod_w, mod_b).reshape(depth, SUBLANES, N_MOD, d)
    g_final = final_norm_g.reshape(1, d)
    zeros_state = jnp.zeros((b, 2, d_rec), F32)
    pending = pending_ctx = None
    by_group = lambda w: w.astype(BF16).reshape((depth * N_GROUPS, EXPERTS_PER_GROUP) + w.shape[2:])
    expert_w = (by_group(expert_w_gate), by_group(expert_w_up), by_group(expert_w_down))

    for li in range(depth):
        last = li == depth - 1
        mod_lat = mod_all[li, :b]
        mod_ctx = jnp.broadcast_to(mod_all[li, b][None], (b, N_MOD, d))
        w_in_l = w_in[li].astype(BF16)
        w_out_l = w_out[li].astype(BF16)
        wgate = (0.5 * jnp.stack([
            jnp.stack([jnp.concatenate([_block_diag(rec_gate_a_w[li, dd, hh * heads_half:(hh + 1) * heads_half]),
                                        _block_diag(rec_gate_x_w[li, dd, hh * heads_half:(hh + 1) * heads_half])], axis=1)
                       for hh in range(2)]) for dd in range(2)])).astype(BF16)
        bgate = 0.5 * jnp.stack([rec_gate_a_b[li], rec_gate_x_b[li]], axis=1)
        rec_p = (wgate, bgate, rec_lambda[li])
        proj_p = (w_in_l, conv_w[li], conv_b[li].reshape(1, d_rec), widths)
        pool_w_bd = _block_diag(pool_w[li]).astype(BF16)
        four_w_bd = _block_diag(fourier_w[li]).astype(BF16)
        w_router = jnp.concatenate([router_group_w[li], jnp.zeros((d, SUBLANES - N_GROUPS), F32), router_expert_w[li],
                                    jnp.zeros((d, LANES - SUBLANES - n_exp), F32)], axis=1).astype(BF16)
        b_router = jnp.concatenate([router_group_b[li], jnp.zeros((SUBLANES - N_GROUPS,), F32), router_expert_b[li],
                                    jnp.zeros((LANES - SUBLANES - n_exp,), F32)]).reshape(1, LANES)
        g_ffn = norm_ffn_g[li].reshape(1, d)

        def mixer_tail(hf, hb, z_g, z_p, z_f, stream, mod, grid_rows):
            y_p = _pool(z_p, pool_w_bd, pool_scale[li], grid_rows)
            y_f = _fourier(z_f, four_w_bd)
            return _combine(hf, hb, z_g, y_p, y_f, stream, mod, w_out_l, g_ffn, w_router, b_router)

        def expert_stage(h2t, gid):
            n_tok = gid.size
            t_moe = min(MOE_TILE, h2t.shape[1] // SUBLANES)
            slot, tile_group, n_used, pad_lo, pad_hi, n_tiles = _sort_plan(gid.reshape(n_tok), N_GROUPS, t_moe)
            src = _invert(slot, pad_lo, pad_hi, n_tiles * t_moe)
            valid = src >= 0
            pad_rank = jnp.cumsum(jnp.logical_not(valid).astype(jnp.int32)) - 1
            dst = jnp.where(valid, src, n_tok + pad_rank)
            n_valid = jnp.sum(valid.reshape(n_tiles, t_moe).astype(jnp.int32), axis=1)
            return _moe_fused(tile_group, n_used, n_valid, jnp.maximum(src, 0) * SUBLANES, dst * SUBLANES,
                              h2t.reshape(n_tok * SUBLANES, LANES),
                              w_router, b_router, *expert_w, li * N_GROUPS, n_tiles, t_moe)

        if pending_ctx is None:
            uc, zc_g, zc_p, zc_f = _inproj(ctx, norm_mix_g[li], mod_ctx, *proj_p)
        else:
            ctx, uc, zc_g, zc_p, zc_f = _inproj(ctx, norm_mix_g[li], mod_ctx, *proj_p, residual=pending_ctx)
        hf_c, hb_c = _recurrent(uc, *rec_p, zeros_state, F32)
        state = jnp.stack([hf_c[:, -1, :], hb_c[:, 0, :]], axis=1)
        if not last:
            ctx, h2t_c, gid_c = mixer_tail(hf_c, hb_c, zc_g, zc_p, zc_f, ctx, mod_ctx, None)
            pending_ctx = (expert_stage(h2t_c, gid_c), 0, mod_ctx)

        if pending is None:
            u, z_g, z_p, z_f = _inproj(x, norm_mix_g[li], mod_lat, *proj_p)
        else:
            x, u, z_g, z_p, z_f = _inproj(x, norm_mix_g[li], mod_lat, *proj_p, residual=pending)
        hf, hb = _recurrent(u, *rec_p, state, BF16)
        x, h2t, gid = mixer_tail(hf, hb, z_g, z_p, z_f, x, mod_lat, rows)
        pending = (expert_stage(h2t, gid), 0, mod_lat)
    return _final(x, pending[0], pending[2], g_final)
```

```python
import functools
import math

import jax
import jax.numpy as jnp
from jax import lax
from jax.experimental import pallas as pl
from jax.experimental.pallas import tpu as pltpu

F32 = jnp.float32
BF16 = jnp.bfloat16

GRID_W = 64
N_MOD = 6
REC_HEADS = 8
CONV_W = 4
LRU_C = 8.0
POOL_WINDOWS = (2, 4, 8, 16)
FOUR_HEADS = 4
N_GROUPS = 4
EXPERTS_PER_GROUP = 4
EPS = 1e-6

LANES = 128
SUBLANES = 8
POOL_PAD = 8
VMEM_LIMIT = 56 * 1024 * 1024


def _cparams(*sem):
    return pltpu.CompilerParams(dimension_semantics=sem, vmem_limit_bytes=VMEM_LIMIT)


def _split_bf16(a):
    hi = a.astype(BF16)
    lo = (a - hi.astype(F32)).astype(BF16)
    return hi, lo


def _dot(a, b):
    return jnp.dot(a, b, preferred_element_type=F32)


def _sigmoid_of_twice(half_x):
    return 0.5 * jnp.tanh(half_x) + 0.5


def _mod_kernel(s_ref, w_ref, b_ref, o_ref):
    s = s_ref[...]
    s = s * jax.nn.sigmoid(s)
    s_hi, s_lo = _split_bf16(s)
    w_hi, w_lo = _split_bf16(w_ref[...])
    o_ref[...] = _dot(s_hi, w_hi) + _dot(s_hi, w_lo) + _dot(s_lo, w_hi) + b_ref[...]


def _modulation(cvec, mod_w, mod_b):
    depth, d, dm = mod_w.shape
    tn = dm // 4
    return pl.pallas_call(
        _mod_kernel,
        grid=(depth, dm // tn),
        in_specs=[pl.BlockSpec((SUBLANES, d), lambda l, j: (0, 0)),
                  pl.BlockSpec((None, d, tn), lambda l, j: (l, 0, j)),
                  pl.BlockSpec((None, 1, tn), lambda l, j: (l, 0, j))],
        out_specs=pl.BlockSpec((None, SUBLANES, tn), lambda l, j: (l, 0, j)),
        out_shape=jax.ShapeDtypeStruct((depth, SUBLANES, dm), F32),
        compiler_params=_cparams("parallel", "parallel"),
    )(cvec, mod_w, mod_b.reshape(depth, 1, dm))


def _rms_mod(x, g, shift, scale):
    ms = jnp.mean(x * x, axis=-1, keepdims=True)
    y = x * lax.rsqrt(ms + EPS) * g
    return y * (1.0 + scale) + shift


def _from_token_tiles(ref, lead=()):
    n = ref.shape[-2] // SUBLANES
    return jnp.concatenate([ref[lead + (pl.ds(j, n, stride=SUBLANES), slice(None))] for j in range(SUBLANES)], axis=-1)


def _to_token_tiles(ref, val, lead=()):
    n = val.shape[0]
    for j in range(SUBLANES):
        ref[lead + (pl.ds(j, n, stride=SUBLANES), slice(None))] = val[:, j * LANES:(j + 1) * LANES]


def _inproj_kernel(*refs, widths, with_residual, n_tiles):
    i = pl.program_id(1)
    if with_residual:
        (x_ref, xp_ref, xn_ref, y_ref, yp_ref, yn_ref, pmod_ref,
         g_ref, mod_ref, w_ref, cw_ref, cb_ref, xo_ref, *o_refs) = refs
        gate = pmod_ref[5:6, :]
        x = x_ref[...] + gate * _from_token_tiles(y_ref)
        xo_ref[...] = x
        x_halo = jnp.concatenate([xp_ref[...] + gate * _from_token_tiles(yp_ref),
                                  xn_ref[...] + gate * _from_token_tiles(yn_ref)], axis=0)
    else:
        x_ref, xp_ref, xn_ref, g_ref, mod_ref, w_ref, cw_ref, cb_ref, *o_refs = refs
        x = x_ref[...]
        x_halo = jnp.concatenate([xp_ref[...], xn_ref[...]], axis=0)
    t = x.shape[0]
    d_rec = widths[0]
    z = _dot(_rms_mod(x, g_ref[...], mod_ref[0:1, :], mod_ref[1:2, :]).astype(BF16), w_ref[...])
    zh = _dot(_rms_mod(x_halo, g_ref[...], mod_ref[0:1, :], mod_ref[1:2, :]).astype(BF16), w_ref[:, :d_rec])
    zz = jnp.concatenate([jnp.where(i == 0, 0.0, zh[:SUBLANES]), z[:, :d_rec],
                          jnp.where(i == n_tiles - 1, 0.0, zh[SUBLANES:])], axis=0)
    u = cb_ref[...]
    for k in range(CONV_W):
        o = SUBLANES - CONV_W // 2 + k
        u = u + cw_ref[k:k + 1, :] * zz[o:o + t, :]
    seg = t // SUBLANES
    for q in range(d_rec // LANES):
        for s in range(SUBLANES):
            o_refs[0][q, pl.ds(s, seg, stride=SUBLANES), :] = u[s * seg:(s + 1) * seg, q * LANES:(q + 1) * LANES]
    off = d_rec
    for o_ref, wd in zip(o_refs[1:], widths[1:]):
        o_ref[...] = z[:, off:off + wd].astype(o_ref.dtype)
        off += wd


def _inproj(x, g, mod, w_bf16, conv_w, conv_b, widths, residual=None):
    b, l, d = x.shape
    t = min(512, l)
    nt = l // t
    per = t // SUBLANES
    n8 = l // SUBLANES
    tok = lambda c: pl.BlockSpec((None, t, c), lambda bi, i: (bi, i, 0))
    prev8 = lambda bi, i: jnp.maximum(i * per - 1, 0)
    next8 = lambda bi, i: jnp.minimum((i + 1) * per, n8 - 1)
    x_specs = [tok(d), pl.BlockSpec((None, SUBLANES, d), lambda bi, i: (bi, prev8(bi, i), 0)),
               pl.BlockSpec((None, SUBLANES, d), lambda bi, i: (bi, next8(bi, i), 0))]
    modspec = pl.BlockSpec((None, N_MOD, d), lambda bi, i: (bi, 0, 0))
    full = lambda a: pl.BlockSpec(a.shape, lambda bi, i: (0,) * a.ndim)
    tail_specs = [pl.BlockSpec((1, d), lambda bi, i: (0, 0)), modspec, full(w_bf16), full(conv_w), full(conv_b)]
    tail_args = (g.reshape(1, d), mod, w_bf16, conv_w, conv_b)
    slabs = widths[0] // LANES
    z_specs = ([pl.BlockSpec((None, None, slabs, t, LANES), lambda bi, i: (bi, i, 0, 0, 0))]
               + [tok(wd) for wd in widths[1:]])
    z_shapes = ([jax.ShapeDtypeStruct((b, nt, slabs, t, LANES), F32)]
                + [jax.ShapeDtypeStruct((b, l, wd), BF16 if k == 0 else F32) for k, wd in enumerate(widths[1:])])
    if residual is None:
        in_specs, args, out_specs, out_shape = x_specs + tail_specs, (x, x, x) + tail_args, z_specs, z_shapes
    else:
        y_rows, row0, prev_mod = residual
        blk0, blk8 = row0 // t, row0 // SUBLANES
        halo = SUBLANES * SUBLANES
        y_specs = [pl.BlockSpec((t * SUBLANES, LANES), lambda bi, i: (blk0 + bi * nt + i, 0)),
                   pl.BlockSpec((halo, LANES), lambda bi, i: (blk8 + bi * n8 + prev8(bi, i), 0)),
                   pl.BlockSpec((halo, LANES), lambda bi, i: (blk8 + bi * n8 + next8(bi, i), 0))]
        in_specs = x_specs + y_specs + [modspec] + tail_specs
        args = (x, x, x, y_rows, y_rows, y_rows, prev_mod) + tail_args
        out_specs, out_shape = [tok(d)] + z_specs, [jax.ShapeDtypeStruct((b, l, d), F32)] + z_shapes
    return pl.pallas_call(
        functools.partial(_inproj_kernel, widths=widths, with_residual=residual is not None, n_tiles=nt),
        grid=(b, nt), in_specs=in_specs, out_specs=out_specs, out_shape=out_shape,
        compiler_params=_cparams("parallel", "parallel"),
    )(*args)


def _rec_kernel(uf_ref, ub_ref, wg_ref, bg_ref, lam_ref, h0_ref, hf_ref, hb_ref, a_scr, b_scr, carry_scr, *, tc):
    i = pl.program_id(1)
    nq = uf_ref.shape[0]
    c = nq * LANES
    half = c // 2
    seg = tc // SUBLANES
    lanes = lambda q: slice(q * LANES, (q + 1) * LANES)

    @pl.when(i == 0)
    def _():
        carry_scr[...] = h0_ref[...]

    def coefficients(d, u_ref):
        u = jnp.concatenate([u_ref[q] for q in range(nq)], axis=-1)
        log_decay = -LRU_C * jax.nn.softplus(-lam_ref[d:d + 1, :])
        for hh in range(2):
            cols = slice(hh * half, (hh + 1) * half)
            uh = u[:, cols]
            pre = _dot(uh.astype(BF16), wg_ref[d, hh])
            r = _sigmoid_of_twice(pre[:, :half] + bg_ref[d, 0:1, cols])
            ig = _sigmoid_of_twice(pre[:, half:] + bg_ref[d, 1:2, cols])
            log_a = log_decay[:, cols] * r
            a = jnp.exp(log_a)
            th = jnp.tanh(log_a)
            one_minus_a2 = -2.0 * th / (1.0 - th)
            root = jnp.where(one_minus_a2 > 0.0, one_minus_a2 * lax.rsqrt(one_minus_a2), 0.0)
            bt = root * (ig * uh)
            for k in range(half // LANES):
                q = hh * (half // LANES) + k
                a_scr[d, q] = a[:, lanes(k)]
                b_scr[d, q] = bt[:, lanes(k)]

    coefficients(0, uf_ref)
    coefficients(1, ub_ref)

    def step(d, j, h, acc):
        rows8 = pl.ds(pl.multiple_of(j * SUBLANES, SUBLANES), SUBLANES)
        a8 = jnp.concatenate([a_scr[d, q, rows8, :] for q in range(nq)], axis=-1)
        b8 = jnp.concatenate([b_scr[d, q, rows8, :] for q in range(nq)], axis=-1)
        h = a8 * h + b8
        acc = a8 * acc
        for q in range(nq):
            b_scr[d, q, rows8, :] = h[:, lanes(q)]
            a_scr[d, q, rows8, :] = acc[:, lanes(q)]
        return h, acc

    def body(j, carry):
        hf, af, hb, ab = carry
        hf, af = step(0, j, hf, af)
        hb, ab = step(1, seg - 1 - j, hb, ab)
        return hf, af, hb, ab

    zero = jnp.zeros((SUBLANES, c), F32)
    one = jnp.ones((SUBLANES, c), F32)
    hf_end, af_end, hb_end, ab_end = lax.fori_loop(0, seg, body, (zero, one, zero, one), unroll=2)

    def chain(h_end, a_end, state, order):
        entering = [None] * SUBLANES
        for s in order:
            entering[s] = state
            state = a_end[s:s + 1, :] * state + h_end[s:s + 1, :]
        return jnp.concatenate(entering, axis=0), state

    cin_f, out_f = chain(hf_end, af_end, carry_scr[0:1, :], range(SUBLANES))
    cin_b, out_b = chain(hb_end, ab_end, carry_scr[1:2, :], reversed(range(SUBLANES)))
    carry_scr[0:1, :] = out_f
    carry_scr[1:2, :] = out_b

    for d, cin, out_ref in ((0, cin_f, hf_ref), (1, cin_b, hb_ref)):
        for q in range(nq):
            full_h = (b_scr[d, q].reshape(seg, SUBLANES, LANES)
                      + a_scr[d, q].reshape(seg, SUBLANES, LANES) * cin[None, :, lanes(q)])
            b_scr[d, q] = full_h.reshape(tc, LANES)
            for s in range(SUBLANES):
                out_ref[s * seg:(s + 1) * seg, lanes(q)] = (
                    b_scr[d, q, pl.ds(s, seg, stride=SUBLANES), :].astype(out_ref.dtype))


def _recurrent(u, wgate, bgate, lam, h0, out_dtype):
    b, nc, nq, tc, _ = u.shape
    c = nq * LANES
    u_fwd = pl.BlockSpec((None, None, nq, tc, LANES), lambda bi, i: (bi, i, 0, 0, 0))
    u_bwd = pl.BlockSpec((None, None, nq, tc, LANES), lambda bi, i: (bi, nc - 1 - i, 0, 0, 0))
    h_fwd = pl.BlockSpec((None, tc, c), lambda bi, i: (bi, i, 0))
    h_bwd = pl.BlockSpec((None, tc, c), lambda bi, i: (bi, nc - 1 - i, 0))
    full = lambda a: pl.BlockSpec(a.shape, lambda bi, i: (0,) * a.ndim)
    return pl.pallas_call(
        functools.partial(_rec_kernel, tc=tc),
        grid=(b, nc),
        in_specs=[u_fwd, u_bwd, full(wgate), full(bgate), full(lam),
                  pl.BlockSpec((None, 2, c), lambda bi, i: (bi, 0, 0))],
        out_specs=[h_fwd, h_bwd],
        out_shape=[jax.ShapeDtypeStruct((b, nc * tc, c), out_dtype)] * 2,
        scratch_shapes=[pltpu.VMEM((2, nq, tc, LANES), F32), pltpu.VMEM((2, nq, tc, LANES), F32),
                        pltpu.VMEM((2, c), F32)],
        compiler_params=_cparams("parallel", "arbitrary"),
    )(u, u, wgate, bgate, lam, h0)


def _dft_tables(n):
    j = lax.broadcasted_iota(jnp.int32, (n, n), 0)
    k = lax.broadcasted_iota(jnp.int32, (n, n), 1)
    ang = ((j * k) % n).astype(F32) * (2.0 * math.pi / n)
    return jnp.cos(ang), jnp.sin(ang)


def _block_diag(w):
    return jax.scipy.linalg.block_diag(*[w[h] for h in range(w.shape[0])])


def _channel_stage(gr, gi, c_ref, s_ref, w_ref, scale):
    f = (_dot(gr.astype(BF16), c_ref[...]) + _dot(gi.astype(BF16), s_ref[...])) * scale
    return _dot(f.astype(BF16), w_ref[...])


def _slabs_from(scr, val):
    for q in range(scr.shape[0]):
        scr[q] = val[:, q * LANES:(q + 1) * LANES]


def _slab_rows(scr, start, n):
    return jnp.concatenate([scr[q, pl.ds(start, n, stride=SUBLANES), :] for q in range(scr.shape[0])], axis=-1)


def _set_slab_rows(scr, start, val):
    for q in range(scr.shape[0]):
        scr[q, pl.ds(start, val.shape[0], stride=SUBLANES), :] = val[:, q * LANES:(q + 1) * LANES]


def _slabs_value(scr):
    return jnp.concatenate([scr[q] for q in range(scr.shape[0])], axis=-1)


def _four1_kernel(x_ref, f1_ref, twc_ref, tws_ref, ar_ref, ai_ref, xs, ars, ais, *, n1):
    shape = x_ref.shape
    _slabs_from(xs, x_ref[...].reshape(n1 * SUBLANES, shape[-1]))
    for jj in range(SUBLANES):
        a = _dot(f1_ref[...], _slab_rows(xs, jj, n1).astype(BF16))
        ar, ai = a[:n1], a[n1:]
        cc = twc_ref[:, jj:jj + 1]
        ss = tws_ref[:, jj:jj + 1]
        _set_slab_rows(ars, jj, ar * cc + ai * ss)
        _set_slab_rows(ais, jj, ai * cc - ar * ss)
    ar_ref[...] = _slabs_value(ars).reshape(shape)
    ai_ref[...] = _slabs_value(ais).reshape(shape)


def _four2_kernel(ar_ref, ai_ref, f2_ref, c_ref, s_ref, w_ref, o_ref, ys, *, n2, scale):
    grs, gis = [], []
    for kk in range(SUBLANES):
        slab = jnp.concatenate([ar_ref[kk], ai_ref[kk]], axis=0).astype(BF16)
        g = _dot(f2_ref[...], slab)
        grs.append(g[:n2])
        gis.append(g[n2:])
    y = _channel_stage(jnp.concatenate(grs, axis=0), jnp.concatenate(gis, axis=0), c_ref, s_ref, w_ref, scale)
    for kk in range(SUBLANES):
        _set_slab_rows(ys, kk, y[kk * n2:(kk + 1) * n2])
    o_ref[...] = _slabs_value(ys).reshape(o_ref.shape)


def _four_small_kernel(x_ref, f_ref, c_ref, s_ref, w_ref, o_ref, *, n, scale):
    g = _dot(f_ref[...], x_ref[...].astype(BF16))
    o_ref[...] = _channel_stage(g[:n], g[n:], c_ref, s_ref, w_ref, scale)


def _fourier(z_f, four_w_bd):
    b, l, nch = z_f.shape
    hd = nch // FOUR_HEADS
    scale = 1.0 / math.sqrt(l * hd)
    c_h, s_h = _dft_tables(hd)
    eye = jnp.eye(FOUR_HEADS, dtype=F32)
    c_bd = jnp.kron(eye, c_h).astype(BF16)
    s_bd = jnp.kron(eye, s_h).astype(BF16)
    full = lambda a, nd: pl.BlockSpec(a.shape, lambda *_: (0,) * a.ndim)
    if l <= 256:
        cn, sn = _dft_tables(l)
        f = jnp.concatenate([cn, -sn], axis=0).astype(BF16)
        return pl.pallas_call(
            functools.partial(_four_small_kernel, n=l, scale=scale),
            grid=(b,),
            in_specs=[pl.BlockSpec((None, l, nch), lambda bi: (bi, 0, 0)),
                      full(f, 1), full(c_bd, 1), full(s_bd, 1), full(four_w_bd, 1)],
            out_specs=pl.BlockSpec((None, l, nch), lambda bi: (bi, 0, 0)),
            out_shape=jax.ShapeDtypeStruct((b, l, nch), F32),
            compiler_params=_cparams("parallel"),
        )(z_f, f, c_bd, s_bd, four_w_bd)

    n2 = LANES
    n1 = l // n2
    nj = n2 // SUBLANES
    c1, s1 = _dft_tables(n1)
    f1 = jnp.concatenate([c1, -s1], axis=0).astype(BF16)
    c2, s2 = _dft_tables(n2)
    f2 = jnp.concatenate([jnp.concatenate([c2, s2], axis=1),
                          jnp.concatenate([-s2, c2], axis=1)], axis=0).astype(BF16)
    k1 = lax.broadcasted_iota(jnp.int32, (n1, n2), 0)
    t2 = lax.broadcasted_iota(jnp.int32, (n1, n2), 1)
    ang = (k1 * t2).astype(F32) * (2.0 * math.pi / l)
    twc = jnp.cos(ang).reshape(n1, nj, SUBLANES).transpose(1, 0, 2)
    tws = jnp.sin(ang).reshape(n1, nj, SUBLANES).transpose(1, 0, 2)
    ar, ai = pl.pallas_call(
        functools.partial(_four1_kernel, n1=n1),
        grid=(b, nj),
        in_specs=[pl.BlockSpec((None, n1, SUBLANES, nch), lambda bi, j: (bi, 0, j, 0)),
                  full(f1, 2),
                  pl.BlockSpec((None, n1, SUBLANES), lambda bi, j: (j, 0, 0)),
                  pl.BlockSpec((None, n1, SUBLANES), lambda bi, j: (j, 0, 0))],
        out_specs=[pl.BlockSpec((None, n1, SUBLANES, nch), lambda bi, j: (bi, 0, j, 0))] * 2,
        out_shape=[jax.ShapeDtypeStruct((b, n1, n2, nch), F32)] * 2,
        scratch_shapes=[pltpu.VMEM((nch // LANES, n1 * SUBLANES, LANES), F32)] * 3,
        compiler_params=_cparams("parallel", "parallel"),
    )(z_f.reshape(b, n1, n2, nch), f1, twc, tws)
    y = pl.pallas_call(
        functools.partial(_four2_kernel, n2=n2, scale=scale),
        grid=(b, n1 // SUBLANES),
        in_specs=[pl.BlockSpec((None, SUBLANES, n2, nch), lambda bi, j: (bi, j, 0, 0)),
                  pl.BlockSpec((None, SUBLANES, n2, nch), lambda bi, j: (bi, j, 0, 0)),
                  full(f2, 2), full(c_bd, 2), full(s_bd, 2), full(four_w_bd, 2)],
        out_specs=pl.BlockSpec((None, n2, SUBLANES, nch), lambda bi, j: (bi, 0, j, 0)),
        out_shape=jax.ShapeDtypeStruct((b, n2, n1, nch), F32),
        scratch_shapes=[pltpu.VMEM((nch // LANES, n2 * SUBLANES, LANES), F32)],
        compiler_params=_cparams("parallel", "parallel"),
    )(ar, ai, f2, c_bd, s_bd, four_w_bd)
    return y.reshape(b, l, nch)


def _window_counts(idx, n, w):
    return (jnp.minimum(idx + (w - w // 2), n) - jnp.maximum(idx - w // 2, 0)).astype(F32)


def _pool_kernel(cur_ref, prev_ref, next_ref, pw_ref, ps_ref, o_ref, *, r, w, rows_total, n_tiles):
    i = pl.program_id(1)
    two_d = rows_total > 1
    halo = SUBLANES if two_d else 0
    nr = r + 2 * halo
    stride = w + POOL_PAD
    flat = nr * stride
    lane = lax.broadcasted_iota(jnp.int32, (1, 1, LANES), 2)
    low = lane < (LANES // 2)
    col = lax.broadcasted_iota(jnp.int32, (1, w, 1), 1)
    grow = lax.broadcasted_iota(jnp.int32, (r, 1, 1), 0) + i * r
    outs = []
    for hh in range(2):
        cols = slice(hh * LANES, (hh + 1) * LANES)
        w_lo, w_hi = POOL_WINDOWS[2 * hh], POOL_WINDOWS[2 * hh + 1]
        cur = cur_ref[:, cols]
        if two_d:
            prev = jnp.where(i == 0, 0.0, prev_ref[:, cols])
            nxt = jnp.where(i == n_tiles - 1, 0.0, next_ref[:, cols])
            ext = jnp.concatenate([prev, cur, nxt], axis=0)
        else:
            ext = cur
        ext = ext.reshape(nr, w, LANES)
        x = jnp.concatenate([jnp.zeros((nr, POOL_PAD, LANES), F32), ext], axis=1).reshape(flat, LANES)
        p = x + pltpu.roll(x, 1, 0)
        sums = {2: p}
        for ww, s in ((4, 1), (8, 2), (16, 4)):
            if ww > w_hi:
                break
            p = pltpu.roll(p, s, 0) + pltpu.roll(p, flat - s, 0)
            sums[ww] = p
        y = jnp.where(low[0], sums[w_lo], sums[w_hi]).reshape(nr, stride, LANES)[:, POOL_PAD:, :]
        if two_d:
            q2 = y[0:nr - 1] + y[1:nr]
            rsum = {2: q2[7:7 + r]}
            q4 = q2[0:nr - 3] + q2[2:nr - 1]
            rsum[4] = q4[6:6 + r]
            if w_hi > 4:
                q8 = q4[0:nr - 7] + q4[4:nr - 3]
                rsum[8] = q8[4:4 + r]
                q16 = q8[0:nr - 15] + q8[8:nr - 7]
                rsum[16] = q16[0:r]
            tot = jnp.where(low, rsum[w_lo], rsum[w_hi])
            cnt_r = jnp.where(low, _window_counts(grow, rows_total, w_lo), _window_counts(grow, rows_total, w_hi))
            tot = tot / cnt_r
        else:
            tot = y
        cnt_c = jnp.where(low, _window_counts(col, w, w_lo), _window_counts(col, w, w_hi))
        pooled = (tot / cnt_c).reshape(r * w, LANES)
        outs.append(pooled - cur)
    d = jnp.concatenate(outs, axis=-1)
    o_ref[...] = (_dot(d.astype(BF16), pw_ref[...]) * ps_ref[...]).astype(o_ref.dtype)


def _pool(z_p, pool_w_bd, pool_scale, rows):
    b, l, nch = z_p.shape
    if rows is None:
        rows_total, w, r = 1, l, 1
    else:
        rows_total, w, r = rows, l // rows, min(32, rows)
    t = r * w
    n_tiles = l // t
    hb = SUBLANES * w if rows is not None else t
    per = t // hb
    nhb = l // hb
    return pl.pallas_call(
        functools.partial(_pool_kernel, r=r, w=w, rows_total=rows_total, n_tiles=n_tiles),
        grid=(b, n_tiles),
        in_specs=[pl.BlockSpec((None, t, nch), lambda bi, i: (bi, i, 0)),
                  pl.BlockSpec((None, hb, nch), lambda bi, i: (bi, jnp.maximum(i * per - 1, 0), 0)),
                  pl.BlockSpec((None, hb, nch), lambda bi, i: (bi, jnp.minimum((i + 1) * per, nhb - 1), 0)),
                  pl.BlockSpec(pool_w_bd.shape, lambda bi, i: (0, 0)),
                  pl.BlockSpec((1, nch), lambda bi, i: (0, 0))],
        out_specs=pl.BlockSpec((None, t, nch), lambda bi, i: (bi, i, 0)),
        out_shape=jax.ShapeDtypeStruct((b, l, nch), BF16),
        compiler_params=_cparams("parallel", "parallel"),
    )(z_p, z_p, z_p, pool_w_bd, pool_scale.reshape(1, nch))


def _first_argmax4(v):
    m = jnp.maximum(jnp.maximum(v[0], v[1]), jnp.maximum(v[2], v[3]))
    idx = jnp.where(v[0] >= m, 0, jnp.where(v[1] >= m, 1, jnp.where(v[2] >= m, 2, 3)))
    return m, idx


def _select4(idx, v):
    return jnp.where(idx == 0, v[0], jnp.where(idx == 1, v[1], jnp.where(idx == 2, v[2], v[3])))


def _router_logits(h_bf16, wr_ref, br_ref):
    return (_dot(h_bf16, wr_ref[...]) + br_ref[...]).T


def _group_logits(lt):
    return [lt[g:g + 1, :] for g in range(N_GROUPS)]


def _expert_gates(lt, group):
    lg = _group_logits(lt)
    m, _ = _first_argmax4(lg)
    den = sum(jnp.exp(v - m) for v in lg)
    p_group = jnp.exp(_select4(group, lg) - m) / den
    sel = [_select4(group, [lt[SUBLANES + EXPERTS_PER_GROUP * g + e:SUBLANES + EXPERTS_PER_GROUP * g + e + 1, :]
                            for g in range(N_GROUPS)]) for e in range(EXPERTS_PER_GROUP)]
    v1, e1 = _first_argmax4(sel)
    rest = [jnp.where(e1 == e, -jnp.inf, sel[e]) for e in range(EXPERTS_PER_GROUP)]
    v2, e2 = _first_argmax4(rest)
    t2 = jnp.exp(v2 - v1)
    w1 = p_group / (1.0 + t2)
    w2 = p_group * t2 / (1.0 + t2)
    return [jnp.where(e1 == e, w1, jnp.where(e2 == e, w2, 0.0)) for e in range(EXPERTS_PER_GROUP)]


def _combine_kernel(hf_ref, hb_ref, zg_ref, yp_ref, yf_ref, x_ref, mod_ref, wo_ref, g_ref, wr_ref, br_ref,
                    xo_ref, h2t_ref, gid_ref):
    y_a = ((hf_ref[...].astype(F32) + hb_ref[...].astype(F32))
           * jax.nn.gelu(zg_ref[...].astype(F32), approximate=True))
    ycat = jnp.concatenate([y_a.astype(BF16), yp_ref[...].astype(BF16), yf_ref[...].astype(BF16)], axis=-1)
    x_new = x_ref[...] + mod_ref[2:3, :] * _dot(ycat, wo_ref[...])
    xo_ref[...] = x_new
    h2 = _rms_mod(x_new, g_ref[...], mod_ref[3:4, :], mod_ref[4:5, :])
    _to_token_tiles(h2t_ref, h2)
    _, gidx = _first_argmax4(_group_logits(_router_logits(h2.astype(BF16), wr_ref, br_ref)))
    gid_ref[...] = gidx


def _combine(hf, hb, z_g, y_p, y_f, x, mod, w_out, g_ffn, w_router, b_router):
    b, l, d = x.shape
    assert d == SUBLANES * LANES, "a token tile is one (8, 128) f32 tile"
    t = min(1024, l)
    tok = lambda c: pl.BlockSpec((None, t, c), lambda bi, i: (bi, i, 0))
    full = lambda a: pl.BlockSpec(a.shape, lambda bi, i: (0,) * a.ndim)
    return pl.pallas_call(
        _combine_kernel,
        grid=(b, l // t),
        in_specs=[tok(hf.shape[-1]), tok(hb.shape[-1]), tok(z_g.shape[-1]), tok(y_p.shape[-1]), tok(y_f.shape[-1]),
                  tok(d), pl.BlockSpec((None, N_MOD, d), lambda bi, i: (bi, 0, 0)),
                  full(w_out), full(g_ffn), full(w_router), full(b_router)],
        out_specs=[tok(d), pl.BlockSpec((None, t * SUBLANES, LANES), lambda bi, i: (bi, i, 0)),
                   pl.BlockSpec((None, 1, t), lambda bi, i: (bi, 0, i))],
        out_shape=[jax.ShapeDtypeStruct((b, l, d), F32), jax.ShapeDtypeStruct((b, l * SUBLANES, LANES), F32),
                   jax.ShapeDtypeStruct((b, 1, l), jnp.int32)],
        compiler_params=_cparams("parallel", "parallel"),
    )(hf, hb, z_g, y_p, y_f, x, mod, w_out, g_ffn, w_router, b_router)


MOE_TILE = 512


def _sort_plan(gid, n_groups, t):
    n = gid.shape[0]
    onehot = (gid[:, None] == jnp.arange(n_groups, dtype=jnp.int32)[None, :]).astype(jnp.int32)
    csum = jnp.cumsum(onehot, axis=0)
    rank = jnp.sum((csum - onehot) * onehot, axis=1)
    tiles = (csum[-1] + t - 1) // t
    tile_end = jnp.cumsum(tiles)
    slot = jnp.sum(onehot * (tile_end - tiles)[None, :], axis=1) * t + rank
    n_tiles = n // t + n_groups
    tile_idx = jnp.arange(n_tiles, dtype=jnp.int32)
    tile_group = jnp.minimum(jnp.sum((tile_idx[:, None] >= tile_end[None, :]).astype(jnp.int32), axis=1), n_groups - 1)
    pad_lo = jnp.concatenate([(tile_end - tiles) * t + csum[-1], tile_end[-1:] * t]).astype(jnp.int32)
    pad_hi = jnp.concatenate([tile_end * t, jnp.full((1,), n_tiles * t, jnp.int32)]).astype(jnp.int32)
    return (slot.astype(jnp.int32), tile_group.astype(jnp.int32), tile_end[-1:].astype(jnp.int32), pad_lo, pad_hi,
            n_tiles)


def _invert_kernel(slot_ref, pad_lo_ref, pad_hi_ref, src_ref):
    def clear(s, carry):
        src_ref[s] = -1
        return carry

    for k in range(pad_lo_ref.shape[0]):
        lax.fori_loop(pad_lo_ref[k], pad_hi_ref[k], clear, 0)

    def put(tok, carry):
        src_ref[slot_ref[tok]] = tok
        return carry

    lax.fori_loop(0, slot_ref.shape[0], put, 0, unroll=16)


def _invert(slot, pad_lo, pad_hi, n_rows):
    smem = pl.BlockSpec(memory_space=pltpu.SMEM)
    return pl.pallas_call(
        _invert_kernel,
        in_specs=[smem, smem, smem],
        out_specs=smem,
        out_shape=jax.ShapeDtypeStruct((n_rows,), jnp.int32),
    )(slot, pad_lo, pad_hi)


def _moe_fused_kernel(tg_ref, nu_ref, nv_ref, src_ref, dst_ref, h_ref, wr_ref, br_ref, wg_ref, wu_ref, wd_ref, y_ref,
                      rows, ybuf, hb_scr, gt_scr, gsem, ssem, *, t, epg):
    i = pl.program_id(0)
    nu = nu_ref[0]
    s = lax.rem(i, 2)
    tile_rows = t * SUBLANES

    def token(ref, first_row):
        return ref.at[pl.ds(pl.multiple_of(first_row, SUBLANES), SUBLANES)]

    def move_rows(slot, gather_tile=None, scatter_tile=None):
        def body(g, carry):
            for k in range(SUBLANES):
                r = g * SUBLANES + k
                if gather_tile is not None:
                    pltpu.make_async_copy(token(h_ref, src_ref[gather_tile * t + r]),
                                          token(rows.at[slot], r * SUBLANES), gsem.at[slot]).start(priority=0)
                if scatter_tile is not None:
                    pltpu.make_async_copy(token(ybuf.at[slot], r * SUBLANES),
                                          token(y_ref, dst_ref[scatter_tile * t + r]), ssem.at[slot]).start(priority=1)
            return carry
        lax.fori_loop(0, t // SUBLANES, body, 0)

    def wait_gather(slot):
        pltpu.make_async_copy(h_ref.at[pl.ds(0, tile_rows)], rows.at[slot], gsem.at[slot]).wait()

    def wait_scatter(slot):
        pltpu.make_async_copy(ybuf.at[slot], y_ref.at[pl.ds(0, tile_rows)], ssem.at[slot]).wait()

    @pl.when(i == 0)
    def _():
        gt_scr[...] = jnp.zeros_like(gt_scr)
        move_rows(0, gather_tile=0)

    @pl.when(jnp.logical_and(i >= 2, i - 2 < nu))
    def _():
        wait_scatter(s)

    @pl.when(i < nu)
    def _():
        wait_gather(s)
        hb_scr[...] = _from_token_tiles(rows, (s,)).astype(BF16)
        tok_id = lax.broadcasted_iota(jnp.int32, (1, t), 1)
        gate_rows = _expert_gates(_router_logits(hb_scr[...], wr_ref, br_ref), tg_ref[i])
        for e in range(epg):
            gt_scr[e:e + 1, :] = jnp.where(tok_id < nv_ref[i], gate_rows[e], 0.0)
        gates = gt_scr[...].T
        has_next = i + 1 < nu
        has_prev = i >= 1

        @pl.when(jnp.logical_and(has_next, has_prev))
        def _():
            move_rows(1 - s, gather_tile=i + 1, scatter_tile=i - 1)

        @pl.when(jnp.logical_and(has_next, jnp.logical_not(has_prev)))
        def _():
            move_rows(1 - s, gather_tile=i + 1)

        @pl.when(jnp.logical_and(jnp.logical_not(has_next), has_prev))
        def _():
            move_rows(1 - s, scatter_tile=i - 1)

        acc = None
        for e in range(epg):
            hid = jax.nn.silu(_dot(hb_scr[...], wg_ref[e])) * _dot(hb_scr[...], wu_ref[e])
            y = gates[:, e:e + 1] * _dot(hid.astype(BF16), wd_ref[e])
            acc = y if acc is None else acc + y
        _to_token_tiles(ybuf, acc, (s,))

    @pl.when(i == nu)
    def _():
        move_rows(1 - s, scatter_tile=i - 1)

    @pl.when(jnp.logical_and(i > nu, i < pl.num_programs(0) - 1))
    def _():
        rows[0] = jnp.zeros(rows.shape[1:], F32)
        first = pl.multiple_of(dst_ref[(i - 1) * t], SUBLANES)
        fill = pltpu.make_async_copy(rows.at[0], y_ref.at[pl.ds(first, tile_rows)], gsem.at[0])
        fill.start()
        fill.wait()


def _moe_fused(tile_group, n_used, n_valid, src, dst, h2t, w_router, b_router, wg, wu, wd, group0, n_tiles, t):
    epg = wg.shape[1]
    last = n_tiles - 1
    d = SUBLANES * LANES
    tile_rows = t * SUBLANES
    full = lambda a: pl.BlockSpec(a.shape, lambda i, *prefetch: (0,) * a.ndim)
    by_group = lambda a: pl.BlockSpec((None,) + a.shape[1:],
                                      lambda i, tg, *rest: (group0 + tg[jnp.minimum(i, last)], 0, 0, 0))
    return pl.pallas_call(
        functools.partial(_moe_fused_kernel, t=t, epg=epg),
        grid_spec=pltpu.PrefetchScalarGridSpec(
            num_scalar_prefetch=5, grid=(n_tiles + 2,),
            in_specs=[pl.BlockSpec(memory_space=pl.ANY), full(w_router), full(b_router),
                      by_group(wg), by_group(wu), by_group(wd)],
            out_specs=pl.BlockSpec(memory_space=pl.ANY),
            scratch_shapes=[pltpu.VMEM((2, tile_rows, LANES), F32), pltpu.VMEM((2, tile_rows, LANES), F32),
                            pltpu.VMEM((t, d), BF16), pltpu.VMEM((LANES, t), F32),
                            pltpu.SemaphoreType.DMA((2,)), pltpu.SemaphoreType.DMA((2,))]),
        out_shape=jax.ShapeDtypeStruct((n_tiles * tile_rows, LANES), F32),
        compiler_params=_cparams("arbitrary"),
    )(tile_group, n_used, n_valid, src, dst, h2t, w_router, b_router, wg, wu, wd)


def _final_kernel(x_ref, y_ref, mod_ref, g_ref, o_ref):
    out = x_ref[...] + mod_ref[5:6, :] * _from_token_tiles(y_ref)
    ms = jnp.mean(out * out, axis=-1, keepdims=True)
    o_ref[...] = out * lax.rsqrt(ms + EPS) * g_ref[...]


def _final(x, y_rows, mod, g_final):
    b, l, d = x.shape
    t = min(1024, l)
    nt = l // t
    return pl.pallas_call(
        _final_kernel,
        grid=(b, nt),
        in_specs=[pl.BlockSpec((None, t, d), lambda bi, i: (bi, i, 0)),
                  pl.BlockSpec((t * SUBLANES, LANES), lambda bi, i: (bi * nt + i, 0)),
                  pl.BlockSpec((None, N_MOD, d), lambda bi, i: (bi, 0, 0)),
                  pl.BlockSpec((1, d), lambda bi, i: (0, 0))],
        out_specs=pl.BlockSpec((None, t, d), lambda bi, i: (bi, i, 0)),
        out_shape=jax.ShapeDtypeStruct((b, l, d), F32),
        compiler_params=_cparams("parallel", "parallel"),
    )(x, y_rows, mod, g_final)


def kernel(x, c, ctx, c_ctx, mod_w, mod_b, norm_mix_g, norm_ffn_g, w_in, conv_w, conv_b, rec_gate_a_w, rec_gate_a_b,
           rec_gate_x_w, rec_gate_x_b, rec_lambda, pool_w, pool_scale, fourier_w, w_out, router_group_w,
           router_group_b, router_expert_w, router_expert_b, expert_w_gate, expert_w_up, expert_w_down, final_norm_g):
    b, l, d = x.shape
    depth = mod_w.shape[0]
    d_rec = conv_w.shape[-1]
    d_pool = pool_scale.shape[-1]
    d_four = fourier_w.shape[1] * fourier_w.shape[2]
    widths = (d_rec, d_rec, d_pool, d_four)
    rows = l // GRID_W
    heads_half = REC_HEADS // 2
    n_exp = expert_w_gate.shape[1]

    cvec = jnp.concatenate([c, c_ctx[None, :], jnp.zeros((SUBLANES - b - 1, d), F32)], axis=0)
    mod_all = _modulation(cvec, mod_w, mod_b).reshape(depth, SUBLANES, N_MOD, d)
    g_final = final_norm_g.reshape(1, d)
    zeros_state = jnp.zeros((b, 2, d_rec), F32)
    pending = pending_ctx = None
    by_group = lambda w: w.astype(BF16).reshape((depth * N_GROUPS, EXPERTS_PER_GROUP) + w.shape[2:])
    expert_w = (by_group(expert_w_gate), by_group(expert_w_up), by_group(expert_w_down))

    for li in range(depth):
        last = li == depth - 1
        mod_lat = mod_all[li, :b]
        mod_ctx = jnp.broadcast_to(mod_all[li, b][None], (b, N_MOD, d))
        w_in_l = w_in[li].astype(BF16)
        w_out_l = w_out[li].astype(BF16)
        wgate = (0.5 * jnp.stack([
            jnp.stack([jnp.concatenate([_block_diag(rec_gate_a_w[li, dd, hh * heads_half:(hh + 1) * heads_half]),
                                        _block_diag(rec_gate_x_w[li, dd, hh * heads_half:(hh + 1) * heads_half])], axis=1)
                       for hh in range(2)]) for dd in range(2)])).astype(BF16)
        bgate = 0.5 * jnp.stack([rec_gate_a_b[li], rec_gate_x_b[li]], axis=1)
        rec_p = (wgate, bgate, rec_lambda[li])
        proj_p = (w_in_l, conv_w[li], conv_b[li].reshape(1, d_rec), widths)
        pool_w_bd = _block_diag(pool_w[li]).astype(BF16)
        four_w_bd = _block_diag(fourier_w[li]).astype(BF16)
        w_router = jnp.concatenate([router_group_w[li], jnp.zeros((d, SUBLANES - N_GROUPS), F32), router_expert_w[li],
                                    jnp.zeros((d, LANES - SUBLANES - n_exp), F32)], axis=1).astype(BF16)
        b_router = jnp.concatenate([router_group_b[li], jnp.zeros((SUBLANES - N_GROUPS,), F32), router_expert_b[li],
                                    jnp.zeros((LANES - SUBLANES - n_exp,), F32)]).reshape(1, LANES)
        g_ffn = norm_ffn_g[li].reshape(1, d)

        def mixer_tail(hf, hb, z_g, z_p, z_f, stream, mod, grid_rows):
            y_p = _pool(z_p, pool_w_bd, pool_scale[li], grid_rows)
            y_f = _fourier(z_f, four_w_bd)
            return _combine(hf, hb, z_g, y_p, y_f, stream, mod, w_out_l, g_ffn, w_router, b_router)

        def expert_stage(h2t, gid):
            n_tok = gid.size
            t_moe = min(MOE_TILE, h2t.shape[1] // SUBLANES)
            slot, tile_group, n_used, pad_lo, pad_hi, n_tiles = _sort_plan(gid.reshape(n_tok), N_GROUPS, t_moe)
            src = _invert(slot, pad_lo, pad_hi, n_tiles * t_moe)
            valid = src >= 0
            pad_rank = jnp.cumsum(jnp.logical_not(valid).astype(jnp.int32)) - 1
            dst = jnp.where(valid, src, n_tok + pad_rank)
            n_valid = jnp.sum(valid.reshape(n_tiles, t_moe).astype(jnp.int32), axis=1)
            return _moe_fused(tile_group, n_used, n_valid, jnp.maximum(src, 0) * SUBLANES, dst * SUBLANES,
                              h2t.reshape(n_tok * SUBLANES, LANES),
                              w_router, b_router, *expert_w, li * N_GROUPS, n_tiles, t_moe)

        if pending_ctx is None:
            uc, zc_g, zc_p, zc_f = _inproj(ctx, norm_mix_g[li], mod_ctx, *proj_p)
        else:
            ctx, uc, zc_g, zc_p, zc_f = _inproj(ctx, norm_mix_g[li], mod_ctx, *proj_p, residual=pending_ctx)
        hf_c, hb_c = _recurrent(uc, *rec_p, zeros_state, F32)
        state = jnp.stack([hf_c[:, -1, :], hb_c[:, 0, :]], axis=1)
        if not last:
            ctx, h2t_c, gid_c = mixer_tail(hf_c, hb_c, zc_g, zc_p, zc_f, ctx, mod_ctx, None)
            pending_ctx = (expert_stage(h2t_c, gid_c), 0, mod_ctx)

        if pending is None:
            u, z_g, z_p, z_f = _inproj(x, norm_mix_g[li], mod_lat, *proj_p)
        else:
            x, u, z_g, z_p, z_f = _inproj(x, norm_mix_g[li], mod_lat, *proj_p, residual=pending)
        hf, hb = _recurrent(u, *rec_p, state, BF16)
        x, h2t, gid = mixer_tail(hf, hb, z_g, z_p, z_f, x, mod_lat, rows)
        pending = (expert_stage(h2t, gid), 0, mod_lat)
    return _final(x, pending[0], pending[2], g_final)
```

```python
import functools
import math

import jax
import jax.numpy as jnp
from jax import lax
from jax.experimental import pallas as pl
from jax.experimental.pallas import tpu as pltpu

F32 = jnp.float32
BF16 = jnp.bfloat16

GRID_W = 64
N_MOD = 6
REC_HEADS = 8
CONV_W = 4
LRU_C = 8.0
POOL_WINDOWS = (2, 4, 8, 16)
FOUR_HEADS = 4
N_GROUPS = 4
EXPERTS_PER_GROUP = 4
EPS = 1e-6

LANES = 128
SUBLANES = 8
POOL_PAD = 8
VMEM_LIMIT = 56 * 1024 * 1024


def _cparams(*sem):
    return pltpu.CompilerParams(dimension_semantics=sem, vmem_limit_bytes=VMEM_LIMIT)


def _split_bf16(a):
    hi = a.astype(BF16)
    lo = (a - hi.astype(F32)).astype(BF16)
    return hi, lo


def _dot(a, b):
    return jnp.dot(a, b, preferred_element_type=F32)


def _sigmoid_of_twice(half_x):
    return 0.5 * jnp.tanh(half_x) + 0.5


def _mod_kernel(s_ref, w_ref, b_ref, o_ref):
    s = s_ref[...]
    s = s * jax.nn.sigmoid(s)
    s_hi, s_lo = _split_bf16(s)
    w_hi, w_lo = _split_bf16(w_ref[...])
    o_ref[...] = _dot(s_hi, w_hi) + _dot(s_hi, w_lo) + _dot(s_lo, w_hi) + b_ref[...]


def _modulation(cvec, mod_w, mod_b):
    depth, d, dm = mod_w.shape
    tn = dm // 4
    return pl.pallas_call(
        _mod_kernel,
        grid=(depth, dm // tn),
        in_specs=[pl.BlockSpec((SUBLANES, d), lambda l, j: (0, 0)),
                  pl.BlockSpec((None, d, tn), lambda l, j: (l, 0, j)),
                  pl.BlockSpec((None, 1, tn), lambda l, j: (l, 0, j))],
        out_specs=pl.BlockSpec((None, SUBLANES, tn), lambda l, j: (l, 0, j)),
        out_shape=jax.ShapeDtypeStruct((depth, SUBLANES, dm), F32),
        compiler_params=_cparams("parallel", "parallel"),
    )(cvec, mod_w, mod_b.reshape(depth, 1, dm))


def _rms_mod(x, g, shift, scale):
    ms = jnp.mean(x * x, axis=-1, keepdims=True)
    y = x * lax.rsqrt(ms + EPS) * g
    return y * (1.0 + scale) + shift


def _from_token_tiles(ref, lead=()):
    n = ref.shape[-2] // SUBLANES
    return jnp.concatenate([ref[lead + (pl.ds(j, n, stride=SUBLANES), slice(None))] for j in range(SUBLANES)], axis=-1)


def _to_token_tiles(ref, val, lead=()):
    n = val.shape[0]
    for j in range(SUBLANES):
        ref[lead + (pl.ds(j, n, stride=SUBLANES), slice(None))] = val[:, j * LANES:(j + 1) * LANES]


def _inproj_kernel(*refs, widths, with_residual, n_tiles):
    i = pl.program_id(1)
    if with_residual:
        (x_ref, xp_ref, xn_ref, y_ref, yp_ref, yn_ref, pmod_ref,
         g_ref, mod_ref, w_ref, cw_ref, cb_ref, xo_ref, *o_refs) = refs
        gate = pmod_ref[5:6, :]
        x = x_ref[...] + gate * _from_token_tiles(y_ref)
        xo_ref[...] = x
        x_halo = jnp.concatenate([xp_ref[...] + gate * _from_token_tiles(yp_ref),
                                  xn_ref[...] + gate * _from_token_tiles(yn_ref)], axis=0)
    else:
        x_ref, xp_ref, xn_ref, g_ref, mod_ref, w_ref, cw_ref, cb_ref, *o_refs = refs
        x = x_ref[...]
        x_halo = jnp.concatenate([xp_ref[...], xn_ref[...]], axis=0)
    t = x.shape[0]
    d_rec = widths[0]
    z = _dot(_rms_mod(x, g_ref[...], mod_ref[0:1, :], mod_ref[1:2, :]).astype(BF16), w_ref[...])
    zh = _dot(_rms_mod(x_halo, g_ref[...], mod_ref[0:1, :], mod_ref[1:2, :]).astype(BF16), w_ref[:, :d_rec])
    zz = jnp.concatenate([jnp.where(i == 0, 0.0, zh[:SUBLANES]), z[:, :d_rec],
                          jnp.where(i == n_tiles - 1, 0.0, zh[SUBLANES:])], axis=0)
    u = cb_ref[...]
    for k in range(CONV_W):
        o = SUBLANES - CONV_W // 2 + k
        u = u + cw_ref[k:k + 1, :] * zz[o:o + t, :]
    seg = t // SUBLANES
    for q in range(d_rec // LANES):
        for s in range(SUBLANES):
            o_refs[0][q, pl.ds(s, seg, stride=SUBLANES), :] = u[s * seg:(s + 1) * seg, q * LANES:(q + 1) * LANES]
    off = d_rec
    for o_ref, wd in zip(o_refs[1:], widths[1:]):
        o_ref[...] = z[:, off:off + wd].astype(o_ref.dtype)
        off += wd


def _inproj(x, g, mod, w_bf16, conv_w, conv_b, widths, residual=None):
    b, l, d = x.shape
    t = min(1024 if residual is None else 512, l)
    nt = l // t
    per = t // SUBLANES
    n8 = l // SUBLANES
    tok = lambda c: pl.BlockSpec((None, t, c), lambda bi, i: (bi, i, 0))
    prev8 = lambda bi, i: jnp.maximum(i * per - 1, 0)
    next8 = lambda bi, i: jnp.minimum((i + 1) * per, n8 - 1)
    x_specs = [tok(d), pl.BlockSpec((None, SUBLANES, d), lambda bi, i: (bi, prev8(bi, i), 0)),
               pl.BlockSpec((None, SUBLANES, d), lambda bi, i: (bi, next8(bi, i), 0))]
    modspec = pl.BlockSpec((None, N_MOD, d), lambda bi, i: (bi, 0, 0))
    full = lambda a: pl.BlockSpec(a.shape, lambda bi, i: (0,) * a.ndim)
    tail_specs = [pl.BlockSpec((1, d), lambda bi, i: (0, 0)), modspec, full(w_bf16), full(conv_w), full(conv_b)]
    tail_args = (g.reshape(1, d), mod, w_bf16, conv_w, conv_b)
    slabs = widths[0] // LANES
    z_specs = ([pl.BlockSpec((None, None, slabs, t, LANES), lambda bi, i: (bi, i, 0, 0, 0))]
               + [tok(wd) for wd in widths[1:]])
    z_shapes = ([jax.ShapeDtypeStruct((b, nt, slabs, t, LANES), F32)]
                + [jax.ShapeDtypeStruct((b, l, wd), BF16 if k == 0 else F32) for k, wd in enumerate(widths[1:])])
    if residual is None:
        in_specs, args, out_specs, out_shape = x_specs + tail_specs, (x, x, x) + tail_args, z_specs, z_shapes
    else:
        y_rows, row0, prev_mod = residual
        blk0, blk8 = row0 // t, row0 // SUBLANES
        halo = SUBLANES * SUBLANES
        y_specs = [pl.BlockSpec((t * SUBLANES, LANES), lambda bi, i: (blk0 + bi * nt + i, 0)),
                   pl.BlockSpec((halo, LANES), lambda bi, i: (blk8 + bi * n8 + prev8(bi, i), 0)),
                   pl.BlockSpec((halo, LANES), lambda bi, i: (blk8 + bi * n8 + next8(bi, i), 0))]
        in_specs = x_specs + y_specs + [modspec] + tail_specs
        args = (x, x, x, y_rows, y_rows, y_rows, prev_mod) + tail_args
        out_specs, out_shape = [tok(d)] + z_specs, [jax.ShapeDtypeStruct((b, l, d), F32)] + z_shapes
    return pl.pallas_call(
        functools.partial(_inproj_kernel, widths=widths, with_residual=residual is not None, n_tiles=nt),
        grid=(b, nt), in_specs=in_specs, out_specs=out_specs, out_shape=out_shape,
        compiler_params=_cparams("parallel", "parallel"),
    )(*args)


def _rec_kernel(uf_ref, ub_ref, wg_ref, bg_ref, lam_ref, h0_ref, hf_ref, hb_ref, a_scr, b_scr, carry_scr, *, tc):
    i = pl.program_id(1)
    nq = uf_ref.shape[0]
    c = nq * LANES
    half = c // 2
    seg = tc // SUBLANES
    lanes = lambda q: slice(q * LANES, (q + 1) * LANES)

    @pl.when(i == 0)
    def _():
        carry_scr[...] = h0_ref[...]

    def coefficients(d, u_ref):
        u = jnp.concatenate([u_ref[q] for q in range(nq)], axis=-1)
        log_decay = -LRU_C * jax.nn.softplus(-lam_ref[d:d + 1, :])
        for hh in range(2):
            cols = slice(hh * half, (hh + 1) * half)
            uh = u[:, cols]
            pre = _dot(uh.astype(BF16), wg_ref[d, hh])
            r = _sigmoid_of_twice(pre[:, :half] + bg_ref[d, 0:1, cols])
            ig = _sigmoid_of_twice(pre[:, half:] + bg_ref[d, 1:2, cols])
            log_a = log_decay[:, cols] * r
            a = jnp.exp(log_a)
            th = jnp.tanh(log_a)
            one_minus_a2 = -2.0 * th / (1.0 - th)
            root = jnp.where(one_minus_a2 > 0.0, one_minus_a2 * lax.rsqrt(one_minus_a2), 0.0)
            bt = root * (ig * uh)
            for k in range(half // LANES):
                q = hh * (half // LANES) + k
                a_scr[d, q] = a[:, lanes(k)]
                b_scr[d, q] = bt[:, lanes(k)]

    coefficients(0, uf_ref)
    coefficients(1, ub_ref)

    def step(d, j, h, acc):
        rows8 = pl.ds(pl.multiple_of(j * SUBLANES, SUBLANES), SUBLANES)
        a8 = jnp.concatenate([a_scr[d, q, rows8, :] for q in range(nq)], axis=-1)
        b8 = jnp.concatenate([b_scr[d, q, rows8, :] for q in range(nq)], axis=-1)
        h = a8 * h + b8
        acc = a8 * acc
        for q in range(nq):
            b_scr[d, q, rows8, :] = h[:, lanes(q)]
            a_scr[d, q, rows8, :] = acc[:, lanes(q)]
        return h, acc

    def body(j, carry):
        hf, af, hb, ab = carry
        hf, af = step(0, j, hf, af)
        hb, ab = step(1, seg - 1 - j, hb, ab)
        return hf, af, hb, ab

    zero = jnp.zeros((SUBLANES, c), F32)
    one = jnp.ones((SUBLANES, c), F32)
    hf_end, af_end, hb_end, ab_end = lax.fori_loop(0, seg, body, (zero, one, zero, one), unroll=2)

    def chain(h_end, a_end, state, order):
        entering = [None] * SUBLANES
        for s in order:
            entering[s] = state
            state = a_end[s:s + 1, :] * state + h_end[s:s + 1, :]
        return jnp.concatenate(entering, axis=0), state

    cin_f, out_f = chain(hf_end, af_end, carry_scr[0:1, :], range(SUBLANES))
    cin_b, out_b = chain(hb_end, ab_end, carry_scr[1:2, :], reversed(range(SUBLANES)))
    carry_scr[0:1, :] = out_f
    carry_scr[1:2, :] = out_b

    for d, cin, out_ref in ((0, cin_f, hf_ref), (1, cin_b, hb_ref)):
        for q in range(nq):
            full_h = (b_scr[d, q].reshape(seg, SUBLANES, LANES)
                      + a_scr[d, q].reshape(seg, SUBLANES, LANES) * cin[None, :, lanes(q)])
            b_scr[d, q] = full_h.reshape(tc, LANES)
            for s in range(SUBLANES):
                out_ref[s * seg:(s + 1) * seg, lanes(q)] = (
                    b_scr[d, q, pl.ds(s, seg, stride=SUBLANES), :].astype(out_ref.dtype))


def _recurrent(u, wgate, bgate, lam, h0, out_dtype):
    b, nc, nq, tc, _ = u.shape
    c = nq * LANES
    u_fwd = pl.BlockSpec((None, None, nq, tc, LANES), lambda bi, i: (bi, i, 0, 0, 0))
    u_bwd = pl.BlockSpec((None, None, nq, tc, LANES), lambda bi, i: (bi, nc - 1 - i, 0, 0, 0))
    h_fwd = pl.BlockSpec((None, tc, c), lambda bi, i: (bi, i, 0))
    h_bwd = pl.BlockSpec((None, tc, c), lambda bi, i: (bi, nc - 1 - i, 0))
    full = lambda a: pl.BlockSpec(a.shape, lambda bi, i: (0,) * a.ndim)
    return pl.pallas_call(
        functools.partial(_rec_kernel, tc=tc),
        grid=(b, nc),
        in_specs=[u_fwd, u_bwd, full(wgate), full(bgate), full(lam),
                  pl.BlockSpec((None, 2, c), lambda bi, i: (bi, 0, 0))],
        out_specs=[h_fwd, h_bwd],
        out_shape=[jax.ShapeDtypeStruct((b, nc * tc, c), out_dtype)] * 2,
        scratch_shapes=[pltpu.VMEM((2, nq, tc, LANES), F32), pltpu.VMEM((2, nq, tc, LANES), F32),
                        pltpu.VMEM((2, c), F32)],
        compiler_params=_cparams("parallel", "arbitrary"),
    )(u, u, wgate, bgate, lam, h0)


def _dft_tables(n):
    j = lax.broadcasted_iota(jnp.int32, (n, n), 0)
    k = lax.broadcasted_iota(jnp.int32, (n, n), 1)
    ang = ((j * k) % n).astype(F32) * (2.0 * math.pi / n)
    return jnp.cos(ang), jnp.sin(ang)


def _block_diag(w):
    return jax.scipy.linalg.block_diag(*[w[h] for h in range(w.shape[0])])


def _channel_stage(gr, gi, c_ref, s_ref, w_ref, scale):
    f = (_dot(gr.astype(BF16), c_ref[...]) + _dot(gi.astype(BF16), s_ref[...])) * scale
    return _dot(f.astype(BF16), w_ref[...])


def _slabs_from(scr, val):
    for q in range(scr.shape[0]):
        scr[q] = val[:, q * LANES:(q + 1) * LANES]


def _slab_rows(scr, start, n):
    return jnp.concatenate([scr[q, pl.ds(start, n, stride=SUBLANES), :] for q in range(scr.shape[0])], axis=-1)


def _set_slab_rows(scr, start, val):
    for q in range(scr.shape[0]):
        scr[q, pl.ds(start, val.shape[0], stride=SUBLANES), :] = val[:, q * LANES:(q + 1) * LANES]


def _slabs_value(scr):
    return jnp.concatenate([scr[q] for q in range(scr.shape[0])], axis=-1)


def _four1_kernel(x_ref, f1_ref, twc_ref, tws_ref, ar_ref, ai_ref, xs, ars, ais, *, n1):
    shape = x_ref.shape
    _slabs_from(xs, x_ref[...].reshape(n1 * SUBLANES, shape[-1]))
    for jj in range(SUBLANES):
        a = _dot(f1_ref[...], _slab_rows(xs, jj, n1).astype(BF16))
        ar, ai = a[:n1], a[n1:]
        cc = twc_ref[:, jj:jj + 1]
        ss = tws_ref[:, jj:jj + 1]
        _set_slab_rows(ars, jj, ar * cc + ai * ss)
        _set_slab_rows(ais, jj, ai * cc - ar * ss)
    ar_ref[...] = _slabs_value(ars).reshape(shape)
    ai_ref[...] = _slabs_value(ais).reshape(shape)


def _four2_kernel(ar_ref, ai_ref, f2_ref, c_ref, s_ref, w_ref, o_ref, ys, *, n2, scale):
    grs, gis = [], []
    for kk in range(SUBLANES):
        slab = jnp.concatenate([ar_ref[kk], ai_ref[kk]], axis=0).astype(BF16)
        g = _dot(f2_ref[...], slab)
        grs.append(g[:n2])
        gis.append(g[n2:])
    y = _channel_stage(jnp.concatenate(grs, axis=0), jnp.concatenate(gis, axis=0), c_ref, s_ref, w_ref, scale)
    for kk in range(SUBLANES):
        _set_slab_rows(ys, kk, y[kk * n2:(kk + 1) * n2])
    o_ref[...] = _slabs_value(ys).reshape(o_ref.shape)


def _four_small_kernel(x_ref, f_ref, c_ref, s_ref, w_ref, o_ref, *, n, scale):
    g = _dot(f_ref[...], x_ref[...].astype(BF16))
    o_ref[...] = _channel_stage(g[:n], g[n:], c_ref, s_ref, w_ref, scale)


def _fourier(z_f, four_w_bd):
    b, l, nch = z_f.shape
    hd = nch // FOUR_HEADS
    scale = 1.0 / math.sqrt(l * hd)
    c_h, s_h = _dft_tables(hd)
    eye = jnp.eye(FOUR_HEADS, dtype=F32)
    c_bd = jnp.kron(eye, c_h).astype(BF16)
    s_bd = jnp.kron(eye, s_h).astype(BF16)
    full = lambda a, nd: pl.BlockSpec(a.shape, lambda *_: (0,) * a.ndim)
    if l <= 256:
        cn, sn = _dft_tables(l)
        f = jnp.concatenate([cn, -sn], axis=0).astype(BF16)
        return pl.pallas_call(
            functools.partial(_four_small_kernel, n=l, scale=scale),
            grid=(b,),
            in_specs=[pl.BlockSpec((None, l, nch), lambda bi: (bi, 0, 0)),
                      full(f, 1), full(c_bd, 1), full(s_bd, 1), full(four_w_bd, 1)],
            out_specs=pl.BlockSpec((None, l, nch), lambda bi: (bi, 0, 0)),
            out_shape=jax.ShapeDtypeStruct((b, l, nch), F32),
            compiler_params=_cparams("parallel"),
        )(z_f, f, c_bd, s_bd, four_w_bd)

    n2 = LANES
    n1 = l // n2
    nj = n2 // SUBLANES
    c1, s1 = _dft_tables(n1)
    f1 = jnp.concatenate([c1, -s1], axis=0).astype(BF16)
    c2, s2 = _dft_tables(n2)
    f2 = jnp.concatenate([jnp.concatenate([c2, s2], axis=1),
                          jnp.concatenate([-s2, c2], axis=1)], axis=0).astype(BF16)
    k1 = lax.broadcasted_iota(jnp.int32, (n1, n2), 0)
    t2 = lax.broadcasted_iota(jnp.int32, (n1, n2), 1)
    ang = (k1 * t2).astype(F32) * (2.0 * math.pi / l)
    twc = jnp.cos(ang).reshape(n1, nj, SUBLANES).transpose(1, 0, 2)
    tws = jnp.sin(ang).reshape(n1, nj, SUBLANES).transpose(1, 0, 2)
    ar, ai = pl.pallas_call(
        functools.partial(_four1_kernel, n1=n1),
        grid=(b, nj),
        in_specs=[pl.BlockSpec((None, n1, SUBLANES, nch), lambda bi, j: (bi, 0, j, 0)),
                  full(f1, 2),
                  pl.BlockSpec((None, n1, SUBLANES), lambda bi, j: (j, 0, 0)),
                  pl.BlockSpec((None, n1, SUBLANES), lambda bi, j: (j, 0, 0))],
        out_specs=[pl.BlockSpec((None, n1, SUBLANES, nch), lambda bi, j: (bi, 0, j, 0))] * 2,
        out_shape=[jax.ShapeDtypeStruct((b, n1, n2, nch), F32)] * 2,
        scratch_shapes=[pltpu.VMEM((nch // LANES, n1 * SUBLANES, LANES), F32)] * 3,
        compiler_params=_cparams("parallel", "parallel"),
    )(z_f.reshape(b, n1, n2, nch), f1, twc, tws)
    y = pl.pallas_call(
        functools.partial(_four2_kernel, n2=n2, scale=scale),
        grid=(b, n1 // SUBLANES),
        in_specs=[pl.BlockSpec((None, SUBLANES, n2, nch), lambda bi, j: (bi, j, 0, 0)),
                  pl.BlockSpec((None, SUBLANES, n2, nch), lambda bi, j: (bi, j, 0, 0)),
                  full(f2, 2), full(c_bd, 2), full(s_bd, 2), full(four_w_bd, 2)],
        out_specs=pl.BlockSpec((None, n2, SUBLANES, nch), lambda bi, j: (bi, 0, j, 0)),
        out_shape=jax.ShapeDtypeStruct((b, n2, n1, nch), F32),
        scratch_shapes=[pltpu.VMEM((nch // LANES, n2 * SUBLANES, LANES), F32)],
        compiler_params=_cparams("parallel", "parallel"),
    )(ar, ai, f2, c_bd, s_bd, four_w_bd)
    return y.reshape(b, l, nch)


def _window_counts(idx, n, w):
    return (jnp.minimum(idx + (w - w // 2), n) - jnp.maximum(idx - w // 2, 0)).astype(F32)


def _pool_kernel(cur_ref, prev_ref, next_ref, pw_ref, ps_ref, o_ref, *, r, w, rows_total, n_tiles):
    i = pl.program_id(1)
    two_d = rows_total > 1
    halo = SUBLANES if two_d else 0
    nr = r + 2 * halo
    stride = w + POOL_PAD
    flat = nr * stride
    lane = lax.broadcasted_iota(jnp.int32, (1, 1, LANES), 2)
    low = lane < (LANES // 2)
    col = lax.broadcasted_iota(jnp.int32, (1, w, 1), 1)
    grow = lax.broadcasted_iota(jnp.int32, (r, 1, 1), 0) + i * r
    outs = []
    for hh in range(2):
        cols = slice(hh * LANES, (hh + 1) * LANES)
        w_lo, w_hi = POOL_WINDOWS[2 * hh], POOL_WINDOWS[2 * hh + 1]
        cur = cur_ref[:, cols]
        if two_d:
            prev = jnp.where(i == 0, 0.0, prev_ref[:, cols])
            nxt = jnp.where(i == n_tiles - 1, 0.0, next_ref[:, cols])
            ext = jnp.concatenate([prev, cur, nxt], axis=0)
        else:
            ext = cur
        ext = ext.reshape(nr, w, LANES)
        x = jnp.concatenate([jnp.zeros((nr, POOL_PAD, LANES), F32), ext], axis=1).reshape(flat, LANES)
        p = x + pltpu.roll(x, 1, 0)
        sums = {2: p}
        for ww, s in ((4, 1), (8, 2), (16, 4)):
            if ww > w_hi:
                break
            p = pltpu.roll(p, s, 0) + pltpu.roll(p, flat - s, 0)
            sums[ww] = p
        y = jnp.where(low[0], sums[w_lo], sums[w_hi]).reshape(nr, stride, LANES)[:, POOL_PAD:, :]
        if two_d:
            q2 = y[0:nr - 1] + y[1:nr]
            rsum = {2: q2[7:7 + r]}
            q4 = q2[0:nr - 3] + q2[2:nr - 1]
            rsum[4] = q4[6:6 + r]
            if w_hi > 4:
                q8 = q4[0:nr - 7] + q4[4:nr - 3]
                rsum[8] = q8[4:4 + r]
                q16 = q8[0:nr - 15] + q8[8:nr - 7]
                rsum[16] = q16[0:r]
            tot = jnp.where(low, rsum[w_lo], rsum[w_hi])
            cnt_r = jnp.where(low, _window_counts(grow, rows_total, w_lo), _window_counts(grow, rows_total, w_hi))
            tot = tot / cnt_r
        else:
            tot = y
        cnt_c = jnp.where(low, _window_counts(col, w, w_lo), _window_counts(col, w, w_hi))
        pooled = (tot / cnt_c).reshape(r * w, LANES)
        outs.append(pooled - cur)
    d = jnp.concatenate(outs, axis=-1)
    o_ref[...] = (_dot(d.astype(BF16), pw_ref[...]) * ps_ref[...]).astype(o_ref.dtype)


def _pool(z_p, pool_w_bd, pool_scale, rows):
    b, l, nch = z_p.shape
    if rows is None:
        rows_total, w, r = 1, l, 1
    else:
        rows_total, w, r = rows, l // rows, min(32, rows)
    t = r * w
    n_tiles = l // t
    hb = SUBLANES * w if rows is not None else t
    per = t // hb
    nhb = l // hb
    return pl.pallas_call(
        functools.partial(_pool_kernel, r=r, w=w, rows_total=rows_total, n_tiles=n_tiles),
        grid=(b, n_tiles),
        in_specs=[pl.BlockSpec((None, t, nch), lambda bi, i: (bi, i, 0)),
                  pl.BlockSpec((None, hb, nch), lambda bi, i: (bi, jnp.maximum(i * per - 1, 0), 0)),
                  pl.BlockSpec((None, hb, nch), lambda bi, i: (bi, jnp.minimum((i + 1) * per, nhb - 1), 0)),
                  pl.BlockSpec(pool_w_bd.shape, lambda bi, i: (0, 0)),
                  pl.BlockSpec((1, nch), lambda bi, i: (0, 0))],
        out_specs=pl.BlockSpec((None, t, nch), lambda bi, i: (bi, i, 0)),
        out_shape=jax.ShapeDtypeStruct((b, l, nch), BF16),
        compiler_params=_cparams("parallel", "parallel"),
    )(z_p, z_p, z_p, pool_w_bd, pool_scale.reshape(1, nch))


def _first_argmax4(v):
    m = jnp.maximum(jnp.maximum(v[0], v[1]), jnp.maximum(v[2], v[3]))
    idx = jnp.where(v[0] >= m, 0, jnp.where(v[1] >= m, 1, jnp.where(v[2] >= m, 2, 3)))
    return m, idx


def _select4(idx, v):
    return jnp.where(idx == 0, v[0], jnp.where(idx == 1, v[1], jnp.where(idx == 2, v[2], v[3])))


def _router_logits(h_bf16, wr_ref, br_ref):
    return (_dot(h_bf16, wr_ref[...]) + br_ref[...]).T


def _group_logits(lt):
    return [lt[g:g + 1, :] for g in range(N_GROUPS)]


def _expert_gates(lt, group):
    lg = _group_logits(lt)
    m, _ = _first_argmax4(lg)
    den = sum(jnp.exp(v - m) for v in lg)
    p_group = jnp.exp(_select4(group, lg) - m) / den
    sel = [_select4(group, [lt[SUBLANES + EXPERTS_PER_GROUP * g + e:SUBLANES + EXPERTS_PER_GROUP * g + e + 1, :]
                            for g in range(N_GROUPS)]) for e in range(EXPERTS_PER_GROUP)]
    v1, e1 = _first_argmax4(sel)
    rest = [jnp.where(e1 == e, -jnp.inf, sel[e]) for e in range(EXPERTS_PER_GROUP)]
    v2, e2 = _first_argmax4(rest)
    t2 = jnp.exp(v2 - v1)
    w1 = p_group / (1.0 + t2)
    w2 = p_group * t2 / (1.0 + t2)
    return [jnp.where(e1 == e, w1, jnp.where(e2 == e, w2, 0.0)) for e in range(EXPERTS_PER_GROUP)]


def _combine_kernel(hf_ref, hb_ref, zg_ref, yp_ref, yf_ref, x_ref, mod_ref, wo_ref, g_ref, wr_ref, br_ref,
                    xo_ref, h2t_ref, gid_ref):
    y_a = ((hf_ref[...].astype(F32) + hb_ref[...].astype(F32))
           * jax.nn.gelu(zg_ref[...].astype(F32), approximate=True))
    ycat = jnp.concatenate([y_a.astype(BF16), yp_ref[...].astype(BF16), yf_ref[...].astype(BF16)], axis=-1)
    x_new = x_ref[...] + mod_ref[2:3, :] * _dot(ycat, wo_ref[...])
    xo_ref[...] = x_new
    h2 = _rms_mod(x_new, g_ref[...], mod_ref[3:4, :], mod_ref[4:5, :])
    _to_token_tiles(h2t_ref, h2)
    _, gidx = _first_argmax4(_group_logits(_router_logits(h2.astype(BF16), wr_ref, br_ref)))
    gid_ref[...] = gidx


def _combine(hf, hb, z_g, y_p, y_f, x, mod, w_out, g_ffn, w_router, b_router):
    b, l, d = x.shape
    assert d == SUBLANES * LANES, "a token tile is one (8, 128) f32 tile"
    t = min(1024, l)
    tok = lambda c: pl.BlockSpec((None, t, c), lambda bi, i: (bi, i, 0))
    full = lambda a: pl.BlockSpec(a.shape, lambda bi, i: (0,) * a.ndim)
    return pl.pallas_call(
        _combine_kernel,
        grid=(b, l // t),
        in_specs=[tok(hf.shape[-1]), tok(hb.shape[-1]), tok(z_g.shape[-1]), tok(y_p.shape[-1]), tok(y_f.shape[-1]),
                  tok(d), pl.BlockSpec((None, N_MOD, d), lambda bi, i: (bi, 0, 0)),
                  full(w_out), full(g_ffn), full(w_router), full(b_router)],
        out_specs=[tok(d), pl.BlockSpec((None, t * SUBLANES, LANES), lambda bi, i: (bi, i, 0)),
                   pl.BlockSpec((None, 1, t), lambda bi, i: (bi, 0, i))],
        out_shape=[jax.ShapeDtypeStruct((b, l, d), F32), jax.ShapeDtypeStruct((b, l * SUBLANES, LANES), F32),
                   jax.ShapeDtypeStruct((b, 1, l), jnp.int32)],
        compiler_params=_cparams("parallel", "parallel"),
    )(hf, hb, z_g, y_p, y_f, x, mod, w_out, g_ffn, w_router, b_router)


MOE_TILE = 512


def _sort_plan(gid, n_groups, t):
    n = gid.shape[0]
    onehot = (gid[:, None] == jnp.arange(n_groups, dtype=jnp.int32)[None, :]).astype(jnp.int32)
    csum = jnp.cumsum(onehot, axis=0)
    rank = jnp.sum((csum - onehot) * onehot, axis=1)
    tiles = (csum[-1] + t - 1) // t
    tile_end = jnp.cumsum(tiles)
    slot = jnp.sum(onehot * (tile_end - tiles)[None, :], axis=1) * t + rank
    n_tiles = n // t + n_groups
    tile_idx = jnp.arange(n_tiles, dtype=jnp.int32)
    tile_group = jnp.minimum(jnp.sum((tile_idx[:, None] >= tile_end[None, :]).astype(jnp.int32), axis=1), n_groups - 1)
    pad_lo = jnp.concatenate([(tile_end - tiles) * t + csum[-1], tile_end[-1:] * t]).astype(jnp.int32)
    pad_hi = jnp.concatenate([tile_end * t, jnp.full((1,), n_tiles * t, jnp.int32)]).astype(jnp.int32)
    return (slot.astype(jnp.int32), tile_group.astype(jnp.int32), tile_end[-1:].astype(jnp.int32), pad_lo, pad_hi,
            n_tiles)


def _invert_kernel(slot_ref, pad_lo_ref, pad_hi_ref, src_ref):
    def clear(s, carry):
        src_ref[s] = -1
        return carry

    for k in range(pad_lo_ref.shape[0]):
        lax.fori_loop(pad_lo_ref[k], pad_hi_ref[k], clear, 0)

    def put(tok, carry):
        src_ref[slot_ref[tok]] = tok
        return carry

    lax.fori_loop(0, slot_ref.shape[0], put, 0, unroll=16)


def _invert(slot, pad_lo, pad_hi, n_rows):
    smem = pl.BlockSpec(memory_space=pltpu.SMEM)
    return pl.pallas_call(
        _invert_kernel,
        in_specs=[smem, smem, smem],
        out_specs=smem,
        out_shape=jax.ShapeDtypeStruct((n_rows,), jnp.int32),
    )(slot, pad_lo, pad_hi)


def _moe_fused_kernel(tg_ref, nu_ref, nv_ref, src_ref, dst_ref, h_ref, wr_ref, br_ref, wg_ref, wu_ref, wd_ref, y_ref,
                      rows, ybuf, hb_scr, gt_scr, gsem, ssem, *, t, epg):
    i = pl.program_id(0)
    nu = nu_ref[0]
    s = lax.rem(i, 2)
    tile_rows = t * SUBLANES

    def token(ref, first_row):
        return ref.at[pl.ds(pl.multiple_of(first_row, SUBLANES), SUBLANES)]

    def move_rows(slot, gather_tile=None, scatter_tile=None):
        def body(g, carry):
            for k in range(SUBLANES):
                r = g * SUBLANES + k
                if gather_tile is not None:
                    pltpu.make_async_copy(token(h_ref, src_ref[gather_tile * t + r]),
                                          token(rows.at[slot], r * SUBLANES), gsem.at[slot]).start(priority=0)
                if scatter_tile is not None:
                    pltpu.make_async_copy(token(ybuf.at[slot], r * SUBLANES),
                                          token(y_ref, dst_ref[scatter_tile * t + r]), ssem.at[slot]).start(priority=1)
            return carry
        lax.fori_loop(0, t // SUBLANES, body, 0)

    def wait_gather(slot):
        pltpu.make_async_copy(h_ref.at[pl.ds(0, tile_rows)], rows.at[slot], gsem.at[slot]).wait()

    def wait_scatter(slot):
        pltpu.make_async_copy(ybuf.at[slot], y_ref.at[pl.ds(0, tile_rows)], ssem.at[slot]).wait()

    @pl.when(i == 0)
    def _():
        gt_scr[...] = jnp.zeros_like(gt_scr)
        move_rows(0, gather_tile=0)

    @pl.when(jnp.logical_and(i >= 2, i - 2 < nu))
    def _():
        wait_scatter(s)

    @pl.when(i < nu)
    def _():
        wait_gather(s)
        hb_scr[...] = _from_token_tiles(rows, (s,)).astype(BF16)
        tok_id = lax.broadcasted_iota(jnp.int32, (1, t), 1)
        gate_rows = _expert_gates(_router_logits(hb_scr[...], wr_ref, br_ref), tg_ref[i])
        for e in range(epg):
            gt_scr[e:e + 1, :] = jnp.where(tok_id < nv_ref[i], gate_rows[e], 0.0)
        gates = gt_scr[...].T
        has_next = i + 1 < nu
        has_prev = i >= 1

        @pl.when(jnp.logical_and(has_next, has_prev))
        def _():
            move_rows(1 - s, gather_tile=i + 1, scatter_tile=i - 1)

        @pl.when(jnp.logical_and(has_next, jnp.logical_not(has_prev)))
        def _():
            move_rows(1 - s, gather_tile=i + 1)

        @pl.when(jnp.logical_and(jnp.logical_not(has_next), has_prev))
        def _():
            move_rows(1 - s, scatter_tile=i - 1)

        acc = None
        for e in range(epg):
            hid = jax.nn.silu(_dot(hb_scr[...], wg_ref[e])) * _dot(hb_scr[...], wu_ref[e])
            y = gates[:, e:e + 1] * _dot(hid.astype(BF16), wd_ref[e])
            acc = y if acc is None else acc + y
        _to_token_tiles(ybuf, acc, (s,))

    @pl.when(i == nu)
    def _():
        move_rows(1 - s, scatter_tile=i - 1)

    @pl.when(jnp.logical_and(i > nu, i < pl.num_programs(0) - 1))
    def _():
        rows[0] = jnp.zeros(rows.shape[1:], F32)
        first = pl.multiple_of(dst_ref[(i - 1) * t], SUBLANES)
        fill = pltpu.make_async_copy(rows.at[0], y_ref.at[pl.ds(first, tile_rows)], gsem.at[0])
        fill.start()
        fill.wait()


def _moe_fused(tile_group, n_used, n_valid, src, dst, h2t, w_router, b_router, wg, wu, wd, group0, n_tiles, t):
    epg = wg.shape[1]
    last = n_tiles - 1
    d = SUBLANES * LANES
    tile_rows = t * SUBLANES
    full = lambda a: pl.BlockSpec(a.shape, lambda i, *prefetch: (0,) * a.ndim)
    by_group = lambda a: pl.BlockSpec((None,) + a.shape[1:],
                                      lambda i, tg, *rest: (group0 + tg[jnp.minimum(i, last)], 0, 0, 0))
    return pl.pallas_call(
        functools.partial(_moe_fused_kernel, t=t, epg=epg),
        grid_spec=pltpu.PrefetchScalarGridSpec(
            num_scalar_prefetch=5, grid=(n_tiles + 2,),
            in_specs=[pl.BlockSpec(memory_space=pl.ANY), full(w_router), full(b_router),
                      by_group(wg), by_group(wu), by_group(wd)],
            out_specs=pl.BlockSpec(memory_space=pl.ANY),
            scratch_shapes=[pltpu.VMEM((2, tile_rows, LANES), F32), pltpu.VMEM((2, tile_rows, LANES), F32),
                            pltpu.VMEM((t, d), BF16), pltpu.VMEM((LANES, t), F32),
                            pltpu.SemaphoreType.DMA((2,)), pltpu.SemaphoreType.DMA((2,))]),
        out_shape=jax.ShapeDtypeStruct((n_tiles * tile_rows, LANES), F32),
        compiler_params=_cparams("arbitrary"),
    )(tile_group, n_used, n_valid, src, dst, h2t, w_router, b_router, wg, wu, wd)


def _final_kernel(x_ref, y_ref, mod_ref, g_ref, o_ref):
    out = x_ref[...] + mod_ref[5:6, :] * _from_token_tiles(y_ref)
    ms = jnp.mean(out * out, axis=-1, keepdims=True)
    o_ref[...] = out * lax.rsqrt(ms + EPS) * g_ref[...]


def _final(x, y_rows, mod, g_final):
    b, l, d = x.shape
    t = min(1024, l)
    nt = l // t
    return pl.pallas_call(
        _final_kernel,
        grid=(b, nt),
        in_specs=[pl.BlockSpec((None, t, d), lambda bi, i: (bi, i, 0)),
                  pl.BlockSpec((t * SUBLANES, LANES), lambda bi, i: (bi * nt + i, 0)),
                  pl.BlockSpec((None, N_MOD, d), lambda bi, i: (bi, 0, 0)),
                  pl.BlockSpec((1, d), lambda bi, i: (0, 0))],
        out_specs=pl.BlockSpec((None, t, d), lambda bi, i: (bi, i, 0)),
        out_shape=jax.ShapeDtypeStruct((b, l, d), F32),
        compiler_params=_cparams("parallel", "parallel"),
    )(x, y_rows, mod, g_final)


def kernel(x, c, ctx, c_ctx, mod_w, mod_b, norm_mix_g, norm_ffn_g, w_in, conv_w, conv_b, rec_gate_a_w, rec_gate_a_b,
           rec_gate_x_w, rec_gate_x_b, rec_lambda, pool_w, pool_scale, fourier_w, w_out, router_group_w,
           router_group_b, router_expert_w, router_expert_b, expert_w_gate, expert_w_up, expert_w_down, final_norm_g):
    b, l, d = x.shape
    depth = mod_w.shape[0]
    d_rec = conv_w.shape[-1]
    d_pool = pool_scale.shape[-1]
    d_four = fourier_w.shape[1] * fourier_w.shape[2]
    widths = (d_rec, d_rec, d_pool, d_four)
    rows = l // GRID_W
    heads_half = REC_HEADS // 2
    n_exp = expert_w_gate.shape[1]

    cvec = jnp.concatenate([c, c_ctx[None, :], jnp.zeros((SUBLANES - b - 1, d), F32)], axis=0)
    mod_all = _modulation(cvec, mod_w, mod_b).reshape(depth, SUBLANES, N_MOD, d)
    g_final = final_norm_g.reshape(1, d)
    zeros_state = jnp.zeros((b, 2, d_rec), F32)
    pending = pending_ctx = None
    by_group = lambda w: w.astype(BF16).reshape((depth * N_GROUPS, EXPERTS_PER_GROUP) + w.shape[2:])
    expert_w = (by_group(expert_w_gate), by_group(expert_w_up), by_group(expert_w_down))

    for li in range(depth):
        last = li == depth - 1
        mod_lat = mod_all[li, :b]
        mod_ctx = jnp.broadcast_to(mod_all[li, b][None], (b, N_MOD, d))
        w_in_l = w_in[li].astype(BF16)
        w_out_l = w_out[li].astype(BF16)
        wgate = (0.5 * jnp.stack([
            jnp.stack([jnp.concatenate([_block_diag(rec_gate_a_w[li, dd, hh * heads_half:(hh + 1) * heads_half]),
                                        _block_diag(rec_gate_x_w[li, dd, hh * heads_half:(hh + 1) * heads_half])], axis=1)
                       for hh in range(2)]) for dd in range(2)])).astype(BF16)
        bgate = 0.5 * jnp.stack([rec_gate_a_b[li], rec_gate_x_b[li]], axis=1)
        rec_p = (wgate, bgate, rec_lambda[li])
        proj_p = (w_in_l, conv_w[li], conv_b[li].reshape(1, d_rec), widths)
        pool_w_bd = _block_diag(pool_w[li]).astype(BF16)
        four_w_bd = _block_diag(fourier_w[li]).astype(BF16)
        w_router = jnp.concatenate([router_group_w[li], jnp.zeros((d, SUBLANES - N_GROUPS), F32), router_expert_w[li],
                                    jnp.zeros((d, LANES - SUBLANES - n_exp), F32)], axis=1).astype(BF16)
        b_router = jnp.concatenate([router_group_b[li], jnp.zeros((SUBLANES - N_GROUPS,), F32), router_expert_b[li],
                                    jnp.zeros((LANES - SUBLANES - n_exp,), F32)]).reshape(1, LANES)
        g_ffn = norm_ffn_g[li].reshape(1, d)

        def mixer_tail(hf, hb, z_g, z_p, z_f, stream, mod, grid_rows):
            y_p = _pool(z_p, pool_w_bd, pool_scale[li], grid_rows)
            y_f = _fourier(z_f, four_w_bd)
            return _combine(hf, hb, z_g, y_p, y_f, stream, mod, w_out_l, g_ffn, w_router, b_router)

        def expert_stage(h2t, gid):
            n_tok = gid.size
            t_moe = min(MOE_TILE, h2t.shape[1] // SUBLANES)
            slot, tile_group, n_used, pad_lo, pad_hi, n_tiles = _sort_plan(gid.reshape(n_tok), N_GROUPS, t_moe)
            src = _invert(slot, pad_lo, pad_hi, n_tiles * t_moe)
            valid = src >= 0
            pad_rank = jnp.cumsum(jnp.logical_not(valid).astype(jnp.int32)) - 1
            dst = jnp.where(valid, src, n_tok + pad_rank)
            n_valid = jnp.sum(valid.reshape(n_tiles, t_moe).astype(jnp.int32), axis=1)
            return _moe_fused(tile_group, n_used, n_valid, jnp.maximum(src, 0) * SUBLANES, dst * SUBLANES,
                              h2t.reshape(n_tok * SUBLANES, LANES),
                              w_router, b_router, *expert_w, li * N_GROUPS, n_tiles, t_moe)

        if pending_ctx is None:
            uc, zc_g, zc_p, zc_f = _inproj(ctx, norm_mix_g[li], mod_ctx, *proj_p)
        else:
            ctx, uc, zc_g, zc_p, zc_f = _inproj(ctx, norm_mix_g[li], mod_ctx, *proj_p, residual=pending_ctx)
        hf_c, hb_c = _recurrent(uc, *rec_p, zeros_state, F32)
        state = jnp.stack([hf_c[:, -1, :], hb_c[:, 0, :]], axis=1)
        if not last:
            ctx, h2t_c, gid_c = mixer_tail(hf_c, hb_c, zc_g, zc_p, zc_f, ctx, mod_ctx, None)
            pending_ctx = (expert_stage(h2t_c, gid_c), 0, mod_ctx)

        if pending is None:
            u, z_g, z_p, z_f = _inproj(x, norm_mix_g[li], mod_lat, *proj_p)
        else:
            x, u, z_g, z_p, z_f = _inproj(x, norm_mix_g[li], mod_lat, *proj_p, residual=pending)
        hf, hb = _recurrent(u, *rec_p, state, BF16)
        x, h2t, gid = mixer_tail(hf, hb, z_g, z_p, z_f, x, mod_lat, rows)
        pending = (expert_stage(h2t, gid), 0, mod_lat)
    return _final(x, pending[0], pending[2], g_final)
```

```python
import functools
import math

import jax
import jax.numpy as jnp
from jax import lax
from jax.experimental import pallas as pl
from jax.experimental.pallas import tpu as pltpu

F32 = jnp.float32
BF16 = jnp.bfloat16

GRID_W = 64
N_MOD = 6
REC_HEADS = 8
CONV_W = 4
LRU_C = 8.0
POOL_WINDOWS = (2, 4, 8, 16)
FOUR_HEADS = 4
N_GROUPS = 4
EXPERTS_PER_GROUP = 4
EPS = 1e-6

LANES = 128
SUBLANES = 8
POOL_PAD = 8
VMEM_LIMIT = 56 * 1024 * 1024


def _cparams(*sem):
    return pltpu.CompilerParams(dimension_semantics=sem, vmem_limit_bytes=VMEM_LIMIT)


def _split_bf16(a):
    hi = a.astype(BF16)
    lo = (a - hi.astype(F32)).astype(BF16)
    return hi, lo


def _dot(a, b):
    return jnp.dot(a, b, preferred_element_type=F32)


def _sigmoid_of_twice(half_x):
    return 0.5 * jnp.tanh(half_x) + 0.5


def _mod_kernel(s_ref, w_ref, b_ref, o_ref):
    s = s_ref[...]
    s = s * jax.nn.sigmoid(s)
    s_hi, s_lo = _split_bf16(s)
    w_hi, w_lo = _split_bf16(w_ref[...])
    o_ref[...] = _dot(s_hi, w_hi) + _dot(s_hi, w_lo) + _dot(s_lo, w_hi) + b_ref[...]


def _modulation(cvec, mod_w, mod_b):
    depth, d, dm = mod_w.shape
    tn = dm // 4
    return pl.pallas_call(
        _mod_kernel,
        grid=(depth, dm // tn),
        in_specs=[pl.BlockSpec((SUBLANES, d), lambda l, j: (0, 0)),
                  pl.BlockSpec((None, d, tn), lambda l, j: (l, 0, j)),
                  pl.BlockSpec((None, 1, tn), lambda l, j: (l, 0, j))],
        out_specs=pl.BlockSpec((None, SUBLANES, tn), lambda l, j: (l, 0, j)),
        out_shape=jax.ShapeDtypeStruct((depth, SUBLANES, dm), F32),
        compiler_params=_cparams("parallel", "parallel"),
    )(cvec, mod_w, mod_b.reshape(depth, 1, dm))


def _rms_mod(x, g, shift, scale):
    ms = jnp.mean(x * x, axis=-1, keepdims=True)
    y = x * lax.rsqrt(ms + EPS) * g
    return y * (1.0 + scale) + shift


def _from_token_tiles(ref, lead=()):
    n = ref.shape[-2] // SUBLANES
    return jnp.concatenate([ref[lead + (pl.ds(j, n, stride=SUBLANES), slice(None))] for j in range(SUBLANES)], axis=-1)


def _to_token_tiles(ref, val, lead=()):
    n = val.shape[0]
    for j in range(SUBLANES):
        ref[lead + (pl.ds(j, n, stride=SUBLANES), slice(None))] = val[:, j * LANES:(j + 1) * LANES]


def _inproj_kernel(*refs, widths, with_residual, n_tiles):
    i = pl.program_id(1)
    if with_residual:
        (x_ref, xp_ref, xn_ref, y_ref, yp_ref, yn_ref, pmod_ref,
         g_ref, mod_ref, w_ref, cw_ref, cb_ref, xo_ref, *o_refs) = refs
        gate = pmod_ref[5:6, :]
        x = x_ref[...] + gate * _from_token_tiles(y_ref)
        xo_ref[...] = x
        x_halo = jnp.concatenate([xp_ref[...] + gate * _from_token_tiles(yp_ref),
                                  xn_ref[...] + gate * _from_token_tiles(yn_ref)], axis=0)
    else:
        x_ref, xp_ref, xn_ref, g_ref, mod_ref, w_ref, cw_ref, cb_ref, *o_refs = refs
        x = x_ref[...]
        x_halo = jnp.concatenate([xp_ref[...], xn_ref[...]], axis=0)
    t = x.shape[0]
    d_rec = widths[0]
    z = _dot(_rms_mod(x, g_ref[...], mod_ref[0:1, :], mod_ref[1:2, :]).astype(BF16), w_ref[...])
    zh = _dot(_rms_mod(x_halo, g_ref[...], mod_ref[0:1, :], mod_ref[1:2, :]).astype(BF16), w_ref[:, :d_rec])
    zz = jnp.concatenate([jnp.where(i == 0, 0.0, zh[:SUBLANES]), z[:, :d_rec],
                          jnp.where(i == n_tiles - 1, 0.0, zh[SUBLANES:])], axis=0)
    u = cb_ref[...]
    for k in range(CONV_W):
        o = SUBLANES - CONV_W // 2 + k
        u = u + cw_ref[k:k + 1, :] * zz[o:o + t, :]
    seg = t // SUBLANES
    for q in range(d_rec // LANES):
        for s in range(SUBLANES):
            o_refs[0][q, pl.ds(s, seg, stride=SUBLANES), :] = u[s * seg:(s + 1) * seg, q * LANES:(q + 1) * LANES]
    off = d_rec
    for o_ref, wd in zip(o_refs[1:], widths[1:]):
        o_ref[...] = z[:, off:off + wd].astype(o_ref.dtype)
        off += wd


def _inproj(x, g, mod, w_bf16, conv_w, conv_b, widths, residual=None):
    b, l, d = x.shape
    t = min(1024 if residual is None else 512, l)
    nt = l // t
    per = t // SUBLANES
    n8 = l // SUBLANES
    tok = lambda c: pl.BlockSpec((None, t, c), lambda bi, i: (bi, i, 0))
    prev8 = lambda bi, i: jnp.maximum(i * per - 1, 0)
    next8 = lambda bi, i: jnp.minimum((i + 1) * per, n8 - 1)
    x_specs = [tok(d), pl.BlockSpec((None, SUBLANES, d), lambda bi, i: (bi, prev8(bi, i), 0)),
               pl.BlockSpec((None, SUBLANES, d), lambda bi, i: (bi, next8(bi, i), 0))]
    modspec = pl.BlockSpec((None, N_MOD, d), lambda bi, i: (bi, 0, 0))
    full = lambda a: pl.BlockSpec(a.shape, lambda bi, i: (0,) * a.ndim)
    tail_specs = [pl.BlockSpec((1, d), lambda bi, i: (0, 0)), modspec, full(w_bf16), full(conv_w), full(conv_b)]
    tail_args = (g.reshape(1, d), mod, w_bf16, conv_w, conv_b)
    slabs = widths[0] // LANES
    z_specs = ([pl.BlockSpec((None, None, slabs, t, LANES), lambda bi, i: (bi, i, 0, 0, 0))]
               + [tok(wd) for wd in widths[1:]])
    z_shapes = ([jax.ShapeDtypeStruct((b, nt, slabs, t, LANES), F32)]
                + [jax.ShapeDtypeStruct((b, l, wd), BF16 if k == 0 else F32) for k, wd in enumerate(widths[1:])])
    if residual is None:
        in_specs, args, out_specs, out_shape = x_specs + tail_specs, (x, x, x) + tail_args, z_specs, z_shapes
    else:
        y_rows, row0, prev_mod = residual
        blk0, blk8 = row0 // t, row0 // SUBLANES
        halo = SUBLANES * SUBLANES
        y_specs = [pl.BlockSpec((t * SUBLANES, LANES), lambda bi, i: (blk0 + bi * nt + i, 0)),
                   pl.BlockSpec((halo, LANES), lambda bi, i: (blk8 + bi * n8 + prev8(bi, i), 0)),
                   pl.BlockSpec((halo, LANES), lambda bi, i: (blk8 + bi * n8 + next8(bi, i), 0))]
        in_specs = x_specs + y_specs + [modspec] + tail_specs
        args = (x, x, x, y_rows, y_rows, y_rows, prev_mod) + tail_args
        out_specs, out_shape = [tok(d)] + z_specs, [jax.ShapeDtypeStruct((b, l, d), F32)] + z_shapes
    return pl.pallas_call(
        functools.partial(_inproj_kernel, widths=widths, with_residual=residual is not None, n_tiles=nt),
        grid=(b, nt), in_specs=in_specs, out_specs=out_specs, out_shape=out_shape,
        compiler_params=_cparams("parallel", "parallel"),
    )(*args)


def _rec_kernel(uf_ref, ub_ref, wg_ref, bg_ref, lam_ref, h0_ref, hf_ref, hb_ref, a_scr, b_scr, carry_scr, *, tc):
    i = pl.program_id(1)
    nq = uf_ref.shape[0]
    c = nq * LANES
    half = c // 2
    seg = tc // SUBLANES
    lanes = lambda q: slice(q * LANES, (q + 1) * LANES)

    @pl.when(i == 0)
    def _():
        carry_scr[...] = h0_ref[...]

    def coefficients(d, u_ref):
        u = jnp.concatenate([u_ref[q] for q in range(nq)], axis=-1)
        log_decay = -LRU_C * jax.nn.softplus(-lam_ref[d:d + 1, :])
        for hh in range(2):
            cols = slice(hh * half, (hh + 1) * half)
            uh = u[:, cols]
            pre = _dot(uh.astype(BF16), wg_ref[d, hh])
            r = _sigmoid_of_twice(pre[:, :half] + bg_ref[d, 0:1, cols])
            ig = _sigmoid_of_twice(pre[:, half:] + bg_ref[d, 1:2, cols])
            log_a = log_decay[:, cols] * r
            a = jnp.exp(log_a)
            th = jnp.tanh(log_a)
            one_minus_a2 = -2.0 * th / (1.0 - th)
            root = jnp.where(one_minus_a2 > 0.0, one_minus_a2 * lax.rsqrt(one_minus_a2), 0.0)
            bt = root * (ig * uh)
            for k in range(half // LANES):
                q = hh * (half // LANES) + k
                a_scr[d, q] = a[:, lanes(k)]
                b_scr[d, q] = bt[:, lanes(k)]

    coefficients(0, uf_ref)
    coefficients(1, ub_ref)

    def step(d, j, h, acc):
        rows8 = pl.ds(pl.multiple_of(j * SUBLANES, SUBLANES), SUBLANES)
        a8 = jnp.concatenate([a_scr[d, q, rows8, :] for q in range(nq)], axis=-1)
        b8 = jnp.concatenate([b_scr[d, q, rows8, :] for q in range(nq)], axis=-1)
        h = a8 * h + b8
        acc = a8 * acc
        for q in range(nq):
            b_scr[d, q, rows8, :] = h[:, lanes(q)]
            a_scr[d, q, rows8, :] = acc[:, lanes(q)]
        return h, acc

    def body(j, carry):
        hf, af, hb, ab = carry
        hf, af = step(0, j, hf, af)
        hb, ab = step(1, seg - 1 - j, hb, ab)
        return hf, af, hb, ab

    zero = jnp.zeros((SUBLANES, c), F32)
    one = jnp.ones((SUBLANES, c), F32)
    hf_end, af_end, hb_end, ab_end = lax.fori_loop(0, seg, body, (zero, one, zero, one), unroll=2)

    def chain(h_end, a_end, state, order):
        entering = [None] * SUBLANES
        for s in order:
            entering[s] = state
            state = a_end[s:s + 1, :] * state + h_end[s:s + 1, :]
        return jnp.concatenate(entering, axis=0), state

    cin_f, out_f = chain(hf_end, af_end, carry_scr[0:1, :], range(SUBLANES))
    cin_b, out_b = chain(hb_end, ab_end, carry_scr[1:2, :], reversed(range(SUBLANES)))
    carry_scr[0:1, :] = out_f
    carry_scr[1:2, :] = out_b

    for d, cin, out_ref in ((0, cin_f, hf_ref), (1, cin_b, hb_ref)):
        for q in range(nq):
            full_h = (b_scr[d, q].reshape(seg, SUBLANES, LANES)
                      + a_scr[d, q].reshape(seg, SUBLANES, LANES) * cin[None, :, lanes(q)])
            b_scr[d, q] = full_h.reshape(tc, LANES)
            for s in range(SUBLANES):
                out_ref[s * seg:(s + 1) * seg, lanes(q)] = (
                    b_scr[d, q, pl.ds(s, seg, stride=SUBLANES), :].astype(out_ref.dtype))


def _recurrent(u, wgate, bgate, lam, h0, out_dtype):
    b, nc, nq, tc, _ = u.shape
    c = nq * LANES
    u_fwd = pl.BlockSpec((None, None, nq, tc, LANES), lambda bi, i: (bi, i, 0, 0, 0))
    u_bwd = pl.BlockSpec((None, None, nq, tc, LANES), lambda bi, i: (bi, nc - 1 - i, 0, 0, 0))
    h_fwd = pl.BlockSpec((None, tc, c), lambda bi, i: (bi, i, 0))
    h_bwd = pl.BlockSpec((None, tc, c), lambda bi, i: (bi, nc - 1 - i, 0))
    full = lambda a: pl.BlockSpec(a.shape, lambda bi, i: (0,) * a.ndim)
    return pl.pallas_call(
        functools.partial(_rec_kernel, tc=tc),
        grid=(b, nc),
        in_specs=[u_fwd, u_bwd, full(wgate), full(bgate), full(lam),
                  pl.BlockSpec((None, 2, c), lambda bi, i: (bi, 0, 0))],
        out_specs=[h_fwd, h_bwd],
        out_shape=[jax.ShapeDtypeStruct((b, nc * tc, c), out_dtype)] * 2,
        scratch_shapes=[pltpu.VMEM((2, nq, tc, LANES), F32), pltpu.VMEM((2, nq, tc, LANES), F32),
                        pltpu.VMEM((2, c), F32)],
        compiler_params=_cparams("parallel", "arbitrary"),
    )(u, u, wgate, bgate, lam, h0)


def _dft_tables(n):
    j = lax.broadcasted_iota(jnp.int32, (n, n), 0)
    k = lax.broadcasted_iota(jnp.int32, (n, n), 1)
    ang = ((j * k) % n).astype(F32) * (2.0 * math.pi / n)
    return jnp.cos(ang), jnp.sin(ang)


def _block_diag(w):
    return jax.scipy.linalg.block_diag(*[w[h] for h in range(w.shape[0])])


def _channel_stage(gr, gi, c_ref, s_ref, w_ref, scale):
    f = (_dot(gr.astype(BF16), c_ref[...]) + _dot(gi.astype(BF16), s_ref[...])) * scale
    return _dot(f.astype(BF16), w_ref[...])


def _slabs_from(scr, val):
    for q in range(scr.shape[0]):
        scr[q] = val[:, q * LANES:(q + 1) * LANES]


def _slab_rows(scr, start, n):
    return jnp.concatenate([scr[q, pl.ds(start, n, stride=SUBLANES), :] for q in range(scr.shape[0])], axis=-1)


def _set_slab_rows(scr, start, val):
    for q in range(scr.shape[0]):
        scr[q, pl.ds(start, val.shape[0], stride=SUBLANES), :] = val[:, q * LANES:(q + 1) * LANES]


def _slabs_value(scr):
    return jnp.concatenate([scr[q] for q in range(scr.shape[0])], axis=-1)


def _four1_kernel(x_ref, f1_ref, twc_ref, tws_ref, ar_ref, ai_ref, xs, ars, ais, *, n1):
    shape = x_ref.shape
    _slabs_from(xs, x_ref[...].reshape(n1 * SUBLANES, shape[-1]))
    for jj in range(SUBLANES):
        a = _dot(f1_ref[...], _slab_rows(xs, jj, n1).astype(BF16))
        ar, ai = a[:n1], a[n1:]
        cc = twc_ref[:, jj:jj + 1]
        ss = tws_ref[:, jj:jj + 1]
        _set_slab_rows(ars, jj, ar * cc + ai * ss)
        _set_slab_rows(ais, jj, ai * cc - ar * ss)
    ar_ref[...] = _slabs_value(ars).reshape(shape)
    ai_ref[...] = _slabs_value(ais).reshape(shape)


def _four2_kernel(ar_ref, ai_ref, f2_ref, c_ref, s_ref, w_ref, o_ref, ys, *, n2, scale):
    grs, gis = [], []
    for kk in range(SUBLANES):
        slab = jnp.concatenate([ar_ref[kk], ai_ref[kk]], axis=0).astype(BF16)
        g = _dot(f2_ref[...], slab)
        grs.append(g[:n2])
        gis.append(g[n2:])
    y = _channel_stage(jnp.concatenate(grs, axis=0), jnp.concatenate(gis, axis=0), c_ref, s_ref, w_ref, scale)
    for kk in range(SUBLANES):
        _set_slab_rows(ys, kk, y[kk * n2:(kk + 1) * n2])
    o_ref[...] = _slabs_value(ys).reshape(o_ref.shape)


def _four_small_kernel(x_ref, f_ref, c_ref, s_ref, w_ref, o_ref, *, n, scale):
    g = _dot(f_ref[...], x_ref[...].astype(BF16))
    o_ref[...] = _channel_stage(g[:n], g[n:], c_ref, s_ref, w_ref, scale)


def _fourier(z_f, four_w_bd):
    b, l, nch = z_f.shape
    hd = nch // FOUR_HEADS
    scale = 1.0 / math.sqrt(l * hd)
    c_h, s_h = _dft_tables(hd)
    eye = jnp.eye(FOUR_HEADS, dtype=F32)
    c_bd = jnp.kron(eye, c_h).astype(BF16)
    s_bd = jnp.kron(eye, s_h).astype(BF16)
    full = lambda a, nd: pl.BlockSpec(a.shape, lambda *_: (0,) * a.ndim)
    if l <= 256:
        cn, sn = _dft_tables(l)
        f = jnp.concatenate([cn, -sn], axis=0).astype(BF16)
        return pl.pallas_call(
            functools.partial(_four_small_kernel, n=l, scale=scale),
            grid=(b,),
            in_specs=[pl.BlockSpec((None, l, nch), lambda bi: (bi, 0, 0)),
                      full(f, 1), full(c_bd, 1), full(s_bd, 1), full(four_w_bd, 1)],
            out_specs=pl.BlockSpec((None, l, nch), lambda bi: (bi, 0, 0)),
            out_shape=jax.ShapeDtypeStruct((b, l, nch), F32),
            compiler_params=_cparams("parallel"),
        )(z_f, f, c_bd, s_bd, four_w_bd)

    n2 = LANES
    n1 = l // n2
    nj = n2 // SUBLANES
    c1, s1 = _dft_tables(n1)
    f1 = jnp.concatenate([c1, -s1], axis=0).astype(BF16)
    c2, s2 = _dft_tables(n2)
    f2 = jnp.concatenate([jnp.concatenate([c2, s2], axis=1),
                          jnp.concatenate([-s2, c2], axis=1)], axis=0).astype(BF16)
    k1 = lax.broadcasted_iota(jnp.int32, (n1, n2), 0)
    t2 = lax.broadcasted_iota(jnp.int32, (n1, n2), 1)
    ang = (k1 * t2).astype(F32) * (2.0 * math.pi / l)
    twc = jnp.cos(ang).reshape(n1, nj, SUBLANES).transpose(1, 0, 2)
    tws = jnp.sin(ang).reshape(n1, nj, SUBLANES).transpose(1, 0, 2)
    ar, ai = pl.pallas_call(
        functools.partial(_four1_kernel, n1=n1),
        grid=(b, nj),
        in_specs=[pl.BlockSpec((None, n1, SUBLANES, nch), lambda bi, j: (bi, 0, j, 0)),
                  full(f1, 2),
                  pl.BlockSpec((None, n1, SUBLANES), lambda bi, j: (j, 0, 0)),
                  pl.BlockSpec((None, n1, SUBLANES), lambda bi, j: (j, 0, 0))],
        out_specs=[pl.BlockSpec((None, n1, SUBLANES, nch), lambda bi, j: (bi, 0, j, 0))] * 2,
        out_shape=[jax.ShapeDtypeStruct((b, n1, n2, nch), F32)] * 2,
        scratch_shapes=[pltpu.VMEM((nch // LANES, n1 * SUBLANES, LANES), F32)] * 3,
        compiler_params=_cparams("parallel", "parallel"),
    )(z_f.reshape(b, n1, n2, nch), f1, twc, tws)
    y = pl.pallas_call(
        functools.partial(_four2_kernel, n2=n2, scale=scale),
        grid=(b, n1 // SUBLANES),
        in_specs=[pl.BlockSpec((None, SUBLANES, n2, nch), lambda bi, j: (bi, j, 0, 0)),
                  pl.BlockSpec((None, SUBLANES, n2, nch), lambda bi, j: (bi, j, 0, 0)),
                  full(f2, 2), full(c_bd, 2), full(s_bd, 2), full(four_w_bd, 2)],
        out_specs=pl.BlockSpec((None, n2, SUBLANES, nch), lambda bi, j: (bi, 0, j, 0)),
        out_shape=jax.ShapeDtypeStruct((b, n2, n1, nch), F32),
        scratch_shapes=[pltpu.VMEM((nch // LANES, n2 * SUBLANES, LANES), F32)],
        compiler_params=_cparams("parallel", "parallel"),
    )(ar, ai, f2, c_bd, s_bd, four_w_bd)
    return y.reshape(b, l, nch)


def _window_counts(idx, n, w):
    return (jnp.minimum(idx + (w - w // 2), n) - jnp.maximum(idx - w // 2, 0)).astype(F32)


def _pool_kernel(cur_ref, prev_ref, next_ref, pw_ref, ps_ref, o_ref, *, r, w, rows_total, n_tiles):
    i = pl.program_id(1)
    two_d = rows_total > 1
    halo = SUBLANES if two_d else 0
    nr = r + 2 * halo
    stride = w + POOL_PAD
    flat = nr * stride
    lane = lax.broadcasted_iota(jnp.int32, (1, 1, LANES), 2)
    low = lane < (LANES // 2)
    col = lax.broadcasted_iota(jnp.int32, (1, w, 1), 1)
    grow = lax.broadcasted_iota(jnp.int32, (r, 1, 1), 0) + i * r
    outs = []
    for hh in range(2):
        cols = slice(hh * LANES, (hh + 1) * LANES)
        w_lo, w_hi = POOL_WINDOWS[2 * hh], POOL_WINDOWS[2 * hh + 1]
        cur = cur_ref[:, cols]
        if two_d:
            prev = jnp.where(i == 0, 0.0, prev_ref[:, cols])
            nxt = jnp.where(i == n_tiles - 1, 0.0, next_ref[:, cols])
            ext = jnp.concatenate([prev, cur, nxt], axis=0)
        else:
            ext = cur
        ext = ext.reshape(nr, w, LANES)
        x = jnp.concatenate([jnp.zeros((nr, POOL_PAD, LANES), F32), ext], axis=1).reshape(flat, LANES)
        p = x + pltpu.roll(x, 1, 0)
        sums = {2: p}
        for ww, s in ((4, 1), (8, 2), (16, 4)):
            if ww > w_hi:
                break
            p = pltpu.roll(p, s, 0) + pltpu.roll(p, flat - s, 0)
            sums[ww] = p
        y = jnp.where(low[0], sums[w_lo], sums[w_hi]).reshape(nr, stride, LANES)[:, POOL_PAD:, :]
        if two_d:
            q2 = y[0:nr - 1] + y[1:nr]
            rsum = {2: q2[7:7 + r]}
            q4 = q2[0:nr - 3] + q2[2:nr - 1]
            rsum[4] = q4[6:6 + r]
            if w_hi > 4:
                q8 = q4[0:nr - 7] + q4[4:nr - 3]
                rsum[8] = q8[4:4 + r]
                q16 = q8[0:nr - 15] + q8[8:nr - 7]
                rsum[16] = q16[0:r]
            tot = jnp.where(low, rsum[w_lo], rsum[w_hi])
            cnt_r = jnp.where(low, _window_counts(grow, rows_total, w_lo), _window_counts(grow, rows_total, w_hi))
            tot = tot / cnt_r
        else:
            tot = y
        cnt_c = jnp.where(low, _window_counts(col, w, w_lo), _window_counts(col, w, w_hi))
        pooled = (tot / cnt_c).reshape(r * w, LANES)
        outs.append(pooled - cur)
    d = jnp.concatenate(outs, axis=-1)
    o_ref[...] = (_dot(d.astype(BF16), pw_ref[...]) * ps_ref[...]).astype(o_ref.dtype)


def _pool(z_p, pool_w_bd, pool_scale, rows):
    b, l, nch = z_p.shape
    if rows is None:
        rows_total, w, r = 1, l, 1
    else:
        rows_total, w, r = rows, l // rows, min(32, rows)
    t = r * w
    n_tiles = l // t
    hb = SUBLANES * w if rows is not None else t
    per = t // hb
    nhb = l // hb
    return pl.pallas_call(
        functools.partial(_pool_kernel, r=r, w=w, rows_total=rows_total, n_tiles=n_tiles),
        grid=(b, n_tiles),
        in_specs=[pl.BlockSpec((None, t, nch), lambda bi, i: (bi, i, 0)),
                  pl.BlockSpec((None, hb, nch), lambda bi, i: (bi, jnp.maximum(i * per - 1, 0), 0)),
                  pl.BlockSpec((None, hb, nch), lambda bi, i: (bi, jnp.minimum((i + 1) * per, nhb - 1), 0)),
                  pl.BlockSpec(pool_w_bd.shape, lambda bi, i: (0, 0)),
                  pl.BlockSpec((1, nch), lambda bi, i: (0, 0))],
        out_specs=pl.BlockSpec((None, t, nch), lambda bi, i: (bi, i, 0)),
        out_shape=jax.ShapeDtypeStruct((b, l, nch), BF16),
        compiler_params=_cparams("parallel", "parallel"),
    )(z_p, z_p, z_p, pool_w_bd, pool_scale.reshape(1, nch))


def _first_argmax4(v):
    m = jnp.maximum(jnp.maximum(v[0], v[1]), jnp.maximum(v[2], v[3]))
    idx = jnp.where(v[0] >= m, 0, jnp.where(v[1] >= m, 1, jnp.where(v[2] >= m, 2, 3)))
    return m, idx


def _select4(idx, v):
    return jnp.where(idx == 0, v[0], jnp.where(idx == 1, v[1], jnp.where(idx == 2, v[2], v[3])))


def _router_logits(h_bf16, wr_ref, br_ref):
    return (_dot(h_bf16, wr_ref[...]) + br_ref[...]).T


def _group_logits(lt):
    return [lt[g:g + 1, :] for g in range(N_GROUPS)]


def _expert_gates(lt, group):
    lg = _group_logits(lt)
    m, _ = _first_argmax4(lg)
    den = sum(jnp.exp(v - m) for v in lg)
    p_group = jnp.exp(_select4(group, lg) - m) / den
    sel = [_select4(group, [lt[SUBLANES + EXPERTS_PER_GROUP * g + e:SUBLANES + EXPERTS_PER_GROUP * g + e + 1, :]
                            for g in range(N_GROUPS)]) for e in range(EXPERTS_PER_GROUP)]
    v1, e1 = _first_argmax4(sel)
    rest = [jnp.where(e1 == e, -jnp.inf, sel[e]) for e in range(EXPERTS_PER_GROUP)]
    v2, e2 = _first_argmax4(rest)
    t2 = jnp.exp(v2 - v1)
    w1 = p_group / (1.0 + t2)
    w2 = p_group * t2 / (1.0 + t2)
    return [jnp.where(e1 == e, w1, jnp.where(e2 == e, w2, 0.0)) for e in range(EXPERTS_PER_GROUP)]


def _combine_kernel(hf_ref, hb_ref, zg_ref, yp_ref, yf_ref, x_ref, mod_ref, wo_ref, g_ref, wr_ref, br_ref,
                    xo_ref, h2t_ref, gid_ref):
    y_a = ((hf_ref[...].astype(F32) + hb_ref[...].astype(F32))
           * jax.nn.gelu(zg_ref[...].astype(F32), approximate=True))
    ycat = jnp.concatenate([y_a.astype(BF16), yp_ref[...].astype(BF16), yf_ref[...].astype(BF16)], axis=-1)
    x_new = x_ref[...] + mod_ref[2:3, :] * _dot(ycat, wo_ref[...])
    xo_ref[...] = x_new
    h2 = _rms_mod(x_new, g_ref[...], mod_ref[3:4, :], mod_ref[4:5, :])
    _to_token_tiles(h2t_ref, h2)
    _, gidx = _first_argmax4(_group_logits(_router_logits(h2.astype(BF16), wr_ref, br_ref)))
    gid_ref[...] = gidx


def _combine(hf, hb, z_g, y_p, y_f, x, mod, w_out, g_ffn, w_router, b_router):
    b, l, d = x.shape
    assert d == SUBLANES * LANES, "a token tile is one (8, 128) f32 tile"
    t = min(1024, l)
    tok = lambda c: pl.BlockSpec((None, t, c), lambda bi, i: (bi, i, 0))
    full = lambda a: pl.BlockSpec(a.shape, lambda bi, i: (0,) * a.ndim)
    return pl.pallas_call(
        _combine_kernel,
        grid=(b, l // t),
        in_specs=[tok(hf.shape[-1]), tok(hb.shape[-1]), tok(z_g.shape[-1]), tok(y_p.shape[-1]), tok(y_f.shape[-1]),
                  tok(d), pl.BlockSpec((None, N_MOD, d), lambda bi, i: (bi, 0, 0)),
                  full(w_out), full(g_ffn), full(w_router), full(b_router)],
        out_specs=[tok(d), pl.BlockSpec((None, t * SUBLANES, LANES), lambda bi, i: (bi, i, 0)),
                   pl.BlockSpec((None, 1, t), lambda bi, i: (bi, 0, i))],
        out_shape=[jax.ShapeDtypeStruct((b, l, d), F32), jax.ShapeDtypeStruct((b, l * SUBLANES, LANES), F32),
                   jax.ShapeDtypeStruct((b, 1, l), jnp.int32)],
        compiler_params=_cparams("parallel", "parallel"),
    )(hf, hb, z_g, y_p, y_f, x, mod, w_out, g_ffn, w_router, b_router)


MOE_TILE = 512


def _sort_plan(gid, n_groups, t):
    n = gid.shape[0]
    onehot = (gid[:, None] == jnp.arange(n_groups, dtype=jnp.int32)[None, :]).astype(jnp.int32)
    csum = jnp.cumsum(onehot, axis=0)
    rank = jnp.sum((csum - onehot) * onehot, axis=1)
    tiles = (csum[-1] + t - 1) // t
    tile_end = jnp.cumsum(tiles)
    slot = jnp.sum(onehot * (tile_end - tiles)[None, :], axis=1) * t + rank
    n_tiles = n // t + n_groups
    tile_idx = jnp.arange(n_tiles, dtype=jnp.int32)
    tile_group = jnp.minimum(jnp.sum((tile_idx[:, None] >= tile_end[None, :]).astype(jnp.int32), axis=1), n_groups - 1)
    pad_lo = jnp.concatenate([(tile_end - tiles) * t + csum[-1], tile_end[-1:] * t]).astype(jnp.int32)
    pad_hi = jnp.concatenate([tile_end * t, jnp.full((1,), n_tiles * t, jnp.int32)]).astype(jnp.int32)
    return (slot.astype(jnp.int32), tile_group.astype(jnp.int32), tile_end[-1:].astype(jnp.int32), pad_lo, pad_hi,
            n_tiles)


def _invert_kernel(slot_ref, pad_lo_ref, pad_hi_ref, src_ref):
    def clear(s, carry):
        src_ref[s] = -1
        return carry

    for k in range(pad_lo_ref.shape[0]):
        lax.fori_loop(pad_lo_ref[k], pad_hi_ref[k], clear, 0)

    def put(tok, carry):
        src_ref[slot_ref[tok]] = tok
        return carry

    lax.fori_loop(0, slot_ref.shape[0], put, 0, unroll=16)


def _invert(slot, pad_lo, pad_hi, n_rows):
    smem = pl.BlockSpec(memory_space=pltpu.SMEM)
    return pl.pallas_call(
        _invert_kernel,
        in_specs=[smem, smem, smem],
        out_specs=smem,
        out_shape=jax.ShapeDtypeStruct((n_rows,), jnp.int32),
    )(slot, pad_lo, pad_hi)


def _moe_fused_kernel(tg_ref, nu_ref, nv_ref, src_ref, dst_ref, h_ref, wr_ref, br_ref, wg_ref, wu_ref, wd_ref, y_ref,
                      rows, ybuf, hb_scr, gt_scr, gsem, ssem, *, t, epg):
    i = pl.program_id(0)
    nu = nu_ref[0]
    s = lax.rem(i, 2)
    tile_rows = t * SUBLANES

    def token(ref, first_row):
        return ref.at[pl.ds(pl.multiple_of(first_row, SUBLANES), SUBLANES)]

    def move_rows(slot, gather_tile=None, scatter_tile=None):
        def body(g, carry):
            for k in range(SUBLANES):
                r = g * SUBLANES + k
                if gather_tile is not None:
                    pltpu.make_async_copy(token(h_ref, src_ref[gather_tile * t + r]),
                                          token(rows.at[slot], r * SUBLANES), gsem.at[slot]).start(priority=k % 2)
                if scatter_tile is not None:
                    pltpu.make_async_copy(token(ybuf.at[slot], r * SUBLANES),
                                          token(y_ref, dst_ref[scatter_tile * t + r]),
                                          ssem.at[slot]).start(priority=(k + 1) % 2)
            return carry
        lax.fori_loop(0, t // SUBLANES, body, 0)

    def wait_gather(slot):
        pltpu.make_async_copy(h_ref.at[pl.ds(0, tile_rows)], rows.at[slot], gsem.at[slot]).wait()

    def wait_scatter(slot):
        pltpu.make_async_copy(ybuf.at[slot], y_ref.at[pl.ds(0, tile_rows)], ssem.at[slot]).wait()

    @pl.when(i == 0)
    def _():
        gt_scr[...] = jnp.zeros_like(gt_scr)
        move_rows(0, gather_tile=0)

    @pl.when(jnp.logical_and(i >= 2, i - 2 < nu))
    def _():
        wait_scatter(s)

    @pl.when(i < nu)
    def _():
        wait_gather(s)
        hb_scr[...] = _from_token_tiles(rows, (s,)).astype(BF16)
        tok_id = lax.broadcasted_iota(jnp.int32, (1, t), 1)
        gate_rows = _expert_gates(_router_logits(hb_scr[...], wr_ref, br_ref), tg_ref[i])
        for e in range(epg):
            gt_scr[e:e + 1, :] = jnp.where(tok_id < nv_ref[i], gate_rows[e], 0.0)
        gates = gt_scr[...].T
        has_next = i + 1 < nu
        has_prev = i >= 1

        @pl.when(jnp.logical_and(has_next, has_prev))
        def _():
            move_rows(1 - s, gather_tile=i + 1, scatter_tile=i - 1)

        @pl.when(jnp.logical_and(has_next, jnp.logical_not(has_prev)))
        def _():
            move_rows(1 - s, gather_tile=i + 1)

        @pl.when(jnp.logical_and(jnp.logical_not(has_next), has_prev))
        def _():
            move_rows(1 - s, scatter_tile=i - 1)

        acc = None
        for e in range(epg):
            hid = jax.nn.silu(_dot(hb_scr[...], wg_ref[e])) * _dot(hb_scr[...], wu_ref[e])
            y = gates[:, e:e + 1] * _dot(hid.astype(BF16), wd_ref[e])
            acc = y if acc is None else acc + y
        _to_token_tiles(ybuf, acc, (s,))

    @pl.when(i == nu)
    def _():
        move_rows(1 - s, scatter_tile=i - 1)

    @pl.when(jnp.logical_and(i > nu, i < pl.num_programs(0) - 1))
    def _():
        rows[0] = jnp.zeros(rows.shape[1:], F32)
        first = pl.multiple_of(dst_ref[(i - 1) * t], SUBLANES)
        fill = pltpu.make_async_copy(rows.at[0], y_ref.at[pl.ds(first, tile_rows)], gsem.at[0])
        fill.start()
        fill.wait()


def _moe_fused(tile_group, n_used, n_valid, src, dst, h2t, w_router, b_router, wg, wu, wd, group0, n_tiles, t):
    epg = wg.shape[1]
    last = n_tiles - 1
    d = SUBLANES * LANES
    tile_rows = t * SUBLANES
    full = lambda a: pl.BlockSpec(a.shape, lambda i, *prefetch: (0,) * a.ndim)
    by_group = lambda a: pl.BlockSpec((None,) + a.shape[1:],
                                      lambda i, tg, *rest: (group0 + tg[jnp.minimum(i, last)], 0, 0, 0))
    return pl.pallas_call(
        functools.partial(_moe_fused_kernel, t=t, epg=epg),
        grid_spec=pltpu.PrefetchScalarGridSpec(
            num_scalar_prefetch=5, grid=(n_tiles + 2,),
            in_specs=[pl.BlockSpec(memory_space=pl.ANY), full(w_router), full(b_router),
                      by_group(wg), by_group(wu), by_group(wd)],
            out_specs=pl.BlockSpec(memory_space=pl.ANY),
            scratch_shapes=[pltpu.VMEM((2, tile_rows, LANES), F32), pltpu.VMEM((2, tile_rows, LANES), F32),
                            pltpu.VMEM((t, d), BF16), pltpu.VMEM((LANES, t), F32),
                            pltpu.SemaphoreType.DMA((2,)), pltpu.SemaphoreType.DMA((2,))]),
        out_shape=jax.ShapeDtypeStruct((n_tiles * tile_rows, LANES), F32),
        compiler_params=_cparams("arbitrary"),
    )(tile_group, n_used, n_valid, src, dst, h2t, w_router, b_router, wg, wu, wd)


def _final_kernel(x_ref, y_ref, mod_ref, g_ref, o_ref):
    out = x_ref[...] + mod_ref[5:6, :] * _from_token_tiles(y_ref)
    ms = jnp.mean(out * out, axis=-1, keepdims=True)
    o_ref[...] = out * lax.rsqrt(ms + EPS) * g_ref[...]


def _final(x, y_rows, mod, g_final):
    b, l, d = x.shape
    t = min(1024, l)
    nt = l // t
    return pl.pallas_call(
        _final_kernel,
        grid=(b, nt),
        in_specs=[pl.BlockSpec((None, t, d), lambda bi, i: (bi, i, 0)),
                  pl.BlockSpec((t * SUBLANES, LANES), lambda bi, i: (bi * nt + i, 0)),
                  pl.BlockSpec((None, N_MOD, d), lambda bi, i: (bi, 0, 0)),
                  pl.BlockSpec((1, d), lambda bi, i: (0, 0))],
        out_specs=pl.BlockSpec((None, t, d), lambda bi, i: (bi, i, 0)),
        out_shape=jax.ShapeDtypeStruct((b, l, d), F32),
        compiler_params=_cparams("parallel", "parallel"),
    )(x, y_rows, mod, g_final)


def kernel(x, c, ctx, c_ctx, mod_w, mod_b, norm_mix_g, norm_ffn_g, w_in, conv_w, conv_b, rec_gate_a_w, rec_gate_a_b,
           rec_gate_x_w, rec_gate_x_b, rec_lambda, pool_w, pool_scale, fourier_w, w_out, router_group_w,
           router_group_b, router_expert_w, router_expert_b, expert_w_gate, expert_w_up, expert_w_down, final_norm_g):
    b, l, d = x.shape
    depth = mod_w.shape[0]
    d_rec = conv_w.shape[-1]
    d_pool = pool_scale.shape[-1]
    d_four = fourier_w.shape[1] * fourier_w.shape[2]
    widths = (d_rec, d_rec, d_pool, d_four)
    rows = l // GRID_W
    heads_half = REC_HEADS // 2
    n_exp = expert_w_gate.shape[1]

    cvec = jnp.concatenate([c, c_ctx[None, :], jnp.zeros((SUBLANES - b - 1, d), F32)], axis=0)
    mod_all = _modulation(cvec, mod_w, mod_b).reshape(depth, SUBLANES, N_MOD, d)
    g_final = final_norm_g.reshape(1, d)
    zeros_state = jnp.zeros((b, 2, d_rec), F32)
    pending = pending_ctx = None
    by_group = lambda w: w.astype(BF16).reshape((depth * N_GROUPS, EXPERTS_PER_GROUP) + w.shape[2:])
    expert_w = (by_group(expert_w_gate), by_group(expert_w_up), by_group(expert_w_down))

    for li in range(depth):
        last = li == depth - 1
        mod_lat = mod_all[li, :b]
        mod_ctx = jnp.broadcast_to(mod_all[li, b][None], (b, N_MOD, d))
        w_in_l = w_in[li].astype(BF16)
        w_out_l = w_out[li].astype(BF16)
        wgate = (0.5 * jnp.stack([
            jnp.stack([jnp.concatenate([_block_diag(rec_gate_a_w[li, dd, hh * heads_half:(hh + 1) * heads_half]),
                                        _block_diag(rec_gate_x_w[li, dd, hh * heads_half:(hh + 1) * heads_half])], axis=1)
                       for hh in range(2)]) for dd in range(2)])).astype(BF16)
        bgate = 0.5 * jnp.stack([rec_gate_a_b[li], rec_gate_x_b[li]], axis=1)
        rec_p = (wgate, bgate, rec_lambda[li])
        proj_p = (w_in_l, conv_w[li], conv_b[li].reshape(1, d_rec), widths)
        pool_w_bd = _block_diag(pool_w[li]).astype(BF16)
        four_w_bd = _block_diag(fourier_w[li]).astype(BF16)
        w_router = jnp.concatenate([router_group_w[li], jnp.zeros((d, SUBLANES - N_GROUPS), F32), router_expert_w[li],
                                    jnp.zeros((d, LANES - SUBLANES - n_exp), F32)], axis=1).astype(BF16)
        b_router = jnp.concatenate([router_group_b[li], jnp.zeros((SUBLANES - N_GROUPS,), F32), router_expert_b[li],
                                    jnp.zeros((LANES - SUBLANES - n_exp,), F32)]).reshape(1, LANES)
        g_ffn = norm_ffn_g[li].reshape(1, d)

        def mixer_tail(hf, hb, z_g, z_p, z_f, stream, mod, grid_rows):
            y_p = _pool(z_p, pool_w_bd, pool_scale[li], grid_rows)
            y_f = _fourier(z_f, four_w_bd)
            return _combine(hf, hb, z_g, y_p, y_f, stream, mod, w_out_l, g_ffn, w_router, b_router)

        def expert_stage(h2t, gid):
            n_tok = gid.size
            t_moe = min(MOE_TILE, h2t.shape[1] // SUBLANES)
            slot, tile_group, n_used, pad_lo, pad_hi, n_tiles = _sort_plan(gid.reshape(n_tok), N_GROUPS, t_moe)
            src = _invert(slot, pad_lo, pad_hi, n_tiles * t_moe)
            valid = src >= 0
            pad_rank = jnp.cumsum(jnp.logical_not(valid).astype(jnp.int32)) - 1
            dst = jnp.where(valid, src, n_tok + pad_rank)
            n_valid = jnp.sum(valid.reshape(n_tiles, t_moe).astype(jnp.int32), axis=1)
            return _moe_fused(tile_group, n_used, n_valid, jnp.maximum(src, 0) * SUBLANES, dst * SUBLANES,
                              h2t.reshape(n_tok * SUBLANES, LANES),
                              w_router, b_router, *expert_w, li * N_GROUPS, n_tiles, t_moe)

        if pending_ctx is None:
            uc, zc_g, zc_p, zc_f = _inproj(ctx, norm_mix_g[li], mod_ctx, *proj_p)
        else:
            ctx, uc, zc_g, zc_p, zc_f = _inproj(ctx, norm_mix_g[li], mod_ctx, *proj_p, residual=pending_ctx)
        hf_c, hb_c = _recurrent(uc, *rec_p, zeros_state, F32)
        state = jnp.stack([hf_c[:, -1, :], hb_c[:, 0, :]], axis=1)
        if not last:
            ctx, h2t_c, gid_c = mixer_tail(hf_c, hb_c, zc_g, zc_p, zc_f, ctx, mod_ctx, None)
            pending_ctx = (expert_stage(h2t_c, gid_c), 0, mod_ctx)

        if pending is None:
            u, z_g, z_p, z_f = _inproj(x, norm_mix_g[li], mod_lat, *proj_p)
        else:
            x, u, z_g, z_p, z_f = _inproj(x, norm_mix_g[li], mod_lat, *proj_p, residual=pending)
        hf, hb = _recurrent(u, *rec_p, state, BF16)
        x, h2t, gid = mixer_tail(hf, hb, z_g, z_p, z_f, x, mod_lat, rows)
        pending = (expert_stage(h2t, gid), 0, mod_lat)
    return _final(x, pending[0], pending[2], g_final)
```
